```python
import math
import jax, jax.numpy as jnp
from jax import lax
import numpy as np

D_MODEL = 1024
BATCH = 8
SEQ = 8192
DEPTH = 4

N_META = 16
POOL_WIDTH = D_MODEL // 2
POOL_WINDOWS = (2, 4, 8, 16)
POOL_GROUPS = len(POOL_WINDOWS)
POOL_GROUP_DIM = POOL_WIDTH // POOL_GROUPS
N_HEADS = 8
QK_NOPE_DIM = 64
QK_ROPE_DIM = 32
QK_HEAD_DIM = QK_NOPE_DIM + QK_ROPE_DIM
V_HEAD_DIM = 64
MLA_WIDTH = N_HEADS * V_HEAD_DIM
KV_LORA_RANK = 256
Q_LORA_RANK = 768
ROPE_THETA = 10000.0
NORM_EPS = 1e-6
Q_BLOCK = 128
MASK_VALUE = -1e30

IN_SPLITS = (POOL_WIDTH, POOL_WIDTH, Q_LORA_RANK, KV_LORA_RANK, QK_ROPE_DIM, MLA_WIDTH, D_MODEL, D_MODEL)
D_IN = sum(IN_SPLITS)
IN_SPLIT_POINTS = tuple(int(v) for v in np.cumsum(IN_SPLITS)[:-1])

kernel_name = "hybrid_pool_mla_gated_trunk"


def rmsnorm(x, gain):
    xf = x.astype(jnp.float32)
    inv = lax.rsqrt(jnp.mean(xf * xf, axis=-1, keepdims=True) + NORM_EPS)
    return (xf * inv * gain.astype(jnp.float32)).astype(x.dtype)


def apply_rope(x, pos):
    half = x.shape[-1] // 2
    inv_freq = ROPE_THETA ** (-jnp.arange(half, dtype=jnp.float32) / half)
    ang = pos.astype(jnp.float32)[..., None] * inv_freq
    cos = jnp.cos(ang)[:, :, None, :]
    sin = jnp.sin(ang)[:, :, None, :]
    xf = x.astype(jnp.float32)
    x1, x2 = xf[..., :half], xf[..., half:]
    return jnp.concatenate([x1 * cos - x2 * sin, x2 * cos + x1 * sin], axis=-1).astype(x.dtype)


def pool_mix(u, w_group, scale):
    B, L, _ = u.shape
    ug = u.reshape(B, L, POOL_GROUPS, POOL_GROUP_DIM).astype(jnp.float32)
    csum = jnp.cumsum(ug, axis=1)
    t1 = jnp.arange(1, L + 1, dtype=jnp.float32)
    means = []
    for g, w in enumerate(POOL_WINDOWS):
        s = csum[:, :, g]
        lag = jnp.pad(s, ((0, 0), (w, 0), (0, 0)))[:, :L]
        cnt = jnp.minimum(t1, float(w))[None, :, None]
        means.append((s - lag) / cnt)
    mixed = (jnp.stack(means, axis=2) - ug).astype(u.dtype)
    y = jnp.einsum('blgc,gcd->blgd', mixed, w_group)
    return y.reshape(B, L, POOL_WIDTH) * scale


def causal_block_attention(q, k, v):
    B, L = q.shape[0], q.shape[1]
    pad_front = (-N_META) % Q_BLOCK
    pad_back = (-(L + pad_front)) % Q_BLOCK
    padw = ((0, 0), (pad_front, pad_back), (0, 0), (0, 0))
    q, k, v = jnp.pad(q, padw), jnp.pad(k, padw), jnp.pad(v, padw)
    n_blocks = q.shape[1] // Q_BLOCK
    scale = 1.0 / math.sqrt(QK_HEAD_DIM)
    outs = []
    for i in range(n_blocks):
        q0 = i * Q_BLOCK
        kend = q0 + Q_BLOCK
        s = jnp.einsum('bqhd,bkhd->bhqk', q[:, q0:kend], k[:, :kend]).astype(jnp.float32) * scale
        qi = jnp.arange(q0, kend)[:, None]
        ki = jnp.arange(kend)[None, :]
        valid = (ki <= qi) & (ki >= pad_front)
        s = jnp.where(valid, s, MASK_VALUE)
        p = jax.nn.softmax(s, axis=-1).astype(v.dtype)
        outs.append(jnp.einsum('bhqk,bkhd->bqhd', p, v[:, :kend]))
    o = jnp.concatenate(outs, axis=1)
    return o[:, pad_front:pad_front + L]


def mla(c_q_raw, c_kv_raw, k_rope_raw, pos, g_qa, g_kva, w_q_b, w_kv_b, g_qn, g_kn):
    B, L, _ = c_q_raw.shape
    c_q = rmsnorm(c_q_raw, g_qa)
    c_kv = rmsnorm(c_kv_raw, g_kva)
    q = (c_q @ w_q_b).reshape(B, L, N_HEADS, QK_HEAD_DIM)
    kv = (c_kv @ w_kv_b).reshape(B, L, N_HEADS, QK_NOPE_DIM + V_HEAD_DIM)
    k_nope, v = kv[..., :QK_NOPE_DIM], kv[..., QK_NOPE_DIM:]
    k_pe = jnp.broadcast_to(k_rope_raw[:, :, None, :], (B, L, N_HEADS, QK_ROPE_DIM))
    k = jnp.concatenate([k_nope, k_pe], axis=-1)
    q = rmsnorm(q, g_qn)
    k = rmsnorm(k, g_kn)
    q = jnp.concatenate([q[..., :QK_NOPE_DIM], apply_rope(q[..., QK_NOPE_DIM:], pos)], axis=-1)
    k = jnp.concatenate([k[..., :QK_NOPE_DIM], apply_rope(k[..., QK_NOPE_DIM:], pos)], axis=-1)
    o = causal_block_attention(q, k, v)
    return o.reshape(B, L, MLA_WIDTH)


def _fwd_setup_inputs(seed: int = 0) -> dict:
    key = jax.random.key(seed)
    ks = jax.random.split(key, 18)
    f32 = jnp.float32

    def nrm(k, shape, scale):
        return jax.random.normal(k, shape, f32) * scale

    def gain(k, shape):
        return 1.0 + 0.02 * jax.random.normal(k, shape, f32)

    x = jax.random.normal(ks[0], (BATCH, SEQ, D_MODEL), f32)
    offset = jax.random.randint(ks[1], (BATCH, 1), 0, 4096, dtype=jnp.int32)
    positions = offset + jnp.arange(SEQ, dtype=jnp.int32)[None, :]
    return {
        "x": x,
        "positions": positions,
        "meta_tokens": nrm(ks[2], (N_META, D_MODEL), 1.0),
        "norm_gain": gain(ks[3], (DEPTH, D_MODEL)),
        "w_in": nrm(ks[4], (DEPTH, D_MODEL, D_IN), D_MODEL ** -0.5),
        "pool_w_group": nrm(ks[5], (DEPTH, POOL_GROUPS, POOL_GROUP_DIM, POOL_GROUP_DIM), POOL_GROUP_DIM ** -0.5),
        "pool_scale": gain(ks[6], (DEPTH, POOL_WIDTH)),
        "pool_w_up": nrm(ks[7], (DEPTH, POOL_WIDTH, D_MODEL), POOL_WIDTH ** -0.5),
        "q_a_norm_gain": gain(ks[8], (DEPTH, Q_LORA_RANK)),
        "kv_a_norm_gain": gain(ks[9], (DEPTH, KV_LORA_RANK)),
        "w_q_b": nrm(ks[10], (DEPTH, Q_LORA_RANK, N_HEADS * QK_HEAD_DIM), Q_LORA_RANK ** -0.5),
        "w_kv_b": nrm(ks[11], (DEPTH, KV_LORA_RANK, N_HEADS * (QK_NOPE_DIM + V_HEAD_DIM)), KV_LORA_RANK ** -0.5),
        "q_norm_gain": gain(ks[12], (DEPTH, QK_HEAD_DIM)),
        "k_norm_gain": gain(ks[13], (DEPTH, QK_HEAD_DIM)),
        "mla_w_up": nrm(ks[14], (DEPTH, MLA_WIDTH, D_MODEL), MLA_WIDTH ** -0.5),
        "w_out": nrm(ks[15], (DEPTH, D_MODEL, D_MODEL), (D_MODEL * 2 * DEPTH) ** -0.5),
    }


def _fwd_reference(x, positions, meta_tokens, norm_gain, w_in, pool_w_group, pool_scale, pool_w_up,
              q_a_norm_gain, kv_a_norm_gain, w_q_b, w_kv_b, q_norm_gain, k_norm_gain, mla_w_up, w_out):
    B = x.shape[0]
    meta = jnp.broadcast_to(meta_tokens[None].astype(x.dtype), (B, N_META, D_MODEL))
    h_res = jnp.concatenate([meta, x], axis=1)
    meta_pos = jnp.broadcast_to(jnp.arange(N_META, dtype=jnp.int32)[None], (B, N_META))
    pos = jnp.concatenate([meta_pos, positions + N_META], axis=1)

    for l in range(DEPTH):
        h = rmsnorm(h_res, norm_gain[l])
        proj = h @ w_in[l]
        u_pool, z_pool, c_q, c_kv, k_rope, z_mla, g_pool, g_mla = jnp.split(proj, IN_SPLIT_POINTS, axis=-1)
        y_pool = (pool_mix(u_pool, pool_w_group[l], pool_scale[l]) * jax.nn.silu(z_pool)) @ pool_w_up[l]
        o_mla = mla(c_q, c_kv, k_rope, pos, q_a_norm_gain[l], kv_a_norm_gain[l], w_q_b[l], w_kv_b[l],
                    q_norm_gain[l], k_norm_gain[l])
        y_mla = (o_mla * jax.nn.silu(z_mla)) @ mla_w_up[l]
        merged = jax.nn.sigmoid(g_pool) * y_pool + jax.nn.sigmoid(g_mla) * y_mla
        h_res = h_res + merged @ w_out[l]

    return h_res[:, N_META:]


import jax as _jax
import jax.numpy as _jnp

TWIN_FORMAT = 'train_step'
FWD_PARAMS = ['x', 'positions', 'meta_tokens', 'norm_gain', 'w_in', 'pool_w_group', 'pool_scale', 'pool_w_up', 'q_a_norm_gain', 'kv_a_norm_gain', 'w_q_b', 'w_kv_b', 'q_norm_gain', 'k_norm_gain', 'mla_w_up', 'w_out']
TWIN_WEIGHTS = ['meta_tokens', 'norm_gain', 'w_in', 'pool_w_group', 'pool_scale', 'pool_w_up', 'q_a_norm_gain', 'kv_a_norm_gain', 'w_q_b', 'w_kv_b', 'q_norm_gain', 'k_norm_gain', 'mla_w_up', 'w_out']
TWIN_DIFF_INPUT = 'x'
TWIN_INPUTS = ['x', 'positions', 'meta_tokens', 'norm_gain', 'w_in', 'pool_w_group', 'pool_scale', 'pool_w_up', 'q_a_norm_gain', 'kv_a_norm_gain', 'w_q_b', 'w_kv_b', 'q_norm_gain', 'k_norm_gain', 'mla_w_up', 'w_out', 'loss_target', 'm_meta_tokens', 'm_norm_gain', 'm_w_in', 'm_pool_w_group', 'm_pool_scale', 'm_pool_w_up', 'm_q_a_norm_gain', 'm_kv_a_norm_gain', 'm_w_q_b', 'm_w_kv_b', 'm_q_norm_gain', 'm_k_norm_gain', 'm_mla_w_up', 'm_w_out', 'v_meta_tokens', 'v_norm_gain', 'v_w_in', 'v_pool_w_group', 'v_pool_scale', 'v_pool_w_up', 'v_q_a_norm_gain', 'v_kv_a_norm_gain', 'v_w_q_b', 'v_w_kv_b', 'v_q_norm_gain', 'v_k_norm_gain', 'v_mla_w_up', 'v_w_out']
TWIN_OUTPUTS = ['loss', 'grad_x', 'grad_meta_tokens', 'grad_norm_gain', 'grad_w_in', 'grad_pool_w_group', 'grad_pool_scale', 'grad_pool_w_up', 'grad_q_a_norm_gain', 'grad_kv_a_norm_gain', 'grad_w_q_b', 'grad_w_kv_b', 'grad_q_norm_gain', 'grad_k_norm_gain', 'grad_mla_w_up', 'grad_w_out', 'delta_meta_tokens', 'delta_norm_gain', 'delta_w_in', 'delta_pool_w_group', 'delta_pool_scale', 'delta_pool_w_up', 'delta_q_a_norm_gain', 'delta_kv_a_norm_gain', 'delta_w_q_b', 'delta_w_kv_b', 'delta_q_norm_gain', 'delta_k_norm_gain', 'delta_mla_w_up', 'delta_w_out', 'new_m_meta_tokens', 'new_m_norm_gain', 'new_m_w_in', 'new_m_pool_w_group', 'new_m_pool_scale', 'new_m_pool_w_up', 'new_m_q_a_norm_gain', 'new_m_kv_a_norm_gain', 'new_m_w_q_b', 'new_m_w_kv_b', 'new_m_q_norm_gain', 'new_m_k_norm_gain', 'new_m_mla_w_up', 'new_m_w_out', 'new_v_meta_tokens', 'new_v_norm_gain', 'new_v_w_in', 'new_v_pool_w_group', 'new_v_pool_scale', 'new_v_pool_w_up', 'new_v_q_a_norm_gain', 'new_v_kv_a_norm_gain', 'new_v_w_q_b', 'new_v_w_kv_b', 'new_v_q_norm_gain', 'new_v_k_norm_gain', 'new_v_mla_w_up', 'new_v_w_out']
TWIN_LEAF_KINDS = {'loss': 'loss', 'grad_x': 'grad_x', 'grad_meta_tokens': 'grad_w', 'grad_norm_gain': 'grad_w', 'grad_w_in': 'grad_w', 'grad_pool_w_group': 'grad_w', 'grad_pool_scale': 'grad_w', 'grad_pool_w_up': 'grad_w', 'grad_q_a_norm_gain': 'grad_w', 'grad_kv_a_norm_gain': 'grad_w', 'grad_w_q_b': 'grad_w', 'grad_w_kv_b': 'grad_w', 'grad_q_norm_gain': 'grad_w', 'grad_k_norm_gain': 'grad_w', 'grad_mla_w_up': 'grad_w', 'grad_w_out': 'grad_w', 'delta_meta_tokens': 'delta_w', 'delta_norm_gain': 'delta_w', 'delta_w_in': 'delta_w', 'delta_pool_w_group': 'delta_w', 'delta_pool_scale': 'delta_w', 'delta_pool_w_up': 'delta_w', 'delta_q_a_norm_gain': 'delta_w', 'delta_kv_a_norm_gain': 'delta_w', 'delta_w_q_b': 'delta_w', 'delta_w_kv_b': 'delta_w', 'delta_q_norm_gain': 'delta_w', 'delta_k_norm_gain': 'delta_w', 'delta_mla_w_up': 'delta_w', 'delta_w_out': 'delta_w', 'new_m_meta_tokens': 'new_m', 'new_m_norm_gain': 'new_m', 'new_m_w_in': 'new_m', 'new_m_pool_w_group': 'new_m', 'new_m_pool_scale': 'new_m', 'new_m_pool_w_up': 'new_m', 'new_m_q_a_norm_gain': 'new_m', 'new_m_kv_a_norm_gain': 'new_m', 'new_m_w_q_b': 'new_m', 'new_m_w_kv_b': 'new_m', 'new_m_q_norm_gain': 'new_m', 'new_m_k_norm_gain': 'new_m', 'new_m_mla_w_up': 'new_m', 'new_m_w_out': 'new_m', 'new_v_meta_tokens': 'new_v', 'new_v_norm_gain': 'new_v', 'new_v_w_in': 'new_v', 'new_v_pool_w_group': 'new_v', 'new_v_pool_scale': 'new_v', 'new_v_pool_w_up': 'new_v', 'new_v_q_a_norm_gain': 'new_v', 'new_v_kv_a_norm_gain': 'new_v', 'new_v_w_q_b': 'new_v', 'new_v_w_kv_b': 'new_v', 'new_v_q_norm_gain': 'new_v', 'new_v_k_norm_gain': 'new_v', 'new_v_mla_w_up': 'new_v', 'new_v_w_out': 'new_v'}


def _forward(args):
    return _fwd_reference(*[args[k] for k in FWD_PARAMS])


def _output_shape():
    def fwd():
        inp = _fwd_setup_inputs(0)
        return _fwd_reference(*[inp[k] for k in FWD_PARAMS])
    out = _jax.eval_shape(fwd)
    return out.shape, out.dtype

N_MICROBATCH = 1
ADAM_LR = 0.001
ADAM_B1 = 0.9
ADAM_B2 = 0.999
ADAM_EPS = 1e-08
ADAM_WD = 0.01
ADAM_STEP = 10
PER_EXAMPLE_BATCH_AXIS = {'x': 0, 'positions': 0, 'loss_target': 0}
SHARED_INPUTS = []
_WEIGHT_DTYPES = {'meta_tokens': _jnp.float32, 'norm_gain': _jnp.float32, 'w_in': _jnp.float32, 'pool_w_group': _jnp.float32, 'pool_scale': _jnp.float32, 'pool_w_up': _jnp.float32, 'q_a_norm_gain': _jnp.float32, 'kv_a_norm_gain': _jnp.float32, 'w_q_b': _jnp.float32, 'w_kv_b': _jnp.float32, 'q_norm_gain': _jnp.float32, 'k_norm_gain': _jnp.float32, 'mla_w_up': _jnp.float32, 'w_out': _jnp.float32}
MOMENT_SCALE = {'meta_tokens': 2.999403e-03, 'norm_gain': 1.404485e+00, 'w_in': 3.279390e-02, 'pool_w_group': 1.056070e-01, 'pool_scale': 1.286663e+00, 'pool_w_up': 5.358436e-02, 'q_a_norm_gain': 7.848464e-03, 'kv_a_norm_gain': 2.773629e-02, 'w_q_b': 7.851746e-03, 'w_kv_b': 9.515580e-03, 'q_norm_gain': 4.233869e-02, 'k_norm_gain': 4.239885e-02, 'mla_w_up': 7.508215e-03, 'w_out': 1.456147e-01}


def _to_microbatches(a, axis):
    t = _jnp.moveaxis(a, axis, 0)
    t = t.reshape((N_MICROBATCH, t.shape[0] // N_MICROBATCH) + t.shape[1:])
    return _jnp.moveaxis(t, 1, axis + 1)


def setup_inputs(seed: int = 0) -> dict:
    inp = _fwd_setup_inputs(seed)
    key = _jax.random.fold_in(_jax.random.key(seed), 7919)
    shape, _ = _output_shape()
    out = dict(inp)
    out["loss_target"] = _jax.random.normal(_jax.random.fold_in(key, 0), shape, _jnp.float32)
    for i, name in enumerate(TWIN_WEIGHTS):
        w = inp[name].astype(_jnp.float32)
        if MOMENT_SCALE is None:
            s = _jnp.sqrt(_jnp.mean(_jnp.square(w)) + 1e-30)
        else:
            s = MOMENT_SCALE[name]
        km, kv = _jax.random.split(_jax.random.fold_in(key, i + 1))
        out[name] = w
        out["m_" + name] = s * _jax.random.normal(km, w.shape, _jnp.float32)
        out["v_" + name] = (s * s) * _jax.random.uniform(kv, w.shape, _jnp.float32, 0.5, 1.5)
    if N_MICROBATCH > 1:
        for name, axis in PER_EXAMPLE_BATCH_AXIS.items():
            out[name] = _to_microbatches(out[name], axis)
    return {'x': out['x'], 'positions': out['positions'], 'meta_tokens': out['meta_tokens'], 'norm_gain': out['norm_gain'], 'w_in': out['w_in'], 'pool_w_group': out['pool_w_group'], 'pool_scale': out['pool_scale'], 'pool_w_up': out['pool_w_up'], 'q_a_norm_gain': out['q_a_norm_gain'], 'kv_a_norm_gain': out['kv_a_norm_gain'], 'w_q_b': out['w_q_b'], 'w_kv_b': out['w_kv_b'], 'q_norm_gain': out['q_norm_gain'], 'k_norm_gain': out['k_norm_gain'], 'mla_w_up': out['mla_w_up'], 'w_out': out['w_out'], 'loss_target': out['loss_target'], 'm_meta_tokens': out['m_meta_tokens'], 'm_norm_gain': out['m_norm_gain'], 'm_w_in': out['m_w_in'], 'm_pool_w_group': out['m_pool_w_group'], 'm_pool_scale': out['m_pool_scale'], 'm_pool_w_up': out['m_pool_w_up'], 'm_q_a_norm_gain': out['m_q_a_norm_gain'], 'm_kv_a_norm_gain': out['m_kv_a_norm_gain'], 'm_w_q_b': out['m_w_q_b'], 'm_w_kv_b': out['m_w_kv_b'], 'm_q_norm_gain': out['m_q_norm_gain'], 'm_k_norm_gain': out['m_k_norm_gain'], 'm_mla_w_up': out['m_mla_w_up'], 'm_w_out': out['m_w_out'], 'v_meta_tokens': out['v_meta_tokens'], 'v_norm_gain': out['v_norm_gain'], 'v_w_in': out['v_w_in'], 'v_pool_w_group': out['v_pool_w_group'], 'v_pool_scale': out['v_pool_scale'], 'v_pool_w_up': out['v_pool_w_up'], 'v_q_a_norm_gain': out['v_q_a_norm_gain'], 'v_kv_a_norm_gain': out['v_kv_a_norm_gain'], 'v_w_q_b': out['v_w_q_b'], 'v_w_kv_b': out['v_w_kv_b'], 'v_q_norm_gain': out['v_q_norm_gain'], 'v_k_norm_gain': out['v_k_norm_gain'], 'v_mla_w_up': out['v_mla_w_up'], 'v_w_out': out['v_w_out']}


def _loss(weights, diff, rest, loss_target):
    with _jax.named_scope("forward"):
        args = {**rest, TWIN_DIFF_INPUT: diff, **{k: w.astype(_WEIGHT_DTYPES[k]) for k, w in weights.items()}}
        y = _forward(args)
    with _jax.named_scope("loss_head"):
        err = _jnp.square(y.astype(_jnp.float32) - loss_target)
        return 0.5 * _jnp.sum(_jnp.mean(err, axis=-1)) if err.ndim else 0.5 * err


def _adamw(w, g, m, v):
    m = ADAM_B1 * m + (1.0 - ADAM_B1) * g
    v = ADAM_B2 * v + (1.0 - ADAM_B2) * _jnp.square(g)
    m_hat = m / (1.0 - ADAM_B1 ** ADAM_STEP)
    v_hat = v / (1.0 - ADAM_B2 ** ADAM_STEP)
    delta = -ADAM_LR * (m_hat / (_jnp.sqrt(v_hat) + ADAM_EPS) + ADAM_WD * w)
    return delta, m, v


def reference(x, positions, meta_tokens, norm_gain, w_in, pool_w_group, pool_scale, pool_w_up, q_a_norm_gain, kv_a_norm_gain, w_q_b, w_kv_b, q_norm_gain, k_norm_gain, mla_w_up, w_out, loss_target, m_meta_tokens, m_norm_gain, m_w_in, m_pool_w_group, m_pool_scale, m_pool_w_up, m_q_a_norm_gain, m_kv_a_norm_gain, m_w_q_b, m_w_kv_b, m_q_norm_gain, m_k_norm_gain, m_mla_w_up, m_w_out, v_meta_tokens, v_norm_gain, v_w_in, v_pool_w_group, v_pool_scale, v_pool_w_up, v_q_a_norm_gain, v_kv_a_norm_gain, v_w_q_b, v_w_kv_b, v_q_norm_gain, v_k_norm_gain, v_mla_w_up, v_w_out):
    given = dict(x=x, positions=positions, meta_tokens=meta_tokens, norm_gain=norm_gain, w_in=w_in, pool_w_group=pool_w_group, pool_scale=pool_scale, pool_w_up=pool_w_up, q_a_norm_gain=q_a_norm_gain, kv_a_norm_gain=kv_a_norm_gain, w_q_b=w_q_b, w_kv_b=w_kv_b, q_norm_gain=q_norm_gain, k_norm_gain=k_norm_gain, mla_w_up=mla_w_up, w_out=w_out, loss_target=loss_target, m_meta_tokens=m_meta_tokens, m_norm_gain=m_norm_gain, m_w_in=m_w_in, m_pool_w_group=m_pool_w_group, m_pool_scale=m_pool_scale, m_pool_w_up=m_pool_w_up, m_q_a_norm_gain=m_q_a_norm_gain, m_kv_a_norm_gain=m_kv_a_norm_gain, m_w_q_b=m_w_q_b, m_w_kv_b=m_w_kv_b, m_q_norm_gain=m_q_norm_gain, m_k_norm_gain=m_k_norm_gain, m_mla_w_up=m_mla_w_up, m_w_out=m_w_out, v_meta_tokens=v_meta_tokens, v_norm_gain=v_norm_gain, v_w_in=v_w_in, v_pool_w_group=v_pool_w_group, v_pool_scale=v_pool_scale, v_pool_w_up=v_pool_w_up, v_q_a_norm_gain=v_q_a_norm_gain, v_kv_a_norm_gain=v_kv_a_norm_gain, v_w_q_b=v_w_q_b, v_w_kv_b=v_w_kv_b, v_q_norm_gain=v_q_norm_gain, v_k_norm_gain=v_k_norm_gain, v_mla_w_up=v_mla_w_up, v_w_out=v_w_out)
    weights = {n: given[n] for n in TWIN_WEIGHTS}
    shared = {n: given[n] for n in SHARED_INPUTS}
    per_example = {n: given[n] for n in ['x', 'positions']}
    grad_fn = _jax.value_and_grad(_loss, argnums=(0, 1))

    def one_microbatch(ex, loss_target):
        ex = dict(ex)
        diff = ex.pop(TWIN_DIFF_INPUT)
        return grad_fn(weights, diff, {**shared, **ex}, loss_target)

    if N_MICROBATCH == 1:
        loss, (grad_w, grad_x) = one_microbatch(per_example, given["loss_target"])
    else:
        def body(carry, xs):
            loss_sum, grad_sum = carry
            l_k, (gw_k, gx_k) = one_microbatch(xs[0], xs[1])
            with _jax.named_scope("update"):
                return (loss_sum + l_k, _jax.tree.map(_jnp.add, grad_sum, gw_k)), gx_k

        init = (_jnp.zeros((), _jnp.float32), _jax.tree.map(_jnp.zeros_like, weights))
        (loss, grad_w), grad_x = _jax.lax.scan(body, init, (per_example, given["loss_target"]))
    with _jax.named_scope("update"):
        delta_w, new_m, new_v = {}, {}, {}
        for n in TWIN_WEIGHTS:
            delta_w[n], new_m[n], new_v[n] = _adamw(weights[n], grad_w[n], given["m_" + n], given["v_" + n])
    return (loss, grad_x, *[grad_w[n] for n in TWIN_WEIGHTS], *[delta_w[n] for n in TWIN_WEIGHTS],
            *[new_m[n] for n in TWIN_WEIGHTS], *[new_v[n] for n in TWIN_WEIGHTS])
```

```python
import functools
import math

import numpy as np
import jax
import jax.numpy as jnp
from jax import lax
from jax.experimental import pallas as pl
from jax.experimental.pallas import tpu as pltpu

F32 = jnp.float32
BF16 = jnp.bfloat16
MESH = pl.DeviceIdType.MESH

D_MODEL = 1024
DEPTH = 4
N_META = 16
POOL_WIDTH = 512
POOL_WINDOWS = (2, 4, 8, 16)
POOL_GROUPS = 4
GROUP_DIM = 128
N_HEADS = 8
NOPE = 64
ROPE = 32
QK_DIM = 96
V_DIM = 64
MLA_WIDTH = 512
KV_RANK = 256
Q_RANK = 768
ROPE_THETA = 10000.0
EPS = 1e-6
MASK_VALUE = -1e30
ATTN_BLOCK = 128
PAD_FRONT = (-N_META) % ATTN_BLOCK
ROW0 = PAD_FRONT + N_META
HEAD_PAD = 128
HALO = 16
N_CHIPS = 4
N_DEV = 8

IN_NAMES = ("u", "zp", "cq", "ckv", "zm", "gp", "gm", "kr")
IN_WIDTHS = (512, 512, 768, 256, 512, 1024, 1024, 128)
IN_OFFS = tuple(int(v) for v in np.cumsum((0,) + IN_WIDTHS[:-1]))
IN_PAD = sum(IN_WIDTHS)
KR_LANE0 = NOPE

ADAM_LR = 0.001
ADAM_B1 = 0.9
ADAM_B2 = 0.999
ADAM_EPS = 1e-08
ADAM_WD = 0.01
ADAM_STEP = 10

VMEM_LIMIT = 56 * 1024 * 1024


def _cp(sem=None, vmem=VMEM_LIMIT):
    kw = dict(vmem_limit_bytes=vmem)
    if sem is not None:
        kw["dimension_semantics"] = sem
    return pltpu.CompilerParams(**kw)


def _row_tile(n_rows):
    for t in (640, 512, 256, 128):
        if n_rows % t == 0:
            return t
    raise ValueError(f"no row tile for {n_rows}")


def _nt(a, b):
    return lax.dot_general(a, b, (((1,), (1,)), ((), ())), preferred_element_type=F32)


def _tn(a, b):
    return lax.dot_general(a, b, (((0,), (0,)), ((), ())), preferred_element_type=F32)


def _mm(a, b):
    return jnp.dot(a, b, preferred_element_type=F32)


def _sigmoid(x):
    return 1.0 / (1.0 + jnp.exp(-x))


def _resident(shape):
    nd = len(shape)
    return pl.BlockSpec(shape, lambda *_: (0,) * nd, pipeline_mode=pl.Buffered(1))


def _rows(tm, width):
    return pl.BlockSpec((tm, width), lambda i: (i, 0))


def _rope_tables(pos_col, inv_freq_row):
    lp = pos_col.shape[0]
    tm = _row_tile(lp)

    def body(p_ref, f_ref, c_ref, s1_ref, s2_ref):
        ang = p_ref[...].astype(F32) * f_ref[...]
        lane = lax.broadcasted_iota(jnp.int32, ang.shape, 1)
        cs = jnp.cos(ang)
        sn = jnp.sin(ang)
        c_ref[...] = jnp.where(lane < NOPE, 1.0, jnp.where(lane < QK_DIM, cs, 0.0))
        s1_ref[...] = jnp.where((lane >= NOPE) & (lane < NOPE + ROPE // 2), -sn, 0.0)
        s2_ref[...] = jnp.where((lane >= NOPE + ROPE // 2) & (lane < QK_DIM), sn, 0.0)

    out = jax.ShapeDtypeStruct((lp, HEAD_PAD), F32)
    return pl.pallas_call(
        body, name="rope_tables", grid=(lp // tm,),
        in_specs=[pl.BlockSpec((tm, 1), lambda i: (i, 0)), pl.BlockSpec((1, HEAD_PAD), lambda i: (0, 0))],
        out_specs=[_rows(tm, HEAD_PAD)] * 3, out_shape=[out] * 3,
        compiler_params=_cp(("parallel",)),
    )(pos_col, inv_freq_row)


def _rope(y, c, s1, s2):
    return y * c + pltpu.roll(y, HEAD_PAD - ROPE // 2, 1) * s1 + pltpu.roll(y, ROPE // 2, 1) * s2


def _rope_t(g, c, s1, s2):
    return g * c + pltpu.roll(g * s1, ROPE // 2, 1) + pltpu.roll(g * s2, HEAD_PAD - ROPE // 2, 1)


def _inproj_fwd(x, gain, w_pad):
    lp = x.shape[0]
    tm = _row_tile(lp)

    def body(x_ref, g_ref, w_ref, *outs):
        xf = x_ref[...]
        inv = lax.rsqrt(jnp.mean(xf * xf, axis=-1, keepdims=True) + EPS)
        h = (xf * inv * g_ref[...]).astype(BF16)
        for o_ref, off, wd in zip(outs, IN_OFFS, IN_WIDTHS):
            o_ref[...] = _mm(h, w_ref[:, off:off + wd]).astype(o_ref.dtype)

    return pl.pallas_call(
        body, name="inproj_fwd", grid=(lp // tm,),
        in_specs=[_rows(tm, D_MODEL), pl.BlockSpec((1, D_MODEL), lambda i: (0, 0)), _resident((D_MODEL, IN_PAD))],
        out_specs=[_rows(tm, wd) for wd in IN_WIDTHS],
        out_shape=[jax.ShapeDtypeStruct((lp, wd), BF16) for wd in IN_WIDTHS],
        compiler_params=_cp(("parallel",)),
    )(x, gain, w_pad)


def _inv_counts(tile_idx, tm):
    row = tile_idx * tm + lax.broadcasted_iota(jnp.int32, (tm, 1), 0)
    t1 = jnp.maximum(row - PAD_FRONT + 1, 1).astype(F32)
    return [1.0 / jnp.minimum(t1, float(w)) for w in POOL_WINDOWS]


def _trailing_sums(e):
    s2 = e + pltpu.roll(e, 1, 0)
    s4 = s2 + pltpu.roll(s2, 2, 0)
    s8 = s4 + pltpu.roll(s4, 4, 0)
    s16 = s8 + pltpu.roll(s8, 8, 0)
    return (s2, s4, s8, s16)


def _leading_sums(e):
    n = e.shape[0]
    s2 = e + pltpu.roll(e, n - 1, 0)
    s4 = s2 + pltpu.roll(s2, n - 2, 0)
    s8 = s4 + pltpu.roll(s4, n - 4, 0)
    s16 = s8 + pltpu.roll(s8, n - 8, 0)
    return (s2, s4, s8, s16)


def _pool_fwd(u, zp, wg, scale):
    lp = u.shape[0]
    tm = _row_tile(lp)

    def body(u_ref, z_ref, wg_ref, sc_ref, a_ref, ext_ref):
        i = pl.program_id(0)

        @pl.when(i == 0)
        def _():
            ext_ref[0:HALO, :] = jnp.zeros((HALO, POOL_WIDTH), F32)

        ext_ref[HALO:HALO + tm, :] = u_ref[...].astype(F32)
        e = ext_ref[...]
        sums = _trailing_sums(e)
        ext_ref[0:HALO, :] = e[tm:tm + HALO, :]
        inv_cnt = _inv_counts(i, tm)
        for g in range(POOL_GROUPS):
            cols = slice(g * GROUP_DIM, (g + 1) * GROUP_DIM)
            mixed = sums[g][HALO:, cols] * inv_cnt[g] - e[HALO:, cols]
            y = _mm(mixed.astype(BF16), wg_ref[g]) * sc_ref[:, cols]
            zf = z_ref[:, cols].astype(F32)
            a_ref[:, cols] = (y * (zf * _sigmoid(zf))).astype(a_ref.dtype)

    return pl.pallas_call(
        body, name="pool_fwd", grid=(lp // tm,),
        in_specs=[_rows(tm, POOL_WIDTH), _rows(tm, POOL_WIDTH),
                  pl.BlockSpec((POOL_GROUPS, GROUP_DIM, GROUP_DIM), lambda i: (0, 0, 0)),
                  pl.BlockSpec((1, POOL_WIDTH), lambda i: (0, 0))],
        out_specs=_rows(tm, POOL_WIDTH), out_shape=jax.ShapeDtypeStruct((lp, POOL_WIDTH), BF16),
        scratch_shapes=[pltpu.VMEM((HALO + tm, POOL_WIDTH), F32)],
        compiler_params=_cp(("arbitrary",)),
    )(u, zp, wg, scale)


def _rms_fwd(xf, gain):
    inv = lax.rsqrt(jnp.mean(xf * xf, axis=-1, keepdims=True) + EPS)
    xhat = xf * inv
    return inv, xhat, xhat * gain


def _rms_bwd(dy, inv, xhat, gain):
    dgain = jnp.sum(dy * xhat, axis=0, keepdims=True)
    dyg = dy * gain
    dx = inv * (dyg - xhat * jnp.mean(dyg * xhat, axis=-1, keepdims=True))
    return dx, dgain


def _head_norm_fwd(xh, gain128):
    inv = lax.rsqrt(jnp.sum(xh * xh, axis=-1, keepdims=True) * (1.0 / QK_DIM) + EPS)
    xhat = xh * inv
    return inv, xhat, xhat * gain128


def _head_norm_bwd(dy, inv, xhat, gain128):
    dyg = dy * gain128
    return inv * (dyg - xhat * (jnp.sum(dyg * xhat, axis=-1, keepdims=True) * (1.0 / QK_DIM)))


def _mla_prep_fwd(cq, ckv, kr, tabs, gqa, gkva, gqn, gkn, wq, wkn, wv):
    lp = cq.shape[0]
    tm = _row_tile(lp)

    def body(cq_ref, ckv_ref, kr_ref, c_ref, s1_ref, s2_ref, gqa_ref, gkva_ref, gqn_ref, gkn_ref,
             wq_ref, wkn_ref, wv_ref, q_ref, k_ref, v_ref):
        c, s1, s2 = c_ref[...], s1_ref[...], s2_ref[...]
        _, _, cqn = _rms_fwd(cq_ref[...].astype(F32), gqa_ref[...])
        qraw = _mm(cqn.astype(BF16), wq_ref[...])
        for h in range(N_HEADS):
            hb = slice(h * HEAD_PAD, (h + 1) * HEAD_PAD)
            _, _, yh = _head_norm_fwd(qraw[:, hb], gqn_ref[...])
            q_ref[:, hb] = _rope(yh, c, s1, s2).astype(q_ref.dtype)
        _, _, ckvn = _rms_fwd(ckv_ref[...].astype(F32), gkva_ref[...])
        ckvn_b = ckvn.astype(BF16)
        knraw = _mm(ckvn_b, wkn_ref[...])
        krs = kr_ref[...].astype(F32)
        for h in range(N_HEADS):
            hb = slice(h * HEAD_PAD, (h + 1) * HEAD_PAD)
            _, _, yh = _head_norm_fwd(knraw[:, hb] + krs, gkn_ref[...])
            k_ref[:, hb] = _rope(yh, c, s1, s2).astype(k_ref.dtype)
        v_ref[...] = _mm(ckvn_b, wv_ref[...]).astype(v_ref.dtype)

    hw = N_HEADS * HEAD_PAD
    vec = lambda n: pl.BlockSpec((1, n), lambda i: (0, 0))
    return pl.pallas_call(
        body, name="mla_prep_fwd", grid=(lp // tm,),
        in_specs=[_rows(tm, Q_RANK), _rows(tm, KV_RANK), _rows(tm, HEAD_PAD)] + [_rows(tm, HEAD_PAD)] * 3
        + [vec(Q_RANK), vec(KV_RANK), vec(HEAD_PAD), vec(HEAD_PAD),
           _resident((Q_RANK, hw)), _resident((KV_RANK, hw)), _resident((KV_RANK, MLA_WIDTH))],
        out_specs=[_rows(tm, hw), _rows(tm, hw), _rows(tm, MLA_WIDTH)],
        out_shape=[jax.ShapeDtypeStruct((lp, hw), BF16), jax.ShapeDtypeStruct((lp, hw), BF16),
                   jax.ShapeDtypeStruct((lp, MLA_WIDTH), BF16)],
        compiler_params=_cp(("parallel",)),
    )(cq, ckv, kr, *tabs, gqa, gkva, gqn, gkn, wq, wkn, wv)


def _attn_tiles(lp):
    t = _row_tile(lp)
    return t, t


def _causal_mask(s, q0, k0):
    qi = q0 + lax.broadcasted_iota(jnp.int32, s.shape, 0)
    ki = k0 + lax.broadcasted_iota(jnp.int32, s.shape, 1)
    return jnp.where((ki <= qi) & (ki >= PAD_FRONT), s, MASK_VALUE)


def _flash_fwd(q, k, v):
    lp = q.shape[0]
    tq, tk = _attn_tiles(lp)
    nq, nk = lp // tq, lp // tk
    scale = 1.0 / math.sqrt(QK_DIM)

    def last_kv(i):
        return ((i + 1) * tq - 1) // tk

    def body(q_ref, k_ref, v_ref, o_ref, lse_ref, m_scr, l_scr, acc_scr):
        i, t = pl.program_id(1), pl.program_id(2)

        @pl.when(t == 0)
        def _():
            m_scr[...] = jnp.full(m_scr.shape, MASK_VALUE, F32)
            l_scr[...] = jnp.zeros(l_scr.shape, F32)
            acc_scr[...] = jnp.zeros(acc_scr.shape, F32)

        def step(masked):
            for hh in range(2):
                hb = slice(hh * HEAD_PAD, (hh + 1) * HEAD_PAD)
                s = _nt(q_ref[:, hb], k_ref[:, hb]) * scale
                if masked:
                    s = _causal_mask(s, i * tq, t * tk)
                m_prev = m_scr[hh]
                m_new = jnp.maximum(m_prev, jnp.max(s, axis=-1, keepdims=True))
                alpha = jnp.exp(m_prev - m_new)
                p = jnp.exp(s - m_new)
                l_scr[hh] = alpha * l_scr[hh] + jnp.sum(p, axis=-1, keepdims=True)
                m_scr[hh] = m_new
                acc_scr[hh] = alpha * acc_scr[hh] + _mm(p.astype(BF16), v_ref[...])

        on_diag = ((t + 1) * tk - 1 > i * tq) | (t == 0)

        @pl.when((t <= last_kv(i)) & on_diag)
        def _():
            step(True)

        @pl.when((t <= last_kv(i)) & jnp.logical_not(on_diag))
        def _():
            step(False)

        @pl.when(t == last_kv(i))
        def _():
            lane = lax.broadcasted_iota(jnp.int32, (tq, HEAD_PAD), 1)
            o = jnp.where(lane < V_DIM, acc_scr[0] / l_scr[0], acc_scr[1] / l_scr[1])
            o_ref[...] = o.astype(o_ref.dtype)
            lse_ref[...] = jnp.where(lane < V_DIM, m_scr[0] + jnp.log(l_scr[0]), m_scr[1] + jnp.log(l_scr[1]))

    kv_idx = lambda j, i, t: (jnp.minimum(t, last_kv(i)), j)
    return pl.pallas_call(
        body, name="flash_fwd", grid=(N_HEADS // 2, nq, nk),
        in_specs=[pl.BlockSpec((tq, 2 * HEAD_PAD), lambda j, i, t: (i, j)),
                  pl.BlockSpec((tk, 2 * HEAD_PAD), kv_idx), pl.BlockSpec((tk, HEAD_PAD), kv_idx)],
        out_specs=[pl.BlockSpec((tq, HEAD_PAD), lambda j, i, t: (i, j))] * 2,
        out_shape=[jax.ShapeDtypeStruct((lp, MLA_WIDTH), BF16), jax.ShapeDtypeStruct((lp, MLA_WIDTH), F32)],
        scratch_shapes=[pltpu.VMEM((2, tq, 1), F32), pltpu.VMEM((2, tq, 1), F32), pltpu.VMEM((2, tq, HEAD_PAD), F32)],
        compiler_params=_cp(("parallel", "parallel", "arbitrary")),
    )(q, k, v)


def _merge_fwd(x, a_pool, o, zm, gp, gm, wpu, wmu, wout):
    lp = x.shape[0]
    tm = _row_tile(lp)

    def body(x_ref, ap_ref, o_ref, zm_ref, gp_ref, gm_ref, wpu_ref, wmu_ref, wout_ref, xn_ref, yp_ref, ym_ref):
        yp = _mm(ap_ref[...], wpu_ref[...])
        zf = zm_ref[...].astype(F32)
        amla = o_ref[...].astype(F32) * (zf * _sigmoid(zf))
        ym = _mm(amla.astype(BF16), wmu_ref[...])
        merged = _sigmoid(gp_ref[...].astype(F32)) * yp + _sigmoid(gm_ref[...].astype(F32)) * ym
        xn_ref[...] = x_ref[...] + _mm(merged.astype(BF16), wout_ref[...])
        yp_ref[...] = yp.astype(yp_ref.dtype)
        ym_ref[...] = ym.astype(ym_ref.dtype)

    return pl.pallas_call(
        body, name="merge_fwd", grid=(lp // tm,),
        in_specs=[_rows(tm, D_MODEL), _rows(tm, POOL_WIDTH), _rows(tm, MLA_WIDTH), _rows(tm, MLA_WIDTH),
                  _rows(tm, D_MODEL), _rows(tm, D_MODEL),
                  _resident((POOL_WIDTH, D_MODEL)), _resident((MLA_WIDTH, D_MODEL)), _resident((D_MODEL, D_MODEL))],
        out_specs=[_rows(tm, D_MODEL)] * 3,
        out_shape=[jax.ShapeDtypeStruct((lp, D_MODEL), F32), jax.ShapeDtypeStruct((lp, D_MODEL), BF16),
                   jax.ShapeDtypeStruct((lp, D_MODEL), BF16)],
        compiler_params=_cp(("parallel",)),
    )(x, a_pool, o, zm, gp, gm, wpu, wmu, wout)


def _loss_head(y, target_pad):
    lp = y.shape[0]
    tm = _row_tile(lp)

    def body(y_ref, t_ref, d_ref, l_ref):
        i = pl.program_id(0)

        @pl.when(i == 0)
        def _():
            l_ref[...] = jnp.zeros(l_ref.shape, F32)

        row = i * tm + lax.broadcasted_iota(jnp.int32, (tm, 1), 0)
        err = jnp.where(row >= ROW0, y_ref[...] - t_ref[...], 0.0)
        d_ref[...] = err * (1.0 / D_MODEL)
        l_ref[...] += jnp.sum(err * err) * (0.5 / D_MODEL)

    return pl.pallas_call(
        body, name="loss_head", grid=(lp // tm,),
        in_specs=[_rows(tm, D_MODEL), _rows(tm, D_MODEL)],
        out_specs=[_rows(tm, D_MODEL), pl.BlockSpec((8, 128), lambda i: (0, 0))],
        out_shape=[jax.ShapeDtypeStruct((lp, D_MODEL), F32), jax.ShapeDtypeStruct((8, 128), F32)],
        compiler_params=_cp(("arbitrary",)),
    )(y, target_pad)


def _pair_rowsum(prod):
    lane = lax.broadcasted_iota(jnp.int32, prod.shape, 1)
    lo = jnp.sum(jnp.where(lane < V_DIM, prod, 0.0), axis=-1, keepdims=True)
    hi = jnp.sum(jnp.where(lane < V_DIM, 0.0, prod), axis=-1, keepdims=True)
    return jnp.where(lane < V_DIM, lo, hi)


def _merge_bwd(dres, yp, ym, gp, gm, o, zm, wout, wpu, wmu):
    lp = dres.shape[0]
    tm = _row_tile(lp)

    def body(dres_ref, yp_ref, ym_ref, gp_ref, gm_ref, o_ref, zm_ref, wout_ref, wpu_ref, wmu_ref,
             merged_ref, dyp_ref, dym_ref, dgp_ref, dgm_ref, dap_ref, amla_ref, do_ref, dzm_ref, delta_ref):
        dmerged = _nt(dres_ref[...].astype(BF16), wout_ref[...])
        sp = _sigmoid(gp_ref[...].astype(F32))
        sm = _sigmoid(gm_ref[...].astype(F32))
        ypf = yp_ref[...].astype(F32)
        ymf = ym_ref[...].astype(F32)
        merged_ref[...] = (sp * ypf + sm * ymf).astype(merged_ref.dtype)
        dyp = (dmerged * sp).astype(BF16)
        dym = (dmerged * sm).astype(BF16)
        dyp_ref[...] = dyp
        dym_ref[...] = dym
        dgp_ref[...] = (dmerged * ypf * sp * (1.0 - sp)).astype(dgp_ref.dtype)
        dgm_ref[...] = (dmerged * ymf * sm * (1.0 - sm)).astype(dgm_ref.dtype)
        dap_ref[...] = _nt(dyp, wpu_ref[...]).astype(dap_ref.dtype)
        dam = _nt(dym, wmu_ref[...])
        zf = zm_ref[...].astype(F32)
        sg = _sigmoid(zf)
        si = zf * sg
        of = o_ref[...].astype(F32)
        amla_ref[...] = (of * si).astype(amla_ref.dtype)
        do = dam * si
        do_ref[...] = do.astype(do_ref.dtype)
        dzm_ref[...] = (dam * of * (sg * (1.0 + zf * (1.0 - sg)))).astype(dzm_ref.dtype)
        prod = do * of
        for j in range(N_HEADS // 2):
            hb = slice(j * HEAD_PAD, (j + 1) * HEAD_PAD)
            delta_ref[:, hb] = _pair_rowsum(prod[:, hb])

    bf = lambda w: jax.ShapeDtypeStruct((lp, w), BF16)
    return pl.pallas_call(
        body, name="merge_bwd", grid=(lp // tm,),
        in_specs=[_rows(tm, D_MODEL)] * 5 + [_rows(tm, MLA_WIDTH)] * 2
        + [_resident((D_MODEL, D_MODEL)), _resident((POOL_WIDTH, D_MODEL)), _resident((MLA_WIDTH, D_MODEL))],
        out_specs=[_rows(tm, D_MODEL)] * 5 + [_rows(tm, POOL_WIDTH)] + [_rows(tm, MLA_WIDTH)] * 4,
        out_shape=[bf(D_MODEL)] * 5 + [bf(POOL_WIDTH)] + [bf(MLA_WIDTH)] * 3 + [jax.ShapeDtypeStruct((lp, MLA_WIDTH), F32)],
        compiler_params=_cp(("parallel",)),
    )(dres, yp, ym, gp, gm, o, zm, wout, wpu, wmu)


def _flash_bwd(q, k, v, do, lse, delta):
    lp = q.shape[0]
    tq, tk = _attn_tiles(lp)
    nq, nk = lp // tq, lp // tk
    scale = 1.0 / math.sqrt(QK_DIM)

    def first_q(t):
        return (t * tk) // tq

    def body(q_ref, k_ref, v_ref, do_ref, lse_ref, dl_ref, dq_ref, dk_ref, dv_ref, dq_acc, dk_acc, dv_acc):
        t, i = pl.program_id(1), pl.program_id(2)

        @pl.when((t == 0) & (i == 0))
        def _():
            dq_acc[...] = jnp.zeros(dq_acc.shape, F32)

        @pl.when(i == 0)
        def _():
            dk_acc[...] = jnp.zeros(dk_acc.shape, F32)
            dv_acc[...] = jnp.zeros(dv_acc.shape, F32)

        def step(masked):
            lane = lax.broadcasted_iota(jnp.int32, (tq, HEAD_PAD), 1)
            rows = pl.ds(pl.multiple_of(i * tq, tq), tq)
            for hh in range(2):
                hb = slice(hh * HEAD_PAD, (hh + 1) * HEAD_PAD)
                qh, kh = q_ref[:, hb], k_ref[:, hb]
                s = _nt(qh, kh) * scale
                if masked:
                    s = _causal_mask(s, i * tq, t * tk)
                p = jnp.exp(s - lse_ref[:, hh * V_DIM:hh * V_DIM + 1])
                mine = (lane < V_DIM) if hh == 0 else (lane >= V_DIM)
                doh = jnp.where(mine, do_ref[...], jnp.zeros_like(do_ref[...]))
                dv_acc[...] += _tn(p.astype(BF16), doh)
                dp = _nt(doh, v_ref[...])
                ds = (p * (dp - dl_ref[:, hh * V_DIM:hh * V_DIM + 1]) * scale).astype(BF16)
                dk_acc[hh] += _tn(ds, qh)
                dq_acc[rows, hb] += _mm(ds, kh)

        active = i >= first_q(t)
        on_diag = ((t + 1) * tk - 1 > i * tq) | (t == 0)

        @pl.when(active & on_diag)
        def _():
            step(True)

        @pl.when(active & jnp.logical_not(on_diag))
        def _():
            step(False)

        @pl.when(i == nq - 1)
        def _():
            dk_ref[:, 0:HEAD_PAD] = dk_acc[0].astype(dk_ref.dtype)
            dk_ref[:, HEAD_PAD:2 * HEAD_PAD] = dk_acc[1].astype(dk_ref.dtype)
            dv_ref[...] = dv_acc[...].astype(dv_ref.dtype)

        @pl.when((t == nk - 1) & (i == nq - 1))
        def _():
            dq_ref[...] = dq_acc[...].astype(dq_ref.dtype)

    q_idx = lambda j, t, i: (jnp.maximum(i, first_q(t)), j)
    kv_idx = lambda j, t, i: (t, j)
    hw = N_HEADS * HEAD_PAD
    return pl.pallas_call(
        body, name="flash_bwd", grid=(N_HEADS // 2, nk, nq),
        in_specs=[pl.BlockSpec((tq, 2 * HEAD_PAD), q_idx), pl.BlockSpec((tk, 2 * HEAD_PAD), kv_idx),
                  pl.BlockSpec((tk, HEAD_PAD), kv_idx), pl.BlockSpec((tq, HEAD_PAD), q_idx),
                  pl.BlockSpec((tq, HEAD_PAD), q_idx), pl.BlockSpec((tq, HEAD_PAD), q_idx)],
        out_specs=[pl.BlockSpec((lp, 2 * HEAD_PAD), lambda j, t, i: (0, j)),
                   pl.BlockSpec((tk, 2 * HEAD_PAD), kv_idx), pl.BlockSpec((tk, HEAD_PAD), kv_idx)],
        out_shape=[jax.ShapeDtypeStruct((lp, hw), BF16), jax.ShapeDtypeStruct((lp, hw), BF16),
                   jax.ShapeDtypeStruct((lp, MLA_WIDTH), BF16)],
        scratch_shapes=[pltpu.VMEM((lp, 2 * HEAD_PAD), F32), pltpu.VMEM((2, tk, HEAD_PAD), F32),
                        pltpu.VMEM((tk, HEAD_PAD), F32)],
        compiler_params=_cp(("parallel", "arbitrary", "arbitrary")),
    )(q, k, v, do, lse, delta)


def _mla_prep_bwd(dq, dk, dv, cq, ckv, kr, tabs, gqa, gkva, gqn, gkn, wq, wkn, wv):
    lp = cq.shape[0]
    tm = _row_tile(lp)
    hw = N_HEADS * HEAD_PAD

    def body(dq_ref, dk_ref, dv_ref, cq_ref, ckv_ref, kr_ref, c_ref, s1_ref, s2_ref, gqa_ref, gkva_ref, gqn_ref,
             gkn_ref, wq_ref, wkn_ref, wv_ref, dcq_ref, dckv_ref, dkr_ref, dwq_ref, dwkn_ref, dwv_ref,
             dgqa_ref, dgkva_ref, dgqn_ref, dgkn_ref, draw_scr):
        @pl.when(pl.program_id(0) == 0)
        def _():
            for r in (dwq_ref, dwkn_ref, dwv_ref, dgqa_ref, dgkva_ref, dgqn_ref, dgkn_ref):
                r[...] = jnp.zeros(r.shape, F32)

        c, s1, s2 = c_ref[...], s1_ref[...], s2_ref[...]
        lane = lax.broadcasted_iota(jnp.int32, (tm, HEAD_PAD), 1)

        inv_q, xhat_q, cqn = _rms_fwd(cq_ref[...].astype(F32), gqa_ref[...])
        cqn_b = cqn.astype(BF16)
        qraw = _mm(cqn_b, wq_ref[...])
        dgqn = jnp.zeros((1, HEAD_PAD), F32)
        for h in range(N_HEADS):
            hb = slice(h * HEAD_PAD, (h + 1) * HEAD_PAD)
            inv, xhat, _ = _head_norm_fwd(qraw[:, hb], gqn_ref[...])
            dy = _rope_t(dq_ref[:, hb].astype(F32), c, s1, s2)
            dgqn += jnp.sum(dy * xhat, axis=0, keepdims=True)
            draw_scr[:, hb] = _head_norm_bwd(dy, inv, xhat, gqn_ref[...]).astype(BF16)
        dgqn_ref[...] += dgqn
        dqraw = draw_scr[...]
        dwq_ref[...] += _tn(cqn_b, dqraw)
        dcq, dgqa = _rms_bwd(_nt(dqraw, wq_ref[...]), inv_q, xhat_q, gqa_ref[...])
        dcq_ref[...] = dcq.astype(dcq_ref.dtype)
        dgqa_ref[...] += dgqa

        inv_kv, xhat_kv, ckvn = _rms_fwd(ckv_ref[...].astype(F32), gkva_ref[...])
        ckvn_b = ckvn.astype(BF16)
        knraw = _mm(ckvn_b, wkn_ref[...])
        krs = kr_ref[...].astype(F32)
        dgkn = jnp.zeros((1, HEAD_PAD), F32)
        dkr = jnp.zeros((tm, HEAD_PAD), F32)
        for h in range(N_HEADS):
            hb = slice(h * HEAD_PAD, (h + 1) * HEAD_PAD)
            inv, xhat, _ = _head_norm_fwd(knraw[:, hb] + krs, gkn_ref[...])
            dy = _rope_t(dk_ref[:, hb].astype(F32), c, s1, s2)
            dgkn += jnp.sum(dy * xhat, axis=0, keepdims=True)
            dxh = _head_norm_bwd(dy, inv, xhat, gkn_ref[...])
            dkr += dxh
            draw_scr[:, hb] = jnp.where(lane < NOPE, dxh, 0.0).astype(BF16)
        dgkn_ref[...] += dgkn
        dkr_ref[...] = jnp.where((lane >= KR_LANE0) & (lane < QK_DIM), dkr, 0.0).astype(dkr_ref.dtype)
        dknraw = draw_scr[...]
        dvb = dv_ref[...]
        dwkn_ref[...] += _tn(ckvn_b, dknraw)
        dwv_ref[...] += _tn(ckvn_b, dvb)
        dckvn = _nt(dknraw, wkn_ref[...]) + _nt(dvb, wv_ref[...])
        dckv, dgkva = _rms_bwd(dckvn, inv_kv, xhat_kv, gkva_ref[...])
        dckv_ref[...] = dckv.astype(dckv_ref.dtype)
        dgkva_ref[...] += dgkva

    vec = lambda n: pl.BlockSpec((1, n), lambda i: (0, 0))
    whole = lambda r, c: pl.BlockSpec((r, c), lambda i: (0, 0))
    f = lambda r, c: jax.ShapeDtypeStruct((r, c), F32)
    return pl.pallas_call(
        body, name="mla_prep_bwd", grid=(lp // tm,),
        in_specs=[_rows(tm, hw), _rows(tm, hw), _rows(tm, MLA_WIDTH), _rows(tm, Q_RANK), _rows(tm, KV_RANK),
                  _rows(tm, HEAD_PAD)] + [_rows(tm, HEAD_PAD)] * 3
        + [vec(Q_RANK), vec(KV_RANK), vec(HEAD_PAD), vec(HEAD_PAD),
           _resident((Q_RANK, hw)), _resident((KV_RANK, hw)), _resident((KV_RANK, MLA_WIDTH))],
        out_specs=[_rows(tm, Q_RANK), _rows(tm, KV_RANK), _rows(tm, HEAD_PAD),
                   whole(Q_RANK, hw), whole(KV_RANK, hw), whole(KV_RANK, MLA_WIDTH),
                   vec(Q_RANK), vec(KV_RANK), vec(HEAD_PAD), vec(HEAD_PAD)],
        out_shape=[jax.ShapeDtypeStruct((lp, Q_RANK), BF16), jax.ShapeDtypeStruct((lp, KV_RANK), BF16),
                   jax.ShapeDtypeStruct((lp, HEAD_PAD), BF16),
                   f(Q_RANK, hw), f(KV_RANK, hw), f(KV_RANK, MLA_WIDTH),
                   f(1, Q_RANK), f(1, KV_RANK), f(1, HEAD_PAD), f(1, HEAD_PAD)],
        scratch_shapes=[pltpu.VMEM((tm, hw), BF16)],
        compiler_params=_cp(("arbitrary",)),
    )(dq, dk, dv, cq, ckv, kr, *tabs, gqa, gkva, gqn, gkn, wq, wkn, wv)


def _pool_bwd(dap, u, zp, wg, scale):
    lp = u.shape[0]
    tm = _row_tile(lp)
    n = lp // tm
    per = tm // HALO

    def body(dap_ref, u_ref, uh_ref, z_ref, wg_ref, sc_ref, du_ref, dz_ref, dwg_ref, dsc_ref, ext_u, ext_d):
        i = pl.program_id(0)
        r = n - 1 - i

        @pl.when(i == 0)
        def _():
            dwg_ref[...] = jnp.zeros(dwg_ref.shape, F32)
            dsc_ref[...] = jnp.zeros(dsc_ref.shape, F32)
            ext_d[tm:tm + HALO, :] = jnp.zeros((HALO, POOL_WIDTH), F32)

        ext_u[0:HALO, :] = jnp.where(r == 0, 0.0, uh_ref[...].astype(F32))
        ext_u[HALO:HALO + tm, :] = u_ref[...].astype(F32)
        e = ext_u[...]
        sums = _trailing_sums(e)
        inv_cnt = _inv_counts(r, tm)
        dmixed = []
        for g in range(POOL_GROUPS):
            cols = slice(g * GROUP_DIM, (g + 1) * GROUP_DIM)
            mixed_b = (sums[g][HALO:, cols] * inv_cnt[g] - e[HALO:, cols]).astype(BF16)
            yg = _mm(mixed_b, wg_ref[g])
            zf = z_ref[:, cols].astype(F32)
            sg = _sigmoid(zf)
            da = dap_ref[:, cols].astype(F32)
            dy = da * (zf * sg)
            dz_ref[:, cols] = (da * (yg * sc_ref[:, cols]) * (sg * (1.0 + zf * (1.0 - sg)))).astype(dz_ref.dtype)
            dsc_ref[:, cols] += jnp.sum(dy * yg, axis=0, keepdims=True)
            dyg = (dy * sc_ref[:, cols]).astype(BF16)
            dwg_ref[g] += _tn(mixed_b, dyg)
            dm = _nt(dyg, wg_ref[g])
            dmixed.append(dm)
            ext_d[0:tm, cols] = dm * inv_cnt[g]
        ed = ext_d[...]
        lead = _leading_sums(ed)
        ext_d[tm:tm + HALO, :] = ed[0:HALO, :]
        for g in range(POOL_GROUPS):
            cols = slice(g * GROUP_DIM, (g + 1) * GROUP_DIM)
            du_ref[:, cols] = (lead[g][0:tm, cols] - dmixed[g]).astype(du_ref.dtype)

    rev = lambda i: (n - 1 - i, 0)
    return pl.pallas_call(
        body, name="pool_bwd", grid=(n,),
        in_specs=[pl.BlockSpec((tm, POOL_WIDTH), rev), pl.BlockSpec((tm, POOL_WIDTH), rev),
                  pl.BlockSpec((HALO, POOL_WIDTH), lambda i: (jnp.maximum((n - 1 - i) * per - 1, 0), 0)),
                  pl.BlockSpec((tm, POOL_WIDTH), rev),
                  pl.BlockSpec((POOL_GROUPS, GROUP_DIM, GROUP_DIM), lambda i: (0, 0, 0)),
                  pl.BlockSpec((1, POOL_WIDTH), lambda i: (0, 0))],
        out_specs=[pl.BlockSpec((tm, POOL_WIDTH), rev), pl.BlockSpec((tm, POOL_WIDTH), rev),
                   pl.BlockSpec((POOL_GROUPS, GROUP_DIM, GROUP_DIM), lambda i: (0, 0, 0)),
                   pl.BlockSpec((1, POOL_WIDTH), lambda i: (0, 0))],
        out_shape=[jax.ShapeDtypeStruct((lp, POOL_WIDTH), BF16), jax.ShapeDtypeStruct((lp, POOL_WIDTH), BF16),
                   jax.ShapeDtypeStruct((POOL_GROUPS, GROUP_DIM, GROUP_DIM), F32),
                   jax.ShapeDtypeStruct((1, POOL_WIDTH), F32)],
        scratch_shapes=[pltpu.VMEM((HALO + tm, POOL_WIDTH), F32), pltpu.VMEM((tm + HALO, POOL_WIDTH), F32)],
        compiler_params=_cp(("arbitrary",)),
    )(dap, u, u, zp, wg, scale)


def _inproj_bwd(dres, x, gain, w_pad, dparts):
    lp = x.shape[0]
    tm = _row_tile(lp)

    def body(dres_ref, x_ref, g_ref, w_ref, *rest):
        dps = rest[:len(IN_WIDTHS)]
        dprev_ref, h_ref, dg_ref = rest[len(IN_WIDTHS):]

        @pl.when(pl.program_id(0) == 0)
        def _():
            dg_ref[...] = jnp.zeros(dg_ref.shape, F32)

        dh = jnp.zeros((tm, D_MODEL), F32)
        for dp_ref, off, wd in zip(dps, IN_OFFS, IN_WIDTHS):
            dh += _nt(dp_ref[...], w_ref[:, off:off + wd])
        inv, xhat, hn = _rms_fwd(x_ref[...], g_ref[...])
        h_ref[...] = hn.astype(h_ref.dtype)
        dx, dgain = _rms_bwd(dh, inv, xhat, g_ref[...])
        dg_ref[...] += dgain
        dprev_ref[...] = dres_ref[...] + dx

    return pl.pallas_call(
        body, name="inproj_bwd", grid=(lp // tm,),
        in_specs=[_rows(tm, D_MODEL), _rows(tm, D_MODEL), pl.BlockSpec((1, D_MODEL), lambda i: (0, 0)),
                  _resident((D_MODEL, IN_PAD))] + [_rows(tm, wd) for wd in IN_WIDTHS],
        out_specs=[_rows(tm, D_MODEL), _rows(tm, D_MODEL), pl.BlockSpec((1, D_MODEL), lambda i: (0, 0))],
        out_shape=[jax.ShapeDtypeStruct((lp, D_MODEL), F32), jax.ShapeDtypeStruct((lp, D_MODEL), BF16),
                   jax.ShapeDtypeStruct((1, D_MODEL), F32)],
        compiler_params=_cp(("arbitrary",)),
    )(dres, x, gain, w_pad, *dparts)


def _weight_grads(a, bs, name):
    lp, m = a.shape
    tk = _row_tile(lp)
    nb = len(bs)

    def body(a_ref, *rest):
        b_refs, o_refs = rest[:nb], rest[nb:]

        @pl.when(pl.program_id(0) == 0)
        def _():
            for o_ref in o_refs:
                o_ref[...] = jnp.zeros(o_ref.shape, F32)

        ab = a_ref[...].astype(BF16)
        for b_ref, o_ref in zip(b_refs, o_refs):
            o_ref[...] += _tn(ab, b_ref[...].astype(BF16))

    return pl.pallas_call(
        body, name=name, grid=(lp // tk,),
        in_specs=[_rows(tk, m)] + [_rows(tk, b.shape[1]) for b in bs],
        out_specs=[pl.BlockSpec((m, b.shape[1]), lambda i: (0, 0)) for b in bs],
        out_shape=[jax.ShapeDtypeStruct((m, b.shape[1]), F32) for b in bs],
        compiler_params=_cp(("arbitrary",)),
    )(a, *bs)


HBM_SPEC = pl.BlockSpec(memory_space=pltpu.HBM)


def _my_place():
    return lax.axis_index("x"), lax.axis_index("y"), lax.axis_index("c")


def _other_chips(x, y):
    return [(1 - x, y), (x, 1 - y), (1 - x, 1 - y)]


def _chip_exchange(arrs, scatter, name):
    n = len(arrs)

    def body(*refs):
        ins, outs = refs[:n], refs[n:2 * n]
        send_sems, recv_sems, local_sems = refs[2 * n:]
        x, y, c = _my_place()
        me = 2 * x + y
        chips = _other_chips(x, y)

        def remote(a, k):
            px, py = chips[k]
            return pltpu.make_async_remote_copy(
                src_ref=ins[a].at[2 * px + py] if scatter else ins[a], dst_ref=outs[a].at[me],
                send_sem=send_sems.at[a * 3 + k], recv_sem=recv_sems.at[a * 3 + k],
                device_id=(px, py, c), device_id_type=MESH)

        def arrival(a, k):
            px, py = chips[k]
            return pltpu.make_async_remote_copy(
                src_ref=ins[a].at[me] if scatter else ins[a], dst_ref=outs[a].at[2 * px + py],
                send_sem=send_sems.at[a * 3 + k], recv_sem=recv_sems.at[a * 3 + k],
                device_id=(px, py, c), device_id_type=MESH)

        local = [pltpu.make_async_copy(ins[a].at[me] if scatter else ins[a], outs[a].at[me], local_sems.at[a])
                 for a in range(n)]
        sends = [remote(a, k) for a in range(n) for k in range(3)]
        for cp in local + sends:
            cp.start()
        for a in range(n):
            for k in range(3):
                arrival(a, k).wait_recv()
        for cp in sends:
            cp.wait_send()
        for cp in local:
            cp.wait()

    out_shape = [jax.ShapeDtypeStruct(a.shape if scatter else (N_CHIPS,) + a.shape, a.dtype) for a in arrs]
    return pl.pallas_call(
        body, name=name, in_specs=[HBM_SPEC] * n, out_specs=[HBM_SPEC] * n, out_shape=out_shape,
        scratch_shapes=[pltpu.SemaphoreType.DMA((3 * n,)), pltpu.SemaphoreType.DMA((3 * n,)),
                        pltpu.SemaphoreType.DMA((n,))],
    )(*arrs)


def _sibling_exchange(arrs, name):
    n = len(arrs)

    def body(*refs):
        ins, outs = refs[:n], refs[n:2 * n]
        send_sems, recv_sems = refs[2 * n:]
        x, y, c = _my_place()
        cps = [pltpu.make_async_remote_copy(src_ref=ins[a], dst_ref=outs[a], send_sem=send_sems.at[a],
                                            recv_sem=recv_sems.at[a], device_id=(x, y, 1 - c), device_id_type=MESH)
               for a in range(n)]
        for cp in cps:
            cp.start()
        for cp in cps:
            cp.wait_recv()
        for cp in cps:
            cp.wait_send()

    return pl.pallas_call(
        body, name=name, in_specs=[HBM_SPEC] * n, out_specs=[HBM_SPEC] * n,
        out_shape=[jax.ShapeDtypeStruct(a.shape, a.dtype) for a in arrs],
        scratch_shapes=[pltpu.SemaphoreType.DMA((n,)), pltpu.SemaphoreType.DMA((n,))],
    )(*arrs)


def _all_reduce_small(pack):
    rows = pack.shape[0]

    def body(p_ref, o_ref, g_scr, send_sems, recv_sems):
        x, y, c = _my_place()
        me = 4 * x + 2 * y + c
        flips = [(dx, dy, dc) for dx in (0, 1) for dy in (0, 1) for dc in (0, 1) if (dx, dy, dc) != (0, 0, 0)]

        def peer(f):
            return (x if f[0] == 0 else 1 - x, y if f[1] == 0 else 1 - y, c if f[2] == 0 else 1 - c)

        def copy(k, slot):
            return pltpu.make_async_remote_copy(src_ref=p_ref, dst_ref=g_scr.at[slot], send_sem=send_sems.at[k],
                                                recv_sem=recv_sems.at[k], device_id=peer(flips[k]), device_id_type=MESH)

        sends = [copy(k, me) for k in range(len(flips))]
        for cp in sends:
            cp.start()
        g_scr[me] = p_ref[...]
        for k, f in enumerate(flips):
            px, py, pc = peer(f)
            copy(k, 4 * px + 2 * py + pc).wait_recv()
        for cp in sends:
            cp.wait_send()
        acc = g_scr[0]
        for d in range(1, N_DEV):
            acc = acc + g_scr[d]
        o_ref[...] = acc

    vm = pl.BlockSpec(memory_space=pltpu.VMEM)
    return pl.pallas_call(
        body, name="all_reduce_small", in_specs=[vm], out_specs=vm,
        out_shape=jax.ShapeDtypeStruct(pack.shape, F32),
        scratch_shapes=[pltpu.VMEM((N_DEV, rows, 128), F32), pltpu.SemaphoreType.DMA((N_DEV - 1,)),
                        pltpu.SemaphoreType.DMA((N_DEV - 1,))],
        compiler_params=_cp(),
    )(pack)


def _as3d(a):
    return a.reshape((-1,) + a.shape[-2:])


def _row_block(r):
    for t in (256, 192, 128, 64, 32, 16, 8):
        if r % t == 0:
            return t
    return r


def _sum_pieces(pieces, name):
    _, na, r, c = pieces.shape
    rt = _row_block(r)

    def body(p_ref, o_ref):
        acc = p_ref[0, 0].astype(F32)
        for s in range(1, N_CHIPS):
            acc = acc + p_ref[s, 0].astype(F32)
        o_ref[0] = acc

    return pl.pallas_call(
        body, name=name, grid=(na, r // rt),
        in_specs=[pl.BlockSpec((N_CHIPS, 1, rt, c), lambda a, i: (0, a, i, 0))],
        out_specs=pl.BlockSpec((1, rt, c), lambda a, i: (a, i, 0)),
        out_shape=jax.ShapeDtypeStruct((na, r, c), F32),
        compiler_params=_cp(("parallel", "parallel")),
    )(pieces)


def _adamw(w, g_parts, m, v, name):
    na, r, c = w.shape
    rt = _row_block(r)
    ng = len(g_parts)

    def body(w_ref, *rest):
        g_refs = rest[:ng]
        m_ref, v_ref, g_out, d_out, m_out, v_out = rest[ng:]
        g = g_refs[0][...]
        for gr in g_refs[1:]:
            g = g + gr[...]
        m_new = ADAM_B1 * m_ref[...] + (1.0 - ADAM_B1) * g
        v_new = ADAM_B2 * v_ref[...] + (1.0 - ADAM_B2) * (g * g)
        m_hat = m_new / (1.0 - ADAM_B1 ** ADAM_STEP)
        v_hat = v_new / (1.0 - ADAM_B2 ** ADAM_STEP)
        g_out[...] = g
        d_out[...] = -ADAM_LR * (m_hat / (jnp.sqrt(v_hat) + ADAM_EPS) + ADAM_WD * w_ref[...])
        m_out[...] = m_new
        v_out[...] = v_new

    spec = pl.BlockSpec((1, rt, c), lambda a, i: (a, i, 0))
    out = jax.ShapeDtypeStruct((na, r, c), F32)
    return pl.pallas_call(
        body, name=name, grid=(na, r // rt), in_specs=[spec] * (3 + ng), out_specs=[spec] * 4, out_shape=[out] * 4,
        compiler_params=_cp(("parallel", "parallel")),
    )(w, *g_parts, m, v)


def _cols_from_shards(g):
    g = jnp.moveaxis(g, 0, -2)
    return g.reshape(g.shape[:-2] + (g.shape[-2] * g.shape[-1],))


def _rows_from_shards(g):
    g = jnp.moveaxis(g, 0, -3)
    return g.reshape(g.shape[:-3] + (g.shape[-3] * g.shape[-2], g.shape[-1]))


def _cols_to_shards(w):
    w = w.reshape(w.shape[:-1] + (N_CHIPS, w.shape[-1] // N_CHIPS))
    return jnp.moveaxis(w, -2, 0)


def _rows_to_shards(w):
    w = w.reshape(w.shape[:-2] + (N_CHIPS, w.shape[-2] // N_CHIPS, w.shape[-1]))
    return jnp.moveaxis(w, -3, 0)


def _pad_w_in(w):
    z = lambda n: jnp.zeros(w.shape[:-1] + (n,), w.dtype)
    return jnp.concatenate([w[..., :2048], w[..., 2080:4640], z(KR_LANE0), w[..., 2048:2080], z(HEAD_PAD - QK_DIM)], axis=-1)


def _unpad_w_in(parts):
    u, zp, cq, ckv, zm, gp, gm, kr = parts
    return jnp.concatenate([u, zp, cq, ckv, kr[:, KR_LANE0:QK_DIM], zm, gp, gm], axis=-1)


def _pad_heads(w, real):
    w = w.reshape(w.shape[:-1] + (N_HEADS, real))
    w = jnp.pad(w, [(0, 0)] * (w.ndim - 1) + [(0, HEAD_PAD - real)])
    return w.reshape(w.shape[:-2] + (N_HEADS * HEAD_PAD,))


def _flat_rows(a):
    a = a.reshape(-1)
    return jnp.pad(a, (0, (-a.shape[0]) % 128)).reshape(-1, 128)


def kernel(x, positions, meta_tokens, norm_gain, w_in, pool_w_group, pool_scale, pool_w_up, q_a_norm_gain, kv_a_norm_gain, w_q_b, w_kv_b, q_norm_gain, k_norm_gain, mla_w_up, w_out, loss_target, m_meta_tokens, m_norm_gain, m_w_in, m_pool_w_group, m_pool_scale, m_pool_w_up, m_q_a_norm_gain, m_kv_a_norm_gain, m_w_q_b, m_w_kv_b, m_q_norm_gain, m_k_norm_gain, m_mla_w_up, m_w_out, v_meta_tokens, v_norm_gain, v_w_in, v_pool_w_group, v_pool_scale, v_pool_w_up, v_q_a_norm_gain, v_kv_a_norm_gain, v_w_q_b, v_w_kv_b, v_q_norm_gain, v_k_norm_gain, v_mla_w_up, v_w_out):
    seq = x.shape[1]
    lp = -(-(ROW0 + seq) // ATTN_BLOCK) * ATTN_BLOCK
    pad_back = lp - ROW0 - seq
    chip = 2 * lax.axis_index("x") + lax.axis_index("y")

    big = dict(w_in=w_in, pool_w_up=pool_w_up, w_q_b=w_q_b, w_kv_b=w_kv_b, mla_w_up=mla_w_up, w_out=w_out)
    row_sharded = ("w_q_b", "w_out")
    names = list(big)
    gathered = _chip_exchange([big[n].astype(BF16) for n in names] + [meta_tokens], scatter=False, name="gather_weights")
    full = {n: (_rows_from_shards(g) if n in row_sharded else _cols_from_shards(g)) for n, g in zip(names, gathered)}
    meta_full = _cols_from_shards(gathered[-1])

    w_pad = _pad_w_in(full["w_in"])
    wq = _pad_heads(full["w_q_b"], QK_DIM)
    wkv = full["w_kv_b"].reshape(DEPTH, KV_RANK, N_HEADS, NOPE + V_DIM)
    wkn = _pad_heads(wkv[..., :NOPE].reshape(DEPTH, KV_RANK, N_HEADS * NOPE), NOPE)
    wv = wkv[..., NOPE:].reshape(DEPTH, KV_RANK, MLA_WIDTH)
    wg = pool_w_group.astype(BF16)
    gqn = jnp.pad(q_norm_gain, ((0, 0), (0, HEAD_PAD - QK_DIM)))
    gkn = jnp.pad(k_norm_gain, ((0, 0), (0, HEAD_PAD - QK_DIM)))

    x_pad = jnp.concatenate([jnp.zeros((PAD_FRONT, D_MODEL), F32), meta_full, x[0], jnp.zeros((pad_back, D_MODEL), F32)], axis=0)
    t_pad = jnp.concatenate([jnp.zeros((ROW0, D_MODEL), F32), loss_target[0], jnp.zeros((pad_back, D_MODEL), F32)], axis=0)
    pos_pad = jnp.concatenate([jnp.zeros((PAD_FRONT,), jnp.int32), jnp.arange(N_META, dtype=jnp.int32),
                               positions[0] + N_META, jnp.zeros((pad_back,), jnp.int32)])
    half = ROPE // 2
    inv_freq = (ROPE_THETA ** (-np.arange(half, dtype=np.float32) / half)).astype(np.float32)
    freq_row = np.zeros((1, HEAD_PAD), np.float32)
    freq_row[0, NOPE:NOPE + half] = inv_freq
    freq_row[0, NOPE + half:QK_DIM] = inv_freq
    tabs = _rope_tables(pos_pad[:, None], jnp.asarray(freq_row))

    row = lambda a, l: a[l][None, :]

    saved = []
    h_res = x_pad
    for l in range(DEPTH):
        u, zp, cq, ckv, zm, gp, gm, kr = _inproj_fwd(h_res, row(norm_gain, l), w_pad[l])
        a_pool = _pool_fwd(u, zp, wg[l], row(pool_scale, l))
        q, k, v = _mla_prep_fwd(cq, ckv, kr, tabs, row(q_a_norm_gain, l), row(kv_a_norm_gain, l), row(gqn, l), row(gkn, l),
                                wq[l], wkn[l], wv[l])
        o, lse = _flash_fwd(q, k, v)
        h_next, yp, ym = _merge_fwd(h_res, a_pool, o, zm, gp, gm, full["pool_w_up"][l], full["mla_w_up"][l], full["w_out"][l])
        saved.append(dict(x=h_res, u=u, zp=zp, cq=cq, ckv=ckv, zm=zm, gp=gp, gm=gm, kr=kr, a_pool=a_pool, q=q, k=k, v=v,
                          o=o, lse=lse, yp=yp, ym=ym))
        h_res = h_next
    dres, loss_blk = _loss_head(h_res, t_pad)

    gw = {n: [None] * DEPTH for n in names}
    gs = {n: [None] * DEPTH for n in ("norm_gain", "pool_w_group", "pool_scale", "q_a", "kv_a", "q_norm", "k_norm")}
    for l in reversed(range(DEPTH)):
        s = saved[l]
        merged, dyp, dym, dgp, dgm, dap, amla, do, dzm, delta = _merge_bwd(
            dres, s["yp"], s["ym"], s["gp"], s["gm"], s["o"], s["zm"], full["w_out"][l], full["pool_w_up"][l], full["mla_w_up"][l])
        (gw["w_out"][l],) = _weight_grads(merged, [dres], "grad_w_out")
        (gw["pool_w_up"][l],) = _weight_grads(s["a_pool"], [dyp], "grad_pool_w_up")
        (gw["mla_w_up"][l],) = _weight_grads(amla, [dym], "grad_mla_w_up")
        dq, dk, dv = _flash_bwd(s["q"], s["k"], s["v"], do, s["lse"], delta)
        dcq, dckv, dkr, dwq, dwkn, dwv, gs["q_a"][l], gs["kv_a"][l], dgqn, dgkn = _mla_prep_bwd(
            dq, dk, dv, s["cq"], s["ckv"], s["kr"], tabs, row(q_a_norm_gain, l), row(kv_a_norm_gain, l), row(gqn, l), row(gkn, l),
            wq[l], wkn[l], wv[l])
        gs["q_norm"][l] = dgqn[:, :QK_DIM]
        gs["k_norm"][l] = dgkn[:, :QK_DIM]
        gw["w_q_b"][l] = dwq.reshape(Q_RANK, N_HEADS, HEAD_PAD)[..., :QK_DIM].reshape(Q_RANK, N_HEADS * QK_DIM)
        gw["w_kv_b"][l] = jnp.concatenate([dwkn.reshape(KV_RANK, N_HEADS, HEAD_PAD)[..., :NOPE],
                                           dwv.reshape(KV_RANK, N_HEADS, V_DIM)], axis=-1).reshape(KV_RANK, N_HEADS * (NOPE + V_DIM))
        du, dzp, gs["pool_w_group"][l], gs["pool_scale"][l] = _pool_bwd(dap, s["u"], s["zp"], wg[l], row(pool_scale, l))
        dparts = [du, dzp, dcq, dckv, dzm, dgp, dgm, dkr]
        dres, h, gs["norm_gain"][l] = _inproj_bwd(dres, s["x"], row(norm_gain, l), w_pad[l], dparts)
        ga = _weight_grads(h, [du, dzp, dcq, dckv, dkr], "grad_w_in_a")
        gb = _weight_grads(h, [dzm, dgp, dgm], "grad_w_in_b")
        gw["w_in"][l] = _unpad_w_in([ga[0], ga[1], ga[2], ga[3], gb[0], gb[1], gb[2], ga[4]])
    grad_x = dres[ROW0:ROW0 + seq][None]

    stacks = [(_rows_to_shards if n in row_sharded else _cols_to_shards)(jnp.stack(gw[n])).astype(BF16) for n in names]
    pieces = _chip_exchange(stacks, scatter=True, name="scatter_grads")
    sums = [_sum_pieces(p, "sum_" + n) for n, p in zip(names, pieces)]
    other = _sibling_exchange(sums, name="swap_core_sums")
    moments = dict(w_in=(m_w_in, v_w_in), pool_w_up=(m_pool_w_up, v_pool_w_up), w_q_b=(m_w_q_b, v_w_q_b),
                   w_kv_b=(m_w_kv_b, v_w_kv_b), mla_w_up=(m_mla_w_up, v_mla_w_up), w_out=(m_w_out, v_w_out))
    big_out = {n: _adamw(big[n], [sm, ot], moments[n][0], moments[n][1], "adamw_" + n)
               for n, sm, ot in zip(names, sums, other)}

    small_names = ("norm_gain", "pool_w_group", "pool_scale", "q_a", "kv_a", "q_norm", "k_norm")
    small_w = dict(norm_gain=(norm_gain, m_norm_gain, v_norm_gain), pool_w_group=(pool_w_group, m_pool_w_group, v_pool_w_group),
                   pool_scale=(pool_scale, m_pool_scale, v_pool_scale), q_a=(q_a_norm_gain, m_q_a_norm_gain, v_q_a_norm_gain),
                   kv_a=(kv_a_norm_gain, m_kv_a_norm_gain, v_kv_a_norm_gain), q_norm=(q_norm_gain, m_q_norm_gain, v_q_norm_gain),
                   k_norm=(k_norm_gain, m_k_norm_gain, v_k_norm_gain))
    small_g = {n: jnp.stack(gs[n]).reshape(small_w[n][0].shape) for n in small_names}
    blocks = [_flat_rows(small_g[n]) for n in small_names]
    n_rows = [b.shape[0] for b in blocks]
    meta_rows = N_META * D_MODEL // 128
    pack = jnp.concatenate(blocks + [dres[PAD_FRONT:ROW0].reshape(meta_rows, 128), loss_blk[0:1]], axis=0)
    pack = jnp.pad(pack, ((0, (-pack.shape[0]) % 8), (0, 0)))
    total = _all_reduce_small(pack)
    n_small = sum(n_rows)
    loss = total[n_small + meta_rows, 0]
    gmeta = lax.dynamic_slice_in_dim(total[n_small:n_small + meta_rows].reshape(N_META, D_MODEL), chip * (D_MODEL // N_CHIPS),
                                     D_MODEL // N_CHIPS, axis=1)

    def packed(idx, meta_part):
        p = jnp.concatenate([_flat_rows(small_w[n][idx]) for n in small_names] + [_flat_rows(meta_part)], axis=0)
        return jnp.pad(p, ((0, (-p.shape[0]) % 8), (0, 0)))[None]

    g_pack = jnp.concatenate([total[:n_small], _flat_rows(gmeta)], axis=0)
    g_pack = jnp.pad(g_pack, ((0, (-g_pack.shape[0]) % 8), (0, 0)))[None]
    small_out = _adamw(packed(0, meta_tokens), [g_pack], packed(1, m_meta_tokens), packed(2, v_meta_tokens), "adamw_small")

    def unpack(p):
        res, r0 = {}, 0
        for n, nr in zip(small_names, n_rows):
            shape = small_w[n][0].shape
            res[n] = p[0, r0:r0 + nr].reshape(-1)[:math.prod(shape)].reshape(shape)
            r0 += nr
        res["meta"] = p[0, r0:r0 + N_META * (D_MODEL // N_CHIPS) // 128].reshape(N_META, D_MODEL // N_CHIPS)
        return res

    small_res = [unpack(p) for p in small_out]

    def leaf(kind, name):
        key = {"meta_tokens": "meta", "q_a_norm_gain": "q_a", "kv_a_norm_gain": "kv_a", "q_norm_gain": "q_norm",
               "k_norm_gain": "k_norm"}.get(name, name)
        if name in big_out:
            return big_out[name][kind].reshape(big[name].shape)
        return small_res[kind][key]

    order = ("meta_tokens", "norm_gain", "w_in", "pool_w_group", "pool_scale", "pool_w_up", "q_a_norm_gain", "kv_a_norm_gain",
             "w_q_b", "w_kv_b", "q_norm_gain", "k_norm_gain", "mla_w_up", "w_out")
    outs = [loss, grad_x]
    for kind in range(4):
        outs += [leaf(kind, n) for n in order]
    return tuple(outs)
```

```python
import functools
import math

import numpy as np
import jax
import jax.numpy as jnp
from jax import lax
from jax.experimental import pallas as pl
from jax.experimental.pallas import tpu as pltpu

F32 = jnp.float32
BF16 = jnp.bfloat16
MESH = pl.DeviceIdType.MESH

D_MODEL = 1024
DEPTH = 4
N_META = 16
POOL_WIDTH = 512
POOL_WINDOWS = (2, 4, 8, 16)
POOL_GROUPS = 4
GROUP_DIM = 128
N_HEADS = 8
NOPE = 64
ROPE = 32
QK_DIM = 96
V_DIM = 64
MLA_WIDTH = 512
KV_RANK = 256
Q_RANK = 768
ROPE_THETA = 10000.0
EPS = 1e-6
MASK_VALUE = -1e30
ATTN_BLOCK = 128
PAD_FRONT = (-N_META) % ATTN_BLOCK
ROW0 = PAD_FRONT + N_META
HEAD_PAD = 128
HALO = 16
N_CHIPS = 4
N_DEV = 8

IN_NAMES = ("u", "zp", "cq", "ckv", "zm", "gp", "gm", "kr")
IN_WIDTHS = (512, 512, 768, 256, 512, 1024, 1024, 128)
IN_OFFS = tuple(int(v) for v in np.cumsum((0,) + IN_WIDTHS[:-1]))
IN_PAD = sum(IN_WIDTHS)
KR_LANE0 = NOPE

ADAM_LR = 0.001
ADAM_B1 = 0.9
ADAM_B2 = 0.999
ADAM_EPS = 1e-08
ADAM_WD = 0.01
ADAM_STEP = 10

VMEM_LIMIT = 56 * 1024 * 1024
ATTN_TILE = 768
ROW_TILES = (768, 384)
ROW_TILES_HEAVY = (384,)
ROW_BLOCK = 32
LOG2E = 1.4426950408889634
LN2 = 0.6931471805599453
Q_PRESCALE = LOG2E / math.sqrt(QK_DIM)


def _cp(sem=None, vmem=VMEM_LIMIT):
    kw = dict(vmem_limit_bytes=vmem)
    if sem is not None:
        kw["dimension_semantics"] = sem
    return pltpu.CompilerParams(**kw)


def _row_tile(n_rows, prefs=None):
    for t in prefs or ROW_TILES:
        if n_rows % t == 0:
            return t
    raise ValueError(f"no row tile for {n_rows}")


def _nt(a, b):
    return lax.dot_general(a, b, (((1,), (1,)), ((), ())), preferred_element_type=F32)


def _tn(a, b):
    return lax.dot_general(a, b, (((0,), (0,)), ((), ())), preferred_element_type=F32)


def _mm(a, b):
    return jnp.dot(a, b, preferred_element_type=F32)


def _sigmoid(x):
    return 1.0 / (1.0 + jnp.exp(-x))


def _resident(shape):
    nd = len(shape)
    return pl.BlockSpec(shape, lambda *_: (0,) * nd, pipeline_mode=pl.Buffered(1))


def _rows(tm, width):
    return pl.BlockSpec((tm, width), lambda i: (i, 0))


def _rope_tables(pos_col, inv_freq_row):
    lp = pos_col.shape[0]
    tm = _row_tile(lp)

    def body(p_ref, f_ref, c_ref, s1_ref, s2_ref):
        ang = p_ref[...].astype(F32) * f_ref[...]
        lane = lax.broadcasted_iota(jnp.int32, ang.shape, 1)
        cs = jnp.cos(ang)
        sn = jnp.sin(ang)
        c_ref[...] = jnp.where(lane < NOPE, 1.0, jnp.where(lane < QK_DIM, cs, 0.0))
        s1_ref[...] = jnp.where((lane >= NOPE) & (lane < NOPE + ROPE // 2), -sn, 0.0)
        s2_ref[...] = jnp.where((lane >= NOPE + ROPE // 2) & (lane < QK_DIM), sn, 0.0)

    out = jax.ShapeDtypeStruct((lp, HEAD_PAD), F32)
    return pl.pallas_call(
        body, name="rope_tables", grid=(lp // tm,),
        in_specs=[pl.BlockSpec((tm, 1), lambda i: (i, 0)), pl.BlockSpec((1, HEAD_PAD), lambda i: (0, 0))],
        out_specs=[_rows(tm, HEAD_PAD)] * 3, out_shape=[out] * 3,
        compiler_params=_cp(("parallel",)),
    )(pos_col, inv_freq_row)


def _rope(y, c, s1, s2):
    return y * c + pltpu.roll(y, HEAD_PAD - ROPE // 2, 1) * s1 + pltpu.roll(y, ROPE // 2, 1) * s2


def _rope_t(g, c, s1, s2):
    return g * c + pltpu.roll(g * s1, ROPE // 2, 1) + pltpu.roll(g * s2, HEAD_PAD - ROPE // 2, 1)


def _inproj_fwd(x, gain, w_pad):
    lp = x.shape[0]
    tm = _row_tile(lp)

    def body(x_ref, g_ref, w_ref, *outs):
        xf = x_ref[...]
        inv = lax.rsqrt(jnp.mean(xf * xf, axis=-1, keepdims=True) + EPS)
        h = (xf * inv * g_ref[...]).astype(BF16)
        for o_ref, off, wd in zip(outs, IN_OFFS, IN_WIDTHS):
            o_ref[...] = _mm(h, w_ref[:, off:off + wd]).astype(o_ref.dtype)

    return pl.pallas_call(
        body, name="inproj_fwd", grid=(lp // tm,),
        in_specs=[_rows(tm, D_MODEL), pl.BlockSpec((1, D_MODEL), lambda i: (0, 0)), _resident((D_MODEL, IN_PAD))],
        out_specs=[_rows(tm, wd) for wd in IN_WIDTHS],
        out_shape=[jax.ShapeDtypeStruct((lp, wd), BF16) for wd in IN_WIDTHS],
        compiler_params=_cp(("parallel",)),
    )(x, gain, w_pad)


def _inv_counts(tile_idx, tm):
    row = tile_idx * tm + lax.broadcasted_iota(jnp.int32, (tm, 1), 0)
    t1 = jnp.maximum(row - PAD_FRONT + 1, 1).astype(F32)
    return [1.0 / jnp.minimum(t1, float(w)) for w in POOL_WINDOWS]


def _trailing_sums(e):
    s2 = e + pltpu.roll(e, 1, 0)
    s4 = s2 + pltpu.roll(s2, 2, 0)
    s8 = s4 + pltpu.roll(s4, 4, 0)
    s16 = s8 + pltpu.roll(s8, 8, 0)
    return (s2, s4, s8, s16)


def _leading_sums(e):
    n = e.shape[0]
    s2 = e + pltpu.roll(e, n - 1, 0)
    s4 = s2 + pltpu.roll(s2, n - 2, 0)
    s8 = s4 + pltpu.roll(s4, n - 4, 0)
    s16 = s8 + pltpu.roll(s8, n - 8, 0)
    return (s2, s4, s8, s16)


def _pool_fwd(u, zp, wg, scale):
    lp = u.shape[0]
    tm = _row_tile(lp)

    def body(u_ref, z_ref, wg_ref, sc_ref, a_ref, ext_ref):
        i = pl.program_id(0)

        @pl.when(i == 0)
        def _():
            ext_ref[0:HALO, :] = jnp.zeros((HALO, POOL_WIDTH), F32)

        ext_ref[HALO:HALO + tm, :] = u_ref[...].astype(F32)
        e = ext_ref[...]
        sums = _trailing_sums(e)
        ext_ref[0:HALO, :] = e[tm:tm + HALO, :]
        inv_cnt = _inv_counts(i, tm)
        for g in range(POOL_GROUPS):
            cols = slice(g * GROUP_DIM, (g + 1) * GROUP_DIM)
            mixed = sums[g][HALO:, cols] * inv_cnt[g] - e[HALO:, cols]
            y = _mm(mixed.astype(BF16), wg_ref[g]) * sc_ref[:, cols]
            zf = z_ref[:, cols].astype(F32)
            a_ref[:, cols] = (y * (zf * _sigmoid(zf))).astype(a_ref.dtype)

    return pl.pallas_call(
        body, name="pool_fwd", grid=(lp // tm,),
        in_specs=[_rows(tm, POOL_WIDTH), _rows(tm, POOL_WIDTH),
                  pl.BlockSpec((POOL_GROUPS, GROUP_DIM, GROUP_DIM), lambda i: (0, 0, 0)),
                  pl.BlockSpec((1, POOL_WIDTH), lambda i: (0, 0))],
        out_specs=_rows(tm, POOL_WIDTH), out_shape=jax.ShapeDtypeStruct((lp, POOL_WIDTH), BF16),
        scratch_shapes=[pltpu.VMEM((HALO + tm, POOL_WIDTH), F32)],
        compiler_params=_cp(("arbitrary",)),
    )(u, zp, wg, scale)


def _rms_fwd(xf, gain):
    inv = lax.rsqrt(jnp.mean(xf * xf, axis=-1, keepdims=True) + EPS)
    xhat = xf * inv
    return inv, xhat, xhat * gain


def _rms_bwd(dy, inv, xhat, gain):
    dgain = jnp.sum(dy * xhat, axis=0, keepdims=True)
    dyg = dy * gain
    dx = inv * (dyg - xhat * jnp.mean(dyg * xhat, axis=-1, keepdims=True))
    return dx, dgain


def _head_norm_fwd(xh, gain128):
    inv = lax.rsqrt(jnp.sum(xh * xh, axis=-1, keepdims=True) * (1.0 / QK_DIM) + EPS)
    xhat = xh * inv
    return inv, xhat, xhat * gain128


def _head_norm_bwd(dy, inv, xhat, gain128):
    dyg = dy * gain128
    return inv * (dyg - xhat * (jnp.sum(dyg * xhat, axis=-1, keepdims=True) * (1.0 / QK_DIM)))


def _mla_prep_fwd(cq, ckv, kr, tabs, gqa, gkva, gqn, gkn, wq, wkn, wv):
    lp = cq.shape[0]
    tm = _row_tile(lp)

    def body(cq_ref, ckv_ref, kr_ref, c_ref, s1_ref, s2_ref, gqa_ref, gkva_ref, gqn_ref, gkn_ref,
             wq_ref, wkn_ref, wv_ref, q_ref, k_ref, v_ref):
        c, s1, s2 = c_ref[...], s1_ref[...], s2_ref[...]
        _, _, cqn = _rms_fwd(cq_ref[...].astype(F32), gqa_ref[...])
        qraw = _mm(cqn.astype(BF16), wq_ref[...])
        for h in range(N_HEADS):
            hb = slice(h * HEAD_PAD, (h + 1) * HEAD_PAD)
            _, _, yh = _head_norm_fwd(qraw[:, hb], gqn_ref[...])
            q_ref[:, hb] = (_rope(yh, c, s1, s2) * Q_PRESCALE).astype(q_ref.dtype)
        _, _, ckvn = _rms_fwd(ckv_ref[...].astype(F32), gkva_ref[...])
        ckvn_b = ckvn.astype(BF16)
        knraw = _mm(ckvn_b, wkn_ref[...])
        krs = kr_ref[...].astype(F32)
        for h in range(N_HEADS):
            hb = slice(h * HEAD_PAD, (h + 1) * HEAD_PAD)
            _, _, yh = _head_norm_fwd(knraw[:, hb] + krs, gkn_ref[...])
            k_ref[:, hb] = _rope(yh, c, s1, s2).astype(k_ref.dtype)
        v_ref[...] = _mm(ckvn_b, wv_ref[...]).astype(v_ref.dtype)

    hw = N_HEADS * HEAD_PAD
    vec = lambda n: pl.BlockSpec((1, n), lambda i: (0, 0))
    return pl.pallas_call(
        body, name="mla_prep_fwd", grid=(lp // tm,),
        in_specs=[_rows(tm, Q_RANK), _rows(tm, KV_RANK), _rows(tm, HEAD_PAD)] + [_rows(tm, HEAD_PAD)] * 3
        + [vec(Q_RANK), vec(KV_RANK), vec(HEAD_PAD), vec(HEAD_PAD),
           _resident((Q_RANK, hw)), _resident((KV_RANK, hw)), _resident((KV_RANK, MLA_WIDTH))],
        out_specs=[_rows(tm, hw), _rows(tm, hw), _rows(tm, MLA_WIDTH)],
        out_shape=[jax.ShapeDtypeStruct((lp, hw), BF16), jax.ShapeDtypeStruct((lp, hw), BF16),
                   jax.ShapeDtypeStruct((lp, MLA_WIDTH), BF16)],
        compiler_params=_cp(("parallel",)),
    )(cq, ckv, kr, *tabs, gqa, gkva, gqn, gkn, wq, wkn, wv)


def _causal_mask(s, q0, k0):
    qi = q0 + lax.broadcasted_iota(jnp.int32, s.shape, 0)
    ki = k0 + lax.broadcasted_iota(jnp.int32, s.shape, 1)
    return jnp.where((ki <= qi) & (ki >= PAD_FRONT), s, MASK_VALUE)


def _score_chunks(kind, r, tk):
    if kind == "inner":
        return [(c0, False) for c0 in range(0, tk, ATTN_BLOCK)]
    if kind == "first":
        return [(c0, c0 == 0) for c0 in range(0, tk, ATTN_BLOCK)]
    return [(c0, True) for c0 in range(0, min(tk, (r + 1) * ROW_BLOCK), ATTN_BLOCK)]


def _tile_kinds(i, t):
    return (("diag", t == i), ("first", (t == 0) & (i > 0)), ("inner", (t > 0) & (t < i)))


def _lanes(col, width=HEAD_PAD):
    return jnp.broadcast_to(col, (col.shape[0], width))


def _flash_fwd(q, k, v, gather=()):
    lp = q.shape[0]
    tq = tk = ATTN_TILE
    nq = lp // tq
    nj = N_HEADS // 2
    n_blocks = tq // ROW_BLOCK
    ex = _ChipExchange(list(gather), scatter=False)

    def body(q_ref, k_ref, v_ref, *rest):
        ex_in, (o_ref, lse_ref), ex_out = rest[:ex.n], rest[ex.n:ex.n + 2], rest[ex.n + 2:2 * ex.n + 2]
        m_scr, l_scr, acc_scr, s_scr, p_scr, part_scr = rest[2 * ex.n + 2:2 * ex.n + 8]
        ex_sems = rest[2 * ex.n + 8:]
        j, i, t = pl.program_id(0), pl.program_id(1), pl.program_id(2)
        if ex.n:
            pl.when((j == 0) & (i == 0) & (t == 0))(lambda: ex.start(ex_in, ex_out, ex_sems))

        @pl.when(t == 0)
        def _():
            m_scr[...] = jnp.full(m_scr.shape, MASK_VALUE, F32)
            l_scr[...] = jnp.zeros(l_scr.shape, F32)
            acc_scr[...] = jnp.zeros(acc_scr.shape, F32)

        def step(kind):
            def scores(hh, r, c0, masked):
                s = s_scr[hh, r * ROW_BLOCK:(r + 1) * ROW_BLOCK, c0:c0 + ATTN_BLOCK]
                return _causal_mask(s, i * tq + r * ROW_BLOCK, t * tk + c0) if masked else s

            for hh in range(2):
                hb = slice(hh * HEAD_PAD, (hh + 1) * HEAD_PAD)
                s_scr[hh] = _nt(q_ref[:, hb], k_ref[:, hb])
            for hh in range(2):
                for r in range(n_blocks):
                    part = None
                    for c0, masked in _score_chunks(kind, r, tk):
                        s = scores(hh, r, c0, masked)
                        part = s if part is None else jnp.maximum(part, s)
                    part_scr[r * ROW_BLOCK:(r + 1) * ROW_BLOCK, :] = part
                m_prev = m_scr[hh]
                m_new = jnp.maximum(m_prev, _lanes(jnp.max(part_scr[...], axis=-1, keepdims=True)))
                alpha = jnp.exp2(m_prev - m_new)
                m_scr[hh] = m_new
                for r in range(n_blocks):
                    rows = slice(r * ROW_BLOCK, (r + 1) * ROW_BLOCK)
                    m_r = m_scr[hh, rows, :]
                    chunks = _score_chunks(kind, r, tk)
                    part = None
                    for c0, masked in chunks:
                        p = jnp.exp2(scores(hh, r, c0, masked) - m_r)
                        part = p if part is None else part + p
                        p_scr[hh, rows, c0:c0 + ATTN_BLOCK] = p.astype(BF16)
                    done = chunks[-1][0] + ATTN_BLOCK
                    if done < tk:
                        p_scr[hh, rows, done:tk] = jnp.zeros((ROW_BLOCK, tk - done), BF16)
                    part_scr[rows, :] = part
                l_scr[hh] = alpha * l_scr[hh] + _lanes(jnp.sum(part_scr[...], axis=-1, keepdims=True))
                acc_scr[hh] = alpha * acc_scr[hh] + _mm(p_scr[hh], v_ref[...])

        for kind, pred in _tile_kinds(i, t):
            pl.when(pred)(functools.partial(step, kind))

        @pl.when(t == i)
        def _():
            lane = lax.broadcasted_iota(jnp.int32, (tq, HEAD_PAD), 1)
            o = jnp.where(lane < V_DIM, acc_scr[0] / l_scr[0], acc_scr[1] / l_scr[1])
            o_ref[...] = o.astype(o_ref.dtype)
            lse_ref[...] = jnp.where(lane < V_DIM, m_scr[0] + jnp.log(l_scr[0]) * LOG2E, m_scr[1] + jnp.log(l_scr[1]) * LOG2E)

        if ex.n:
            pl.when((j == nj - 1) & (i == nq - 1) & (t == nq - 1))(lambda: ex.wait(ex_in, ex_out, ex_sems))

    kv_idx = lambda j, i, t: (jnp.minimum(t, i), j)
    o, lse, *gathered = pl.pallas_call(
        body, name="flash_fwd_gather" if ex.n else "flash_fwd", grid=(nj, nq, nq),
        in_specs=[pl.BlockSpec((tq, 2 * HEAD_PAD), lambda j, i, t: (i, j)),
                  pl.BlockSpec((tk, 2 * HEAD_PAD), kv_idx), pl.BlockSpec((tk, HEAD_PAD), kv_idx)] + ex.specs,
        out_specs=[pl.BlockSpec((tq, HEAD_PAD), lambda j, i, t: (i, j))] * 2 + ex.specs,
        out_shape=[jax.ShapeDtypeStruct((lp, MLA_WIDTH), BF16), jax.ShapeDtypeStruct((lp, MLA_WIDTH), F32)] + ex.out_shape,
        scratch_shapes=[pltpu.VMEM((2, tq, HEAD_PAD), F32), pltpu.VMEM((2, tq, HEAD_PAD), F32),
                        pltpu.VMEM((2, tq, HEAD_PAD), F32), pltpu.VMEM((2, tq, tk), F32), pltpu.VMEM((2, tq, tk), BF16),
                        pltpu.VMEM((tq, HEAD_PAD), F32)] + (ex.sems if ex.n else []),
        compiler_params=_cp(("arbitrary",) * 3),
    )(q, k, v, *gather)
    return o, lse, gathered


def _merge_fwd(x, a_pool, o, zm, gp, gm, wpu, wmu, wout):
    lp = x.shape[0]
    tm = _row_tile(lp)

    def body(x_ref, ap_ref, o_ref, zm_ref, gp_ref, gm_ref, wpu_ref, wmu_ref, wout_ref, xn_ref, yp_ref, ym_ref):
        yp = _mm(ap_ref[...], wpu_ref[...])
        zf = zm_ref[...].astype(F32)
        amla = o_ref[...].astype(F32) * (zf * _sigmoid(zf))
        ym = _mm(amla.astype(BF16), wmu_ref[...])
        merged = _sigmoid(gp_ref[...].astype(F32)) * yp + _sigmoid(gm_ref[...].astype(F32)) * ym
        xn_ref[...] = x_ref[...] + _mm(merged.astype(BF16), wout_ref[...])
        yp_ref[...] = yp.astype(yp_ref.dtype)
        ym_ref[...] = ym.astype(ym_ref.dtype)

    return pl.pallas_call(
        body, name="merge_fwd", grid=(lp // tm,),
        in_specs=[_rows(tm, D_MODEL), _rows(tm, POOL_WIDTH), _rows(tm, MLA_WIDTH), _rows(tm, MLA_WIDTH),
                  _rows(tm, D_MODEL), _rows(tm, D_MODEL),
                  _resident((POOL_WIDTH, D_MODEL)), _resident((MLA_WIDTH, D_MODEL)), _resident((D_MODEL, D_MODEL))],
        out_specs=[_rows(tm, D_MODEL)] * 3,
        out_shape=[jax.ShapeDtypeStruct((lp, D_MODEL), F32), jax.ShapeDtypeStruct((lp, D_MODEL), BF16),
                   jax.ShapeDtypeStruct((lp, D_MODEL), BF16)],
        compiler_params=_cp(("parallel",)),
    )(x, a_pool, o, zm, gp, gm, wpu, wmu, wout)


def _loss_head(y, target_pad):
    lp = y.shape[0]
    tm = _row_tile(lp)

    def body(y_ref, t_ref, d_ref, l_ref):
        i = pl.program_id(0)

        @pl.when(i == 0)
        def _():
            l_ref[...] = jnp.zeros(l_ref.shape, F32)

        row = i * tm + lax.broadcasted_iota(jnp.int32, (tm, 1), 0)
        err = jnp.where(row >= ROW0, y_ref[...] - t_ref[...], 0.0)
        d_ref[...] = err * (1.0 / D_MODEL)
        l_ref[...] += jnp.sum(err * err) * (0.5 / D_MODEL)

    return pl.pallas_call(
        body, name="loss_head", grid=(lp // tm,),
        in_specs=[_rows(tm, D_MODEL), _rows(tm, D_MODEL)],
        out_specs=[_rows(tm, D_MODEL), pl.BlockSpec((8, 128), lambda i: (0, 0))],
        out_shape=[jax.ShapeDtypeStruct((lp, D_MODEL), F32), jax.ShapeDtypeStruct((8, 128), F32)],
        compiler_params=_cp(("arbitrary",)),
    )(y, target_pad)


def _pair_rowsum(prod):
    lane = lax.broadcasted_iota(jnp.int32, prod.shape, 1)
    lo = jnp.sum(jnp.where(lane < V_DIM, prod, 0.0), axis=-1, keepdims=True)
    hi = jnp.sum(jnp.where(lane < V_DIM, 0.0, prod), axis=-1, keepdims=True)
    return jnp.where(lane < V_DIM, lo, hi)


def _merge_bwd(dres, yp, ym, gp, gm, o, zm, wout, wpu, wmu):
    lp = dres.shape[0]
    tm = _row_tile(lp, ROW_TILES_HEAVY)

    def body(dres_ref, yp_ref, ym_ref, gp_ref, gm_ref, o_ref, zm_ref, wout_ref, wpu_ref, wmu_ref,
             merged_ref, dyp_ref, dym_ref, dgp_ref, dgm_ref, dap_ref, amla_ref, do_ref, dzm_ref, delta_ref):
        dmerged = _nt(dres_ref[...].astype(BF16), wout_ref[...])
        sp = _sigmoid(gp_ref[...].astype(F32))
        sm = _sigmoid(gm_ref[...].astype(F32))
        ypf = yp_ref[...].astype(F32)
        ymf = ym_ref[...].astype(F32)
        merged_ref[...] = (sp * ypf + sm * ymf).astype(merged_ref.dtype)
        dyp = (dmerged * sp).astype(BF16)
        dym = (dmerged * sm).astype(BF16)
        dyp_ref[...] = dyp
        dym_ref[...] = dym
        dgp_ref[...] = (dmerged * ypf * sp * (1.0 - sp)).astype(dgp_ref.dtype)
        dgm_ref[...] = (dmerged * ymf * sm * (1.0 - sm)).astype(dgm_ref.dtype)
        dap_ref[...] = _nt(dyp, wpu_ref[...]).astype(dap_ref.dtype)
        dam = _nt(dym, wmu_ref[...])
        zf = zm_ref[...].astype(F32)
        sg = _sigmoid(zf)
        si = zf * sg
        of = o_ref[...].astype(F32)
        amla_ref[...] = (of * si).astype(amla_ref.dtype)
        do = dam * si
        do_ref[...] = do.astype(do_ref.dtype)
        dzm_ref[...] = (dam * of * (sg * (1.0 + zf * (1.0 - sg)))).astype(dzm_ref.dtype)
        prod = do * of
        for j in range(N_HEADS // 2):
            hb = slice(j * HEAD_PAD, (j + 1) * HEAD_PAD)
            delta_ref[:, hb] = _pair_rowsum(prod[:, hb])

    bf = lambda w: jax.ShapeDtypeStruct((lp, w), BF16)
    return pl.pallas_call(
        body, name="merge_bwd", grid=(lp // tm,),
        in_specs=[_rows(tm, D_MODEL)] * 5 + [_rows(tm, MLA_WIDTH)] * 2
        + [_resident((D_MODEL, D_MODEL)), _resident((POOL_WIDTH, D_MODEL)), _resident((MLA_WIDTH, D_MODEL))],
        out_specs=[_rows(tm, D_MODEL)] * 5 + [_rows(tm, POOL_WIDTH)] + [_rows(tm, MLA_WIDTH)] * 4,
        out_shape=[bf(D_MODEL)] * 5 + [bf(POOL_WIDTH)] + [bf(MLA_WIDTH)] * 3 + [jax.ShapeDtypeStruct((lp, MLA_WIDTH), F32)],
        compiler_params=_cp(("parallel",)),
    )(dres, yp, ym, gp, gm, o, zm, wout, wpu, wmu)


def _flash_bwd(q, k, v, do, lse, delta, scatter=()):
    lp = q.shape[0]
    tq = tk = ATTN_TILE
    nq = lp // tq
    nj = N_HEADS // 2
    scale = 1.0 / math.sqrt(QK_DIM)
    ex = _ChipExchange(list(scatter), scatter=True)

    def body(q_ref, k_ref, v_ref, do_ref, lse_ref, dl_ref, *rest):
        ex_in, (dq_ref, dk_ref, dv_ref), ex_out = rest[:ex.n], rest[ex.n:ex.n + 3], rest[ex.n + 3:2 * ex.n + 3]
        dq_acc, dk_acc, dv_acc, s_scr, dp_scr, p_scr, ds_scr, doh_scr, stat_scr = rest[2 * ex.n + 3:2 * ex.n + 12]
        ex_sems = rest[2 * ex.n + 12:]
        j, t, i = pl.program_id(0), pl.program_id(1), pl.program_id(2)
        if ex.n:
            pl.when((j == 0) & (i == 0) & (t == 0))(lambda: ex.start(ex_in, ex_out, ex_sems))

        @pl.when((t == 0) & (i == 0))
        def _():
            dq_acc[...] = jnp.zeros(dq_acc.shape, F32)

        @pl.when(i == 0)
        def _():
            dk_acc[...] = jnp.zeros(dk_acc.shape, F32)
            dv_acc[...] = jnp.zeros(dv_acc.shape, F32)

        def step(kind):
            lane = lax.broadcasted_iota(jnp.int32, (tq, HEAD_PAD), 1)
            q_rows = pl.ds(pl.multiple_of(i * tq, tq), tq)
            for hh in range(2):
                hb = slice(hh * HEAD_PAD, (hh + 1) * HEAD_PAD)
                mine = (lane < V_DIM) if hh == 0 else (lane >= V_DIM)
                doh_scr[hh] = jnp.where(mine, do_ref[...], jnp.zeros_like(do_ref[...]))
                s_scr[hh] = _nt(q_ref[:, hb], k_ref[:, hb])
                dp_scr[hh] = _nt(doh_scr[hh], v_ref[...])
            for hh in range(2):
                hb = slice(hh * HEAD_PAD, (hh + 1) * HEAD_PAD)
                col = slice(hh * V_DIM, hh * V_DIM + 1)
                stat_scr[0] = _lanes(lse_ref[:, col])
                stat_scr[1] = _lanes(dl_ref[:, col])
                for r in range(tq // ROW_BLOCK):
                    rows = slice(r * ROW_BLOCK, (r + 1) * ROW_BLOCK)
                    lse_r = stat_scr[0, rows, :]
                    dl_r = stat_scr[1, rows, :]
                    chunks = _score_chunks(kind, r, tk)
                    for c0, masked in chunks:
                        cols = slice(c0, c0 + ATTN_BLOCK)
                        s = s_scr[hh, rows, cols]
                        if masked:
                            s = _causal_mask(s, i * tq + r * ROW_BLOCK, t * tk + c0)
                        p = jnp.exp2(s - lse_r)
                        p_scr[hh, rows, cols] = p.astype(BF16)
                        ds_scr[hh, rows, cols] = (p * (dp_scr[hh, rows, cols] - dl_r)).astype(BF16)
                    done = chunks[-1][0] + ATTN_BLOCK
                    if done < tk:
                        zeros = jnp.zeros((ROW_BLOCK, tk - done), BF16)
                        p_scr[hh, rows, done:tk] = zeros
                        ds_scr[hh, rows, done:tk] = zeros
                dv_acc[...] += _tn(p_scr[hh], doh_scr[hh])
                dk_acc[hh] += _tn(ds_scr[hh], q_ref[:, hb])
                dq_acc[q_rows, hb] += _mm(ds_scr[hh], k_ref[:, hb])

        for kind, pred in _tile_kinds(i, t):
            pl.when(pred)(functools.partial(step, kind))

        @pl.when(i == nq - 1)
        def _():
            dk_ref[:, 0:HEAD_PAD] = (dk_acc[0] * LN2).astype(dk_ref.dtype)
            dk_ref[:, HEAD_PAD:2 * HEAD_PAD] = (dk_acc[1] * LN2).astype(dk_ref.dtype)
            dv_ref[...] = dv_acc[...].astype(dv_ref.dtype)

        @pl.when((t == nq - 1) & (i == nq - 1))
        def _():
            dq_ref[...] = (dq_acc[...] * scale).astype(dq_ref.dtype)

        if ex.n:
            pl.when((j == nj - 1) & (i == nq - 1) & (t == nq - 1))(lambda: ex.wait(ex_in, ex_out, ex_sems))

    q_idx = lambda j, t, i: (jnp.maximum(i, t), j)
    kv_idx = lambda j, t, i: (t, j)
    hw = N_HEADS * HEAD_PAD
    dq, dk, dv, *pieces = pl.pallas_call(
        body, name="flash_bwd_scatter" if ex.n else "flash_bwd", grid=(nj, nq, nq),
        in_specs=[pl.BlockSpec((tq, 2 * HEAD_PAD), q_idx), pl.BlockSpec((tk, 2 * HEAD_PAD), kv_idx),
                  pl.BlockSpec((tk, HEAD_PAD), kv_idx), pl.BlockSpec((tq, HEAD_PAD), q_idx),
                  pl.BlockSpec((tq, HEAD_PAD), q_idx), pl.BlockSpec((tq, HEAD_PAD), q_idx)] + ex.specs,
        out_specs=[pl.BlockSpec((lp, 2 * HEAD_PAD), lambda j, t, i: (0, j)),
                   pl.BlockSpec((tk, 2 * HEAD_PAD), kv_idx), pl.BlockSpec((tk, HEAD_PAD), kv_idx)] + ex.specs,
        out_shape=[jax.ShapeDtypeStruct((lp, hw), BF16), jax.ShapeDtypeStruct((lp, hw), BF16),
                   jax.ShapeDtypeStruct((lp, MLA_WIDTH), BF16)] + ex.out_shape,
        scratch_shapes=[pltpu.VMEM((lp, 2 * HEAD_PAD), F32), pltpu.VMEM((2, tk, HEAD_PAD), F32),
                        pltpu.VMEM((tk, HEAD_PAD), F32),
                        pltpu.VMEM((2, tq, tk), F32), pltpu.VMEM((2, tq, tk), F32),
                        pltpu.VMEM((2, tq, tk), BF16), pltpu.VMEM((2, tq, tk), BF16), pltpu.VMEM((2, tq, HEAD_PAD), BF16),
                        pltpu.VMEM((2, tq, HEAD_PAD), F32)] + (ex.sems if ex.n else []),
        compiler_params=_cp(("arbitrary",) * 3),
    )(q, k, v, do, lse, delta, *scatter)
    return dq, dk, dv, pieces


def _mla_prep_bwd(dq, dk, dv, cq, ckv, kr, tabs, gqa, gkva, gqn, gkn, wq, wkn, wv):
    lp = cq.shape[0]
    tm = _row_tile(lp, ROW_TILES_HEAVY)
    hw = N_HEADS * HEAD_PAD

    def body(dq_ref, dk_ref, dv_ref, cq_ref, ckv_ref, kr_ref, c_ref, s1_ref, s2_ref, gqa_ref, gkva_ref, gqn_ref,
             gkn_ref, wq_ref, wkn_ref, wv_ref, dcq_ref, dckv_ref, dkr_ref, dwq_ref, dwkn_ref, dwv_ref,
             dgqa_ref, dgkva_ref, dgqn_ref, dgkn_ref, draw_scr):
        @pl.when(pl.program_id(0) == 0)
        def _():
            for r in (dwq_ref, dwkn_ref, dwv_ref, dgqa_ref, dgkva_ref, dgqn_ref, dgkn_ref):
                r[...] = jnp.zeros(r.shape, F32)

        c, s1, s2 = c_ref[...], s1_ref[...], s2_ref[...]
        lane = lax.broadcasted_iota(jnp.int32, (tm, HEAD_PAD), 1)

        inv_q, xhat_q, cqn = _rms_fwd(cq_ref[...].astype(F32), gqa_ref[...])
        cqn_b = cqn.astype(BF16)
        qraw = _mm(cqn_b, wq_ref[...])
        dgqn = jnp.zeros((1, HEAD_PAD), F32)
        for h in range(N_HEADS):
            hb = slice(h * HEAD_PAD, (h + 1) * HEAD_PAD)
            inv, xhat, _ = _head_norm_fwd(qraw[:, hb], gqn_ref[...])
            dy = _rope_t(dq_ref[:, hb].astype(F32), c, s1, s2)
            dgqn += jnp.sum(dy * xhat, axis=0, keepdims=True)
            draw_scr[:, hb] = _head_norm_bwd(dy, inv, xhat, gqn_ref[...]).astype(BF16)
        dgqn_ref[...] += dgqn
        dqraw = draw_scr[...]
        dwq_ref[...] += _tn(cqn_b, dqraw)
        dcq, dgqa = _rms_bwd(_nt(dqraw, wq_ref[...]), inv_q, xhat_q, gqa_ref[...])
        dcq_ref[...] = dcq.astype(dcq_ref.dtype)
        dgqa_ref[...] += dgqa

        inv_kv, xhat_kv, ckvn = _rms_fwd(ckv_ref[...].astype(F32), gkva_ref[...])
        ckvn_b = ckvn.astype(BF16)
        knraw = _mm(ckvn_b, wkn_ref[...])
        krs = kr_ref[...].astype(F32)
        dgkn = jnp.zeros((1, HEAD_PAD), F32)
        dkr = jnp.zeros((tm, HEAD_PAD), F32)
        for h in range(N_HEADS):
            hb = slice(h * HEAD_PAD, (h + 1) * HEAD_PAD)
            inv, xhat, _ = _head_norm_fwd(knraw[:, hb] + krs, gkn_ref[...])
            dy = _rope_t(dk_ref[:, hb].astype(F32), c, s1, s2)
            dgkn += jnp.sum(dy * xhat, axis=0, keepdims=True)
            dxh = _head_norm_bwd(dy, inv, xhat, gkn_ref[...])
            dkr += dxh
            draw_scr[:, hb] = jnp.where(lane < NOPE, dxh, 0.0).astype(BF16)
        dgkn_ref[...] += dgkn
        dkr_ref[...] = jnp.where((lane >= KR_LANE0) & (lane < QK_DIM), dkr, 0.0).astype(dkr_ref.dtype)
        dknraw = draw_scr[...]
        dvb = dv_ref[...]
        dwkn_ref[...] += _tn(ckvn_b, dknraw)
        dwv_ref[...] += _tn(ckvn_b, dvb)
        dckvn = _nt(dknraw, wkn_ref[...]) + _nt(dvb, wv_ref[...])
        dckv, dgkva = _rms_bwd(dckvn, inv_kv, xhat_kv, gkva_ref[...])
        dckv_ref[...] = dckv.astype(dckv_ref.dtype)
        dgkva_ref[...] += dgkva

    vec = lambda n: pl.BlockSpec((1, n), lambda i: (0, 0))
    whole = lambda r, c: pl.BlockSpec((r, c), lambda i: (0, 0))
    f = lambda r, c: jax.ShapeDtypeStruct((r, c), F32)
    return pl.pallas_call(
        body, name="mla_prep_bwd", grid=(lp // tm,),
        in_specs=[_rows(tm, hw), _rows(tm, hw), _rows(tm, MLA_WIDTH), _rows(tm, Q_RANK), _rows(tm, KV_RANK),
                  _rows(tm, HEAD_PAD)] + [_rows(tm, HEAD_PAD)] * 3
        + [vec(Q_RANK), vec(KV_RANK), vec(HEAD_PAD), vec(HEAD_PAD),
           _resident((Q_RANK, hw)), _resident((KV_RANK, hw)), _resident((KV_RANK, MLA_WIDTH))],
        out_specs=[_rows(tm, Q_RANK), _rows(tm, KV_RANK), _rows(tm, HEAD_PAD),
                   whole(Q_RANK, hw), whole(KV_RANK, hw), whole(KV_RANK, MLA_WIDTH),
                   vec(Q_RANK), vec(KV_RANK), vec(HEAD_PAD), vec(HEAD_PAD)],
        out_shape=[jax.ShapeDtypeStruct((lp, Q_RANK), BF16), jax.ShapeDtypeStruct((lp, KV_RANK), BF16),
                   jax.ShapeDtypeStruct((lp, HEAD_PAD), BF16),
                   f(Q_RANK, hw), f(KV_RANK, hw), f(KV_RANK, MLA_WIDTH),
                   f(1, Q_RANK), f(1, KV_RANK), f(1, HEAD_PAD), f(1, HEAD_PAD)],
        scratch_shapes=[pltpu.VMEM((tm, hw), BF16)],
        compiler_params=_cp(("arbitrary",)),
    )(dq, dk, dv, cq, ckv, kr, *tabs, gqa, gkva, gqn, gkn, wq, wkn, wv)


def _pool_bwd(dap, u, zp, wg, scale):
    lp = u.shape[0]
    tm = _row_tile(lp)
    n = lp // tm
    per = tm // HALO

    def body(dap_ref, u_ref, uh_ref, z_ref, wg_ref, sc_ref, du_ref, dz_ref, dwg_ref, dsc_ref, ext_u, ext_d):
        i = pl.program_id(0)
        r = n - 1 - i

        @pl.when(i == 0)
        def _():
            dwg_ref[...] = jnp.zeros(dwg_ref.shape, F32)
            dsc_ref[...] = jnp.zeros(dsc_ref.shape, F32)
            ext_d[tm:tm + HALO, :] = jnp.zeros((HALO, POOL_WIDTH), F32)

        ext_u[0:HALO, :] = jnp.where(r == 0, 0.0, uh_ref[...].astype(F32))
        ext_u[HALO:HALO + tm, :] = u_ref[...].astype(F32)
        e = ext_u[...]
        sums = _trailing_sums(e)
        inv_cnt = _inv_counts(r, tm)
        dmixed = []
        for g in range(POOL_GROUPS):
            cols = slice(g * GROUP_DIM, (g + 1) * GROUP_DIM)
            mixed_b = (sums[g][HALO:, cols] * inv_cnt[g] - e[HALO:, cols]).astype(BF16)
            yg = _mm(mixed_b, wg_ref[g])
            zf = z_ref[:, cols].astype(F32)
            sg = _sigmoid(zf)
            da = dap_ref[:, cols].astype(F32)
            dy = da * (zf * sg)
            dz_ref[:, cols] = (da * (yg * sc_ref[:, cols]) * (sg * (1.0 + zf * (1.0 - sg)))).astype(dz_ref.dtype)
            dsc_ref[:, cols] += jnp.sum(dy * yg, axis=0, keepdims=True)
            dyg = (dy * sc_ref[:, cols]).astype(BF16)
            dwg_ref[g] += _tn(mixed_b, dyg)
            dm = _nt(dyg, wg_ref[g])
            dmixed.append(dm)
            ext_d[0:tm, cols] = dm * inv_cnt[g]
        ed = ext_d[...]
        lead = _leading_sums(ed)
        ext_d[tm:tm + HALO, :] = ed[0:HALO, :]
        for g in range(POOL_GROUPS):
            cols = slice(g * GROUP_DIM, (g + 1) * GROUP_DIM)
            du_ref[:, cols] = (lead[g][0:tm, cols] - dmixed[g]).astype(du_ref.dtype)

    rev = lambda i: (n - 1 - i, 0)
    return pl.pallas_call(
        body, name="pool_bwd", grid=(n,),
        in_specs=[pl.BlockSpec((tm, POOL_WIDTH), rev), pl.BlockSpec((tm, POOL_WIDTH), rev),
                  pl.BlockSpec((HALO, POOL_WIDTH), lambda i: (jnp.maximum((n - 1 - i) * per - 1, 0), 0)),
                  pl.BlockSpec((tm, POOL_WIDTH), rev),
                  pl.BlockSpec((POOL_GROUPS, GROUP_DIM, GROUP_DIM), lambda i: (0, 0, 0)),
                  pl.BlockSpec((1, POOL_WIDTH), lambda i: (0, 0))],
        out_specs=[pl.BlockSpec((tm, POOL_WIDTH), rev), pl.BlockSpec((tm, POOL_WIDTH), rev),
                   pl.BlockSpec((POOL_GROUPS, GROUP_DIM, GROUP_DIM), lambda i: (0, 0, 0)),
                   pl.BlockSpec((1, POOL_WIDTH), lambda i: (0, 0))],
        out_shape=[jax.ShapeDtypeStruct((lp, POOL_WIDTH), BF16), jax.ShapeDtypeStruct((lp, POOL_WIDTH), BF16),
                   jax.ShapeDtypeStruct((POOL_GROUPS, GROUP_DIM, GROUP_DIM), F32),
                   jax.ShapeDtypeStruct((1, POOL_WIDTH), F32)],
        scratch_shapes=[pltpu.VMEM((HALO + tm, POOL_WIDTH), F32), pltpu.VMEM((tm + HALO, POOL_WIDTH), F32)],
        compiler_params=_cp(("arbitrary",)),
    )(dap, u, u, zp, wg, scale)


def _inproj_bwd(dres, x, gain, w_pad, dparts):
    lp = x.shape[0]
    tm = _row_tile(lp, ROW_TILES_HEAVY)

    def body(dres_ref, x_ref, g_ref, w_ref, *rest):
        dps = rest[:len(IN_WIDTHS)]
        dprev_ref, h_ref, dg_ref = rest[len(IN_WIDTHS):]

        @pl.when(pl.program_id(0) == 0)
        def _():
            dg_ref[...] = jnp.zeros(dg_ref.shape, F32)

        dh = jnp.zeros((tm, D_MODEL), F32)
        for dp_ref, off, wd in zip(dps, IN_OFFS, IN_WIDTHS):
            dh += _nt(dp_ref[...], w_ref[:, off:off + wd])
        inv, xhat, hn = _rms_fwd(x_ref[...], g_ref[...])
        h_ref[...] = hn.astype(h_ref.dtype)
        dx, dgain = _rms_bwd(dh, inv, xhat, g_ref[...])
        dg_ref[...] += dgain
        dprev_ref[...] = dres_ref[...] + dx

    return pl.pallas_call(
        body, name="inproj_bwd", grid=(lp // tm,),
        in_specs=[_rows(tm, D_MODEL), _rows(tm, D_MODEL), pl.BlockSpec((1, D_MODEL), lambda i: (0, 0)),
                  _resident((D_MODEL, IN_PAD))] + [_rows(tm, wd) for wd in IN_WIDTHS],
        out_specs=[_rows(tm, D_MODEL), _rows(tm, D_MODEL), pl.BlockSpec((1, D_MODEL), lambda i: (0, 0))],
        out_shape=[jax.ShapeDtypeStruct((lp, D_MODEL), F32), jax.ShapeDtypeStruct((lp, D_MODEL), BF16),
                   jax.ShapeDtypeStruct((1, D_MODEL), F32)],
        compiler_params=_cp(("arbitrary",)),
    )(dres, x, gain, w_pad, *dparts)


def _weight_grads(a, bs, name):
    lp, m = a.shape
    tk = _row_tile(lp)
    nb = len(bs)

    def body(a_ref, *rest):
        b_refs, o_refs = rest[:nb], rest[nb:]

        @pl.when(pl.program_id(0) == 0)
        def _():
            for o_ref in o_refs:
                o_ref[...] = jnp.zeros(o_ref.shape, F32)

        ab = a_ref[...].astype(BF16)
        for b_ref, o_ref in zip(b_refs, o_refs):
            o_ref[...] += _tn(ab, b_ref[...].astype(BF16))

    return pl.pallas_call(
        body, name=name, grid=(lp // tk,),
        in_specs=[_rows(tk, m)] + [_rows(tk, b.shape[1]) for b in bs],
        out_specs=[pl.BlockSpec((m, b.shape[1]), lambda i: (0, 0)) for b in bs],
        out_shape=[jax.ShapeDtypeStruct((m, b.shape[1]), F32) for b in bs],
        compiler_params=_cp(("arbitrary",)),
    )(a, *bs)


HBM_SPEC = pl.BlockSpec(memory_space=pltpu.HBM)


def _my_place():
    return lax.axis_index("x"), lax.axis_index("y"), lax.axis_index("c")


def _other_chips(x, y):
    return [(1 - x, y), (x, 1 - y), (1 - x, 1 - y)]


class _ChipExchange:
    def __init__(self, arrs, scatter):
        self.n = len(arrs)
        self.scatter = scatter
        self.out_shape = [jax.ShapeDtypeStruct(a.shape if scatter else (N_CHIPS,) + a.shape, a.dtype) for a in arrs]
        self.specs = [HBM_SPEC] * self.n
        self.sems = [pltpu.SemaphoreType.DMA((3 * self.n,)), pltpu.SemaphoreType.DMA((3 * self.n,)),
                     pltpu.SemaphoreType.DMA((self.n,))]

    def _copies(self, ins, outs, sems):
        send_sems, recv_sems, local_sems = sems
        x, y, c = _my_place()
        me = 2 * x + y
        chips = _other_chips(x, y)
        mine = lambda a: ins[a].at[me] if self.scatter else ins[a]

        def remote(a, k, arriving):
            px, py = chips[k]
            there = 2 * px + py
            return pltpu.make_async_remote_copy(
                src_ref=mine(a) if arriving or not self.scatter else ins[a].at[there],
                dst_ref=outs[a].at[there if arriving else me],
                send_sem=send_sems.at[a * 3 + k], recv_sem=recv_sems.at[a * 3 + k],
                device_id=(px, py, c), device_id_type=MESH)

        pairs = [(a, k) for a in range(self.n) for k in range(3)]
        local = [pltpu.make_async_copy(mine(a), outs[a].at[me], local_sems.at[a]) for a in range(self.n)]
        return local, [remote(a, k, False) for a, k in pairs], [remote(a, k, True) for a, k in pairs]

    def start(self, ins, outs, sems):
        local, sends, _ = self._copies(ins, outs, sems)
        for cp in local + sends:
            cp.start()

    def wait(self, ins, outs, sems):
        local, sends, arrivals = self._copies(ins, outs, sems)
        for cp in arrivals:
            cp.wait_recv()
        for cp in sends:
            cp.wait_send()
        for cp in local:
            cp.wait()


def _chip_exchange(arrs, scatter, name):
    ex = _ChipExchange(arrs, scatter)

    def body(*refs):
        ins, outs, sems = refs[:ex.n], refs[ex.n:2 * ex.n], refs[2 * ex.n:]
        ex.start(ins, outs, sems)
        ex.wait(ins, outs, sems)

    return pl.pallas_call(body, name=name, in_specs=ex.specs, out_specs=ex.specs, out_shape=ex.out_shape,
                          scratch_shapes=ex.sems)(*arrs)


def _sibling_exchange(arrs, name):
    n = len(arrs)

    def body(*refs):
        ins, outs = refs[:n], refs[n:2 * n]
        send_sems, recv_sems = refs[2 * n:]
        x, y, c = _my_place()
        cps = [pltpu.make_async_remote_copy(src_ref=ins[a], dst_ref=outs[a], send_sem=send_sems.at[a],
                                            recv_sem=recv_sems.at[a], device_id=(x, y, 1 - c), device_id_type=MESH)
               for a in range(n)]
        for cp in cps:
            cp.start()
        for cp in cps:
            cp.wait_recv()
        for cp in cps:
            cp.wait_send()

    return pl.pallas_call(
        body, name=name, in_specs=[HBM_SPEC] * n, out_specs=[HBM_SPEC] * n,
        out_shape=[jax.ShapeDtypeStruct(a.shape, a.dtype) for a in arrs],
        scratch_shapes=[pltpu.SemaphoreType.DMA((n,)), pltpu.SemaphoreType.DMA((n,))],
    )(*arrs)


def _all_reduce_small(pack):
    rows = pack.shape[0]

    def body(p_ref, o_ref, g_scr, send_sems, recv_sems):
        x, y, c = _my_place()
        me = 4 * x + 2 * y + c
        flips = [(dx, dy, dc) for dx in (0, 1) for dy in (0, 1) for dc in (0, 1) if (dx, dy, dc) != (0, 0, 0)]

        def peer(f):
            return (x if f[0] == 0 else 1 - x, y if f[1] == 0 else 1 - y, c if f[2] == 0 else 1 - c)

        def copy(k, slot):
            return pltpu.make_async_remote_copy(src_ref=p_ref, dst_ref=g_scr.at[slot], send_sem=send_sems.at[k],
                                                recv_sem=recv_sems.at[k], device_id=peer(flips[k]), device_id_type=MESH)

        sends = [copy(k, me) for k in range(len(flips))]
        for cp in sends:
            cp.start()
        g_scr[me] = p_ref[...]
        for k, f in enumerate(flips):
            px, py, pc = peer(f)
            copy(k, 4 * px + 2 * py + pc).wait_recv()
        for cp in sends:
            cp.wait_send()
        acc = g_scr[0]
        for d in range(1, N_DEV):
            acc = acc + g_scr[d]
        o_ref[...] = acc

    vm = pl.BlockSpec(memory_space=pltpu.VMEM)
    return pl.pallas_call(
        body, name="all_reduce_small", in_specs=[vm], out_specs=vm,
        out_shape=jax.ShapeDtypeStruct(pack.shape, F32),
        scratch_shapes=[pltpu.VMEM((N_DEV, rows, 128), F32), pltpu.SemaphoreType.DMA((N_DEV - 1,)),
                        pltpu.SemaphoreType.DMA((N_DEV - 1,))],
        compiler_params=_cp(),
    )(pack)


def _as3d(a):
    return a.reshape((-1,) + a.shape[-2:])


def _row_block(r):
    for t in (256, 192, 128, 64, 32, 16, 8):
        if r % t == 0:
            return t
    return r


def _sum_pieces(pieces, name):
    _, na, r, c = pieces.shape
    rt = _row_block(r)

    def body(p_ref, o_ref):
        acc = p_ref[0, 0].astype(F32)
        for s in range(1, N_CHIPS):
            acc = acc + p_ref[s, 0].astype(F32)
        o_ref[0] = acc

    return pl.pallas_call(
        body, name=name, grid=(na, r // rt),
        in_specs=[pl.BlockSpec((N_CHIPS, 1, rt, c), lambda a, i: (0, a, i, 0))],
        out_specs=pl.BlockSpec((1, rt, c), lambda a, i: (a, i, 0)),
        out_shape=jax.ShapeDtypeStruct((na, r, c), F32),
        compiler_params=_cp(("parallel", "parallel")),
    )(pieces)


def _adamw(w, g_parts, m, v, name):
    na, r, c = w.shape
    rt = _row_block(r)
    ng = len(g_parts)

    def body(w_ref, *rest):
        g_refs = rest[:ng]
        m_ref, v_ref, g_out, d_out, m_out, v_out = rest[ng:]
        g = g_refs[0][...]
        for gr in g_refs[1:]:
            g = g + gr[...]
        m_new = ADAM_B1 * m_ref[...] + (1.0 - ADAM_B1) * g
        v_new = ADAM_B2 * v_ref[...] + (1.0 - ADAM_B2) * (g * g)
        m_hat = m_new / (1.0 - ADAM_B1 ** ADAM_STEP)
        v_hat = v_new / (1.0 - ADAM_B2 ** ADAM_STEP)
        g_out[...] = g
        d_out[...] = -ADAM_LR * (m_hat / (jnp.sqrt(v_hat) + ADAM_EPS) + ADAM_WD * w_ref[...])
        m_out[...] = m_new
        v_out[...] = v_new

    spec = pl.BlockSpec((1, rt, c), lambda a, i: (a, i, 0))
    out = jax.ShapeDtypeStruct((na, r, c), F32)
    return pl.pallas_call(
        body, name=name, grid=(na, r // rt), in_specs=[spec] * (3 + ng), out_specs=[spec] * 4, out_shape=[out] * 4,
        compiler_params=_cp(("parallel", "parallel")),
    )(w, *g_parts, m, v)


def _cols_from_shards(g):
    g = jnp.moveaxis(g, 0, -2)
    return g.reshape(g.shape[:-2] + (g.shape[-2] * g.shape[-1],))


def _rows_from_shards(g):
    g = jnp.moveaxis(g, 0, -3)
    return g.reshape(g.shape[:-3] + (g.shape[-3] * g.shape[-2], g.shape[-1]))


def _cols_to_shards(w):
    w = w.reshape(w.shape[:-1] + (N_CHIPS, w.shape[-1] // N_CHIPS))
    return jnp.moveaxis(w, -2, 0)


def _rows_to_shards(w):
    w = w.reshape(w.shape[:-2] + (N_CHIPS, w.shape[-2] // N_CHIPS, w.shape[-1]))
    return jnp.moveaxis(w, -3, 0)


def _pad_w_in(w):
    z = lambda n: jnp.zeros(w.shape[:-1] + (n,), w.dtype)
    return jnp.concatenate([w[..., :2048], w[..., 2080:4640], z(KR_LANE0), w[..., 2048:2080], z(HEAD_PAD - QK_DIM)], axis=-1)


def _unpad_w_in(parts):
    u, zp, cq, ckv, zm, gp, gm, kr = parts
    return jnp.concatenate([u, zp, cq, ckv, kr[:, KR_LANE0:QK_DIM], zm, gp, gm], axis=-1)


def _pad_heads(w, real):
    w = w.reshape(w.shape[:-1] + (N_HEADS, real))
    w = jnp.pad(w, [(0, 0)] * (w.ndim - 1) + [(0, HEAD_PAD - real)])
    return w.reshape(w.shape[:-2] + (N_HEADS * HEAD_PAD,))


def _flat_rows(a):
    a = a.reshape(-1)
    return jnp.pad(a, (0, (-a.shape[0]) % (8 * 128))).reshape(-1, 128)


def kernel(x, positions, meta_tokens, norm_gain, w_in, pool_w_group, pool_scale, pool_w_up, q_a_norm_gain, kv_a_norm_gain, w_q_b, w_kv_b, q_norm_gain, k_norm_gain, mla_w_up, w_out, loss_target, m_meta_tokens, m_norm_gain, m_w_in, m_pool_w_group, m_pool_scale, m_pool_w_up, m_q_a_norm_gain, m_kv_a_norm_gain, m_w_q_b, m_w_kv_b, m_q_norm_gain, m_k_norm_gain, m_mla_w_up, m_w_out, v_meta_tokens, v_norm_gain, v_w_in, v_pool_w_group, v_pool_scale, v_pool_w_up, v_q_a_norm_gain, v_kv_a_norm_gain, v_w_q_b, v_w_kv_b, v_q_norm_gain, v_k_norm_gain, v_mla_w_up, v_w_out):
    seq = x.shape[1]
    lp = -(-(ROW0 + seq) // ATTN_TILE) * ATTN_TILE
    pad_back = lp - ROW0 - seq
    chip = 2 * lax.axis_index("x") + lax.axis_index("y")

    big = dict(w_in=w_in, pool_w_up=pool_w_up, w_q_b=w_q_b, w_kv_b=w_kv_b, mla_w_up=mla_w_up, w_out=w_out)
    row_sharded = ("w_q_b", "w_out")
    names = list(big)
    shards = [[big[n][l].astype(BF16) for n in names] for l in range(DEPTH)]
    from_shards = lambda n: _rows_from_shards if n in row_sharded else _cols_from_shards
    to_shards = lambda n: _rows_to_shards if n in row_sharded else _cols_to_shards

    def layer_weights(gathered):
        w = {n: from_shards(n)(g) for n, g in zip(names, gathered)}
        wkv = w["w_kv_b"].reshape(KV_RANK, N_HEADS, NOPE + V_DIM)
        return dict(w_pad=_pad_w_in(w["w_in"]), wq=_pad_heads(w["w_q_b"], QK_DIM),
                    wkn=_pad_heads(wkv[..., :NOPE].reshape(KV_RANK, N_HEADS * NOPE), NOPE),
                    wv=wkv[..., NOPE:].reshape(KV_RANK, MLA_WIDTH),
                    wpu=w["pool_w_up"], wmu=w["mla_w_up"], wout=w["w_out"])

    *first, meta_g = _chip_exchange(shards[0] + [meta_tokens], scatter=False, name="gather_layer0")
    weights = [layer_weights(first)]
    meta_full = _cols_from_shards(meta_g)
    wg = pool_w_group.astype(BF16)
    gqn = jnp.pad(q_norm_gain, ((0, 0), (0, HEAD_PAD - QK_DIM)))
    gkn = jnp.pad(k_norm_gain, ((0, 0), (0, HEAD_PAD - QK_DIM)))

    x_pad = jnp.concatenate([jnp.zeros((PAD_FRONT, D_MODEL), F32), meta_full, x[0], jnp.zeros((pad_back, D_MODEL), F32)], axis=0)
    t_pad = jnp.concatenate([jnp.zeros((ROW0, D_MODEL), F32), loss_target[0], jnp.zeros((pad_back, D_MODEL), F32)], axis=0)
    pos_pad = jnp.concatenate([jnp.zeros((PAD_FRONT,), jnp.int32), jnp.arange(N_META, dtype=jnp.int32),
                               positions[0] + N_META, jnp.zeros((pad_back,), jnp.int32)])
    half = ROPE // 2
    inv_freq = (ROPE_THETA ** (-np.arange(half, dtype=np.float32) / half)).astype(np.float32)
    freq_row = np.zeros((1, HEAD_PAD), np.float32)
    freq_row[0, NOPE:NOPE + half] = inv_freq
    freq_row[0, NOPE + half:QK_DIM] = inv_freq
    tabs = _rope_tables(pos_pad[:, None], jnp.asarray(freq_row))

    row = lambda a, l: a[l][None, :]

    saved = []
    h_res = x_pad
    for l in range(DEPTH):
        w = weights[l]
        u, zp, cq, ckv, zm, gp, gm, kr = _inproj_fwd(h_res, row(norm_gain, l), w["w_pad"])
        a_pool = _pool_fwd(u, zp, wg[l], row(pool_scale, l))
        q, k, v = _mla_prep_fwd(cq, ckv, kr, tabs, row(q_a_norm_gain, l), row(kv_a_norm_gain, l), row(gqn, l), row(gkn, l),
                                w["wq"], w["wkn"], w["wv"])
        o, lse, nxt = _flash_fwd(q, k, v, gather=shards[l + 1] if l + 1 < DEPTH else ())
        if nxt:
            weights.append(layer_weights(nxt))
        h_next, yp, ym = _merge_fwd(h_res, a_pool, o, zm, gp, gm, w["wpu"], w["wmu"], w["wout"])
        saved.append(dict(x=h_res, u=u, zp=zp, cq=cq, ckv=ckv, zm=zm, gp=gp, gm=gm, kr=kr, a_pool=a_pool, q=q, k=k, v=v,
                          o=o, lse=lse, yp=yp, ym=ym))
        h_res = h_next
    dres, loss_blk = _loss_head(h_res, t_pad)

    gw = {n: [None] * DEPTH for n in names}
    pieces = [None] * DEPTH
    grad_stacks = lambda l: [to_shards(n)(gw[n][l]).astype(BF16) for n in names]
    gs = {n: [None] * DEPTH for n in ("norm_gain", "pool_w_group", "pool_scale", "q_a", "kv_a", "q_norm", "k_norm")}
    for l in reversed(range(DEPTH)):
        s, w = saved[l], weights[l]
        merged, dyp, dym, dgp, dgm, dap, amla, do, dzm, delta = _merge_bwd(
            dres, s["yp"], s["ym"], s["gp"], s["gm"], s["o"], s["zm"], w["wout"], w["wpu"], w["wmu"])
        (gw["w_out"][l],) = _weight_grads(merged, [dres], "grad_w_out")
        (gw["pool_w_up"][l],) = _weight_grads(s["a_pool"], [dyp], "grad_pool_w_up")
        (gw["mla_w_up"][l],) = _weight_grads(amla, [dym], "grad_mla_w_up")
        dq, dk, dv, got = _flash_bwd(s["q"], s["k"], s["v"], do, s["lse"], delta,
                                     scatter=grad_stacks(l + 1) if l + 1 < DEPTH else ())
        if got:
            pieces[l + 1] = got
        dcq, dckv, dkr, dwq, dwkn, dwv, gs["q_a"][l], gs["kv_a"][l], dgqn, dgkn = _mla_prep_bwd(
            dq, dk, dv, s["cq"], s["ckv"], s["kr"], tabs, row(q_a_norm_gain, l), row(kv_a_norm_gain, l), row(gqn, l), row(gkn, l),
            w["wq"], w["wkn"], w["wv"])
        gs["q_norm"][l] = dgqn[:, :QK_DIM]
        gs["k_norm"][l] = dgkn[:, :QK_DIM]
        gw["w_q_b"][l] = dwq.reshape(Q_RANK, N_HEADS, HEAD_PAD)[..., :QK_DIM].reshape(Q_RANK, N_HEADS * QK_DIM)
        gw["w_kv_b"][l] = jnp.concatenate([dwkn.reshape(KV_RANK, N_HEADS, HEAD_PAD)[..., :NOPE],
                                           dwv.reshape(KV_RANK, N_HEADS, V_DIM)], axis=-1).reshape(KV_RANK, N_HEADS * (NOPE + V_DIM))
        du, dzp, gs["pool_w_group"][l], gs["pool_scale"][l] = _pool_bwd(dap, s["u"], s["zp"], wg[l], row(pool_scale, l))
        dparts = [du, dzp, dcq, dckv, dzm, dgp, dgm, dkr]
        dres, h, gs["norm_gain"][l] = _inproj_bwd(dres, s["x"], row(norm_gain, l), w["w_pad"], dparts)
        ga = _weight_grads(h, [du, dzp, dcq, dckv, dkr], "grad_w_in_a")
        gb = _weight_grads(h, [dzm, dgp, dgm], "grad_w_in_b")
        gw["w_in"][l] = _unpad_w_in([ga[0], ga[1], ga[2], ga[3], gb[0], gb[1], gb[2], ga[4]])
    grad_x = dres[ROW0:ROW0 + seq][None]

    pieces[0] = _chip_exchange(grad_stacks(0), scatter=True, name="scatter_layer0")
    sums = [_sum_pieces(jnp.stack([pieces[l][a] for l in range(DEPTH)], axis=1), "sum_" + n) for a, n in enumerate(names)]
    other = _sibling_exchange(sums, name="swap_core_sums")
    moments = dict(w_in=(m_w_in, v_w_in), pool_w_up=(m_pool_w_up, v_pool_w_up), w_q_b=(m_w_q_b, v_w_q_b),
                   w_kv_b=(m_w_kv_b, v_w_kv_b), mla_w_up=(m_mla_w_up, v_mla_w_up), w_out=(m_w_out, v_w_out))
    big_out = {n: _adamw(big[n], [sm, ot], moments[n][0], moments[n][1], "adamw_" + n)
               for n, sm, ot in zip(names, sums, other)}

    small_names = ("norm_gain", "pool_w_group", "pool_scale", "q_a", "kv_a", "q_norm", "k_norm")
    small_w = dict(norm_gain=(norm_gain, m_norm_gain, v_norm_gain), pool_w_group=(pool_w_group, m_pool_w_group, v_pool_w_group),
                   pool_scale=(pool_scale, m_pool_scale, v_pool_scale), q_a=(q_a_norm_gain, m_q_a_norm_gain, v_q_a_norm_gain),
                   kv_a=(kv_a_norm_gain, m_kv_a_norm_gain, v_kv_a_norm_gain), q_norm=(q_norm_gain, m_q_norm_gain, v_q_norm_gain),
                   k_norm=(k_norm_gain, m_k_norm_gain, v_k_norm_gain))
    small_g = {n: jnp.stack(gs[n]).reshape(small_w[n][0].shape) for n in small_names}
    blocks = [_flat_rows(small_g[n]) for n in small_names]
    n_rows = [b.shape[0] for b in blocks]
    meta_rows = N_META * D_MODEL // 128
    pack = jnp.concatenate(blocks + [dres[PAD_FRONT:ROW0].reshape(meta_rows, 128), loss_blk], axis=0)
    pack = jnp.pad(pack, ((0, (-pack.shape[0]) % 8), (0, 0)))
    total = _all_reduce_small(pack)
    n_small = sum(n_rows)
    loss = total[n_small + meta_rows, 0]
    gmeta = lax.dynamic_slice_in_dim(total[n_small:n_small + meta_rows].reshape(N_META, D_MODEL), chip * (D_MODEL // N_CHIPS),
                                     D_MODEL // N_CHIPS, axis=1)

    def packed(idx, meta_part):
        p = jnp.concatenate([_flat_rows(small_w[n][idx]) for n in small_names] + [_flat_rows(meta_part)], axis=0)
        return jnp.pad(p, ((0, (-p.shape[0]) % 8), (0, 0)))[None]

    g_pack = jnp.concatenate([total[:n_small], _flat_rows(gmeta)], axis=0)
    g_pack = jnp.pad(g_pack, ((0, (-g_pack.shape[0]) % 8), (0, 0)))[None]
    small_out = _adamw(packed(0, meta_tokens), [g_pack], packed(1, m_meta_tokens), packed(2, v_meta_tokens), "adamw_small")

    def unpack(p):
        res, r0 = {}, 0
        for n, nr in zip(small_names, n_rows):
            shape = small_w[n][0].shape
            res[n] = p[0, r0:r0 + nr].reshape(-1)[:math.prod(shape)].reshape(shape)
            r0 += nr
        res["meta"] = p[0, r0:r0 + N_META * (D_MODEL // N_CHIPS) // 128].reshape(N_META, D_MODEL // N_CHIPS)
        return res

    small_res = [unpack(p) for p in small_out]

    def leaf(kind, name):
        key = {"meta_tokens": "meta", "q_a_norm_gain": "q_a", "kv_a_norm_gain": "kv_a", "q_norm_gain": "q_norm",
               "k_norm_gain": "k_norm"}.get(name, name)
        if name in big_out:
            return big_out[name][kind].reshape(big[name].shape)
        return small_res[kind][key]

    order = ("meta_tokens", "norm_gain", "w_in", "pool_w_group", "pool_scale", "pool_w_up", "q_a_norm_gain", "kv_a_norm_gain",
             "w_q_b", "w_kv_b", "q_norm_gain", "k_norm_gain", "mla_w_up", "w_out")
    outs = [loss, grad_x]
    for kind in range(4):
        outs += [leaf(kind, n) for n in order]
    return tuple(outs)
```

```python
import functools
import math

import numpy as np
import jax
import jax.numpy as jnp
from jax import lax
from jax.experimental import pallas as pl
from jax.experimental.pallas import tpu as pltpu

F32 = jnp.float32
BF16 = jnp.bfloat16
MESH = pl.DeviceIdType.MESH

D_MODEL = 1024
DEPTH = 4
N_META = 16
POOL_WIDTH = 512
POOL_WINDOWS = (2, 4, 8, 16)
POOL_GROUPS = 4
GROUP_DIM = 128
N_HEADS = 8
NOPE = 64
ROPE = 32
QK_DIM = 96
V_DIM = 64
MLA_WIDTH = 512
KV_RANK = 256
Q_RANK = 768
ROPE_THETA = 10000.0
EPS = 1e-6
MASK_VALUE = -1e30
ATTN_BLOCK = 128
PAD_FRONT = (-N_META) % ATTN_BLOCK
ROW0 = PAD_FRONT + N_META
HEAD_PAD = 128
HALO = 16
N_CHIPS = 4
N_DEV = 8

IN_NAMES = ("u", "zp", "cq", "ckv", "zm", "gp", "gm", "kr")
IN_WIDTHS = (512, 512, 768, 256, 512, 1024, 1024, 128)
IN_OFFS = tuple(int(v) for v in np.cumsum((0,) + IN_WIDTHS[:-1]))
IN_PAD = sum(IN_WIDTHS)
KR_LANE0 = NOPE

ADAM_LR = 0.001
ADAM_B1 = 0.9
ADAM_B2 = 0.999
ADAM_EPS = 1e-08
ADAM_WD = 0.01
ADAM_STEP = 10

VMEM_LIMIT = 56 * 1024 * 1024
ATTN_TILE = 768
ROW_TILES = (768, 384)
ROW_TILES_HEAVY = (384,)
ROW_BLOCK = 32
LOG2E = 1.4426950408889634
LN2 = 0.6931471805599453
Q_PRESCALE = LOG2E / math.sqrt(QK_DIM)


def _cp(sem=None, vmem=VMEM_LIMIT):
    kw = dict(vmem_limit_bytes=vmem)
    if sem is not None:
        kw["dimension_semantics"] = sem
    return pltpu.CompilerParams(**kw)


def _row_tile(n_rows, prefs=None):
    for t in prefs or ROW_TILES:
        if n_rows % t == 0:
            return t
    raise ValueError(f"no row tile for {n_rows}")


def _nt(a, b):
    return lax.dot_general(a, b, (((1,), (1,)), ((), ())), preferred_element_type=F32)


def _tn(a, b):
    return lax.dot_general(a, b, (((0,), (0,)), ((), ())), preferred_element_type=F32)


def _mm(a, b):
    return jnp.dot(a, b, preferred_element_type=F32)


def _sigmoid(x):
    return 0.5 * jnp.tanh(0.5 * x) + 0.5


def _resident(shape):
    nd = len(shape)
    return pl.BlockSpec(shape, lambda *_: (0,) * nd, pipeline_mode=pl.Buffered(1))


def _rows(tm, width):
    return pl.BlockSpec((tm, width), lambda i: (i, 0))


def _rope_tables(pos_col, inv_freq_row):
    lp = pos_col.shape[0]
    tm = _row_tile(lp)

    def body(p_ref, f_ref, c_ref, s1_ref, s2_ref):
        ang = p_ref[...].astype(F32) * f_ref[...]
        lane = lax.broadcasted_iota(jnp.int32, ang.shape, 1)
        cs = jnp.cos(ang)
        sn = jnp.sin(ang)
        c_ref[...] = jnp.where(lane < NOPE, 1.0, jnp.where(lane < QK_DIM, cs, 0.0))
        s1_ref[...] = jnp.where((lane >= NOPE) & (lane < NOPE + ROPE // 2), -sn, 0.0)
        s2_ref[...] = jnp.where((lane >= NOPE + ROPE // 2) & (lane < QK_DIM), sn, 0.0)

    out = jax.ShapeDtypeStruct((lp, HEAD_PAD), F32)
    return pl.pallas_call(
        body, name="rope_tables", grid=(lp // tm,),
        in_specs=[pl.BlockSpec((tm, 1), lambda i: (i, 0)), pl.BlockSpec((1, HEAD_PAD), lambda i: (0, 0))],
        out_specs=[_rows(tm, HEAD_PAD)] * 3, out_shape=[out] * 3,
        compiler_params=_cp(("parallel",)),
    )(pos_col, inv_freq_row)


def _rope(y, c, s1, s2):
    return y * c + pltpu.roll(y, HEAD_PAD - ROPE // 2, 1) * s1 + pltpu.roll(y, ROPE // 2, 1) * s2


def _rope_t(g, c, s1, s2):
    return g * c + pltpu.roll(g * s1, ROPE // 2, 1) + pltpu.roll(g * s2, HEAD_PAD - ROPE // 2, 1)


def _inproj_fwd(x, gain, w_pad):
    lp = x.shape[0]
    tm = _row_tile(lp)

    def body(x_ref, g_ref, w_ref, *outs):
        xf = x_ref[...]
        inv = lax.rsqrt(jnp.mean(xf * xf, axis=-1, keepdims=True) + EPS)
        h = (xf * inv * g_ref[...]).astype(BF16)
        for o_ref, off, wd in zip(outs, IN_OFFS, IN_WIDTHS):
            o_ref[...] = _mm(h, w_ref[:, off:off + wd]).astype(o_ref.dtype)

    return pl.pallas_call(
        body, name="inproj_fwd", grid=(lp // tm,),
        in_specs=[_rows(tm, D_MODEL), pl.BlockSpec((1, D_MODEL), lambda i: (0, 0)), _resident((D_MODEL, IN_PAD))],
        out_specs=[_rows(tm, wd) for wd in IN_WIDTHS],
        out_shape=[jax.ShapeDtypeStruct((lp, wd), BF16) for wd in IN_WIDTHS],
        compiler_params=_cp(("parallel",)),
    )(x, gain, w_pad)


def _inv_counts(tile_idx, tm):
    row = tile_idx * tm + lax.broadcasted_iota(jnp.int32, (tm, 1), 0)
    t1 = jnp.maximum(row - PAD_FRONT + 1, 1).astype(F32)
    return [1.0 / jnp.minimum(t1, float(w)) for w in POOL_WINDOWS]


def _trailing_sums(e):
    s2 = e + pltpu.roll(e, 1, 0)
    s4 = s2 + pltpu.roll(s2, 2, 0)
    s8 = s4 + pltpu.roll(s4, 4, 0)
    s16 = s8 + pltpu.roll(s8, 8, 0)
    return (s2, s4, s8, s16)


def _leading_sums(e):
    n = e.shape[0]
    s2 = e + pltpu.roll(e, n - 1, 0)
    s4 = s2 + pltpu.roll(s2, n - 2, 0)
    s8 = s4 + pltpu.roll(s4, n - 4, 0)
    s16 = s8 + pltpu.roll(s8, n - 8, 0)
    return (s2, s4, s8, s16)


def _pool_fwd(u, zp, wg, scale):
    lp = u.shape[0]
    tm = _row_tile(lp)

    def body(u_ref, z_ref, wg_ref, sc_ref, a_ref, ext_ref):
        i = pl.program_id(0)

        @pl.when(i == 0)
        def _():
            ext_ref[0:HALO, :] = jnp.zeros((HALO, POOL_WIDTH), F32)

        ext_ref[HALO:HALO + tm, :] = u_ref[...].astype(F32)
        e = ext_ref[...]
        sums = _trailing_sums(e)
        ext_ref[0:HALO, :] = e[tm:tm + HALO, :]
        inv_cnt = _inv_counts(i, tm)
        for g in range(POOL_GROUPS):
            cols = slice(g * GROUP_DIM, (g + 1) * GROUP_DIM)
            mixed = sums[g][HALO:, cols] * inv_cnt[g] - e[HALO:, cols]
            y = _mm(mixed.astype(BF16), wg_ref[g]) * sc_ref[:, cols]
            zf = z_ref[:, cols].astype(F32)
            a_ref[:, cols] = (y * (zf * _sigmoid(zf))).astype(a_ref.dtype)

    return pl.pallas_call(
        body, name="pool_fwd", grid=(lp // tm,),
        in_specs=[_rows(tm, POOL_WIDTH), _rows(tm, POOL_WIDTH),
                  pl.BlockSpec((POOL_GROUPS, GROUP_DIM, GROUP_DIM), lambda i: (0, 0, 0)),
                  pl.BlockSpec((1, POOL_WIDTH), lambda i: (0, 0))],
        out_specs=_rows(tm, POOL_WIDTH), out_shape=jax.ShapeDtypeStruct((lp, POOL_WIDTH), BF16),
        scratch_shapes=[pltpu.VMEM((HALO + tm, POOL_WIDTH), F32)],
        compiler_params=_cp(("arbitrary",)),
    )(u, zp, wg, scale)


def _rms_fwd(xf, gain):
    inv = lax.rsqrt(jnp.mean(xf * xf, axis=-1, keepdims=True) + EPS)
    xhat = xf * inv
    return inv, xhat, xhat * gain


def _rms_bwd(dy, inv, xhat, gain):
    dgain = jnp.sum(dy * xhat, axis=0, keepdims=True)
    dyg = dy * gain
    dx = inv * (dyg - xhat * jnp.mean(dyg * xhat, axis=-1, keepdims=True))
    return dx, dgain


def _head_norm_fwd(xh, gain128):
    inv = lax.rsqrt(jnp.sum(xh * xh, axis=-1, keepdims=True) * (1.0 / QK_DIM) + EPS)
    xhat = xh * inv
    return inv, xhat, xhat * gain128


def _head_norm_bwd(dy, inv, xhat, gain128):
    dyg = dy * gain128
    return inv * (dyg - xhat * (jnp.sum(dyg * xhat, axis=-1, keepdims=True) * (1.0 / QK_DIM)))


def _mla_prep_fwd(cq, ckv, kr, tabs, gqa, gkva, gqn, gkn, wq, wkn, wv):
    lp = cq.shape[0]
    tm = _row_tile(lp)

    def body(cq_ref, ckv_ref, kr_ref, c_ref, s1_ref, s2_ref, gqa_ref, gkva_ref, gqn_ref, gkn_ref,
             wq_ref, wkn_ref, wv_ref, q_ref, k_ref, v_ref):
        c, s1, s2 = c_ref[...], s1_ref[...], s2_ref[...]
        _, _, cqn = _rms_fwd(cq_ref[...].astype(F32), gqa_ref[...])
        qraw = _mm(cqn.astype(BF16), wq_ref[...])
        for h in range(N_HEADS):
            hb = slice(h * HEAD_PAD, (h + 1) * HEAD_PAD)
            _, _, yh = _head_norm_fwd(qraw[:, hb], gqn_ref[...])
            q_ref[:, hb] = (_rope(yh, c, s1, s2) * Q_PRESCALE).astype(q_ref.dtype)
        _, _, ckvn = _rms_fwd(ckv_ref[...].astype(F32), gkva_ref[...])
        ckvn_b = ckvn.astype(BF16)
        knraw = _mm(ckvn_b, wkn_ref[...])
        krs = kr_ref[...].astype(F32)
        for h in range(N_HEADS):
            hb = slice(h * HEAD_PAD, (h + 1) * HEAD_PAD)
            _, _, yh = _head_norm_fwd(knraw[:, hb] + krs, gkn_ref[...])
            k_ref[:, hb] = _rope(yh, c, s1, s2).astype(k_ref.dtype)
        v_ref[...] = _mm(ckvn_b, wv_ref[...]).astype(v_ref.dtype)

    hw = N_HEADS * HEAD_PAD
    vec = lambda n: pl.BlockSpec((1, n), lambda i: (0, 0))
    return pl.pallas_call(
        body, name="mla_prep_fwd", grid=(lp // tm,),
        in_specs=[_rows(tm, Q_RANK), _rows(tm, KV_RANK), _rows(tm, HEAD_PAD)] + [_rows(tm, HEAD_PAD)] * 3
        + [vec(Q_RANK), vec(KV_RANK), vec(HEAD_PAD), vec(HEAD_PAD),
           _resident((Q_RANK, hw)), _resident((KV_RANK, hw)), _resident((KV_RANK, MLA_WIDTH))],
        out_specs=[_rows(tm, hw), _rows(tm, hw), _rows(tm, MLA_WIDTH)],
        out_shape=[jax.ShapeDtypeStruct((lp, hw), BF16), jax.ShapeDtypeStruct((lp, hw), BF16),
                   jax.ShapeDtypeStruct((lp, MLA_WIDTH), BF16)],
        compiler_params=_cp(("parallel",)),
    )(cq, ckv, kr, *tabs, gqa, gkva, gqn, gkn, wq, wkn, wv)


def _causal_mask(s, q0, k0):
    qi = q0 + lax.broadcasted_iota(jnp.int32, s.shape, 0)
    ki = k0 + lax.broadcasted_iota(jnp.int32, s.shape, 1)
    return jnp.where((ki <= qi) & (ki >= PAD_FRONT), s, MASK_VALUE)


def _score_chunks(kind, r, tk):
    if kind == "inner":
        return [(c0, False) for c0 in range(0, tk, ATTN_BLOCK)]
    if kind == "first":
        return [(c0, c0 == 0) for c0 in range(0, tk, ATTN_BLOCK)]
    return [(c0, True) for c0 in range(0, min(tk, (r + 1) * ROW_BLOCK), ATTN_BLOCK)]


def _tile_kinds(i, t):
    return (("diag", t == i), ("first", (t == 0) & (i > 0)), ("inner", (t > 0) & (t < i)))


def _lanes(col, width=HEAD_PAD):
    return jnp.broadcast_to(col, (col.shape[0], width))


def _flash_fwd(q, k, v, gather=()):
    lp = q.shape[0]
    tq = tk = ATTN_TILE
    nq = lp // tq
    nj = N_HEADS // 2
    n_blocks = tq // ROW_BLOCK
    ex = _ChipExchange(list(gather), scatter=False)

    pairs = [(i, t) for i in range(nq) for t in range(i + 1)]
    i_tab = jnp.asarray([p[0] for p in pairs], jnp.int32)
    t_tab = jnp.asarray([p[1] for p in pairs], jnp.int32)

    def body(i_tab_ref, t_tab_ref, q_ref, k_ref, v_ref, *rest):
        ex_in, (o_ref, lse_ref), ex_out = rest[:ex.n], rest[ex.n:ex.n + 2], rest[ex.n + 2:2 * ex.n + 2]
        m_scr, acc_scr, s_scr, p_scr, part_scr, vext_scr = rest[2 * ex.n + 2:2 * ex.n + 8]
        ex_sems = rest[2 * ex.n + 8:]
        j, step_no = pl.program_id(0), pl.program_id(1)
        i, t = i_tab_ref[step_no], t_tab_ref[step_no]
        if ex.n:
            pl.when((j == 0) & (step_no == 0))(lambda: ex.start(ex_in, ex_out, ex_sems))

        @pl.when(t == 0)
        def _():
            m_scr[...] = jnp.full(m_scr.shape, MASK_VALUE, F32)
            acc_scr[...] = jnp.zeros(acc_scr.shape, F32)

        def step(kind):
            def scores(hh, r, c0, masked):
                s = s_scr[hh, r * ROW_BLOCK:(r + 1) * ROW_BLOCK, c0:c0 + ATTN_BLOCK]
                return _causal_mask(s, i * tq + r * ROW_BLOCK, t * tk + c0) if masked else s

            vext_scr[:, 0:HEAD_PAD] = v_ref[...]
            vext_scr[:, HEAD_PAD:2 * HEAD_PAD] = jnp.ones((tk, HEAD_PAD), BF16)
            for hh in range(2):
                hb = slice(hh * HEAD_PAD, (hh + 1) * HEAD_PAD)
                s_scr[hh] = _nt(q_ref[:, hb], k_ref[:, hb])
            for hh in range(2):
                for r in range(n_blocks):
                    part = None
                    for c0, masked in _score_chunks(kind, r, tk):
                        s = scores(hh, r, c0, masked)
                        part = s if part is None else jnp.maximum(part, s)
                    part_scr[r * ROW_BLOCK:(r + 1) * ROW_BLOCK, :] = part
                m_prev = m_scr[hh]
                m_new = jnp.maximum(m_prev, _lanes(jnp.max(part_scr[...], axis=-1, keepdims=True)))
                alpha = jnp.exp2(m_prev - m_new)
                m_scr[hh] = m_new
                for r in range(n_blocks):
                    rows = slice(r * ROW_BLOCK, (r + 1) * ROW_BLOCK)
                    m_r = m_scr[hh, rows, :]
                    chunks = _score_chunks(kind, r, tk)
                    for c0, masked in chunks:
                        p_scr[hh, rows, c0:c0 + ATTN_BLOCK] = jnp.exp2((scores(hh, r, c0, masked) - m_r).astype(BF16))
                    done = chunks[-1][0] + ATTN_BLOCK
                    if done < tk:
                        p_scr[hh, rows, done:tk] = jnp.zeros((ROW_BLOCK, tk - done), BF16)
                acc_scr[hh] = jnp.concatenate([alpha, alpha], axis=1) * acc_scr[hh] + _mm(p_scr[hh], vext_scr[...])

        for kind, pred in _tile_kinds(i, t):
            pl.when(pred)(functools.partial(step, kind))

        @pl.when(t == i)
        def _():
            lane = lax.broadcasted_iota(jnp.int32, (tq, HEAD_PAD), 1)
            l0, l1 = acc_scr[0, :, HEAD_PAD:2 * HEAD_PAD], acc_scr[1, :, HEAD_PAD:2 * HEAD_PAD]
            o = jnp.where(lane < V_DIM, acc_scr[0, :, 0:HEAD_PAD] / l0, acc_scr[1, :, 0:HEAD_PAD] / l1)
            o_ref[...] = o.astype(o_ref.dtype)
            lse_ref[...] = jnp.where(lane < V_DIM, m_scr[0] + jnp.log(l0) * LOG2E, m_scr[1] + jnp.log(l1) * LOG2E)

        if ex.n:
            pl.when((j == nj - 1) & (step_no == len(pairs) - 1))(lambda: ex.wait(ex_in, ex_out, ex_sems))

    q_idx = lambda j, p, it, tt: (it[p], j)
    kv_idx = lambda j, p, it, tt: (tt[p], j)
    o, lse, *gathered = pl.pallas_call(
        body, name="flash_fwd_gather" if ex.n else "flash_fwd",
        grid_spec=pltpu.PrefetchScalarGridSpec(
            num_scalar_prefetch=2, grid=(nj, len(pairs)),
            in_specs=[pl.BlockSpec((tq, 2 * HEAD_PAD), q_idx), pl.BlockSpec((tk, 2 * HEAD_PAD), kv_idx),
                      pl.BlockSpec((tk, HEAD_PAD), kv_idx)] + ex.specs,
            out_specs=[pl.BlockSpec((tq, HEAD_PAD), q_idx)] * 2 + ex.specs,
            scratch_shapes=[pltpu.VMEM((2, tq, HEAD_PAD), F32), pltpu.VMEM((2, tq, 2 * HEAD_PAD), F32),
                            pltpu.VMEM((2, tq, tk), F32), pltpu.VMEM((2, tq, tk), BF16),
                            pltpu.VMEM((tq, HEAD_PAD), F32), pltpu.VMEM((tk, 2 * HEAD_PAD), BF16)]
            + (ex.sems if ex.n else [])),
        out_shape=[jax.ShapeDtypeStruct((lp, MLA_WIDTH), BF16), jax.ShapeDtypeStruct((lp, MLA_WIDTH), F32)] + ex.out_shape,
        compiler_params=_cp(("arbitrary",) * 2),
    )(i_tab, t_tab, q, k, v, *gather)
    return o, lse, gathered


def _merge_fwd(x, a_pool, o, zm, gp, gm, wpu, wmu, wout):
    lp = x.shape[0]
    tm = _row_tile(lp)

    def body(x_ref, ap_ref, o_ref, zm_ref, gp_ref, gm_ref, wpu_ref, wmu_ref, wout_ref, xn_ref, yp_ref, ym_ref):
        yp = _mm(ap_ref[...], wpu_ref[...])
        zf = zm_ref[...].astype(F32)
        amla = o_ref[...].astype(F32) * (zf * _sigmoid(zf))
        ym = _mm(amla.astype(BF16), wmu_ref[...])
        merged = _sigmoid(gp_ref[...].astype(F32)) * yp + _sigmoid(gm_ref[...].astype(F32)) * ym
        xn_ref[...] = x_ref[...] + _mm(merged.astype(BF16), wout_ref[...])
        yp_ref[...] = yp.astype(yp_ref.dtype)
        ym_ref[...] = ym.astype(ym_ref.dtype)

    return pl.pallas_call(
        body, name="merge_fwd", grid=(lp // tm,),
        in_specs=[_rows(tm, D_MODEL), _rows(tm, POOL_WIDTH), _rows(tm, MLA_WIDTH), _rows(tm, MLA_WIDTH),
                  _rows(tm, D_MODEL), _rows(tm, D_MODEL),
                  _resident((POOL_WIDTH, D_MODEL)), _resident((MLA_WIDTH, D_MODEL)), _resident((D_MODEL, D_MODEL))],
        out_specs=[_rows(tm, D_MODEL)] * 3,
        out_shape=[jax.ShapeDtypeStruct((lp, D_MODEL), F32), jax.ShapeDtypeStruct((lp, D_MODEL), BF16),
                   jax.ShapeDtypeStruct((lp, D_MODEL), BF16)],
        compiler_params=_cp(("parallel",)),
    )(x, a_pool, o, zm, gp, gm, wpu, wmu, wout)


def _loss_head(y, target_pad):
    lp = y.shape[0]
    tm = _row_tile(lp)

    def body(y_ref, t_ref, d_ref, l_ref):
        i = pl.program_id(0)

        @pl.when(i == 0)
        def _():
            l_ref[...] = jnp.zeros(l_ref.shape, F32)

        row = i * tm + lax.broadcasted_iota(jnp.int32, (tm, 1), 0)
        err = jnp.where(row >= ROW0, y_ref[...] - t_ref[...], 0.0)
        d_ref[...] = err * (1.0 / D_MODEL)
        l_ref[...] += jnp.sum(err * err) * (0.5 / D_MODEL)

    return pl.pallas_call(
        body, name="loss_head", grid=(lp // tm,),
        in_specs=[_rows(tm, D_MODEL), _rows(tm, D_MODEL)],
        out_specs=[_rows(tm, D_MODEL), pl.BlockSpec((8, 128), lambda i: (0, 0))],
        out_shape=[jax.ShapeDtypeStruct((lp, D_MODEL), F32), jax.ShapeDtypeStruct((8, 128), F32)],
        compiler_params=_cp(("arbitrary",)),
    )(y, target_pad)


def _pair_rowsum(prod):
    lane = lax.broadcasted_iota(jnp.int32, prod.shape, 1)
    lo = jnp.sum(jnp.where(lane < V_DIM, prod, 0.0), axis=-1, keepdims=True)
    hi = jnp.sum(jnp.where(lane < V_DIM, 0.0, prod), axis=-1, keepdims=True)
    return jnp.where(lane < V_DIM, lo, hi)


def _merge_bwd(dres, yp, ym, gp, gm, o, zm, wout, wpu, wmu):
    lp = dres.shape[0]
    tm = _row_tile(lp, ROW_TILES_HEAVY)

    def body(dres_ref, yp_ref, ym_ref, gp_ref, gm_ref, o_ref, zm_ref, wout_ref, wpu_ref, wmu_ref,
             merged_ref, dyp_ref, dym_ref, dgp_ref, dgm_ref, dap_ref, amla_ref, do_ref, dzm_ref, delta_ref):
        dmerged = _nt(dres_ref[...].astype(BF16), wout_ref[...])
        sp = _sigmoid(gp_ref[...].astype(F32))
        sm = _sigmoid(gm_ref[...].astype(F32))
        ypf = yp_ref[...].astype(F32)
        ymf = ym_ref[...].astype(F32)
        merged_ref[...] = (sp * ypf + sm * ymf).astype(merged_ref.dtype)
        dyp = (dmerged * sp).astype(BF16)
        dym = (dmerged * sm).astype(BF16)
        dyp_ref[...] = dyp
        dym_ref[...] = dym
        dgp_ref[...] = (dmerged * ypf * sp * (1.0 - sp)).astype(dgp_ref.dtype)
        dgm_ref[...] = (dmerged * ymf * sm * (1.0 - sm)).astype(dgm_ref.dtype)
        dap_ref[...] = _nt(dyp, wpu_ref[...]).astype(dap_ref.dtype)
        dam = _nt(dym, wmu_ref[...])
        zf = zm_ref[...].astype(F32)
        sg = _sigmoid(zf)
        si = zf * sg
        of = o_ref[...].astype(F32)
        amla_ref[...] = (of * si).astype(amla_ref.dtype)
        do = dam * si
        do_ref[...] = do.astype(do_ref.dtype)
        dzm_ref[...] = (dam * of * (sg * (1.0 + zf * (1.0 - sg)))).astype(dzm_ref.dtype)
        prod = do * of
        for j in range(N_HEADS // 2):
            hb = slice(j * HEAD_PAD, (j + 1) * HEAD_PAD)
            delta_ref[:, hb] = _pair_rowsum(prod[:, hb])

    bf = lambda w: jax.ShapeDtypeStruct((lp, w), BF16)
    return pl.pallas_call(
        body, name="merge_bwd", grid=(lp // tm,),
        in_specs=[_rows(tm, D_MODEL)] * 5 + [_rows(tm, MLA_WIDTH)] * 2
        + [_resident((D_MODEL, D_MODEL)), _resident((POOL_WIDTH, D_MODEL)), _resident((MLA_WIDTH, D_MODEL))],
        out_specs=[_rows(tm, D_MODEL)] * 5 + [_rows(tm, POOL_WIDTH)] + [_rows(tm, MLA_WIDTH)] * 4,
        out_shape=[bf(D_MODEL)] * 5 + [bf(POOL_WIDTH)] + [bf(MLA_WIDTH)] * 3 + [jax.ShapeDtypeStruct((lp, MLA_WIDTH), F32)],
        compiler_params=_cp(("parallel",)),
    )(dres, yp, ym, gp, gm, o, zm, wout, wpu, wmu)


def _flash_bwd(q, k, v, do, lse, delta, scatter=()):
    lp = q.shape[0]
    tq = tk = ATTN_TILE
    nq = lp // tq
    nj = N_HEADS // 2
    scale = 1.0 / math.sqrt(QK_DIM)
    ex = _ChipExchange(list(scatter), scatter=True)

    pairs = [(t, i) for t in range(nq) for i in range(t, nq)]
    t_tab = jnp.asarray([p[0] for p in pairs], jnp.int32)
    i_tab = jnp.asarray([p[1] for p in pairs], jnp.int32)

    def body(t_tab_ref, i_tab_ref, q_ref, k_ref, v_ref, do_ref, lse_ref, dl_ref, *rest):
        ex_in, (dq_ref, dk_ref, dv_ref), ex_out = rest[:ex.n], rest[ex.n:ex.n + 3], rest[ex.n + 3:2 * ex.n + 3]
        dq_acc, dk_acc, dv_acc, s_scr, dp_scr, p_scr, ds_scr, doh_scr, stat_scr = rest[2 * ex.n + 3:2 * ex.n + 12]
        ex_sems = rest[2 * ex.n + 12:]
        j, step_no = pl.program_id(0), pl.program_id(1)
        t, i = t_tab_ref[step_no], i_tab_ref[step_no]
        if ex.n:
            pl.when((j == 0) & (step_no == 0))(lambda: ex.start(ex_in, ex_out, ex_sems))

        @pl.when(step_no == 0)
        def _():
            dq_acc[...] = jnp.zeros(dq_acc.shape, F32)

        @pl.when(i == t)
        def _():
            dk_acc[...] = jnp.zeros(dk_acc.shape, F32)
            dv_acc[...] = jnp.zeros(dv_acc.shape, F32)

        def step(kind):
            lane = lax.broadcasted_iota(jnp.int32, (tq, HEAD_PAD), 1)
            q_rows = pl.ds(pl.multiple_of(i * tq, tq), tq)
            for hh in range(2):
                hb = slice(hh * HEAD_PAD, (hh + 1) * HEAD_PAD)
                mine = (lane < V_DIM) if hh == 0 else (lane >= V_DIM)
                doh_scr[hh] = jnp.where(mine, do_ref[...], jnp.zeros_like(do_ref[...]))
                s_scr[hh] = _nt(q_ref[:, hb], k_ref[:, hb])
                dp_scr[hh] = _nt(doh_scr[hh], v_ref[...])
            for hh in range(2):
                hb = slice(hh * HEAD_PAD, (hh + 1) * HEAD_PAD)
                col = slice(hh * V_DIM, hh * V_DIM + 1)
                stat_scr[0] = _lanes(lse_ref[:, col])
                stat_scr[1] = _lanes(dl_ref[:, col])
                for r in range(tq // ROW_BLOCK):
                    rows = slice(r * ROW_BLOCK, (r + 1) * ROW_BLOCK)
                    lse_r = stat_scr[0, rows, :]
                    dl_r = stat_scr[1, rows, :]
                    chunks = _score_chunks(kind, r, tk)
                    for c0, masked in chunks:
                        cols = slice(c0, c0 + ATTN_BLOCK)
                        s = s_scr[hh, rows, cols]
                        if masked:
                            s = _causal_mask(s, i * tq + r * ROW_BLOCK, t * tk + c0)
                        p = jnp.exp2(s - lse_r)
                        p_scr[hh, rows, cols] = p.astype(BF16)
                        ds_scr[hh, rows, cols] = (p * (dp_scr[hh, rows, cols] - dl_r)).astype(BF16)
                    done = chunks[-1][0] + ATTN_BLOCK
                    if done < tk:
                        zeros = jnp.zeros((ROW_BLOCK, tk - done), BF16)
                        p_scr[hh, rows, done:tk] = zeros
                        ds_scr[hh, rows, done:tk] = zeros
                dv_acc[...] += _tn(p_scr[hh], doh_scr[hh])
                dk_acc[hh] += _tn(ds_scr[hh], q_ref[:, hb])
                dq_acc[q_rows, hb] += _mm(ds_scr[hh], k_ref[:, hb])

        for kind, pred in _tile_kinds(i, t):
            pl.when(pred)(functools.partial(step, kind))

        @pl.when(i == nq - 1)
        def _():
            dk_ref[:, 0:HEAD_PAD] = (dk_acc[0] * LN2).astype(dk_ref.dtype)
            dk_ref[:, HEAD_PAD:2 * HEAD_PAD] = (dk_acc[1] * LN2).astype(dk_ref.dtype)
            dv_ref[...] = dv_acc[...].astype(dv_ref.dtype)

        @pl.when(step_no == len(pairs) - 1)
        def _():
            dq_ref[...] = (dq_acc[...] * scale).astype(dq_ref.dtype)

        if ex.n:
            pl.when((j == nj - 1) & (step_no == len(pairs) - 1))(lambda: ex.wait(ex_in, ex_out, ex_sems))

    q_idx = lambda j, p, tt, it: (it[p], j)
    kv_idx = lambda j, p, tt, it: (tt[p], j)
    hw = N_HEADS * HEAD_PAD
    dq, dk, dv, *pieces = pl.pallas_call(
        body, name="flash_bwd_scatter" if ex.n else "flash_bwd",
        grid_spec=pltpu.PrefetchScalarGridSpec(
            num_scalar_prefetch=2, grid=(nj, len(pairs)),
            in_specs=[pl.BlockSpec((tq, 2 * HEAD_PAD), q_idx), pl.BlockSpec((tk, 2 * HEAD_PAD), kv_idx),
                      pl.BlockSpec((tk, HEAD_PAD), kv_idx), pl.BlockSpec((tq, HEAD_PAD), q_idx),
                      pl.BlockSpec((tq, HEAD_PAD), q_idx), pl.BlockSpec((tq, HEAD_PAD), q_idx)] + ex.specs,
            out_specs=[pl.BlockSpec((lp, 2 * HEAD_PAD), lambda j, p, tt, it: (0, j)),
                       pl.BlockSpec((tk, 2 * HEAD_PAD), kv_idx), pl.BlockSpec((tk, HEAD_PAD), kv_idx)] + ex.specs,
            scratch_shapes=[pltpu.VMEM((lp, 2 * HEAD_PAD), F32), pltpu.VMEM((2, tk, HEAD_PAD), F32),
                            pltpu.VMEM((tk, HEAD_PAD), F32),
                            pltpu.VMEM((2, tq, tk), F32), pltpu.VMEM((2, tq, tk), F32),
                            pltpu.VMEM((2, tq, tk), BF16), pltpu.VMEM((2, tq, tk), BF16),
                            pltpu.VMEM((2, tq, HEAD_PAD), BF16), pltpu.VMEM((2, tq, HEAD_PAD), F32)]
            + (ex.sems if ex.n else [])),
        out_shape=[jax.ShapeDtypeStruct((lp, hw), BF16), jax.ShapeDtypeStruct((lp, hw), BF16),
                   jax.ShapeDtypeStruct((lp, MLA_WIDTH), BF16)] + ex.out_shape,
        compiler_params=_cp(("arbitrary",) * 2),
    )(t_tab, i_tab, q, k, v, do, lse, delta, *scatter)
    return dq, dk, dv, pieces


def _mla_prep_bwd(dq, dk, dv, cq, ckv, kr, tabs, gqa, gkva, gqn, gkn, wq, wkn, wv):
    lp = cq.shape[0]
    tm = _row_tile(lp, ROW_TILES_HEAVY)
    hw = N_HEADS * HEAD_PAD

    def body(dq_ref, dk_ref, dv_ref, cq_ref, ckv_ref, kr_ref, c_ref, s1_ref, s2_ref, gqa_ref, gkva_ref, gqn_ref,
             gkn_ref, wq_ref, wkn_ref, wv_ref, dcq_ref, dckv_ref, dkr_ref, dwq_ref, dwkn_ref, dwv_ref,
             dgqa_ref, dgkva_ref, dgqn_ref, dgkn_ref, draw_scr):
        @pl.when(pl.program_id(0) == 0)
        def _():
            for r in (dwq_ref, dwkn_ref, dwv_ref, dgqa_ref, dgkva_ref, dgqn_ref, dgkn_ref):
                r[...] = jnp.zeros(r.shape, F32)

        c, s1, s2 = c_ref[...], s1_ref[...], s2_ref[...]
        lane = lax.broadcasted_iota(jnp.int32, (tm, HEAD_PAD), 1)

        inv_q, xhat_q, cqn = _rms_fwd(cq_ref[...].astype(F32), gqa_ref[...])
        cqn_b = cqn.astype(BF16)
        qraw = _mm(cqn_b, wq_ref[...])
        dgqn = jnp.zeros((1, HEAD_PAD), F32)
        for h in range(N_HEADS):
            hb = slice(h * HEAD_PAD, (h + 1) * HEAD_PAD)
            inv, xhat, _ = _head_norm_fwd(qraw[:, hb], gqn_ref[...])
            dy = _rope_t(dq_ref[:, hb].astype(F32), c, s1, s2)
            dgqn += jnp.sum(dy * xhat, axis=0, keepdims=True)
            draw_scr[:, hb] = _head_norm_bwd(dy, inv, xhat, gqn_ref[...]).astype(BF16)
        dgqn_ref[...] += dgqn
        dqraw = draw_scr[...]
        dwq_ref[...] += _tn(cqn_b, dqraw)
        dcq, dgqa = _rms_bwd(_nt(dqraw, wq_ref[...]), inv_q, xhat_q, gqa_ref[...])
        dcq_ref[...] = dcq.astype(dcq_ref.dtype)
        dgqa_ref[...] += dgqa

        inv_kv, xhat_kv, ckvn = _rms_fwd(ckv_ref[...].astype(F32), gkva_ref[...])
        ckvn_b = ckvn.astype(BF16)
        knraw = _mm(ckvn_b, wkn_ref[...])
        krs = kr_ref[...].astype(F32)
        dgkn = jnp.zeros((1, HEAD_PAD), F32)
        dkr = jnp.zeros((tm, HEAD_PAD), F32)
        for h in range(N_HEADS):
            hb = slice(h * HEAD_PAD, (h + 1) * HEAD_PAD)
            inv, xhat, _ = _head_norm_fwd(knraw[:, hb] + krs, gkn_ref[...])
            dy = _rope_t(dk_ref[:, hb].astype(F32), c, s1, s2)
            dgkn += jnp.sum(dy * xhat, axis=0, keepdims=True)
            dxh = _head_norm_bwd(dy, inv, xhat, gkn_ref[...])
            dkr += dxh
            draw_scr[:, hb] = jnp.where(lane < NOPE, dxh, 0.0).astype(BF16)
        dgkn_ref[...] += dgkn
        dkr_ref[...] = jnp.where((lane >= KR_LANE0) & (lane < QK_DIM), dkr, 0.0).astype(dkr_ref.dtype)
        dknraw = draw_scr[...]
        dvb = dv_ref[...]
        dwkn_ref[...] += _tn(ckvn_b, dknraw)
        dwv_ref[...] += _tn(ckvn_b, dvb)
        dckvn = _nt(dknraw, wkn_ref[...]) + _nt(dvb, wv_ref[...])
        dckv, dgkva = _rms_bwd(dckvn, inv_kv, xhat_kv, gkva_ref[...])
        dckv_ref[...] = dckv.astype(dckv_ref.dtype)
        dgkva_ref[...] += dgkva

    vec = lambda n: pl.BlockSpec((1, n), lambda i: (0, 0))
    whole = lambda r, c: pl.BlockSpec((r, c), lambda i: (0, 0))
    f = lambda r, c: jax.ShapeDtypeStruct((r, c), F32)
    return pl.pallas_call(
        body, name="mla_prep_bwd", grid=(lp // tm,),
        in_specs=[_rows(tm, hw), _rows(tm, hw), _rows(tm, MLA_WIDTH), _rows(tm, Q_RANK), _rows(tm, KV_RANK),
                  _rows(tm, HEAD_PAD)] + [_rows(tm, HEAD_PAD)] * 3
        + [vec(Q_RANK), vec(KV_RANK), vec(HEAD_PAD), vec(HEAD_PAD),
           _resident((Q_RANK, hw)), _resident((KV_RANK, hw)), _resident((KV_RANK, MLA_WIDTH))],
        out_specs=[_rows(tm, Q_RANK), _rows(tm, KV_RANK), _rows(tm, HEAD_PAD),
                   whole(Q_RANK, hw), whole(KV_RANK, hw), whole(KV_RANK, MLA_WIDTH),
                   vec(Q_RANK), vec(KV_RANK), vec(HEAD_PAD), vec(HEAD_PAD)],
        out_shape=[jax.ShapeDtypeStruct((lp, Q_RANK), BF16), jax.ShapeDtypeStruct((lp, KV_RANK), BF16),
                   jax.ShapeDtypeStruct((lp, HEAD_PAD), BF16),
                   f(Q_RANK, hw), f(KV_RANK, hw), f(KV_RANK, MLA_WIDTH),
                   f(1, Q_RANK), f(1, KV_RANK), f(1, HEAD_PAD), f(1, HEAD_PAD)],
        scratch_shapes=[pltpu.VMEM((tm, hw), BF16)],
        compiler_params=_cp(("arbitrary",)),
    )(dq, dk, dv, cq, ckv, kr, *tabs, gqa, gkva, gqn, gkn, wq, wkn, wv)


def _pool_bwd(dap, u, zp, wg, scale):
    lp = u.shape[0]
    tm = _row_tile(lp)
    n = lp // tm
    per = tm // HALO

    def body(dap_ref, u_ref, uh_ref, z_ref, wg_ref, sc_ref, du_ref, dz_ref, dwg_ref, dsc_ref, ext_u, ext_d):
        i = pl.program_id(0)
        r = n - 1 - i

        @pl.when(i == 0)
        def _():
            dwg_ref[...] = jnp.zeros(dwg_ref.shape, F32)
            dsc_ref[...] = jnp.zeros(dsc_ref.shape, F32)
            ext_d[tm:tm + HALO, :] = jnp.zeros((HALO, POOL_WIDTH), F32)

        ext_u[0:HALO, :] = jnp.where(r == 0, 0.0, uh_ref[...].astype(F32))
        ext_u[HALO:HALO + tm, :] = u_ref[...].astype(F32)
        e = ext_u[...]
        sums = _trailing_sums(e)
        inv_cnt = _inv_counts(r, tm)
        dmixed = []
        for g in range(POOL_GROUPS):
            cols = slice(g * GROUP_DIM, (g + 1) * GROUP_DIM)
            mixed_b = (sums[g][HALO:, cols] * inv_cnt[g] - e[HALO:, cols]).astype(BF16)
            yg = _mm(mixed_b, wg_ref[g])
            zf = z_ref[:, cols].astype(F32)
            sg = _sigmoid(zf)
            da = dap_ref[:, cols].astype(F32)
            dy = da * (zf * sg)
            dz_ref[:, cols] = (da * (yg * sc_ref[:, cols]) * (sg * (1.0 + zf * (1.0 - sg)))).astype(dz_ref.dtype)
            dsc_ref[:, cols] += jnp.sum(dy * yg, axis=0, keepdims=True)
            dyg = (dy * sc_ref[:, cols]).astype(BF16)
            dwg_ref[g] += _tn(mixed_b, dyg)
            dm = _nt(dyg, wg_ref[g])
            dmixed.append(dm)
            ext_d[0:tm, cols] = dm * inv_cnt[g]
        ed = ext_d[...]
        lead = _leading_sums(ed)
        ext_d[tm:tm + HALO, :] = ed[0:HALO, :]
        for g in range(POOL_GROUPS):
            cols = slice(g * GROUP_DIM, (g + 1) * GROUP_DIM)
            du_ref[:, cols] = (lead[g][0:tm, cols] - dmixed[g]).astype(du_ref.dtype)

    rev = lambda i: (n - 1 - i, 0)
    return pl.pallas_call(
        body, name="pool_bwd", grid=(n,),
        in_specs=[pl.BlockSpec((tm, POOL_WIDTH), rev), pl.BlockSpec((tm, POOL_WIDTH), rev),
                  pl.BlockSpec((HALO, POOL_WIDTH), lambda i: (jnp.maximum((n - 1 - i) * per - 1, 0), 0)),
                  pl.BlockSpec((tm, POOL_WIDTH), rev),
                  pl.BlockSpec((POOL_GROUPS, GROUP_DIM, GROUP_DIM), lambda i: (0, 0, 0)),
                  pl.BlockSpec((1, POOL_WIDTH), lambda i: (0, 0))],
        out_specs=[pl.BlockSpec((tm, POOL_WIDTH), rev), pl.BlockSpec((tm, POOL_WIDTH), rev),
                   pl.BlockSpec((POOL_GROUPS, GROUP_DIM, GROUP_DIM), lambda i: (0, 0, 0)),
                   pl.BlockSpec((1, POOL_WIDTH), lambda i: (0, 0))],
        out_shape=[jax.ShapeDtypeStruct((lp, POOL_WIDTH), BF16), jax.ShapeDtypeStruct((lp, POOL_WIDTH), BF16),
                   jax.ShapeDtypeStruct((POOL_GROUPS, GROUP_DIM, GROUP_DIM), F32),
                   jax.ShapeDtypeStruct((1, POOL_WIDTH), F32)],
        scratch_shapes=[pltpu.VMEM((HALO + tm, POOL_WIDTH), F32), pltpu.VMEM((tm + HALO, POOL_WIDTH), F32)],
        compiler_params=_cp(("arbitrary",)),
    )(dap, u, u, zp, wg, scale)


def _inproj_bwd(dres, x, gain, w_pad, dparts):
    lp = x.shape[0]
    tm = _row_tile(lp, ROW_TILES_HEAVY)

    def body(dres_ref, x_ref, g_ref, w_ref, *rest):
        dps = rest[:len(IN_WIDTHS)]
        dprev_ref, h_ref, dg_ref = rest[len(IN_WIDTHS):]

        @pl.when(pl.program_id(0) == 0)
        def _():
            dg_ref[...] = jnp.zeros(dg_ref.shape, F32)

        dh = jnp.zeros((tm, D_MODEL), F32)
        for dp_ref, off, wd in zip(dps, IN_OFFS, IN_WIDTHS):
            dh += _nt(dp_ref[...], w_ref[:, off:off + wd])
        inv, xhat, hn = _rms_fwd(x_ref[...], g_ref[...])
        h_ref[...] = hn.astype(h_ref.dtype)
        dx, dgain = _rms_bwd(dh, inv, xhat, g_ref[...])
        dg_ref[...] += dgain
        dprev_ref[...] = dres_ref[...] + dx

    return pl.pallas_call(
        body, name="inproj_bwd", grid=(lp // tm,),
        in_specs=[_rows(tm, D_MODEL), _rows(tm, D_MODEL), pl.BlockSpec((1, D_MODEL), lambda i: (0, 0)),
                  _resident((D_MODEL, IN_PAD))] + [_rows(tm, wd) for wd in IN_WIDTHS],
        out_specs=[_rows(tm, D_MODEL), _rows(tm, D_MODEL), pl.BlockSpec((1, D_MODEL), lambda i: (0, 0))],
        out_shape=[jax.ShapeDtypeStruct((lp, D_MODEL), F32), jax.ShapeDtypeStruct((lp, D_MODEL), BF16),
                   jax.ShapeDtypeStruct((1, D_MODEL), F32)],
        compiler_params=_cp(("arbitrary",)),
    )(dres, x, gain, w_pad, *dparts)


def _weight_grads(a, bs, name):
    lp, m = a.shape
    tk = _row_tile(lp)
    nb = len(bs)

    def body(a_ref, *rest):
        b_refs, o_refs = rest[:nb], rest[nb:]

        @pl.when(pl.program_id(0) == 0)
        def _():
            for o_ref in o_refs:
                o_ref[...] = jnp.zeros(o_ref.shape, F32)

        ab = a_ref[...].astype(BF16)
        for b_ref, o_ref in zip(b_refs, o_refs):
            o_ref[...] += _tn(ab, b_ref[...].astype(BF16))

    return pl.pallas_call(
        body, name=name, grid=(lp // tk,),
        in_specs=[_rows(tk, m)] + [_rows(tk, b.shape[1]) for b in bs],
        out_specs=[pl.BlockSpec((m, b.shape[1]), lambda i: (0, 0)) for b in bs],
        out_shape=[jax.ShapeDtypeStruct((m, b.shape[1]), F32) for b in bs],
        compiler_params=_cp(("arbitrary",)),
    )(a, *bs)


HBM_SPEC = pl.BlockSpec(memory_space=pltpu.HBM)


def _my_place():
    return lax.axis_index("x"), lax.axis_index("y"), lax.axis_index("c")


def _other_chips(x, y):
    return [(1 - x, y), (x, 1 - y), (1 - x, 1 - y)]


class _ChipExchange:
    def __init__(self, arrs, scatter):
        self.n = len(arrs)
        self.scatter = scatter
        self.out_shape = [jax.ShapeDtypeStruct(a.shape if scatter else (N_CHIPS,) + a.shape, a.dtype) for a in arrs]
        self.specs = [HBM_SPEC] * self.n
        self.sems = [pltpu.SemaphoreType.DMA((3 * self.n,)), pltpu.SemaphoreType.DMA((3 * self.n,)),
                     pltpu.SemaphoreType.DMA((self.n,))]

    def _copies(self, ins, outs, sems):
        send_sems, recv_sems, local_sems = sems
        x, y, c = _my_place()
        me = 2 * x + y
        chips = _other_chips(x, y)
        mine = lambda a: ins[a].at[me] if self.scatter else ins[a]

        def remote(a, k, arriving):
            px, py = chips[k]
            there = 2 * px + py
            return pltpu.make_async_remote_copy(
                src_ref=mine(a) if arriving or not self.scatter else ins[a].at[there],
                dst_ref=outs[a].at[there if arriving else me],
                send_sem=send_sems.at[a * 3 + k], recv_sem=recv_sems.at[a * 3 + k],
                device_id=(px, py, c), device_id_type=MESH)

        pairs = [(a, k) for a in range(self.n) for k in range(3)]
        local = [pltpu.make_async_copy(mine(a), outs[a].at[me], local_sems.at[a]) for a in range(self.n)]
        return local, [remote(a, k, False) for a, k in pairs], [remote(a, k, True) for a, k in pairs]

    def start(self, ins, outs, sems):
        local, sends, _ = self._copies(ins, outs, sems)
        for cp in local + sends:
            cp.start()

    def wait(self, ins, outs, sems):
        local, sends, arrivals = self._copies(ins, outs, sems)
        for cp in arrivals:
            cp.wait_recv()
        for cp in sends:
            cp.wait_send()
        for cp in local:
            cp.wait()


def _chip_exchange(arrs, scatter, name):
    ex = _ChipExchange(arrs, scatter)

    def body(*refs):
        ins, outs, sems = refs[:ex.n], refs[ex.n:2 * ex.n], refs[2 * ex.n:]
        ex.start(ins, outs, sems)
        ex.wait(ins, outs, sems)

    return pl.pallas_call(body, name=name, in_specs=ex.specs, out_specs=ex.specs, out_shape=ex.out_shape,
                          scratch_shapes=ex.sems)(*arrs)


def _sibling_exchange(arrs, name):
    n = len(arrs)

    def body(*refs):
        ins, outs = refs[:n], refs[n:2 * n]
        send_sems, recv_sems = refs[2 * n:]
        x, y, c = _my_place()
        cps = [pltpu.make_async_remote_copy(src_ref=ins[a], dst_ref=outs[a], send_sem=send_sems.at[a],
                                            recv_sem=recv_sems.at[a], device_id=(x, y, 1 - c), device_id_type=MESH)
               for a in range(n)]
        for cp in cps:
            cp.start()
        for cp in cps:
            cp.wait_recv()
        for cp in cps:
            cp.wait_send()

    return pl.pallas_call(
        body, name=name, in_specs=[HBM_SPEC] * n, out_specs=[HBM_SPEC] * n,
        out_shape=[jax.ShapeDtypeStruct(a.shape, a.dtype) for a in arrs],
        scratch_shapes=[pltpu.SemaphoreType.DMA((n,)), pltpu.SemaphoreType.DMA((n,))],
    )(*arrs)


def _all_reduce_small(pack):
    rows = pack.shape[0]

    def body(p_ref, o_ref, g_scr, send_sems, recv_sems):
        x, y, c = _my_place()
        me = 4 * x + 2 * y + c
        flips = [(dx, dy, dc) for dx in (0, 1) for dy in (0, 1) for dc in (0, 1) if (dx, dy, dc) != (0, 0, 0)]

        def peer(f):
            return (x if f[0] == 0 else 1 - x, y if f[1] == 0 else 1 - y, c if f[2] == 0 else 1 - c)

        def copy(k, slot):
            return pltpu.make_async_remote_copy(src_ref=p_ref, dst_ref=g_scr.at[slot], send_sem=send_sems.at[k],
                                                recv_sem=recv_sems.at[k], device_id=peer(flips[k]), device_id_type=MESH)

        sends = [copy(k, me) for k in range(len(flips))]
        for cp in sends:
            cp.start()
        g_scr[me] = p_ref[...]
        for k, f in enumerate(flips):
            px, py, pc = peer(f)
            copy(k, 4 * px + 2 * py + pc).wait_recv()
        for cp in sends:
            cp.wait_send()
        acc = g_scr[0]
        for d in range(1, N_DEV):
            acc = acc + g_scr[d]
        o_ref[...] = acc

    vm = pl.BlockSpec(memory_space=pltpu.VMEM)
    return pl.pallas_call(
        body, name="all_reduce_small", in_specs=[vm], out_specs=vm,
        out_shape=jax.ShapeDtypeStruct(pack.shape, F32),
        scratch_shapes=[pltpu.VMEM((N_DEV, rows, 128), F32), pltpu.SemaphoreType.DMA((N_DEV - 1,)),
                        pltpu.SemaphoreType.DMA((N_DEV - 1,))],
        compiler_params=_cp(),
    )(pack)


def _as3d(a):
    return a.reshape((-1,) + a.shape[-2:])


def _row_block(r):
    for t in (256, 192, 128, 64, 32, 16, 8):
        if r % t == 0:
            return t
    return r


def _sum_pieces(pieces, name):
    _, na, r, c = pieces.shape
    rt = _row_block(r)

    def body(p_ref, o_ref):
        acc = p_ref[0, 0].astype(F32)
        for s in range(1, N_CHIPS):
            acc = acc + p_ref[s, 0].astype(F32)
        o_ref[0] = acc

    return pl.pallas_call(
        body, name=name, grid=(na, r // rt),
        in_specs=[pl.BlockSpec((N_CHIPS, 1, rt, c), lambda a, i: (0, a, i, 0))],
        out_specs=pl.BlockSpec((1, rt, c), lambda a, i: (a, i, 0)),
        out_shape=jax.ShapeDtypeStruct((na, r, c), F32),
        compiler_params=_cp(("parallel", "parallel")),
    )(pieces)


def _adamw(w, g_parts, m, v, name):
    na, r, c = w.shape
    rt = _row_block(r)
    ng = len(g_parts)

    def body(w_ref, *rest):
        g_refs = rest[:ng]
        m_ref, v_ref, g_out, d_out, m_out, v_out = rest[ng:]
        g = g_refs[0][...]
        for gr in g_refs[1:]:
            g = g + gr[...]
        m_new = ADAM_B1 * m_ref[...] + (1.0 - ADAM_B1) * g
        v_new = ADAM_B2 * v_ref[...] + (1.0 - ADAM_B2) * (g * g)
        m_hat = m_new / (1.0 - ADAM_B1 ** ADAM_STEP)
        v_hat = v_new / (1.0 - ADAM_B2 ** ADAM_STEP)
        g_out[...] = g
        d_out[...] = -ADAM_LR * (m_hat / (jnp.sqrt(v_hat) + ADAM_EPS) + ADAM_WD * w_ref[...])
        m_out[...] = m_new
        v_out[...] = v_new

    spec = pl.BlockSpec((1, rt, c), lambda a, i: (a, i, 0))
    out = jax.ShapeDtypeStruct((na, r, c), F32)
    return pl.pallas_call(
        body, name=name, grid=(na, r // rt), in_specs=[spec] * (3 + ng), out_specs=[spec] * 4, out_shape=[out] * 4,
        compiler_params=_cp(("parallel", "parallel")),
    )(w, *g_parts, m, v)


def _cols_from_shards(g):
    g = jnp.moveaxis(g, 0, -2)
    return g.reshape(g.shape[:-2] + (g.shape[-2] * g.shape[-1],))


def _rows_from_shards(g):
    g = jnp.moveaxis(g, 0, -3)
    return g.reshape(g.shape[:-3] + (g.shape[-3] * g.shape[-2], g.shape[-1]))


def _cols_to_shards(w):
    w = w.reshape(w.shape[:-1] + (N_CHIPS, w.shape[-1] // N_CHIPS))
    return jnp.moveaxis(w, -2, 0)


def _rows_to_shards(w):
    w = w.reshape(w.shape[:-2] + (N_CHIPS, w.shape[-2] // N_CHIPS, w.shape[-1]))
    return jnp.moveaxis(w, -3, 0)


def _pad_w_in(w):
    z = lambda n: jnp.zeros(w.shape[:-1] + (n,), w.dtype)
    return jnp.concatenate([w[..., :2048], w[..., 2080:4640], z(KR_LANE0), w[..., 2048:2080], z(HEAD_PAD - QK_DIM)], axis=-1)


def _unpad_w_in(parts):
    u, zp, cq, ckv, zm, gp, gm, kr = parts
    return jnp.concatenate([u, zp, cq, ckv, kr[:, KR_LANE0:QK_DIM], zm, gp, gm], axis=-1)


def _pad_heads(w, real):
    w = w.reshape(w.shape[:-1] + (N_HEADS, real))
    w = jnp.pad(w, [(0, 0)] * (w.ndim - 1) + [(0, HEAD_PAD - real)])
    return w.reshape(w.shape[:-2] + (N_HEADS * HEAD_PAD,))


def _flat_rows(a):
    a = a.reshape(-1)
    return jnp.pad(a, (0, (-a.shape[0]) % (8 * 128))).reshape(-1, 128)


def kernel(x, positions, meta_tokens, norm_gain, w_in, pool_w_group, pool_scale, pool_w_up, q_a_norm_gain, kv_a_norm_gain, w_q_b, w_kv_b, q_norm_gain, k_norm_gain, mla_w_up, w_out, loss_target, m_meta_tokens, m_norm_gain, m_w_in, m_pool_w_group, m_pool_scale, m_pool_w_up, m_q_a_norm_gain, m_kv_a_norm_gain, m_w_q_b, m_w_kv_b, m_q_norm_gain, m_k_norm_gain, m_mla_w_up, m_w_out, v_meta_tokens, v_norm_gain, v_w_in, v_pool_w_group, v_pool_scale, v_pool_w_up, v_q_a_norm_gain, v_kv_a_norm_gain, v_w_q_b, v_w_kv_b, v_q_norm_gain, v_k_norm_gain, v_mla_w_up, v_w_out):
    seq = x.shape[1]
    lp = -(-(ROW0 + seq) // ATTN_TILE) * ATTN_TILE
    pad_back = lp - ROW0 - seq
    chip = 2 * lax.axis_index("x") + lax.axis_index("y")

    big = dict(w_in=w_in, pool_w_up=pool_w_up, w_q_b=w_q_b, w_kv_b=w_kv_b, mla_w_up=mla_w_up, w_out=w_out)
    row_sharded = ("w_q_b", "w_out")
    names = list(big)
    shards = [[big[n][l].astype(BF16) for n in names] for l in range(DEPTH)]
    from_shards = lambda n: _rows_from_shards if n in row_sharded else _cols_from_shards
    to_shards = lambda n: _rows_to_shards if n in row_sharded else _cols_to_shards

    def layer_weights(gathered):
        w = {n: from_shards(n)(g) for n, g in zip(names, gathered)}
        wkv = w["w_kv_b"].reshape(KV_RANK, N_HEADS, NOPE + V_DIM)
        return dict(w_pad=_pad_w_in(w["w_in"]), wq=_pad_heads(w["w_q_b"], QK_DIM),
                    wkn=_pad_heads(wkv[..., :NOPE].reshape(KV_RANK, N_HEADS * NOPE), NOPE),
                    wv=wkv[..., NOPE:].reshape(KV_RANK, MLA_WIDTH),
                    wpu=w["pool_w_up"], wmu=w["mla_w_up"], wout=w["w_out"])

    *first, meta_g = _chip_exchange(shards[0] + [meta_tokens], scatter=False, name="gather_layer0")
    weights = [layer_weights(first)]
    meta_full = _cols_from_shards(meta_g)
    wg = pool_w_group.astype(BF16)
    gqn = jnp.pad(q_norm_gain, ((0, 0), (0, HEAD_PAD - QK_DIM)))
    gkn = jnp.pad(k_norm_gain, ((0, 0), (0, HEAD_PAD - QK_DIM)))

    x_pad = jnp.concatenate([jnp.zeros((PAD_FRONT, D_MODEL), F32), meta_full, x[0], jnp.zeros((pad_back, D_MODEL), F32)], axis=0)
    t_pad = jnp.concatenate([jnp.zeros((ROW0, D_MODEL), F32), loss_target[0], jnp.zeros((pad_back, D_MODEL), F32)], axis=0)
    pos_pad = jnp.concatenate([jnp.zeros((PAD_FRONT,), jnp.int32), jnp.arange(N_META, dtype=jnp.int32),
                               positions[0] + N_META, jnp.zeros((pad_back,), jnp.int32)])
    half = ROPE // 2
    inv_freq = (ROPE_THETA ** (-np.arange(half, dtype=np.float32) / half)).astype(np.float32)
    freq_row = np.zeros((1, HEAD_PAD), np.float32)
    freq_row[0, NOPE:NOPE + half] = inv_freq
    freq_row[0, NOPE + half:QK_DIM] = inv_freq
    tabs = _rope_tables(pos_pad[:, None], jnp.asarray(freq_row))

    row = lambda a, l: a[l][None, :]

    saved = []
    h_res = x_pad
    for l in range(DEPTH):
        w = weights[l]
        u, zp, cq, ckv, zm, gp, gm, kr = _inproj_fwd(h_res, row(norm_gain, l), w["w_pad"])
        a_pool = _pool_fwd(u, zp, wg[l], row(pool_scale, l))
        q, k, v = _mla_prep_fwd(cq, ckv, kr, tabs, row(q_a_norm_gain, l), row(kv_a_norm_gain, l), row(gqn, l), row(gkn, l),
                                w["wq"], w["wkn"], w["wv"])
        o, lse, nxt = _flash_fwd(q, k, v, gather=shards[l + 1] if l + 1 < DEPTH else ())
        if nxt:
            weights.append(layer_weights(nxt))
        h_next, yp, ym = _merge_fwd(h_res, a_pool, o, zm, gp, gm, w["wpu"], w["wmu"], w["wout"])
        saved.append(dict(x=h_res, u=u, zp=zp, cq=cq, ckv=ckv, zm=zm, gp=gp, gm=gm, kr=kr, a_pool=a_pool, q=q, k=k, v=v,
                          o=o, lse=lse, yp=yp, ym=ym))
        h_res = h_next
    dres, loss_blk = _loss_head(h_res, t_pad)

    gw = {n: [None] * DEPTH for n in names}
    pieces = [None] * DEPTH
    grad_stacks = lambda l: [to_shards(n)(gw[n][l]).astype(BF16) for n in names]
    gs = {n: [None] * DEPTH for n in ("norm_gain", "pool_w_group", "pool_scale", "q_a", "kv_a", "q_norm", "k_norm")}
    for l in reversed(range(DEPTH)):
        s, w = saved[l], weights[l]
        merged, dyp, dym, dgp, dgm, dap, amla, do, dzm, delta = _merge_bwd(
            dres, s["yp"], s["ym"], s["gp"], s["gm"], s["o"], s["zm"], w["wout"], w["wpu"], w["wmu"])
        (gw["w_out"][l],) = _weight_grads(merged, [dres], "grad_w_out")
        (gw["pool_w_up"][l],) = _weight_grads(s["a_pool"], [dyp], "grad_pool_w_up")
        (gw["mla_w_up"][l],) = _weight_grads(amla, [dym], "grad_mla_w_up")
        dq, dk, dv, got = _flash_bwd(s["q"], s["k"], s["v"], do, s["lse"], delta,
                                     scatter=grad_stacks(l + 1) if l + 1 < DEPTH else ())
        if got:
            pieces[l + 1] = got
        dcq, dckv, dkr, dwq, dwkn, dwv, gs["q_a"][l], gs["kv_a"][l], dgqn, dgkn = _mla_prep_bwd(
            dq, dk, dv, s["cq"], s["ckv"], s["kr"], tabs, row(q_a_norm_gain, l), row(kv_a_norm_gain, l), row(gqn, l), row(gkn, l),
            w["wq"], w["wkn"], w["wv"])
        gs["q_norm"][l] = dgqn[:, :QK_DIM]
        gs["k_norm"][l] = dgkn[:, :QK_DIM]
        gw["w_q_b"][l] = dwq.reshape(Q_RANK, N_HEADS, HEAD_PAD)[..., :QK_DIM].reshape(Q_RANK, N_HEADS * QK_DIM)
        gw["w_kv_b"][l] = jnp.concatenate([dwkn.reshape(KV_RANK, N_HEADS, HEAD_PAD)[..., :NOPE],
                                           dwv.reshape(KV_RANK, N_HEADS, V_DIM)], axis=-1).reshape(KV_RANK, N_HEADS * (NOPE + V_DIM))
        du, dzp, gs["pool_w_group"][l], gs["pool_scale"][l] = _pool_bwd(dap, s["u"], s["zp"], wg[l], row(pool_scale, l))
        dparts = [du, dzp, dcq, dckv, dzm, dgp, dgm, dkr]
        dres, h, gs["norm_gain"][l] = _inproj_bwd(dres, s["x"], row(norm_gain, l), w["w_pad"], dparts)
        ga = _weight_grads(h, [du, dzp, dcq, dckv, dkr], "grad_w_in_a")
        gb = _weight_grads(h, [dzm, dgp, dgm], "grad_w_in_b")
        gw["w_in"][l] = _unpad_w_in([ga[0], ga[1], ga[2], ga[3], gb[0], gb[1], gb[2], ga[4]])
    grad_x = dres[ROW0:ROW0 + seq][None]

    pieces[0] = _chip_exchange(grad_stacks(0), scatter=True, name="scatter_layer0")
    sums = [_sum_pieces(jnp.stack([pieces[l][a] for l in range(DEPTH)], axis=1), "sum_" + n) for a, n in enumerate(names)]
    other = _sibling_exchange(sums, name="swap_core_sums")
    moments = dict(w_in=(m_w_in, v_w_in), pool_w_up=(m_pool_w_up, v_pool_w_up), w_q_b=(m_w_q_b, v_w_q_b),
                   w_kv_b=(m_w_kv_b, v_w_kv_b), mla_w_up=(m_mla_w_up, v_mla_w_up), w_out=(m_w_out, v_w_out))
    big_out = {n: _adamw(big[n], [sm, ot], moments[n][0], moments[n][1], "adamw_" + n)
               for n, sm, ot in zip(names, sums, other)}

    small_names = ("norm_gain", "pool_w_group", "pool_scale", "q_a", "kv_a", "q_norm", "k_norm")
    small_w = dict(norm_gain=(norm_gain, m_norm_gain, v_norm_gain), pool_w_group=(pool_w_group, m_pool_w_group, v_pool_w_group),
                   pool_scale=(pool_scale, m_pool_scale, v_pool_scale), q_a=(q_a_norm_gain, m_q_a_norm_gain, v_q_a_norm_gain),
                   kv_a=(kv_a_norm_gain, m_kv_a_norm_gain, v_kv_a_norm_gain), q_norm=(q_norm_gain, m_q_norm_gain, v_q_norm_gain),
                   k_norm=(k_norm_gain, m_k_norm_gain, v_k_norm_gain))
    small_g = {n: jnp.stack(gs[n]).reshape(small_w[n][0].shape) for n in small_names}
    blocks = [_flat_rows(small_g[n]) for n in small_names]
    n_rows = [b.shape[0] for b in blocks]
    meta_rows = N_META * D_MODEL // 128
    pack = jnp.concatenate(blocks + [dres[PAD_FRONT:ROW0].reshape(meta_rows, 128), loss_blk], axis=0)
    pack = jnp.pad(pack, ((0, (-pack.shape[0]) % 8), (0, 0)))
    total = _all_reduce_small(pack)
    n_small = sum(n_rows)
    loss = total[n_small + meta_rows, 0]
    gmeta = lax.dynamic_slice_in_dim(total[n_small:n_small + meta_rows].reshape(N_META, D_MODEL), chip * (D_MODEL // N_CHIPS),
                                     D_MODEL // N_CHIPS, axis=1)

    def packed(idx, meta_part):
        p = jnp.concatenate([_flat_rows(small_w[n][idx]) for n in small_names] + [_flat_rows(meta_part)], axis=0)
        return jnp.pad(p, ((0, (-p.shape[0]) % 8), (0, 0)))[None]

    g_pack = jnp.concatenate([total[:n_small], _flat_rows(gmeta)], axis=0)
    g_pack = jnp.pad(g_pack, ((0, (-g_pack.shape[0]) % 8), (0, 0)))[None]
    small_out = _adamw(packed(0, meta_tokens), [g_pack], packed(1, m_meta_tokens), packed(2, v_meta_tokens), "adamw_small")

    def unpack(p):
        res, r0 = {}, 0
        for n, nr in zip(small_names, n_rows):
            shape = small_w[n][0].shape
            res[n] = p[0, r0:r0 + nr].reshape(-1)[:math.prod(shape)].reshape(shape)
            r0 += nr
        res["meta"] = p[0, r0:r0 + N_META * (D_MODEL // N_CHIPS) // 128].reshape(N_META, D_MODEL // N_CHIPS)
        return res

    small_res = [unpack(p) for p in small_out]

    def leaf(kind, name):
        key = {"meta_tokens": "meta", "q_a_norm_gain": "q_a", "kv_a_norm_gain": "kv_a", "q_norm_gain": "q_norm",
               "k_norm_gain": "k_norm"}.get(name, name)
        if name in big_out:
            return big_out[name][kind].reshape(big[name].shape)
        return small_res[kind][key]

    order = ("meta_tokens", "norm_gain", "w_in", "pool_w_group", "pool_scale", "pool_w_up", "q_a_norm_gain", "kv_a_norm_gain",
             "w_q_b", "w_kv_b", "q_norm_gain", "k_norm_gain", "mla_w_up", "w_out")
    outs = [loss, grad_x]
    for kind in range(4):
        outs += [leaf(kind, n) for n in order]
    return tuple(outs)
```

```python
import functools
import math

import numpy as np
import jax
import jax.numpy as jnp
from jax import lax
from jax.experimental import pallas as pl
from jax.experimental.pallas import tpu as pltpu

F32 = jnp.float32
BF16 = jnp.bfloat16
MESH = pl.DeviceIdType.MESH

D_MODEL = 1024
DEPTH = 4
N_META = 16
POOL_WIDTH = 512
POOL_WINDOWS = (2, 4, 8, 16)
POOL_GROUPS = 4
GROUP_DIM = 128
N_HEADS = 8
NOPE = 64
ROPE = 32
QK_DIM = 96
V_DIM = 64
MLA_WIDTH = 512
KV_RANK = 256
Q_RANK = 768
ROPE_THETA = 10000.0
EPS = 1e-6
MASK_VALUE = -1e30
ATTN_BLOCK = 128
PAD_FRONT = (-N_META) % ATTN_BLOCK
ROW0 = PAD_FRONT + N_META
HEAD_PAD = 128
HALO = 16
N_CHIPS = 4
N_DEV = 8

IN_NAMES = ("u", "zp", "cq", "ckv", "zm", "gp", "gm", "kr")
IN_WIDTHS = (512, 512, 768, 256, 512, 1024, 1024, 128)
IN_OFFS = tuple(int(v) for v in np.cumsum((0,) + IN_WIDTHS[:-1]))
IN_PAD = sum(IN_WIDTHS)
KR_LANE0 = NOPE

ADAM_LR = 0.001
ADAM_B1 = 0.9
ADAM_B2 = 0.999
ADAM_EPS = 1e-08
ADAM_WD = 0.01
ADAM_STEP = 10

VMEM_LIMIT = 56 * 1024 * 1024
ATTN_TILE = 768
ROW_TILES = (768, 384)
ROW_TILES_HEAVY = (384,)
ROW_BLOCK = 32
ONES_ROWS = 16
LOG2E = 1.4426950408889634
LN2 = 0.6931471805599453
Q_PRESCALE = LOG2E / math.sqrt(QK_DIM)


def _cp(sem=None, vmem=VMEM_LIMIT):
    kw = dict(vmem_limit_bytes=vmem)
    if sem is not None:
        kw["dimension_semantics"] = sem
    return pltpu.CompilerParams(**kw)


def _row_tile(n_rows, prefs=None):
    for t in prefs or ROW_TILES:
        if n_rows % t == 0:
            return t
    raise ValueError(f"no row tile for {n_rows}")


def _nt(a, b):
    return lax.dot_general(a, b, (((1,), (1,)), ((), ())), preferred_element_type=F32)


def _tn(a, b):
    return lax.dot_general(a, b, (((0,), (0,)), ((), ())), preferred_element_type=F32)


def _mm(a, b):
    return jnp.dot(a, b, preferred_element_type=F32)


def _sigmoid(x):
    return 0.5 * jnp.tanh(0.5 * x) + 0.5


def _resident(shape):
    nd = len(shape)
    return pl.BlockSpec(shape, lambda *_: (0,) * nd, pipeline_mode=pl.Buffered(1))


def _rows(tm, width):
    return pl.BlockSpec((tm, width), lambda i: (i, 0))


def _rope_tables(pos_col, inv_freq_row):
    lp = pos_col.shape[0]
    tm = _row_tile(lp)

    def body(p_ref, f_ref, c_ref, s1_ref, s2_ref):
        ang = p_ref[...].astype(F32) * f_ref[...]
        lane = lax.broadcasted_iota(jnp.int32, ang.shape, 1)
        cs = jnp.cos(ang)
        sn = jnp.sin(ang)
        c_ref[...] = jnp.where(lane < NOPE, 1.0, jnp.where(lane < QK_DIM, cs, 0.0))
        s1_ref[...] = jnp.where((lane >= NOPE) & (lane < NOPE + ROPE // 2), -sn, 0.0)
        s2_ref[...] = jnp.where((lane >= NOPE + ROPE // 2) & (lane < QK_DIM), sn, 0.0)

    out = jax.ShapeDtypeStruct((lp, HEAD_PAD), F32)
    return pl.pallas_call(
        body, name="rope_tables", grid=(lp // tm,),
        in_specs=[pl.BlockSpec((tm, 1), lambda i: (i, 0)), pl.BlockSpec((1, HEAD_PAD), lambda i: (0, 0))],
        out_specs=[_rows(tm, HEAD_PAD)] * 3, out_shape=[out] * 3,
        compiler_params=_cp(("parallel",)),
    )(pos_col, inv_freq_row)


def _rope(y, c, s1, s2):
    return y * c + pltpu.roll(y, HEAD_PAD - ROPE // 2, 1) * s1 + pltpu.roll(y, ROPE // 2, 1) * s2


def _rope_t(g, c, s1, s2):
    return g * c + pltpu.roll(g * s1, ROPE // 2, 1) + pltpu.roll(g * s2, HEAD_PAD - ROPE // 2, 1)


def _inproj_fwd(x, gain, w_pad):
    lp = x.shape[0]
    tm = _row_tile(lp)

    def body(x_ref, g_ref, w_ref, *outs):
        xf = x_ref[...]
        inv = lax.rsqrt(jnp.mean(xf * xf, axis=-1, keepdims=True) + EPS)
        h = (xf * inv * g_ref[...]).astype(BF16)
        for o_ref, off, wd in zip(outs, IN_OFFS, IN_WIDTHS):
            o_ref[...] = _nt(h, w_ref[off:off + wd, :]).astype(o_ref.dtype)

    return pl.pallas_call(
        body, name="inproj_fwd", grid=(lp // tm,),
        in_specs=[_rows(tm, D_MODEL), pl.BlockSpec((1, D_MODEL), lambda i: (0, 0)), _resident((IN_PAD, D_MODEL))],
        out_specs=[_rows(tm, wd) for wd in IN_WIDTHS],
        out_shape=[jax.ShapeDtypeStruct((lp, wd), BF16) for wd in IN_WIDTHS],
        compiler_params=_cp(("parallel",)),
    )(x, gain, w_pad)


def _inv_counts(tile_idx, tm):
    row = tile_idx * tm + lax.broadcasted_iota(jnp.int32, (tm, 1), 0)
    t1 = jnp.maximum(row - PAD_FRONT + 1, 1).astype(F32)
    return [1.0 / jnp.minimum(t1, float(w)) for w in POOL_WINDOWS]


def _trailing_sums(e):
    s2 = e + pltpu.roll(e, 1, 0)
    s4 = s2 + pltpu.roll(s2, 2, 0)
    s8 = s4 + pltpu.roll(s4, 4, 0)
    s16 = s8 + pltpu.roll(s8, 8, 0)
    return (s2, s4, s8, s16)


def _leading_sums(e):
    n = e.shape[0]
    s2 = e + pltpu.roll(e, n - 1, 0)
    s4 = s2 + pltpu.roll(s2, n - 2, 0)
    s8 = s4 + pltpu.roll(s4, n - 4, 0)
    s16 = s8 + pltpu.roll(s8, n - 8, 0)
    return (s2, s4, s8, s16)


def _pool_fwd(u, zp, wg, scale):
    lp = u.shape[0]
    tm = _row_tile(lp)

    def body(u_ref, z_ref, wg_ref, sc_ref, a_ref, ext_ref):
        i = pl.program_id(0)

        @pl.when(i == 0)
        def _():
            ext_ref[0:HALO, :] = jnp.zeros((HALO, POOL_WIDTH), F32)

        ext_ref[HALO:HALO + tm, :] = u_ref[...].astype(F32)
        e = ext_ref[...]
        sums = _trailing_sums(e)
        ext_ref[0:HALO, :] = e[tm:tm + HALO, :]
        inv_cnt = _inv_counts(i, tm)
        for g in range(POOL_GROUPS):
            cols = slice(g * GROUP_DIM, (g + 1) * GROUP_DIM)
            mixed = sums[g][HALO:, cols] * inv_cnt[g] - e[HALO:, cols]
            y = _mm(mixed.astype(BF16), wg_ref[g]) * sc_ref[:, cols]
            zf = z_ref[:, cols].astype(F32)
            a_ref[:, cols] = (y * (zf * _sigmoid(zf))).astype(a_ref.dtype)

    return pl.pallas_call(
        body, name="pool_fwd", grid=(lp // tm,),
        in_specs=[_rows(tm, POOL_WIDTH), _rows(tm, POOL_WIDTH),
                  pl.BlockSpec((POOL_GROUPS, GROUP_DIM, GROUP_DIM), lambda i: (0, 0, 0)),
                  pl.BlockSpec((1, POOL_WIDTH), lambda i: (0, 0))],
        out_specs=_rows(tm, POOL_WIDTH), out_shape=jax.ShapeDtypeStruct((lp, POOL_WIDTH), BF16),
        scratch_shapes=[pltpu.VMEM((HALO + tm, POOL_WIDTH), F32)],
        compiler_params=_cp(("arbitrary",)),
    )(u, zp, wg, scale)


def _rms_fwd(xf, gain):
    inv = lax.rsqrt(jnp.mean(xf * xf, axis=-1, keepdims=True) + EPS)
    xhat = xf * inv
    return inv, xhat, xhat * gain


def _rms_bwd(dy, inv, xhat, gain):
    dgain = jnp.sum(dy * xhat, axis=0, keepdims=True)
    dyg = dy * gain
    dx = inv * (dyg - xhat * jnp.mean(dyg * xhat, axis=-1, keepdims=True))
    return dx, dgain


def _head_norm_fwd(xh, gain128):
    inv = lax.rsqrt(jnp.sum(xh * xh, axis=-1, keepdims=True) * (1.0 / QK_DIM) + EPS)
    xhat = xh * inv
    return inv, xhat, xhat * gain128


def _head_norm_bwd(dy, inv, xhat, gain128):
    dyg = dy * gain128
    return inv * (dyg - xhat * (jnp.sum(dyg * xhat, axis=-1, keepdims=True) * (1.0 / QK_DIM)))


def _mla_prep_fwd(cq, ckv, kr, tabs, gqa, gkva, gqn, gkn, wq, wkn, wv):
    lp = cq.shape[0]
    tm = _row_tile(lp)

    def body(cq_ref, ckv_ref, kr_ref, c_ref, s1_ref, s2_ref, gqa_ref, gkva_ref, gqn_ref, gkn_ref,
             wq_ref, wkn_ref, wv_ref, q_ref, k_ref, v_ref):
        c, s1, s2 = c_ref[...], s1_ref[...], s2_ref[...]
        _, _, cqn = _rms_fwd(cq_ref[...].astype(F32), gqa_ref[...])
        qraw = _mm(cqn.astype(BF16), wq_ref[...])
        for h in range(N_HEADS):
            hb = slice(h * HEAD_PAD, (h + 1) * HEAD_PAD)
            _, _, yh = _head_norm_fwd(qraw[:, hb], gqn_ref[...])
            q_ref[:, hb] = (_rope(yh, c, s1, s2) * Q_PRESCALE).astype(q_ref.dtype)
        _, _, ckvn = _rms_fwd(ckv_ref[...].astype(F32), gkva_ref[...])
        ckvn_b = ckvn.astype(BF16)
        knraw = _mm(ckvn_b, wkn_ref[...])
        krs = kr_ref[...].astype(F32)
        for h in range(N_HEADS):
            hb = slice(h * HEAD_PAD, (h + 1) * HEAD_PAD)
            _, _, yh = _head_norm_fwd(knraw[:, hb] + krs, gkn_ref[...])
            k_ref[:, hb] = _rope(yh, c, s1, s2).astype(k_ref.dtype)
        v_ref[...] = _mm(ckvn_b, wv_ref[...]).astype(v_ref.dtype)

    hw = N_HEADS * HEAD_PAD
    vec = lambda n: pl.BlockSpec((1, n), lambda i: (0, 0))
    return pl.pallas_call(
        body, name="mla_prep_fwd", grid=(lp // tm,),
        in_specs=[_rows(tm, Q_RANK), _rows(tm, KV_RANK), _rows(tm, HEAD_PAD)] + [_rows(tm, HEAD_PAD)] * 3
        + [vec(Q_RANK), vec(KV_RANK), vec(HEAD_PAD), vec(HEAD_PAD),
           _resident((Q_RANK, hw)), _resident((KV_RANK, hw)), _resident((KV_RANK, MLA_WIDTH))],
        out_specs=[_rows(tm, hw), _rows(tm, hw), _rows(tm, MLA_WIDTH)],
        out_shape=[jax.ShapeDtypeStruct((lp, hw), BF16), jax.ShapeDtypeStruct((lp, hw), BF16),
                   jax.ShapeDtypeStruct((lp, MLA_WIDTH), BF16)],
        compiler_params=_cp(("parallel",)),
    )(cq, ckv, kr, *tabs, gqa, gkva, gqn, gkn, wq, wkn, wv)


def _causal_mask(s, q0, k0):
    qi = q0 + lax.broadcasted_iota(jnp.int32, s.shape, 0)
    ki = k0 + lax.broadcasted_iota(jnp.int32, s.shape, 1)
    return jnp.where((ki <= qi) & (ki >= PAD_FRONT), s, MASK_VALUE)


def _score_chunks(kind, r, tk):
    if kind == "inner":
        return [(c0, False) for c0 in range(0, tk, ATTN_BLOCK)]
    if kind == "first":
        return [(c0, c0 == 0) for c0 in range(0, tk, ATTN_BLOCK)]
    return [(c0, True) for c0 in range(0, min(tk, (r + 1) * ROW_BLOCK), ATTN_BLOCK)]


def _tile_kinds(i, t):
    return (("diag", t == i), ("first", (t == 0) & (i > 0)), ("inner", (t > 0) & (t < i)))


def _lanes(col, width=HEAD_PAD):
    return jnp.broadcast_to(col, (col.shape[0], width))


def _flash_fwd(q, k, v, gather=()):
    lp = q.shape[0]
    tq = tk = ATTN_TILE
    nq = lp // tq
    nj = N_HEADS // 2
    n_blocks = tq // ROW_BLOCK
    ex = _ChipExchange(list(gather), scatter=False)

    pairs = [(i, t) for i in range(nq) for t in range(i + 1)]
    i_tab = jnp.asarray([p[0] for p in pairs], jnp.int32)
    t_tab = jnp.asarray([p[1] for p in pairs], jnp.int32)

    def body(i_tab_ref, t_tab_ref, q_ref, k_ref, v_ref, *rest):
        ex_in, (o_ref, lse_ref), ex_out = rest[:ex.n], rest[ex.n:ex.n + 2], rest[ex.n + 2:2 * ex.n + 2]
        m_scr, acc_scr, s_scr, p_scr, part_scr, vext_scr = rest[2 * ex.n + 2:2 * ex.n + 8]
        ex_sems = rest[2 * ex.n + 8:]
        j, step_no = pl.program_id(0), pl.program_id(1)
        i, t = i_tab_ref[step_no], t_tab_ref[step_no]
        if ex.n:
            pl.when((j == 0) & (step_no == 0))(lambda: ex.start(ex_in, ex_out, ex_sems))

        @pl.when(t == 0)
        def _():
            m_scr[...] = jnp.full(m_scr.shape, MASK_VALUE, F32)
            acc_scr[...] = jnp.zeros(acc_scr.shape, F32)

        def step(kind):
            def scores(hh, r, c0, masked):
                s = s_scr[hh, r * ROW_BLOCK:(r + 1) * ROW_BLOCK, c0:c0 + ATTN_BLOCK]
                return _causal_mask(s, i * tq + r * ROW_BLOCK, t * tk + c0) if masked else s

            vext_scr[0:HEAD_PAD, :] = v_ref[...].T
            vext_scr[HEAD_PAD:HEAD_PAD + ONES_ROWS, :] = jnp.ones((ONES_ROWS, tk), BF16)
            for hh in range(2):
                hb = slice(hh * HEAD_PAD, (hh + 1) * HEAD_PAD)
                s_scr[hh] = _nt(q_ref[:, hb], k_ref[:, hb])
            for hh in range(2):
                for r in range(n_blocks):
                    part = None
                    for c0, masked in _score_chunks(kind, r, tk):
                        s = scores(hh, r, c0, masked)
                        part = s if part is None else jnp.maximum(part, s)
                    part_scr[r * ROW_BLOCK:(r + 1) * ROW_BLOCK, :] = part
                m_prev = m_scr[hh]
                m_new = jnp.maximum(m_prev, _lanes(jnp.max(part_scr[...], axis=-1, keepdims=True)))
                alpha = jnp.exp2(m_prev - m_new)
                m_scr[hh] = m_new
                for r in range(n_blocks):
                    rows = slice(r * ROW_BLOCK, (r + 1) * ROW_BLOCK)
                    m_r = m_scr[hh, rows, :]
                    chunks = _score_chunks(kind, r, tk)
                    for c0, masked in chunks:
                        p_scr[hh, rows, c0:c0 + ATTN_BLOCK] = jnp.exp2((scores(hh, r, c0, masked) - m_r).astype(BF16))
                    done = chunks[-1][0] + ATTN_BLOCK
                    if done < tk:
                        p_scr[hh, rows, done:tk] = jnp.zeros((ROW_BLOCK, tk - done), BF16)
                alpha_t = jnp.broadcast_to(alpha.T[0:1, :], (HEAD_PAD + ONES_ROWS, tq))
                acc_scr[hh] = alpha_t * acc_scr[hh] + _nt(vext_scr[...], p_scr[hh])

        for kind, pred in _tile_kinds(i, t):
            pl.when(pred)(functools.partial(step, kind))

        @pl.when(t == i)
        def _():
            lane = lax.broadcasted_iota(jnp.int32, (tq, HEAD_PAD), 1)
            outs, lses = [], []
            for hh in range(2):
                l_t = jnp.broadcast_to(acc_scr[hh, HEAD_PAD:HEAD_PAD + 1, :], (HEAD_PAD, tq))
                outs.append((acc_scr[hh, 0:HEAD_PAD, :] / l_t).T)
                lses.append(m_scr[hh] + jnp.log(l_t.T) * LOG2E)
            o_ref[...] = jnp.where(lane < V_DIM, outs[0], outs[1]).astype(o_ref.dtype)
            lse_ref[...] = jnp.where(lane < V_DIM, lses[0], lses[1])

        if ex.n:
            pl.when((j == nj - 1) & (step_no == len(pairs) - 1))(lambda: ex.wait(ex_in, ex_out, ex_sems))

    q_idx = lambda j, p, it, tt: (it[p], j)
    kv_idx = lambda j, p, it, tt: (tt[p], j)
    o, lse, *gathered = pl.pallas_call(
        body, name="flash_fwd_gather" if ex.n else "flash_fwd",
        grid_spec=pltpu.PrefetchScalarGridSpec(
            num_scalar_prefetch=2, grid=(nj, len(pairs)),
            in_specs=[pl.BlockSpec((tq, 2 * HEAD_PAD), q_idx), pl.BlockSpec((tk, 2 * HEAD_PAD), kv_idx),
                      pl.BlockSpec((tk, HEAD_PAD), kv_idx)] + ex.specs,
            out_specs=[pl.BlockSpec((tq, HEAD_PAD), q_idx)] * 2 + ex.specs,
            scratch_shapes=[pltpu.VMEM((2, tq, HEAD_PAD), F32), pltpu.VMEM((2, HEAD_PAD + ONES_ROWS, tq), F32),
                            pltpu.VMEM((2, tq, tk), F32), pltpu.VMEM((2, tq, tk), BF16),
                            pltpu.VMEM((tq, HEAD_PAD), F32), pltpu.VMEM((HEAD_PAD + ONES_ROWS, tk), BF16)]
            + (ex.sems if ex.n else [])),
        out_shape=[jax.ShapeDtypeStruct((lp, MLA_WIDTH), BF16), jax.ShapeDtypeStruct((lp, MLA_WIDTH), F32)] + ex.out_shape,
        compiler_params=_cp(("arbitrary",) * 2),
    )(i_tab, t_tab, q, k, v, *gather)
    return o, lse, gathered


def _merge_fwd(x, a_pool, o, zm, gp, gm, wpu, wmu, wout):
    lp = x.shape[0]
    tm = _row_tile(lp)

    def body(x_ref, ap_ref, o_ref, zm_ref, gp_ref, gm_ref, wpu_ref, wmu_ref, wout_ref, xn_ref, yp_ref, ym_ref):
        yp = _mm(ap_ref[...], wpu_ref[...])
        zf = zm_ref[...].astype(F32)
        amla = o_ref[...].astype(F32) * (zf * _sigmoid(zf))
        ym = _mm(amla.astype(BF16), wmu_ref[...])
        merged = _sigmoid(gp_ref[...].astype(F32)) * yp + _sigmoid(gm_ref[...].astype(F32)) * ym
        xn_ref[...] = x_ref[...] + _mm(merged.astype(BF16), wout_ref[...])
        yp_ref[...] = yp.astype(yp_ref.dtype)
        ym_ref[...] = ym.astype(ym_ref.dtype)

    return pl.pallas_call(
        body, name="merge_fwd", grid=(lp // tm,),
        in_specs=[_rows(tm, D_MODEL), _rows(tm, POOL_WIDTH), _rows(tm, MLA_WIDTH), _rows(tm, MLA_WIDTH),
                  _rows(tm, D_MODEL), _rows(tm, D_MODEL),
                  _resident((POOL_WIDTH, D_MODEL)), _resident((MLA_WIDTH, D_MODEL)), _resident((D_MODEL, D_MODEL))],
        out_specs=[_rows(tm, D_MODEL)] * 3,
        out_shape=[jax.ShapeDtypeStruct((lp, D_MODEL), F32), jax.ShapeDtypeStruct((lp, D_MODEL), BF16),
                   jax.ShapeDtypeStruct((lp, D_MODEL), BF16)],
        compiler_params=_cp(("parallel",)),
    )(x, a_pool, o, zm, gp, gm, wpu, wmu, wout)


def _loss_head(y, target):
    lp = y.shape[0]
    tm = ROW0
    n_real = target.shape[0] // tm

    def body(y_ref, t_ref, d_ref, l_ref):
        i = pl.program_id(0)

        @pl.when(i == 0)
        def _():
            l_ref[...] = jnp.zeros(l_ref.shape, F32)

        real = (i >= 1) & (i <= n_real)
        err = jnp.where(real, y_ref[...] - t_ref[...], 0.0)
        d_ref[...] = err * (1.0 / D_MODEL)
        l_ref[...] += jnp.sum(err * err) * (0.5 / D_MODEL)

    return pl.pallas_call(
        body, name="loss_head", grid=(lp // tm,),
        in_specs=[_rows(tm, D_MODEL), pl.BlockSpec((tm, D_MODEL), lambda i: (jnp.clip(i - 1, 0, n_real - 1), 0))],
        out_specs=[_rows(tm, D_MODEL), pl.BlockSpec((8, 128), lambda i: (0, 0))],
        out_shape=[jax.ShapeDtypeStruct((lp, D_MODEL), F32), jax.ShapeDtypeStruct((8, 128), F32)],
        compiler_params=_cp(("arbitrary",)),
    )(y, target)


def _pair_rowsum(prod):
    lane = lax.broadcasted_iota(jnp.int32, prod.shape, 1)
    lo = jnp.sum(jnp.where(lane < V_DIM, prod, 0.0), axis=-1, keepdims=True)
    hi = jnp.sum(jnp.where(lane < V_DIM, 0.0, prod), axis=-1, keepdims=True)
    return jnp.where(lane < V_DIM, lo, hi)


def _merge_bwd(dres, yp, ym, gp, gm, o, zm, wout, wpu, wmu):
    lp = dres.shape[0]
    tm = _row_tile(lp, ROW_TILES_HEAVY)

    def body(dres_ref, yp_ref, ym_ref, gp_ref, gm_ref, o_ref, zm_ref, wout_ref, wpu_ref, wmu_ref,
             merged_ref, dyp_ref, dym_ref, dgp_ref, dgm_ref, dap_ref, amla_ref, do_ref, dzm_ref, delta_ref):
        dmerged = _nt(dres_ref[...].astype(BF16), wout_ref[...])
        sp = _sigmoid(gp_ref[...].astype(F32))
        sm = _sigmoid(gm_ref[...].astype(F32))
        ypf = yp_ref[...].astype(F32)
        ymf = ym_ref[...].astype(F32)
        merged_ref[...] = (sp * ypf + sm * ymf).astype(merged_ref.dtype)
        dyp = (dmerged * sp).astype(BF16)
        dym = (dmerged * sm).astype(BF16)
        dyp_ref[...] = dyp
        dym_ref[...] = dym
        dgp_ref[...] = (dmerged * ypf * sp * (1.0 - sp)).astype(dgp_ref.dtype)
        dgm_ref[...] = (dmerged * ymf * sm * (1.0 - sm)).astype(dgm_ref.dtype)
        dap_ref[...] = _nt(dyp, wpu_ref[...]).astype(dap_ref.dtype)
        dam = _nt(dym, wmu_ref[...])
        zf = zm_ref[...].astype(F32)
        sg = _sigmoid(zf)
        si = zf * sg
        of = o_ref[...].astype(F32)
        amla_ref[...] = (of * si).astype(amla_ref.dtype)
        do = dam * si
        do_ref[...] = do.astype(do_ref.dtype)
        dzm_ref[...] = (dam * of * (sg * (1.0 + zf * (1.0 - sg)))).astype(dzm_ref.dtype)
        prod = do * of
        for j in range(N_HEADS // 2):
            hb = slice(j * HEAD_PAD, (j + 1) * HEAD_PAD)
            delta_ref[:, hb] = _pair_rowsum(prod[:, hb])

    bf = lambda w: jax.ShapeDtypeStruct((lp, w), BF16)
    return pl.pallas_call(
        body, name="merge_bwd", grid=(lp // tm,),
        in_specs=[_rows(tm, D_MODEL)] * 5 + [_rows(tm, MLA_WIDTH)] * 2
        + [_resident((D_MODEL, D_MODEL)), _resident((POOL_WIDTH, D_MODEL)), _resident((MLA_WIDTH, D_MODEL))],
        out_specs=[_rows(tm, D_MODEL)] * 5 + [_rows(tm, POOL_WIDTH)] + [_rows(tm, MLA_WIDTH)] * 4,
        out_shape=[bf(D_MODEL)] * 5 + [bf(POOL_WIDTH)] + [bf(MLA_WIDTH)] * 3 + [jax.ShapeDtypeStruct((lp, MLA_WIDTH), F32)],
        compiler_params=_cp(("parallel",)),
    )(dres, yp, ym, gp, gm, o, zm, wout, wpu, wmu)


def _flash_bwd(q, k, v, do, lse, delta, scatter=()):
    lp = q.shape[0]
    tq = tk = ATTN_TILE
    nq = lp // tq
    nj = N_HEADS // 2
    scale = 1.0 / math.sqrt(QK_DIM)
    ex = _ChipExchange(list(scatter), scatter=True)

    pairs = [(t, i) for t in range(nq) for i in range(t, nq)]
    t_tab = jnp.asarray([p[0] for p in pairs], jnp.int32)
    i_tab = jnp.asarray([p[1] for p in pairs], jnp.int32)

    def body(t_tab_ref, i_tab_ref, q_ref, k_ref, v_ref, do_ref, lse_ref, dl_ref, *rest):
        ex_in, (dq_ref, dk_ref, dv_ref), ex_out = rest[:ex.n], rest[ex.n:ex.n + 3], rest[ex.n + 3:2 * ex.n + 3]
        dq_acc, dk_acc, dv_acc, s_scr, dp_scr, p_scr, ds_scr, doh_scr, stat_scr = rest[2 * ex.n + 3:2 * ex.n + 12]
        ex_sems = rest[2 * ex.n + 12:]
        j, step_no = pl.program_id(0), pl.program_id(1)
        t, i = t_tab_ref[step_no], i_tab_ref[step_no]
        if ex.n:
            pl.when((j == 0) & (step_no == 0))(lambda: ex.start(ex_in, ex_out, ex_sems))

        @pl.when(step_no == 0)
        def _():
            dq_acc[...] = jnp.zeros(dq_acc.shape, F32)

        @pl.when(i == t)
        def _():
            dk_acc[...] = jnp.zeros(dk_acc.shape, F32)
            dv_acc[...] = jnp.zeros(dv_acc.shape, F32)

        def step(kind):
            lane = lax.broadcasted_iota(jnp.int32, (tq, HEAD_PAD), 1)
            for hh in range(2):
                hb = slice(hh * HEAD_PAD, (hh + 1) * HEAD_PAD)
                mine = (lane < V_DIM) if hh == 0 else (lane >= V_DIM)
                doh_scr[hh] = jnp.where(mine, do_ref[...], jnp.zeros_like(do_ref[...]))
                s_scr[hh] = _nt(q_ref[:, hb], k_ref[:, hb])
                dp_scr[hh] = _nt(doh_scr[hh], v_ref[...])
            for hh in range(2):
                hb = slice(hh * HEAD_PAD, (hh + 1) * HEAD_PAD)
                col = slice(hh * V_DIM, hh * V_DIM + 1)
                stat_scr[0] = _lanes(lse_ref[:, col])
                stat_scr[1] = _lanes(dl_ref[:, col])
                for r in range(tq // ROW_BLOCK):
                    rows = slice(r * ROW_BLOCK, (r + 1) * ROW_BLOCK)
                    lse_r = stat_scr[0, rows, :]
                    dl_r = stat_scr[1, rows, :]
                    chunks = _score_chunks(kind, r, tk)
                    for c0, masked in chunks:
                        cols = slice(c0, c0 + ATTN_BLOCK)
                        s = s_scr[hh, rows, cols]
                        if masked:
                            s = _causal_mask(s, i * tq + r * ROW_BLOCK, t * tk + c0)
                        p = jnp.exp2(s - lse_r)
                        p_scr[hh, rows, cols] = p.astype(BF16)
                        ds_scr[hh, rows, cols] = (p * (dp_scr[hh, rows, cols] - dl_r)).astype(BF16)
                    done = chunks[-1][0] + ATTN_BLOCK
                    if done < tk:
                        zeros = jnp.zeros((ROW_BLOCK, tk - done), BF16)
                        p_scr[hh, rows, done:tk] = zeros
                        ds_scr[hh, rows, done:tk] = zeros
                dv_acc[...] += _tn(doh_scr[hh], p_scr[hh])
                dk_acc[hh] += _tn(q_ref[:, hb], ds_scr[hh])
                dq_acc[i, hb, :] += lax.dot_general(k_ref[:, hb], ds_scr[hh], (((0,), (1,)), ((), ())),
                                                    preferred_element_type=F32)

        for kind, pred in _tile_kinds(i, t):
            pl.when(pred)(functools.partial(step, kind))

        @pl.when(i == nq - 1)
        def _():
            dk_ref[:, 0:HEAD_PAD] = (dk_acc[0].T * LN2).astype(dk_ref.dtype)
            dk_ref[:, HEAD_PAD:2 * HEAD_PAD] = (dk_acc[1].T * LN2).astype(dk_ref.dtype)
            dv_ref[...] = dv_acc[...].T.astype(dv_ref.dtype)

        @pl.when(step_no == len(pairs) - 1)
        def _():
            for qi in range(nq):
                for hh in range(2):
                    hb = slice(hh * HEAD_PAD, (hh + 1) * HEAD_PAD)
                    dq_ref[qi * tq:(qi + 1) * tq, hb] = (dq_acc[qi, hb, :].T * scale).astype(dq_ref.dtype)

        if ex.n:
            pl.when((j == nj - 1) & (step_no == len(pairs) - 1))(lambda: ex.wait(ex_in, ex_out, ex_sems))

    q_idx = lambda j, p, tt, it: (it[p], j)
    kv_idx = lambda j, p, tt, it: (tt[p], j)
    hw = N_HEADS * HEAD_PAD
    dq, dk, dv, *pieces = pl.pallas_call(
        body, name="flash_bwd_scatter" if ex.n else "flash_bwd",
        grid_spec=pltpu.PrefetchScalarGridSpec(
            num_scalar_prefetch=2, grid=(nj, len(pairs)),
            in_specs=[pl.BlockSpec((tq, 2 * HEAD_PAD), q_idx), pl.BlockSpec((tk, 2 * HEAD_PAD), kv_idx),
                      pl.BlockSpec((tk, HEAD_PAD), kv_idx), pl.BlockSpec((tq, HEAD_PAD), q_idx),
                      pl.BlockSpec((tq, HEAD_PAD), q_idx), pl.BlockSpec((tq, HEAD_PAD), q_idx)] + ex.specs,
            out_specs=[pl.BlockSpec((lp, 2 * HEAD_PAD), lambda j, p, tt, it: (0, j)),
                       pl.BlockSpec((tk, 2 * HEAD_PAD), kv_idx), pl.BlockSpec((tk, HEAD_PAD), kv_idx)] + ex.specs,
            scratch_shapes=[pltpu.VMEM((nq, 2 * HEAD_PAD, tq), F32), pltpu.VMEM((2, HEAD_PAD, tk), F32),
                            pltpu.VMEM((HEAD_PAD, tk), F32),
                            pltpu.VMEM((2, tq, tk), F32), pltpu.VMEM((2, tq, tk), F32),
                            pltpu.VMEM((2, tq, tk), BF16), pltpu.VMEM((2, tq, tk), BF16),
                            pltpu.VMEM((2, tq, HEAD_PAD), BF16), pltpu.VMEM((2, tq, HEAD_PAD), F32)]
            + (ex.sems if ex.n else [])),
        out_shape=[jax.ShapeDtypeStruct((lp, hw), BF16), jax.ShapeDtypeStruct((lp, hw), BF16),
                   jax.ShapeDtypeStruct((lp, MLA_WIDTH), BF16)] + ex.out_shape,
        compiler_params=_cp(("arbitrary",) * 2),
    )(t_tab, i_tab, q, k, v, do, lse, delta, *scatter)
    return dq, dk, dv, pieces


def _mla_prep_bwd(dq, dk, dv, cq, ckv, kr, tabs, gqa, gkva, gqn, gkn, wq, wkn, wv):
    lp = cq.shape[0]
    tm = _row_tile(lp, ROW_TILES_HEAVY)
    hw = N_HEADS * HEAD_PAD

    def body(dq_ref, dk_ref, dv_ref, cq_ref, ckv_ref, kr_ref, c_ref, s1_ref, s2_ref, gqa_ref, gkva_ref, gqn_ref,
             gkn_ref, wq_ref, wkn_ref, wv_ref, dcq_ref, dckv_ref, dkr_ref, dwq_ref, dwkn_ref, dwv_ref,
             dgqa_ref, dgkva_ref, dgqn_ref, dgkn_ref, draw_scr):
        @pl.when(pl.program_id(0) == 0)
        def _():
            for r in (dwq_ref, dwkn_ref, dwv_ref, dgqa_ref, dgkva_ref, dgqn_ref, dgkn_ref):
                r[...] = jnp.zeros(r.shape, F32)

        c, s1, s2 = c_ref[...], s1_ref[...], s2_ref[...]
        lane = lax.broadcasted_iota(jnp.int32, (tm, HEAD_PAD), 1)

        inv_q, xhat_q, cqn = _rms_fwd(cq_ref[...].astype(F32), gqa_ref[...])
        cqn_b = cqn.astype(BF16)
        qraw = _mm(cqn_b, wq_ref[...])
        dgqn = jnp.zeros((1, HEAD_PAD), F32)
        for h in range(N_HEADS):
            hb = slice(h * HEAD_PAD, (h + 1) * HEAD_PAD)
            inv, xhat, _ = _head_norm_fwd(qraw[:, hb], gqn_ref[...])
            dy = _rope_t(dq_ref[:, hb].astype(F32), c, s1, s2)
            dgqn += jnp.sum(dy * xhat, axis=0, keepdims=True)
            draw_scr[:, hb] = _head_norm_bwd(dy, inv, xhat, gqn_ref[...]).astype(BF16)
        dgqn_ref[...] += dgqn
        dqraw = draw_scr[...]
        dwq_ref[...] += _tn(cqn_b, dqraw)
        dcq, dgqa = _rms_bwd(_nt(dqraw, wq_ref[...]), inv_q, xhat_q, gqa_ref[...])
        dcq_ref[...] = dcq.astype(dcq_ref.dtype)
        dgqa_ref[...] += dgqa

        inv_kv, xhat_kv, ckvn = _rms_fwd(ckv_ref[...].astype(F32), gkva_ref[...])
        ckvn_b = ckvn.astype(BF16)
        knraw = _mm(ckvn_b, wkn_ref[...])
        krs = kr_ref[...].astype(F32)
        dgkn = jnp.zeros((1, HEAD_PAD), F32)
        dkr = jnp.zeros((tm, HEAD_PAD), F32)
        for h in range(N_HEADS):
            hb = slice(h * HEAD_PAD, (h + 1) * HEAD_PAD)
            inv, xhat, _ = _head_norm_fwd(knraw[:, hb] + krs, gkn_ref[...])
            dy = _rope_t(dk_ref[:, hb].astype(F32), c, s1, s2)
            dgkn += jnp.sum(dy * xhat, axis=0, keepdims=True)
            dxh = _head_norm_bwd(dy, inv, xhat, gkn_ref[...])
            dkr += dxh
            draw_scr[:, hb] = jnp.where(lane < NOPE, dxh, 0.0).astype(BF16)
        dgkn_ref[...] += dgkn
        dkr_ref[...] = jnp.where((lane >= KR_LANE0) & (lane < QK_DIM), dkr, 0.0).astype(dkr_ref.dtype)
        dknraw = draw_scr[...]
        dvb = dv_ref[...]
        dwkn_ref[...] += _tn(ckvn_b, dknraw)
        dwv_ref[...] += _tn(ckvn_b, dvb)
        dckvn = _nt(dknraw, wkn_ref[...]) + _nt(dvb, wv_ref[...])
        dckv, dgkva = _rms_bwd(dckvn, inv_kv, xhat_kv, gkva_ref[...])
        dckv_ref[...] = dckv.astype(dckv_ref.dtype)
        dgkva_ref[...] += dgkva

    vec = lambda n: pl.BlockSpec((1, n), lambda i: (0, 0))
    whole = lambda r, c: pl.BlockSpec((r, c), lambda i: (0, 0))
    f = lambda r, c: jax.ShapeDtypeStruct((r, c), F32)
    return pl.pallas_call(
        body, name="mla_prep_bwd", grid=(lp // tm,),
        in_specs=[_rows(tm, hw), _rows(tm, hw), _rows(tm, MLA_WIDTH), _rows(tm, Q_RANK), _rows(tm, KV_RANK),
                  _rows(tm, HEAD_PAD)] + [_rows(tm, HEAD_PAD)] * 3
        + [vec(Q_RANK), vec(KV_RANK), vec(HEAD_PAD), vec(HEAD_PAD),
           _resident((Q_RANK, hw)), _resident((KV_RANK, hw)), _resident((KV_RANK, MLA_WIDTH))],
        out_specs=[_rows(tm, Q_RANK), _rows(tm, KV_RANK), _rows(tm, HEAD_PAD),
                   whole(Q_RANK, hw), whole(KV_RANK, hw), whole(KV_RANK, MLA_WIDTH),
                   vec(Q_RANK), vec(KV_RANK), vec(HEAD_PAD), vec(HEAD_PAD)],
        out_shape=[jax.ShapeDtypeStruct((lp, Q_RANK), BF16), jax.ShapeDtypeStruct((lp, KV_RANK), BF16),
                   jax.ShapeDtypeStruct((lp, HEAD_PAD), BF16),
                   f(Q_RANK, hw), f(KV_RANK, hw), f(KV_RANK, MLA_WIDTH),
                   f(1, Q_RANK), f(1, KV_RANK), f(1, HEAD_PAD), f(1, HEAD_PAD)],
        scratch_shapes=[pltpu.VMEM((tm, hw), BF16)],
        compiler_params=_cp(("arbitrary",)),
    )(dq, dk, dv, cq, ckv, kr, *tabs, gqa, gkva, gqn, gkn, wq, wkn, wv)


def _pool_bwd(dap, u, zp, wg, scale):
    lp = u.shape[0]
    tm = _row_tile(lp)
    n = lp // tm
    per = tm // HALO

    def body(dap_ref, u_ref, uh_ref, z_ref, wg_ref, sc_ref, du_ref, dz_ref, dwg_ref, dsc_ref, ext_u, ext_d):
        i = pl.program_id(0)
        r = n - 1 - i

        @pl.when(i == 0)
        def _():
            dwg_ref[...] = jnp.zeros(dwg_ref.shape, F32)
            dsc_ref[...] = jnp.zeros(dsc_ref.shape, F32)
            ext_d[tm:tm + HALO, :] = jnp.zeros((HALO, POOL_WIDTH), F32)

        ext_u[0:HALO, :] = jnp.where(r == 0, 0.0, uh_ref[...].astype(F32))
        ext_u[HALO:HALO + tm, :] = u_ref[...].astype(F32)
        e = ext_u[...]
        sums = _trailing_sums(e)
        inv_cnt = _inv_counts(r, tm)
        dmixed = []
        for g in range(POOL_GROUPS):
            cols = slice(g * GROUP_DIM, (g + 1) * GROUP_DIM)
            mixed_b = (sums[g][HALO:, cols] * inv_cnt[g] - e[HALO:, cols]).astype(BF16)
            yg = _mm(mixed_b, wg_ref[g])
            zf = z_ref[:, cols].astype(F32)
            sg = _sigmoid(zf)
            da = dap_ref[:, cols].astype(F32)
            dy = da * (zf * sg)
            dz_ref[:, cols] = (da * (yg * sc_ref[:, cols]) * (sg * (1.0 + zf * (1.0 - sg)))).astype(dz_ref.dtype)
            dsc_ref[:, cols] += jnp.sum(dy * yg, axis=0, keepdims=True)
            dyg = (dy * sc_ref[:, cols]).astype(BF16)
            dwg_ref[g] += _tn(mixed_b, dyg)
            dm = _nt(dyg, wg_ref[g])
            dmixed.append(dm)
            ext_d[0:tm, cols] = dm * inv_cnt[g]
        ed = ext_d[...]
        lead = _leading_sums(ed)
        ext_d[tm:tm + HALO, :] = ed[0:HALO, :]
        for g in range(POOL_GROUPS):
            cols = slice(g * GROUP_DIM, (g + 1) * GROUP_DIM)
            du_ref[:, cols] = (lead[g][0:tm, cols] - dmixed[g]).astype(du_ref.dtype)

    rev = lambda i: (n - 1 - i, 0)
    return pl.pallas_call(
        body, name="pool_bwd", grid=(n,),
        in_specs=[pl.BlockSpec((tm, POOL_WIDTH), rev), pl.BlockSpec((tm, POOL_WIDTH), rev),
                  pl.BlockSpec((HALO, POOL_WIDTH), lambda i: (jnp.maximum((n - 1 - i) * per - 1, 0), 0)),
                  pl.BlockSpec((tm, POOL_WIDTH), rev),
                  pl.BlockSpec((POOL_GROUPS, GROUP_DIM, GROUP_DIM), lambda i: (0, 0, 0)),
                  pl.BlockSpec((1, POOL_WIDTH), lambda i: (0, 0))],
        out_specs=[pl.BlockSpec((tm, POOL_WIDTH), rev), pl.BlockSpec((tm, POOL_WIDTH), rev),
                   pl.BlockSpec((POOL_GROUPS, GROUP_DIM, GROUP_DIM), lambda i: (0, 0, 0)),
                   pl.BlockSpec((1, POOL_WIDTH), lambda i: (0, 0))],
        out_shape=[jax.ShapeDtypeStruct((lp, POOL_WIDTH), BF16), jax.ShapeDtypeStruct((lp, POOL_WIDTH), BF16),
                   jax.ShapeDtypeStruct((POOL_GROUPS, GROUP_DIM, GROUP_DIM), F32),
                   jax.ShapeDtypeStruct((1, POOL_WIDTH), F32)],
        scratch_shapes=[pltpu.VMEM((HALO + tm, POOL_WIDTH), F32), pltpu.VMEM((tm + HALO, POOL_WIDTH), F32)],
        compiler_params=_cp(("arbitrary",)),
    )(dap, u, u, zp, wg, scale)


def _inproj_bwd(dres, x, gain, w_pad, dparts):
    lp = x.shape[0]
    tm = _row_tile(lp, ROW_TILES_HEAVY)

    def body(dres_ref, x_ref, g_ref, w_ref, *rest):
        dps = rest[:len(IN_WIDTHS)]
        dprev_ref, h_ref, dg_ref = rest[len(IN_WIDTHS):]

        @pl.when(pl.program_id(0) == 0)
        def _():
            dg_ref[...] = jnp.zeros(dg_ref.shape, F32)

        dh = jnp.zeros((tm, D_MODEL), F32)
        for dp_ref, off, wd in zip(dps, IN_OFFS, IN_WIDTHS):
            dh += _mm(dp_ref[...], w_ref[off:off + wd, :])
        inv, xhat, hn = _rms_fwd(x_ref[...], g_ref[...])
        h_ref[...] = hn.astype(h_ref.dtype)
        dx, dgain = _rms_bwd(dh, inv, xhat, g_ref[...])
        dg_ref[...] += dgain
        dprev_ref[...] = dres_ref[...] + dx

    return pl.pallas_call(
        body, name="inproj_bwd", grid=(lp // tm,),
        in_specs=[_rows(tm, D_MODEL), _rows(tm, D_MODEL), pl.BlockSpec((1, D_MODEL), lambda i: (0, 0)),
                  _resident((IN_PAD, D_MODEL))] + [_rows(tm, wd) for wd in IN_WIDTHS],
        out_specs=[_rows(tm, D_MODEL), _rows(tm, D_MODEL), pl.BlockSpec((1, D_MODEL), lambda i: (0, 0))],
        out_shape=[jax.ShapeDtypeStruct((lp, D_MODEL), F32), jax.ShapeDtypeStruct((lp, D_MODEL), BF16),
                   jax.ShapeDtypeStruct((1, D_MODEL), F32)],
        compiler_params=_cp(("arbitrary",)),
    )(dres, x, gain, w_pad, *dparts)


def _weight_grads(a, bs, name, transposed=False):
    lp, m = a.shape
    tk = _row_tile(lp)
    nb = len(bs)
    shapes = [(b.shape[1], m) if transposed else (m, b.shape[1]) for b in bs]

    def body(a_ref, *rest):
        b_refs, o_refs = rest[:nb], rest[nb:]

        @pl.when(pl.program_id(0) == 0)
        def _():
            for o_ref in o_refs:
                o_ref[...] = jnp.zeros(o_ref.shape, F32)

        ab = a_ref[...].astype(BF16)
        for b_ref, o_ref in zip(b_refs, o_refs):
            bb = b_ref[...].astype(BF16)
            o_ref[...] += _tn(bb, ab) if transposed else _tn(ab, bb)

    return pl.pallas_call(
        body, name=name, grid=(lp // tk,),
        in_specs=[_rows(tk, m)] + [_rows(tk, b.shape[1]) for b in bs],
        out_specs=[pl.BlockSpec(s, lambda i: (0, 0)) for s in shapes],
        out_shape=[jax.ShapeDtypeStruct(s, F32) for s in shapes],
        compiler_params=_cp(("arbitrary",)),
    )(a, *bs)


HBM_SPEC = pl.BlockSpec(memory_space=pltpu.HBM)


def _my_place():
    return lax.axis_index("x"), lax.axis_index("y"), lax.axis_index("c")


def _other_chips(x, y):
    return [(1 - x, y), (x, 1 - y), (1 - x, 1 - y)]


class _ChipExchange:
    def __init__(self, arrs, scatter):
        self.n = len(arrs)
        self.scatter = scatter
        self.out_shape = [jax.ShapeDtypeStruct(a.shape if scatter else (N_CHIPS,) + a.shape, a.dtype) for a in arrs]
        self.specs = [HBM_SPEC] * self.n
        self.sems = [pltpu.SemaphoreType.DMA((3 * self.n,)), pltpu.SemaphoreType.DMA((3 * self.n,)),
                     pltpu.SemaphoreType.DMA((self.n,))]

    def _copies(self, ins, outs, sems):
        send_sems, recv_sems, local_sems = sems
        x, y, c = _my_place()
        me = 2 * x + y
        chips = _other_chips(x, y)
        mine = lambda a: ins[a].at[me] if self.scatter else ins[a]

        def remote(a, k, arriving):
            px, py = chips[k]
            there = 2 * px + py
            return pltpu.make_async_remote_copy(
                src_ref=mine(a) if arriving or not self.scatter else ins[a].at[there],
                dst_ref=outs[a].at[there if arriving else me],
                send_sem=send_sems.at[a * 3 + k], recv_sem=recv_sems.at[a * 3 + k],
                device_id=(px, py, c), device_id_type=MESH)

        pairs = [(a, k) for a in range(self.n) for k in range(3)]
        local = [pltpu.make_async_copy(mine(a), outs[a].at[me], local_sems.at[a]) for a in range(self.n)]
        return local, [remote(a, k, False) for a, k in pairs], [remote(a, k, True) for a, k in pairs]

    def start(self, ins, outs, sems):
        local, sends, _ = self._copies(ins, outs, sems)
        for cp in local + sends:
            cp.start()

    def wait(self, ins, outs, sems):
        local, sends, arrivals = self._copies(ins, outs, sems)
        for cp in arrivals:
            cp.wait_recv()
        for cp in sends:
            cp.wait_send()
        for cp in local:
            cp.wait()


def _chip_exchange(arrs, scatter, name):
    ex = _ChipExchange(arrs, scatter)

    def body(*refs):
        ins, outs, sems = refs[:ex.n], refs[ex.n:2 * ex.n], refs[2 * ex.n:]
        ex.start(ins, outs, sems)
        ex.wait(ins, outs, sems)

    return pl.pallas_call(body, name=name, in_specs=ex.specs, out_specs=ex.specs, out_shape=ex.out_shape,
                          scratch_shapes=ex.sems)(*arrs)


def _sibling_exchange(arrs, name):
    n = len(arrs)

    def body(*refs):
        ins, outs = refs[:n], refs[n:2 * n]
        send_sems, recv_sems = refs[2 * n:]
        x, y, c = _my_place()
        cps = [pltpu.make_async_remote_copy(src_ref=ins[a], dst_ref=outs[a], send_sem=send_sems.at[a],
                                            recv_sem=recv_sems.at[a], device_id=(x, y, 1 - c), device_id_type=MESH)
               for a in range(n)]
        for cp in cps:
            cp.start()
        for cp in cps:
            cp.wait_recv()
        for cp in cps:
            cp.wait_send()

    return pl.pallas_call(
        body, name=name, in_specs=[HBM_SPEC] * n, out_specs=[HBM_SPEC] * n,
        out_shape=[jax.ShapeDtypeStruct(a.shape, a.dtype) for a in arrs],
        scratch_shapes=[pltpu.SemaphoreType.DMA((n,)), pltpu.SemaphoreType.DMA((n,))],
    )(*arrs)


def _all_reduce_small(pack):
    rows = pack.shape[0]

    def body(p_ref, o_ref, g_scr, send_sems, recv_sems):
        x, y, c = _my_place()
        me = 4 * x + 2 * y + c
        flips = [(dx, dy, dc) for dx in (0, 1) for dy in (0, 1) for dc in (0, 1) if (dx, dy, dc) != (0, 0, 0)]

        def peer(f):
            return (x if f[0] == 0 else 1 - x, y if f[1] == 0 else 1 - y, c if f[2] == 0 else 1 - c)

        def copy(k, slot):
            return pltpu.make_async_remote_copy(src_ref=p_ref, dst_ref=g_scr.at[slot], send_sem=send_sems.at[k],
                                                recv_sem=recv_sems.at[k], device_id=peer(flips[k]), device_id_type=MESH)

        sends = [copy(k, me) for k in range(len(flips))]
        for cp in sends:
            cp.start()
        g_scr[me] = p_ref[...]
        for k, f in enumerate(flips):
            px, py, pc = peer(f)
            copy(k, 4 * px + 2 * py + pc).wait_recv()
        for cp in sends:
            cp.wait_send()
        acc = g_scr[0]
        for d in range(1, N_DEV):
            acc = acc + g_scr[d]
        o_ref[...] = acc

    vm = pl.BlockSpec(memory_space=pltpu.VMEM)
    return pl.pallas_call(
        body, name="all_reduce_small", in_specs=[vm], out_specs=vm,
        out_shape=jax.ShapeDtypeStruct(pack.shape, F32),
        scratch_shapes=[pltpu.VMEM((N_DEV, rows, 128), F32), pltpu.SemaphoreType.DMA((N_DEV - 1,)),
                        pltpu.SemaphoreType.DMA((N_DEV - 1,))],
        compiler_params=_cp(),
    )(pack)


def _as3d(a):
    return a.reshape((-1,) + a.shape[-2:])


def _row_block(r, sublanes=8, cap=512):
    fits = [t for t in range(sublanes, min(r, cap) + 1, sublanes) if r % t == 0]
    return fits[-1] if fits else r


def _sum_pieces(pieces, name):
    _, na, r, c = pieces.shape
    rt = _row_block(r, sublanes=16)

    def body(p_ref, o_ref):
        acc = p_ref[0, 0].astype(F32)
        for s in range(1, N_CHIPS):
            acc = acc + p_ref[s, 0].astype(F32)
        o_ref[0] = acc

    return pl.pallas_call(
        body, name=name, grid=(na, r // rt),
        in_specs=[pl.BlockSpec((N_CHIPS, 1, rt, c), lambda a, i: (0, a, i, 0))],
        out_specs=pl.BlockSpec((1, rt, c), lambda a, i: (a, i, 0)),
        out_shape=jax.ShapeDtypeStruct((na, r, c), F32),
        compiler_params=_cp(("parallel", "parallel")),
    )(pieces)


def _adamw(w, g_parts, m, v, name):
    na, r, c = w.shape
    rt = _row_block(r)
    ng = len(g_parts)

    def body(w_ref, *rest):
        g_refs = rest[:ng]
        m_ref, v_ref, g_out, d_out, m_out, v_out = rest[ng:]
        g = g_refs[0][...]
        for gr in g_refs[1:]:
            g = g + gr[...]
        m_new = ADAM_B1 * m_ref[...] + (1.0 - ADAM_B1) * g
        v_new = ADAM_B2 * v_ref[...] + (1.0 - ADAM_B2) * (g * g)
        m_hat = m_new / (1.0 - ADAM_B1 ** ADAM_STEP)
        v_hat = v_new / (1.0 - ADAM_B2 ** ADAM_STEP)
        g_out[...] = g
        d_out[...] = -ADAM_LR * (m_hat / (jnp.sqrt(v_hat) + ADAM_EPS) + ADAM_WD * w_ref[...])
        m_out[...] = m_new
        v_out[...] = v_new

    spec = pl.BlockSpec((1, rt, c), lambda a, i: (a, i, 0))
    out = jax.ShapeDtypeStruct((na, r, c), F32)
    return pl.pallas_call(
        body, name=name, grid=(na, r // rt), in_specs=[spec] * (3 + ng), out_specs=[spec] * 4, out_shape=[out] * 4,
        compiler_params=_cp(("parallel", "parallel")),
    )(w, *g_parts, m, v)


def _cols_from_shards(g):
    g = jnp.moveaxis(g, 0, -2)
    return g.reshape(g.shape[:-2] + (g.shape[-2] * g.shape[-1],))


def _rows_from_shards(g):
    g = jnp.moveaxis(g, 0, -3)
    return g.reshape(g.shape[:-3] + (g.shape[-3] * g.shape[-2], g.shape[-1]))


def _cols_to_shards(w):
    w = w.reshape(w.shape[:-1] + (N_CHIPS, w.shape[-1] // N_CHIPS))
    return jnp.moveaxis(w, -2, 0)


def _rows_to_shards(w):
    w = w.reshape(w.shape[:-2] + (N_CHIPS, w.shape[-2] // N_CHIPS, w.shape[-1]))
    return jnp.moveaxis(w, -3, 0)


def _pad_w_in(wt):
    z = lambda n: jnp.zeros((n, wt.shape[1]), wt.dtype)
    return jnp.concatenate([wt[:2048], wt[2080:4640], z(KR_LANE0), wt[2048:2080], z(HEAD_PAD - QK_DIM)], axis=0)


def _unpad_w_in(parts):
    u, zp, cq, ckv, zm, gp, gm, kr = parts
    return jnp.concatenate([u, zp, cq, ckv, kr[KR_LANE0:QK_DIM], zm, gp, gm], axis=0)


def _pad_heads(w, real):
    w = w.reshape(w.shape[:-1] + (N_HEADS, real))
    w = jnp.pad(w, [(0, 0)] * (w.ndim - 1) + [(0, HEAD_PAD - real)])
    return w.reshape(w.shape[:-2] + (N_HEADS * HEAD_PAD,))


def _flat_rows(a):
    a = a.reshape(-1)
    return jnp.pad(a, (0, (-a.shape[0]) % (8 * 128))).reshape(-1, 128)


def kernel(x, positions, meta_tokens, norm_gain, w_in, pool_w_group, pool_scale, pool_w_up, q_a_norm_gain, kv_a_norm_gain, w_q_b, w_kv_b, q_norm_gain, k_norm_gain, mla_w_up, w_out, loss_target, m_meta_tokens, m_norm_gain, m_w_in, m_pool_w_group, m_pool_scale, m_pool_w_up, m_q_a_norm_gain, m_kv_a_norm_gain, m_w_q_b, m_w_kv_b, m_q_norm_gain, m_k_norm_gain, m_mla_w_up, m_w_out, v_meta_tokens, v_norm_gain, v_w_in, v_pool_w_group, v_pool_scale, v_pool_w_up, v_q_a_norm_gain, v_kv_a_norm_gain, v_w_q_b, v_w_kv_b, v_q_norm_gain, v_k_norm_gain, v_mla_w_up, v_w_out):
    seq = x.shape[1]
    lp = -(-(ROW0 + seq) // ATTN_TILE) * ATTN_TILE
    pad_back = lp - ROW0 - seq
    chip = 2 * lax.axis_index("x") + lax.axis_index("y")

    tr = lambda a: jnp.swapaxes(a, 1, 2)
    big = dict(w_in=tr(w_in), pool_w_up=pool_w_up, w_q_b=w_q_b, w_kv_b=w_kv_b, mla_w_up=mla_w_up, w_out=w_out)
    row_sharded = ("w_in", "w_q_b", "w_out")
    names = list(big)
    shards = [[big[n][l].astype(BF16) for n in names] for l in range(DEPTH)]
    from_shards = lambda n: _rows_from_shards if n in row_sharded else _cols_from_shards
    to_shards = lambda n: _rows_to_shards if n in row_sharded else _cols_to_shards

    def layer_weights(gathered):
        w = {n: from_shards(n)(g) for n, g in zip(names, gathered)}
        wkv = w["w_kv_b"].reshape(KV_RANK, N_HEADS, NOPE + V_DIM)
        return dict(w_pad=_pad_w_in(w["w_in"]), wq=_pad_heads(w["w_q_b"], QK_DIM),
                    wkn=_pad_heads(wkv[..., :NOPE].reshape(KV_RANK, N_HEADS * NOPE), NOPE),
                    wv=wkv[..., NOPE:].reshape(KV_RANK, MLA_WIDTH),
                    wpu=w["pool_w_up"], wmu=w["mla_w_up"], wout=w["w_out"])

    *first, meta_g = _chip_exchange(shards[0] + [meta_tokens], scatter=False, name="gather_layer0")
    weights = [layer_weights(first)]
    meta_full = _cols_from_shards(meta_g)
    wg = pool_w_group.astype(BF16)
    gqn = jnp.pad(q_norm_gain, ((0, 0), (0, HEAD_PAD - QK_DIM)))
    gkn = jnp.pad(k_norm_gain, ((0, 0), (0, HEAD_PAD - QK_DIM)))

    x_pad = jnp.concatenate([jnp.zeros((PAD_FRONT, D_MODEL), F32), meta_full, x[0], jnp.zeros((pad_back, D_MODEL), F32)], axis=0)
    pos_pad = jnp.concatenate([jnp.zeros((PAD_FRONT,), jnp.int32), jnp.arange(N_META, dtype=jnp.int32),
                               positions[0] + N_META, jnp.zeros((pad_back,), jnp.int32)])
    half = ROPE // 2
    inv_freq = (ROPE_THETA ** (-np.arange(half, dtype=np.float32) / half)).astype(np.float32)
    freq_row = np.zeros((1, HEAD_PAD), np.float32)
    freq_row[0, NOPE:NOPE + half] = inv_freq
    freq_row[0, NOPE + half:QK_DIM] = inv_freq
    tabs = _rope_tables(pos_pad[:, None], jnp.asarray(freq_row))

    row = lambda a, l: a[l][None, :]

    saved = []
    h_res = x_pad
    for l in range(DEPTH):
        w = weights[l]
        u, zp, cq, ckv, zm, gp, gm, kr = _inproj_fwd(h_res, row(norm_gain, l), w["w_pad"])
        a_pool = _pool_fwd(u, zp, wg[l], row(pool_scale, l))
        q, k, v = _mla_prep_fwd(cq, ckv, kr, tabs, row(q_a_norm_gain, l), row(kv_a_norm_gain, l), row(gqn, l), row(gkn, l),
                                w["wq"], w["wkn"], w["wv"])
        o, lse, nxt = _flash_fwd(q, k, v, gather=shards[l + 1] if l + 1 < DEPTH else ())
        if nxt:
            weights.append(layer_weights(nxt))
        h_next, yp, ym = _merge_fwd(h_res, a_pool, o, zm, gp, gm, w["wpu"], w["wmu"], w["wout"])
        saved.append(dict(x=h_res, u=u, zp=zp, cq=cq, ckv=ckv, zm=zm, gp=gp, gm=gm, kr=kr, a_pool=a_pool, q=q, k=k, v=v,
                          o=o, lse=lse, yp=yp, ym=ym))
        h_res = h_next
    dres, loss_blk = _loss_head(h_res, loss_target[0])

    gw = {n: [None] * DEPTH for n in names}
    pieces = [None] * DEPTH
    grad_stacks = lambda l: [to_shards(n)(gw[n][l]).astype(BF16) for n in names]
    gs = {n: [None] * DEPTH for n in ("norm_gain", "pool_w_group", "pool_scale", "q_a", "kv_a", "q_norm", "k_norm")}
    for l in reversed(range(DEPTH)):
        s, w = saved[l], weights[l]
        merged, dyp, dym, dgp, dgm, dap, amla, do, dzm, delta = _merge_bwd(
            dres, s["yp"], s["ym"], s["gp"], s["gm"], s["o"], s["zm"], w["wout"], w["wpu"], w["wmu"])
        (gw["w_out"][l],) = _weight_grads(merged, [dres], "grad_w_out")
        (gw["pool_w_up"][l],) = _weight_grads(s["a_pool"], [dyp], "grad_pool_w_up")
        (gw["mla_w_up"][l],) = _weight_grads(amla, [dym], "grad_mla_w_up")
        dq, dk, dv, got = _flash_bwd(s["q"], s["k"], s["v"], do, s["lse"], delta,
                                     scatter=grad_stacks(l + 1) if l + 1 < DEPTH else ())
        if got:
            pieces[l + 1] = got
        dcq, dckv, dkr, dwq, dwkn, dwv, gs["q_a"][l], gs["kv_a"][l], dgqn, dgkn = _mla_prep_bwd(
            dq, dk, dv, s["cq"], s["ckv"], s["kr"], tabs, row(q_a_norm_gain, l), row(kv_a_norm_gain, l), row(gqn, l), row(gkn, l),
            w["wq"], w["wkn"], w["wv"])
        gs["q_norm"][l] = dgqn[:, :QK_DIM]
        gs["k_norm"][l] = dgkn[:, :QK_DIM]
        gw["w_q_b"][l] = dwq.reshape(Q_RANK, N_HEADS, HEAD_PAD)[..., :QK_DIM].reshape(Q_RANK, N_HEADS * QK_DIM)
        gw["w_kv_b"][l] = jnp.concatenate([dwkn.reshape(KV_RANK, N_HEADS, HEAD_PAD)[..., :NOPE],
                                           dwv.reshape(KV_RANK, N_HEADS, V_DIM)], axis=-1).reshape(KV_RANK, N_HEADS * (NOPE + V_DIM))
        du, dzp, gs["pool_w_group"][l], gs["pool_scale"][l] = _pool_bwd(dap, s["u"], s["zp"], wg[l], row(pool_scale, l))
        dparts = [du, dzp, dcq, dckv, dzm, dgp, dgm, dkr]
        dres, h, gs["norm_gain"][l] = _inproj_bwd(dres, s["x"], row(norm_gain, l), w["w_pad"], dparts)
        ga = _weight_grads(h, [du, dzp, dcq, dckv, dkr], "grad_w_in_a", transposed=True)
        gb = _weight_grads(h, [dzm, dgp, dgm], "grad_w_in_b", transposed=True)
        gw["w_in"][l] = _unpad_w_in([ga[0], ga[1], ga[2], ga[3], gb[0], gb[1], gb[2], ga[4]])
    grad_x = dres[ROW0:ROW0 + seq][None]

    pieces[0] = _chip_exchange(grad_stacks(0), scatter=True, name="scatter_layer0")
    sums = [_sum_pieces(jnp.stack([pieces[l][a] for l in range(DEPTH)], axis=1), "sum_" + n) for a, n in enumerate(names)]
    other = _sibling_exchange(sums, name="swap_core_sums")
    moments = dict(w_in=(tr(m_w_in), tr(v_w_in)), pool_w_up=(m_pool_w_up, v_pool_w_up), w_q_b=(m_w_q_b, v_w_q_b),
                   w_kv_b=(m_w_kv_b, v_w_kv_b), mla_w_up=(m_mla_w_up, v_mla_w_up), w_out=(m_w_out, v_w_out))
    big_out = {n: _adamw(big[n], [sm, ot], moments[n][0], moments[n][1], "adamw_" + n)
               for n, sm, ot in zip(names, sums, other)}

    small_names = ("norm_gain", "pool_w_group", "pool_scale", "q_a", "kv_a", "q_norm", "k_norm")
    small_w = dict(norm_gain=(norm_gain, m_norm_gain, v_norm_gain), pool_w_group=(pool_w_group, m_pool_w_group, v_pool_w_group),
                   pool_scale=(pool_scale, m_pool_scale, v_pool_scale), q_a=(q_a_norm_gain, m_q_a_norm_gain, v_q_a_norm_gain),
                   kv_a=(kv_a_norm_gain, m_kv_a_norm_gain, v_kv_a_norm_gain), q_norm=(q_norm_gain, m_q_norm_gain, v_q_norm_gain),
                   k_norm=(k_norm_gain, m_k_norm_gain, v_k_norm_gain))
    small_g = {n: jnp.stack(gs[n]).reshape(small_w[n][0].shape) for n in small_names}
    blocks = [_flat_rows(small_g[n]) for n in small_names]
    n_rows = [b.shape[0] for b in blocks]
    meta_rows = N_META * D_MODEL // 128
    pack = jnp.concatenate(blocks + [dres[PAD_FRONT:ROW0].reshape(meta_rows, 128), loss_blk], axis=0)
    pack = jnp.pad(pack, ((0, (-pack.shape[0]) % 8), (0, 0)))
    total = _all_reduce_small(pack)
    n_small = sum(n_rows)
    loss = total[n_small + meta_rows, 0]
    gmeta = lax.dynamic_slice_in_dim(total[n_small:n_small + meta_rows].reshape(N_META, D_MODEL), chip * (D_MODEL // N_CHIPS),
                                     D_MODEL // N_CHIPS, axis=1)

    def packed(idx, meta_part):
        p = jnp.concatenate([_flat_rows(small_w[n][idx]) for n in small_names] + [_flat_rows(meta_part)], axis=0)
        return jnp.pad(p, ((0, (-p.shape[0]) % 8), (0, 0)))[None]

    g_pack = jnp.concatenate([total[:n_small], _flat_rows(gmeta)], axis=0)
    g_pack = jnp.pad(g_pack, ((0, (-g_pack.shape[0]) % 8), (0, 0)))[None]
    small_out = _adamw(packed(0, meta_tokens), [g_pack], packed(1, m_meta_tokens), packed(2, v_meta_tokens), "adamw_small")

    def unpack(p):
        res, r0 = {}, 0
        for n, nr in zip(small_names, n_rows):
            shape = small_w[n][0].shape
            res[n] = p[0, r0:r0 + nr].reshape(-1)[:math.prod(shape)].reshape(shape)
            r0 += nr
        res["meta"] = p[0, r0:r0 + N_META * (D_MODEL // N_CHIPS) // 128].reshape(N_META, D_MODEL // N_CHIPS)
        return res

    small_res = [unpack(p) for p in small_out]

    def leaf(kind, name):
        key = {"meta_tokens": "meta", "q_a_norm_gain": "q_a", "kv_a_norm_gain": "kv_a", "q_norm_gain": "q_norm",
               "k_norm_gain": "k_norm"}.get(name, name)
        if name in big_out:
            return tr(big_out[name][kind]) if name == "w_in" else big_out[name][kind]
        return small_res[kind][key]

    order = ("meta_tokens", "norm_gain", "w_in", "pool_w_group", "pool_scale", "pool_w_up", "q_a_norm_gain", "kv_a_norm_gain",
             "w_q_b", "w_kv_b", "q_norm_gain", "k_norm_gain", "mla_w_up", "w_out")
    outs = [loss, grad_x]
    for kind in range(4):
        outs += [leaf(kind, n) for n in order]
    return tuple(outs)
```

```python
import functools
import math

import numpy as np
import jax
import jax.numpy as jnp
from jax import lax
from jax.experimental import pallas as pl
from jax.experimental.pallas import tpu as pltpu

F32 = jnp.float32
BF16 = jnp.bfloat16
MESH = pl.DeviceIdType.MESH

D_MODEL = 1024
DEPTH = 4
N_META = 16
POOL_WIDTH = 512
POOL_WINDOWS = (2, 4, 8, 16)
POOL_GROUPS = 4
GROUP_DIM = 128
N_HEADS = 8
NOPE = 64
ROPE = 32
QK_DIM = 96
V_DIM = 64
MLA_WIDTH = 512
KV_RANK = 256
Q_RANK = 768
ROPE_THETA = 10000.0
EPS = 1e-6
MASK_VALUE = -1e30
ATTN_BLOCK = 128
PAD_FRONT = (-N_META) % ATTN_BLOCK
ROW0 = PAD_FRONT + N_META
HEAD_PAD = 128
HALO = 16
N_CHIPS = 4
N_DEV = 8

IN_NAMES = ("u", "zp", "cq", "ckv", "zm", "gp", "gm", "kr")
IN_WIDTHS = (512, 512, 768, 256, 512, 1024, 1024, 128)
IN_OFFS = tuple(int(v) for v in np.cumsum((0,) + IN_WIDTHS[:-1]))
IN_PAD = sum(IN_WIDTHS)
KR_LANE0 = NOPE

ADAM_LR = 0.001
ADAM_B1 = 0.9
ADAM_B2 = 0.999
ADAM_EPS = 1e-08
ADAM_WD = 0.01
ADAM_STEP = 10

VMEM_LIMIT = 56 * 1024 * 1024
ATTN_TILE = 768
ROW_TILES = (768, 384)
ROW_TILES_HEAVY = (384,)
ROW_BLOCK = 32
LOG2E = 1.4426950408889634
LN2 = 0.6931471805599453
Q_PRESCALE = LOG2E / math.sqrt(QK_DIM)


def _cp(sem=None, vmem=VMEM_LIMIT):
    kw = dict(vmem_limit_bytes=vmem)
    if sem is not None:
        kw["dimension_semantics"] = sem
    return pltpu.CompilerParams(**kw)


def _row_tile(n_rows, prefs=None):
    for t in prefs or ROW_TILES:
        if n_rows % t == 0:
            return t
    raise ValueError(f"no row tile for {n_rows}")


def _nt(a, b):
    return lax.dot_general(a, b, (((1,), (1,)), ((), ())), preferred_element_type=F32)


def _tn(a, b):
    return lax.dot_general(a, b, (((0,), (0,)), ((), ())), preferred_element_type=F32)


def _mm(a, b):
    return jnp.dot(a, b, preferred_element_type=F32)


def _sigmoid(x):
    return 0.5 * jnp.tanh(0.5 * x) + 0.5


def _resident(shape):
    nd = len(shape)
    return pl.BlockSpec(shape, lambda *_: (0,) * nd, pipeline_mode=pl.Buffered(1))


def _rows(tm, width):
    return pl.BlockSpec((tm, width), lambda i: (i, 0))


def _rope_tables(pos_col, inv_freq_row):
    lp = pos_col.shape[0]
    tm = _row_tile(lp)

    def body(p_ref, f_ref, c_ref, s1_ref, s2_ref):
        ang = p_ref[...].astype(F32) * f_ref[...]
        lane = lax.broadcasted_iota(jnp.int32, ang.shape, 1)
        cs = jnp.cos(ang)
        sn = jnp.sin(ang)
        c_ref[...] = jnp.where(lane < NOPE, 1.0, jnp.where(lane < QK_DIM, cs, 0.0))
        s1_ref[...] = jnp.where((lane >= NOPE) & (lane < NOPE + ROPE // 2), -sn, 0.0)
        s2_ref[...] = jnp.where((lane >= NOPE + ROPE // 2) & (lane < QK_DIM), sn, 0.0)

    out = jax.ShapeDtypeStruct((lp, HEAD_PAD), F32)
    return pl.pallas_call(
        body, name="rope_tables", grid=(lp // tm,),
        in_specs=[pl.BlockSpec((tm, 1), lambda i: (i, 0)), pl.BlockSpec((1, HEAD_PAD), lambda i: (0, 0))],
        out_specs=[_rows(tm, HEAD_PAD)] * 3, out_shape=[out] * 3,
        compiler_params=_cp(("parallel",)),
    )(pos_col, inv_freq_row)


def _rope(y, c, s1, s2):
    return y * c + pltpu.roll(y, HEAD_PAD - ROPE // 2, 1) * s1 + pltpu.roll(y, ROPE // 2, 1) * s2


def _rope_t(g, c, s1, s2):
    return g * c + pltpu.roll(g * s1, ROPE // 2, 1) + pltpu.roll(g * s2, HEAD_PAD - ROPE // 2, 1)


def _inproj_fwd(x, gain, w_pad, gather=()):
    lp = x.shape[0]
    tm = _row_tile(lp)
    n_out = len(IN_WIDTHS)
    ex = _ChipExchange(list(gather), scatter=False)

    def body(x_ref, g_ref, w_ref, *rest):
        ex_in, outs, ex_out, ex_sems = rest[:ex.n], rest[ex.n:ex.n + n_out], rest[ex.n + n_out:2 * ex.n + n_out], rest[2 * ex.n + n_out:]
        if ex.n:
            pl.when(pl.program_id(0) == 0)(lambda: ex.start(ex_in, ex_out, ex_sems))
        xf = x_ref[...]
        inv = lax.rsqrt(jnp.mean(xf * xf, axis=-1, keepdims=True) + EPS)
        h = (xf * inv * g_ref[...]).astype(BF16)
        for o_ref, off, wd in zip(outs, IN_OFFS, IN_WIDTHS):
            o_ref[...] = _nt(h, w_ref[off:off + wd, :]).astype(o_ref.dtype)
        if ex.n:
            pl.when(pl.program_id(0) == lp // tm - 1)(lambda: ex.wait(ex_in, ex_out, ex_sems))

    res = pl.pallas_call(
        body, name="inproj_fwd_gather" if ex.n else "inproj_fwd", grid=(lp // tm,),
        in_specs=[_rows(tm, D_MODEL), pl.BlockSpec((1, D_MODEL), lambda i: (0, 0)), _resident((IN_PAD, D_MODEL))] + ex.specs,
        out_specs=[_rows(tm, wd) for wd in IN_WIDTHS] + ex.specs,
        out_shape=[jax.ShapeDtypeStruct((lp, wd), BF16) for wd in IN_WIDTHS] + ex.out_shape,
        scratch_shapes=ex.sems if ex.n else [],
        compiler_params=_cp(("arbitrary",)),
    )(x, gain, w_pad, *gather)
    return res[:n_out], res[n_out:]


def _inv_counts(tile_idx, tm):
    row = tile_idx * tm + lax.broadcasted_iota(jnp.int32, (tm, 1), 0)
    t1 = jnp.maximum(row - PAD_FRONT + 1, 1).astype(F32)
    return [1.0 / jnp.minimum(t1, float(w)) for w in POOL_WINDOWS]


def _trailing_sums(e):
    s2 = e + pltpu.roll(e, 1, 0)
    s4 = s2 + pltpu.roll(s2, 2, 0)
    s8 = s4 + pltpu.roll(s4, 4, 0)
    s16 = s8 + pltpu.roll(s8, 8, 0)
    return (s2, s4, s8, s16)


def _leading_sums(e):
    n = e.shape[0]
    s2 = e + pltpu.roll(e, n - 1, 0)
    s4 = s2 + pltpu.roll(s2, n - 2, 0)
    s8 = s4 + pltpu.roll(s4, n - 4, 0)
    s16 = s8 + pltpu.roll(s8, n - 8, 0)
    return (s2, s4, s8, s16)


def _pool_fwd(u, zp, wg, scale):
    lp = u.shape[0]
    tm = _row_tile(lp)

    def body(u_ref, z_ref, wg_ref, sc_ref, a_ref, ext_ref):
        i = pl.program_id(0)

        @pl.when(i == 0)
        def _():
            ext_ref[0:HALO, :] = jnp.zeros((HALO, POOL_WIDTH), F32)

        ext_ref[HALO:HALO + tm, :] = u_ref[...].astype(F32)
        e = ext_ref[...]
        sums = _trailing_sums(e)
        ext_ref[0:HALO, :] = e[tm:tm + HALO, :]
        inv_cnt = _inv_counts(i, tm)
        for g in range(POOL_GROUPS):
            cols = slice(g * GROUP_DIM, (g + 1) * GROUP_DIM)
            mixed = sums[g][HALO:, cols] * inv_cnt[g] - e[HALO:, cols]
            y = _mm(mixed.astype(BF16), wg_ref[g]) * sc_ref[:, cols]
            zf = z_ref[:, cols].astype(F32)
            a_ref[:, cols] = (y * (zf * _sigmoid(zf))).astype(a_ref.dtype)

    return pl.pallas_call(
        body, name="pool_fwd", grid=(lp // tm,),
        in_specs=[_rows(tm, POOL_WIDTH), _rows(tm, POOL_WIDTH),
                  pl.BlockSpec((POOL_GROUPS, GROUP_DIM, GROUP_DIM), lambda i: (0, 0, 0)),
                  pl.BlockSpec((1, POOL_WIDTH), lambda i: (0, 0))],
        out_specs=_rows(tm, POOL_WIDTH), out_shape=jax.ShapeDtypeStruct((lp, POOL_WIDTH), BF16),
        scratch_shapes=[pltpu.VMEM((HALO + tm, POOL_WIDTH), F32)],
        compiler_params=_cp(("arbitrary",)),
    )(u, zp, wg, scale)


def _rms_fwd(xf, gain):
    inv = lax.rsqrt(jnp.mean(xf * xf, axis=-1, keepdims=True) + EPS)
    xhat = xf * inv
    return inv, xhat, xhat * gain


def _rms_bwd(dy, inv, xhat, gain):
    dgain = jnp.sum(dy * xhat, axis=0, keepdims=True)
    dyg = dy * gain
    dx = inv * (dyg - xhat * jnp.mean(dyg * xhat, axis=-1, keepdims=True))
    return dx, dgain


def _head_norm_fwd(xh, gain128):
    inv = lax.rsqrt(jnp.sum(xh * xh, axis=-1, keepdims=True) * (1.0 / QK_DIM) + EPS)
    xhat = xh * inv
    return inv, xhat, xhat * gain128


def _head_norm_bwd(dy, inv, xhat, gain128):
    dyg = dy * gain128
    return inv * (dyg - xhat * (jnp.sum(dyg * xhat, axis=-1, keepdims=True) * (1.0 / QK_DIM)))


def _mla_prep_fwd(cq, ckv, kr, tabs, gqa, gkva, gqn, gkn, wq, wkn, wv):
    lp = cq.shape[0]
    tm = _row_tile(lp)

    def body(cq_ref, ckv_ref, kr_ref, c_ref, s1_ref, s2_ref, gqa_ref, gkva_ref, gqn_ref, gkn_ref,
             wq_ref, wkn_ref, wv_ref, q_ref, k_ref, v_ref):
        c, s1, s2 = c_ref[...], s1_ref[...], s2_ref[...]
        _, _, cqn = _rms_fwd(cq_ref[...].astype(F32), gqa_ref[...])
        qraw = _mm(cqn.astype(BF16), wq_ref[...])
        for h in range(N_HEADS):
            hb = slice(h * HEAD_PAD, (h + 1) * HEAD_PAD)
            _, _, yh = _head_norm_fwd(qraw[:, hb], gqn_ref[...])
            q_ref[:, hb] = (_rope(yh, c, s1, s2) * Q_PRESCALE).astype(q_ref.dtype)
        _, _, ckvn = _rms_fwd(ckv_ref[...].astype(F32), gkva_ref[...])
        ckvn_b = ckvn.astype(BF16)
        knraw = _mm(ckvn_b, wkn_ref[...])
        krs = kr_ref[...].astype(F32)
        for h in range(N_HEADS):
            hb = slice(h * HEAD_PAD, (h + 1) * HEAD_PAD)
            _, _, yh = _head_norm_fwd(knraw[:, hb] + krs, gkn_ref[...])
            k_ref[:, hb] = _rope(yh, c, s1, s2).astype(k_ref.dtype)
        v_ref[...] = _mm(ckvn_b, wv_ref[...]).astype(v_ref.dtype)

    hw = N_HEADS * HEAD_PAD
    vec = lambda n: pl.BlockSpec((1, n), lambda i: (0, 0))
    return pl.pallas_call(
        body, name="mla_prep_fwd", grid=(lp // tm,),
        in_specs=[_rows(tm, Q_RANK), _rows(tm, KV_RANK), _rows(tm, HEAD_PAD)] + [_rows(tm, HEAD_PAD)] * 3
        + [vec(Q_RANK), vec(KV_RANK), vec(HEAD_PAD), vec(HEAD_PAD),
           _resident((Q_RANK, hw)), _resident((KV_RANK, hw)), _resident((KV_RANK, MLA_WIDTH))],
        out_specs=[_rows(tm, hw), _rows(tm, hw), _rows(tm, MLA_WIDTH)],
        out_shape=[jax.ShapeDtypeStruct((lp, hw), BF16), jax.ShapeDtypeStruct((lp, hw), BF16),
                   jax.ShapeDtypeStruct((lp, MLA_WIDTH), BF16)],
        compiler_params=_cp(("parallel",)),
    )(cq, ckv, kr, *tabs, gqa, gkva, gqn, gkn, wq, wkn, wv)


def _causal_mask(s, q0, k0):
    qi = q0 + lax.broadcasted_iota(jnp.int32, s.shape, 0)
    ki = k0 + lax.broadcasted_iota(jnp.int32, s.shape, 1)
    return jnp.where((ki <= qi) & (ki >= PAD_FRONT), s, MASK_VALUE)


def _score_chunks(kind, r, tk):
    if kind == "inner":
        return [(c0, False) for c0 in range(0, tk, ATTN_BLOCK)]
    if kind == "first":
        return [(c0, c0 == 0) for c0 in range(0, tk, ATTN_BLOCK)]
    return [(c0, True) for c0 in range(0, min(tk, (r + 1) * ROW_BLOCK), ATTN_BLOCK)]


def _tile_kinds(i, t):
    return (("diag", t == i), ("first", (t == 0) & (i > 0)), ("inner", (t > 0) & (t < i)))


def _lanes(col, width=HEAD_PAD):
    return jnp.broadcast_to(col, (col.shape[0], width))


def _flash_fwd(q, k, v, gather=()):
    lp = q.shape[0]
    tq = tk = ATTN_TILE
    nq = lp // tq
    nj = N_HEADS // 2
    n_blocks = tq // ROW_BLOCK
    ex = _ChipExchange(list(gather), scatter=False)

    pairs = [(i, t) for i in range(nq) for t in range(i + 1)]
    i_tab = jnp.asarray([p[0] for p in pairs], jnp.int32)
    t_tab = jnp.asarray([p[1] for p in pairs], jnp.int32)

    def body(i_tab_ref, t_tab_ref, q_ref, k_ref, v_ref, *rest):
        ex_in, (o_ref, lse_ref), ex_out = rest[:ex.n], rest[ex.n:ex.n + 2], rest[ex.n + 2:2 * ex.n + 2]
        m_scr, acc_scr, s_scr, p_scr, part_scr, vext_scr = rest[2 * ex.n + 2:2 * ex.n + 8]
        ex_sems = rest[2 * ex.n + 8:]
        j, step_no = pl.program_id(0), pl.program_id(1)
        i, t = i_tab_ref[step_no], t_tab_ref[step_no]
        if ex.n:
            pl.when((j == 0) & (step_no == 0))(lambda: ex.start(ex_in, ex_out, ex_sems))

        @pl.when(t == 0)
        def _():
            m_scr[...] = jnp.full(m_scr.shape, MASK_VALUE, F32)
            acc_scr[...] = jnp.zeros(acc_scr.shape, F32)

        def step(kind):
            def scores(hh, r, c0, masked):
                s = s_scr[hh, r * ROW_BLOCK:(r + 1) * ROW_BLOCK, c0:c0 + ATTN_BLOCK]
                return _causal_mask(s, i * tq + r * ROW_BLOCK, t * tk + c0) if masked else s

            vext_scr[:, 0:HEAD_PAD] = v_ref[...]
            vext_scr[:, HEAD_PAD:2 * HEAD_PAD] = jnp.ones((tk, HEAD_PAD), BF16)
            for hh in range(2):
                hb = slice(hh * HEAD_PAD, (hh + 1) * HEAD_PAD)
                s_scr[hh] = _nt(q_ref[:, hb], k_ref[:, hb])
            for hh in range(2):
                for r in range(n_blocks):
                    part = None
                    for c0, masked in _score_chunks(kind, r, tk):
                        s = scores(hh, r, c0, masked)
                        part = s if part is None else jnp.maximum(part, s)
                    part_scr[r * ROW_BLOCK:(r + 1) * ROW_BLOCK, :] = part
                m_prev = m_scr[hh]
                m_new = jnp.maximum(m_prev, _lanes(jnp.max(part_scr[...], axis=-1, keepdims=True)))
                alpha = jnp.exp2(m_prev - m_new)
                m_scr[hh] = m_new
                for r in range(n_blocks):
                    rows = slice(r * ROW_BLOCK, (r + 1) * ROW_BLOCK)
                    m_r = m_scr[hh, rows, :]
                    chunks = _score_chunks(kind, r, tk)
                    for c0, masked in chunks:
                        p_scr[hh, rows, c0:c0 + ATTN_BLOCK] = jnp.exp2((scores(hh, r, c0, masked) - m_r).astype(BF16))
                    done = chunks[-1][0] + ATTN_BLOCK
                    if done < tk:
                        p_scr[hh, rows, done:tk] = jnp.zeros((ROW_BLOCK, tk - done), BF16)
                acc_scr[hh] = jnp.concatenate([alpha, alpha], axis=1) * acc_scr[hh] + _mm(p_scr[hh], vext_scr[...])

        for kind, pred in _tile_kinds(i, t):
            pl.when(pred)(functools.partial(step, kind))

        @pl.when(t == i)
        def _():
            lane = lax.broadcasted_iota(jnp.int32, (tq, HEAD_PAD), 1)
            l0, l1 = acc_scr[0, :, HEAD_PAD:2 * HEAD_PAD], acc_scr[1, :, HEAD_PAD:2 * HEAD_PAD]
            o = jnp.where(lane < V_DIM, acc_scr[0, :, 0:HEAD_PAD] / l0, acc_scr[1, :, 0:HEAD_PAD] / l1)
            o_ref[...] = o.astype(o_ref.dtype)
            lse_ref[...] = jnp.where(lane < V_DIM, m_scr[0] + jnp.log(l0) * LOG2E, m_scr[1] + jnp.log(l1) * LOG2E)

        if ex.n:
            pl.when((j == nj - 1) & (step_no == len(pairs) - 1))(lambda: ex.wait(ex_in, ex_out, ex_sems))

    q_idx = lambda j, p, it, tt: (it[p], j)
    kv_idx = lambda j, p, it, tt: (tt[p], j)
    o, lse, *gathered = pl.pallas_call(
        body, name="flash_fwd_gather" if ex.n else "flash_fwd",
        grid_spec=pltpu.PrefetchScalarGridSpec(
            num_scalar_prefetch=2, grid=(nj, len(pairs)),
            in_specs=[pl.BlockSpec((tq, 2 * HEAD_PAD), q_idx), pl.BlockSpec((tk, 2 * HEAD_PAD), kv_idx),
                      pl.BlockSpec((tk, HEAD_PAD), kv_idx)] + ex.specs,
            out_specs=[pl.BlockSpec((tq, HEAD_PAD), q_idx)] * 2 + ex.specs,
            scratch_shapes=[pltpu.VMEM((2, tq, HEAD_PAD), F32), pltpu.VMEM((2, tq, 2 * HEAD_PAD), F32),
                            pltpu.VMEM((2, tq, tk), F32), pltpu.VMEM((2, tq, tk), BF16),
                            pltpu.VMEM((tq, HEAD_PAD), F32), pltpu.VMEM((tk, 2 * HEAD_PAD), BF16)]
            + (ex.sems if ex.n else [])),
        out_shape=[jax.ShapeDtypeStruct((lp, MLA_WIDTH), BF16), jax.ShapeDtypeStruct((lp, MLA_WIDTH), F32)] + ex.out_shape,
        compiler_params=_cp(("arbitrary",) * 2),
    )(i_tab, t_tab, q, k, v, *gather)
    return o, lse, gathered


def _merge_fwd(x, a_pool, o, zm, gp, gm, wpu, wmu, wout):
    lp = x.shape[0]
    tm = _row_tile(lp)

    def body(x_ref, ap_ref, o_ref, zm_ref, gp_ref, gm_ref, wpu_ref, wmu_ref, wout_ref, xn_ref, yp_ref, ym_ref):
        yp = _mm(ap_ref[...], wpu_ref[...])
        zf = zm_ref[...].astype(F32)
        amla = o_ref[...].astype(F32) * (zf * _sigmoid(zf))
        ym = _mm(amla.astype(BF16), wmu_ref[...])
        merged = _sigmoid(gp_ref[...].astype(F32)) * yp + _sigmoid(gm_ref[...].astype(F32)) * ym
        xn_ref[...] = x_ref[...] + _mm(merged.astype(BF16), wout_ref[...])
        yp_ref[...] = yp.astype(yp_ref.dtype)
        ym_ref[...] = ym.astype(ym_ref.dtype)

    return pl.pallas_call(
        body, name="merge_fwd", grid=(lp // tm,),
        in_specs=[_rows(tm, D_MODEL), _rows(tm, POOL_WIDTH), _rows(tm, MLA_WIDTH), _rows(tm, MLA_WIDTH),
                  _rows(tm, D_MODEL), _rows(tm, D_MODEL),
                  _resident((POOL_WIDTH, D_MODEL)), _resident((MLA_WIDTH, D_MODEL)), _resident((D_MODEL, D_MODEL))],
        out_specs=[_rows(tm, D_MODEL)] * 3,
        out_shape=[jax.ShapeDtypeStruct((lp, D_MODEL), F32), jax.ShapeDtypeStruct((lp, D_MODEL), BF16),
                   jax.ShapeDtypeStruct((lp, D_MODEL), BF16)],
        compiler_params=_cp(("parallel",)),
    )(x, a_pool, o, zm, gp, gm, wpu, wmu, wout)


def _loss_head(y, target):
    lp = y.shape[0]
    tm = ROW0
    n_real = target.shape[0] // tm

    def body(y_ref, t_ref, d_ref, l_ref):
        i = pl.program_id(0)

        @pl.when(i == 0)
        def _():
            l_ref[...] = jnp.zeros(l_ref.shape, F32)

        real = (i >= 1) & (i <= n_real)
        err = jnp.where(real, y_ref[...] - t_ref[...], 0.0)
        d_ref[...] = err * (1.0 / D_MODEL)
        l_ref[...] += jnp.sum(err * err) * (0.5 / D_MODEL)

    return pl.pallas_call(
        body, name="loss_head", grid=(lp // tm,),
        in_specs=[_rows(tm, D_MODEL), pl.BlockSpec((tm, D_MODEL), lambda i: (jnp.clip(i - 1, 0, n_real - 1), 0))],
        out_specs=[_rows(tm, D_MODEL), pl.BlockSpec((8, 128), lambda i: (0, 0))],
        out_shape=[jax.ShapeDtypeStruct((lp, D_MODEL), F32), jax.ShapeDtypeStruct((8, 128), F32)],
        compiler_params=_cp(("arbitrary",)),
    )(y, target)


def _pair_rowsum(prod):
    lane = lax.broadcasted_iota(jnp.int32, prod.shape, 1)
    lo = jnp.sum(jnp.where(lane < V_DIM, prod, 0.0), axis=-1, keepdims=True)
    hi = jnp.sum(jnp.where(lane < V_DIM, 0.0, prod), axis=-1, keepdims=True)
    return jnp.where(lane < V_DIM, lo, hi)


def _merge_bwd(dres, yp, ym, gp, gm, o, zm, wout, wpu, wmu):
    lp = dres.shape[0]
    tm = _row_tile(lp, ROW_TILES_HEAVY)

    def body(dres_ref, yp_ref, ym_ref, gp_ref, gm_ref, o_ref, zm_ref, wout_ref, wpu_ref, wmu_ref,
             merged_ref, dyp_ref, dym_ref, dgp_ref, dgm_ref, dap_ref, amla_ref, do_ref, dzm_ref, delta_ref):
        dmerged = _nt(dres_ref[...].astype(BF16), wout_ref[...])
        sp = _sigmoid(gp_ref[...].astype(F32))
        sm = _sigmoid(gm_ref[...].astype(F32))
        ypf = yp_ref[...].astype(F32)
        ymf = ym_ref[...].astype(F32)
        merged_ref[...] = (sp * ypf + sm * ymf).astype(merged_ref.dtype)
        dyp = (dmerged * sp).astype(BF16)
        dym = (dmerged * sm).astype(BF16)
        dyp_ref[...] = dyp
        dym_ref[...] = dym
        dgp_ref[...] = (dmerged * ypf * sp * (1.0 - sp)).astype(dgp_ref.dtype)
        dgm_ref[...] = (dmerged * ymf * sm * (1.0 - sm)).astype(dgm_ref.dtype)
        dap_ref[...] = _nt(dyp, wpu_ref[...]).astype(dap_ref.dtype)
        dam = _nt(dym, wmu_ref[...])
        zf = zm_ref[...].astype(F32)
        sg = _sigmoid(zf)
        si = zf * sg
        of = o_ref[...].astype(F32)
        amla_ref[...] = (of * si).astype(amla_ref.dtype)
        do = dam * si
        do_ref[...] = do.astype(do_ref.dtype)
        dzm_ref[...] = (dam * of * (sg * (1.0 + zf * (1.0 - sg)))).astype(dzm_ref.dtype)
        prod = do * of
        for j in range(N_HEADS // 2):
            hb = slice(j * HEAD_PAD, (j + 1) * HEAD_PAD)
            delta_ref[:, hb] = _pair_rowsum(prod[:, hb])

    bf = lambda w: jax.ShapeDtypeStruct((lp, w), BF16)
    return pl.pallas_call(
        body, name="merge_bwd", grid=(lp // tm,),
        in_specs=[_rows(tm, D_MODEL)] * 5 + [_rows(tm, MLA_WIDTH)] * 2
        + [_resident((D_MODEL, D_MODEL)), _resident((POOL_WIDTH, D_MODEL)), _resident((MLA_WIDTH, D_MODEL))],
        out_specs=[_rows(tm, D_MODEL)] * 5 + [_rows(tm, POOL_WIDTH)] + [_rows(tm, MLA_WIDTH)] * 4,
        out_shape=[bf(D_MODEL)] * 5 + [bf(POOL_WIDTH)] + [bf(MLA_WIDTH)] * 3 + [jax.ShapeDtypeStruct((lp, MLA_WIDTH), F32)],
        compiler_params=_cp(("parallel",)),
    )(dres, yp, ym, gp, gm, o, zm, wout, wpu, wmu)


def _flash_bwd(q, k, v, do, lse, delta, scatter=()):
    lp = q.shape[0]
    tq = tk = ATTN_TILE
    nq = lp // tq
    nj = N_HEADS // 2
    scale = 1.0 / math.sqrt(QK_DIM)
    ex = _ChipExchange(list(scatter), scatter=True)

    pairs = [(t, i) for t in range(nq) for i in range(t, nq)]
    t_tab = jnp.asarray([p[0] for p in pairs], jnp.int32)
    i_tab = jnp.asarray([p[1] for p in pairs], jnp.int32)

    def body(t_tab_ref, i_tab_ref, q_ref, k_ref, v_ref, do_ref, lse_ref, dl_ref, *rest):
        ex_in, (dq_ref, dk_ref, dv_ref), ex_out = rest[:ex.n], rest[ex.n:ex.n + 3], rest[ex.n + 3:2 * ex.n + 3]
        dq_acc, dk_acc, dv_acc, s_scr, dp_scr, p_scr, ds_scr, doh_scr, stat_scr = rest[2 * ex.n + 3:2 * ex.n + 12]
        ex_sems = rest[2 * ex.n + 12:]
        j, step_no = pl.program_id(0), pl.program_id(1)
        t, i = t_tab_ref[step_no], i_tab_ref[step_no]
        if ex.n:
            pl.when((j == 0) & (step_no == 0))(lambda: ex.start(ex_in, ex_out, ex_sems))

        @pl.when(step_no == 0)
        def _():
            dq_acc[...] = jnp.zeros(dq_acc.shape, F32)

        @pl.when(i == t)
        def _():
            dk_acc[...] = jnp.zeros(dk_acc.shape, F32)
            dv_acc[...] = jnp.zeros(dv_acc.shape, F32)

        def step(kind):
            lane = lax.broadcasted_iota(jnp.int32, (tq, HEAD_PAD), 1)
            for hh in range(2):
                hb = slice(hh * HEAD_PAD, (hh + 1) * HEAD_PAD)
                mine = (lane < V_DIM) if hh == 0 else (lane >= V_DIM)
                doh_scr[hh] = jnp.where(mine, do_ref[...], jnp.zeros_like(do_ref[...]))
                s_scr[hh] = _nt(q_ref[:, hb], k_ref[:, hb])
                dp_scr[hh] = _nt(doh_scr[hh], v_ref[...])
            for hh in range(2):
                hb = slice(hh * HEAD_PAD, (hh + 1) * HEAD_PAD)
                col = slice(hh * V_DIM, hh * V_DIM + 1)
                stat_scr[0] = _lanes(lse_ref[:, col])
                stat_scr[1] = _lanes(dl_ref[:, col])
                for r in range(tq // ROW_BLOCK):
                    rows = slice(r * ROW_BLOCK, (r + 1) * ROW_BLOCK)
                    lse_r = stat_scr[0, rows, :]
                    dl_r = stat_scr[1, rows, :]
                    chunks = _score_chunks(kind, r, tk)
                    for c0, masked in chunks:
                        cols = slice(c0, c0 + ATTN_BLOCK)
                        s = s_scr[hh, rows, cols]
                        if masked:
                            s = _causal_mask(s, i * tq + r * ROW_BLOCK, t * tk + c0)
                        p = jnp.exp2(s - lse_r)
                        p_scr[hh, rows, cols] = p.astype(BF16)
                        ds_scr[hh, rows, cols] = (p * (dp_scr[hh, rows, cols] - dl_r)).astype(BF16)
                    done = chunks[-1][0] + ATTN_BLOCK
                    if done < tk:
                        zeros = jnp.zeros((ROW_BLOCK, tk - done), BF16)
                        p_scr[hh, rows, done:tk] = zeros
                        ds_scr[hh, rows, done:tk] = zeros
                dv_acc[...] += _tn(doh_scr[hh], p_scr[hh])
                dk_acc[hh] += _tn(q_ref[:, hb], ds_scr[hh])
                dq_acc[i, hb, :] += lax.dot_general(k_ref[:, hb], ds_scr[hh], (((0,), (1,)), ((), ())),
                                                    preferred_element_type=F32)

        for kind, pred in _tile_kinds(i, t):
            pl.when(pred)(functools.partial(step, kind))

        @pl.when(i == nq - 1)
        def _():
            dk_ref[:, 0:HEAD_PAD] = (dk_acc[0].T * LN2).astype(dk_ref.dtype)
            dk_ref[:, HEAD_PAD:2 * HEAD_PAD] = (dk_acc[1].T * LN2).astype(dk_ref.dtype)
            dv_ref[...] = dv_acc[...].T.astype(dv_ref.dtype)

        @pl.when(step_no == len(pairs) - 1)
        def _():
            for qi in range(nq):
                for hh in range(2):
                    hb = slice(hh * HEAD_PAD, (hh + 1) * HEAD_PAD)
                    dq_ref[qi * tq:(qi + 1) * tq, hb] = (dq_acc[qi, hb, :].T * scale).astype(dq_ref.dtype)

        if ex.n:
            pl.when((j == nj - 1) & (step_no == len(pairs) - 1))(lambda: ex.wait(ex_in, ex_out, ex_sems))

    q_idx = lambda j, p, tt, it: (it[p], j)
    kv_idx = lambda j, p, tt, it: (tt[p], j)
    hw = N_HEADS * HEAD_PAD
    dq, dk, dv, *pieces = pl.pallas_call(
        body, name="flash_bwd_scatter" if ex.n else "flash_bwd",
        grid_spec=pltpu.PrefetchScalarGridSpec(
            num_scalar_prefetch=2, grid=(nj, len(pairs)),
            in_specs=[pl.BlockSpec((tq, 2 * HEAD_PAD), q_idx), pl.BlockSpec((tk, 2 * HEAD_PAD), kv_idx),
                      pl.BlockSpec((tk, HEAD_PAD), kv_idx), pl.BlockSpec((tq, HEAD_PAD), q_idx),
                      pl.BlockSpec((tq, HEAD_PAD), q_idx), pl.BlockSpec((tq, HEAD_PAD), q_idx)] + ex.specs,
            out_specs=[pl.BlockSpec((lp, 2 * HEAD_PAD), lambda j, p, tt, it: (0, j)),
                       pl.BlockSpec((tk, 2 * HEAD_PAD), kv_idx), pl.BlockSpec((tk, HEAD_PAD), kv_idx)] + ex.specs,
            scratch_shapes=[pltpu.VMEM((nq, 2 * HEAD_PAD, tq), F32), pltpu.VMEM((2, HEAD_PAD, tk), F32),
                            pltpu.VMEM((HEAD_PAD, tk), F32),
                            pltpu.VMEM((2, tq, tk), F32), pltpu.VMEM((2, tq, tk), F32),
                            pltpu.VMEM((2, tq, tk), BF16), pltpu.VMEM((2, tq, tk), BF16),
                            pltpu.VMEM((2, tq, HEAD_PAD), BF16), pltpu.VMEM((2, tq, HEAD_PAD), F32)]
            + (ex.sems if ex.n else [])),
        out_shape=[jax.ShapeDtypeStruct((lp, hw), BF16), jax.ShapeDtypeStruct((lp, hw), BF16),
                   jax.ShapeDtypeStruct((lp, MLA_WIDTH), BF16)] + ex.out_shape,
        compiler_params=_cp(("arbitrary",) * 2),
    )(t_tab, i_tab, q, k, v, do, lse, delta, *scatter)
    return dq, dk, dv, pieces


def _mla_prep_bwd(dq, dk, dv, cq, ckv, kr, tabs, gqa, gkva, gqn, gkn, wq, wkn, wv):
    lp = cq.shape[0]
    tm = _row_tile(lp, ROW_TILES_HEAVY)
    hw = N_HEADS * HEAD_PAD

    def body(dq_ref, dk_ref, dv_ref, cq_ref, ckv_ref, kr_ref, c_ref, s1_ref, s2_ref, gqa_ref, gkva_ref, gqn_ref,
             gkn_ref, wq_ref, wkn_ref, wv_ref, dcq_ref, dckv_ref, dkr_ref, dwq_ref, dwkn_ref, dwv_ref,
             dgqa_ref, dgkva_ref, dgqn_ref, dgkn_ref, draw_scr):
        @pl.when(pl.program_id(0) == 0)
        def _():
            for r in (dwq_ref, dwkn_ref, dwv_ref, dgqa_ref, dgkva_ref, dgqn_ref, dgkn_ref):
                r[...] = jnp.zeros(r.shape, F32)

        c, s1, s2 = c_ref[...], s1_ref[...], s2_ref[...]
        lane = lax.broadcasted_iota(jnp.int32, (tm, HEAD_PAD), 1)

        inv_q, xhat_q, cqn = _rms_fwd(cq_ref[...].astype(F32), gqa_ref[...])
        cqn_b = cqn.astype(BF16)
        qraw = _mm(cqn_b, wq_ref[...])
        dgqn = jnp.zeros((1, HEAD_PAD), F32)
        for h in range(N_HEADS):
            hb = slice(h * HEAD_PAD, (h + 1) * HEAD_PAD)
            inv, xhat, _ = _head_norm_fwd(qraw[:, hb], gqn_ref[...])
            dy = _rope_t(dq_ref[:, hb].astype(F32), c, s1, s2)
            dgqn += jnp.sum(dy * xhat, axis=0, keepdims=True)
            draw_scr[:, hb] = _head_norm_bwd(dy, inv, xhat, gqn_ref[...]).astype(BF16)
        dgqn_ref[...] += dgqn
        dqraw = draw_scr[...]
        dwq_ref[...] += _tn(cqn_b, dqraw)
        dcq, dgqa = _rms_bwd(_nt(dqraw, wq_ref[...]), inv_q, xhat_q, gqa_ref[...])
        dcq_ref[...] = dcq.astype(dcq_ref.dtype)
        dgqa_ref[...] += dgqa

        inv_kv, xhat_kv, ckvn = _rms_fwd(ckv_ref[...].astype(F32), gkva_ref[...])
        ckvn_b = ckvn.astype(BF16)
        knraw = _mm(ckvn_b, wkn_ref[...])
        krs = kr_ref[...].astype(F32)
        dgkn = jnp.zeros((1, HEAD_PAD), F32)
        dkr = jnp.zeros((tm, HEAD_PAD), F32)
        for h in range(N_HEADS):
            hb = slice(h * HEAD_PAD, (h + 1) * HEAD_PAD)
            inv, xhat, _ = _head_norm_fwd(knraw[:, hb] + krs, gkn_ref[...])
            dy = _rope_t(dk_ref[:, hb].astype(F32), c, s1, s2)
            dgkn += jnp.sum(dy * xhat, axis=0, keepdims=True)
            dxh = _head_norm_bwd(dy, inv, xhat, gkn_ref[...])
            dkr += dxh
            draw_scr[:, hb] = jnp.where(lane < NOPE, dxh, 0.0).astype(BF16)
        dgkn_ref[...] += dgkn
        dkr_ref[...] = jnp.where((lane >= KR_LANE0) & (lane < QK_DIM), dkr, 0.0).astype(dkr_ref.dtype)
        dknraw = draw_scr[...]
        dvb = dv_ref[...]
        dwkn_ref[...] += _tn(ckvn_b, dknraw)
        dwv_ref[...] += _tn(ckvn_b, dvb)
        dckvn = _nt(dknraw, wkn_ref[...]) + _nt(dvb, wv_ref[...])
        dckv, dgkva = _rms_bwd(dckvn, inv_kv, xhat_kv, gkva_ref[...])
        dckv_ref[...] = dckv.astype(dckv_ref.dtype)
        dgkva_ref[...] += dgkva

    vec = lambda n: pl.BlockSpec((1, n), lambda i: (0, 0))
    whole = lambda r, c: pl.BlockSpec((r, c), lambda i: (0, 0))
    f = lambda r, c: jax.ShapeDtypeStruct((r, c), F32)
    return pl.pallas_call(
        body, name="mla_prep_bwd", grid=(lp // tm,),
        in_specs=[_rows(tm, hw), _rows(tm, hw), _rows(tm, MLA_WIDTH), _rows(tm, Q_RANK), _rows(tm, KV_RANK),
                  _rows(tm, HEAD_PAD)] + [_rows(tm, HEAD_PAD)] * 3
        + [vec(Q_RANK), vec(KV_RANK), vec(HEAD_PAD), vec(HEAD_PAD),
           _resident((Q_RANK, hw)), _resident((KV_RANK, hw)), _resident((KV_RANK, MLA_WIDTH))],
        out_specs=[_rows(tm, Q_RANK), _rows(tm, KV_RANK), _rows(tm, HEAD_PAD),
                   whole(Q_RANK, hw), whole(KV_RANK, hw), whole(KV_RANK, MLA_WIDTH),
                   vec(Q_RANK), vec(KV_RANK), vec(HEAD_PAD), vec(HEAD_PAD)],
        out_shape=[jax.ShapeDtypeStruct((lp, Q_RANK), BF16), jax.ShapeDtypeStruct((lp, KV_RANK), BF16),
                   jax.ShapeDtypeStruct((lp, HEAD_PAD), BF16),
                   f(Q_RANK, hw), f(KV_RANK, hw), f(KV_RANK, MLA_WIDTH),
                   f(1, Q_RANK), f(1, KV_RANK), f(1, HEAD_PAD), f(1, HEAD_PAD)],
        scratch_shapes=[pltpu.VMEM((tm, hw), BF16)],
        compiler_params=_cp(("arbitrary",)),
    )(dq, dk, dv, cq, ckv, kr, *tabs, gqa, gkva, gqn, gkn, wq, wkn, wv)


def _pool_bwd(dap, u, zp, wg, scale):
    lp = u.shape[0]
    tm = _row_tile(lp)
    n = lp // tm
    per = tm // HALO

    def body(dap_ref, u_ref, uh_ref, z_ref, wg_ref, sc_ref, du_ref, dz_ref, dwg_ref, dsc_ref, ext_u, ext_d):
        i = pl.program_id(0)
        r = n - 1 - i

        @pl.when(i == 0)
        def _():
            dwg_ref[...] = jnp.zeros(dwg_ref.shape, F32)
            dsc_ref[...] = jnp.zeros(dsc_ref.shape, F32)
            ext_d[tm:tm + HALO, :] = jnp.zeros((HALO, POOL_WIDTH), F32)

        ext_u[0:HALO, :] = jnp.where(r == 0, 0.0, uh_ref[...].astype(F32))
        ext_u[HALO:HALO + tm, :] = u_ref[...].astype(F32)
        e = ext_u[...]
        sums = _trailing_sums(e)
        inv_cnt = _inv_counts(r, tm)
        dmixed = []
        for g in range(POOL_GROUPS):
            cols = slice(g * GROUP_DIM, (g + 1) * GROUP_DIM)
            mixed_b = (sums[g][HALO:, cols] * inv_cnt[g] - e[HALO:, cols]).astype(BF16)
            yg = _mm(mixed_b, wg_ref[g])
            zf = z_ref[:, cols].astype(F32)
            sg = _sigmoid(zf)
            da = dap_ref[:, cols].astype(F32)
            dy = da * (zf * sg)
            dz_ref[:, cols] = (da * (yg * sc_ref[:, cols]) * (sg * (1.0 + zf * (1.0 - sg)))).astype(dz_ref.dtype)
            dsc_ref[:, cols] += jnp.sum(dy * yg, axis=0, keepdims=True)
            dyg = (dy * sc_ref[:, cols]).astype(BF16)
            dwg_ref[g] += _tn(mixed_b, dyg)
            dm = _nt(dyg, wg_ref[g])
            dmixed.append(dm)
            ext_d[0:tm, cols] = dm * inv_cnt[g]
        ed = ext_d[...]
        lead = _leading_sums(ed)
        ext_d[tm:tm + HALO, :] = ed[0:HALO, :]
        for g in range(POOL_GROUPS):
            cols = slice(g * GROUP_DIM, (g + 1) * GROUP_DIM)
            du_ref[:, cols] = (lead[g][0:tm, cols] - dmixed[g]).astype(du_ref.dtype)

    rev = lambda i: (n - 1 - i, 0)
    return pl.pallas_call(
        body, name="pool_bwd", grid=(n,),
        in_specs=[pl.BlockSpec((tm, POOL_WIDTH), rev), pl.BlockSpec((tm, POOL_WIDTH), rev),
                  pl.BlockSpec((HALO, POOL_WIDTH), lambda i: (jnp.maximum((n - 1 - i) * per - 1, 0), 0)),
                  pl.BlockSpec((tm, POOL_WIDTH), rev),
                  pl.BlockSpec((POOL_GROUPS, GROUP_DIM, GROUP_DIM), lambda i: (0, 0, 0)),
                  pl.BlockSpec((1, POOL_WIDTH), lambda i: (0, 0))],
        out_specs=[pl.BlockSpec((tm, POOL_WIDTH), rev), pl.BlockSpec((tm, POOL_WIDTH), rev),
                   pl.BlockSpec((POOL_GROUPS, GROUP_DIM, GROUP_DIM), lambda i: (0, 0, 0)),
                   pl.BlockSpec((1, POOL_WIDTH), lambda i: (0, 0))],
        out_shape=[jax.ShapeDtypeStruct((lp, POOL_WIDTH), BF16), jax.ShapeDtypeStruct((lp, POOL_WIDTH), BF16),
                   jax.ShapeDtypeStruct((POOL_GROUPS, GROUP_DIM, GROUP_DIM), F32),
                   jax.ShapeDtypeStruct((1, POOL_WIDTH), F32)],
        scratch_shapes=[pltpu.VMEM((HALO + tm, POOL_WIDTH), F32), pltpu.VMEM((tm + HALO, POOL_WIDTH), F32)],
        compiler_params=_cp(("arbitrary",)),
    )(dap, u, u, zp, wg, scale)


def _inproj_bwd(dres, x, gain, w_pad, dparts, scatter=()):
    lp = x.shape[0]
    tm = _row_tile(lp, ROW_TILES_HEAVY)
    n_dp = len(IN_WIDTHS)
    ex = _ChipExchange(list(scatter), scatter=True)

    def body(dres_ref, x_ref, g_ref, w_ref, *rest):
        dps, ex_in = rest[:n_dp], rest[n_dp:n_dp + ex.n]
        dprev_ref, h_ref, dg_ref = rest[n_dp + ex.n:n_dp + ex.n + 3]
        ex_out, ex_sems = rest[n_dp + ex.n + 3:n_dp + 2 * ex.n + 3], rest[n_dp + 2 * ex.n + 3:]
        if ex.n:
            pl.when(pl.program_id(0) == 0)(lambda: ex.start(ex_in, ex_out, ex_sems))

        @pl.when(pl.program_id(0) == 0)
        def _():
            dg_ref[...] = jnp.zeros(dg_ref.shape, F32)

        dh = jnp.zeros((tm, D_MODEL), F32)
        for dp_ref, off, wd in zip(dps, IN_OFFS, IN_WIDTHS):
            dh += _mm(dp_ref[...], w_ref[off:off + wd, :])
        inv, xhat, hn = _rms_fwd(x_ref[...], g_ref[...])
        h_ref[...] = hn.astype(h_ref.dtype)
        dx, dgain = _rms_bwd(dh, inv, xhat, g_ref[...])
        dg_ref[...] += dgain
        dprev_ref[...] = dres_ref[...] + dx
        if ex.n:
            pl.when(pl.program_id(0) == lp // tm - 1)(lambda: ex.wait(ex_in, ex_out, ex_sems))

    dprev, h, dgain, *pieces = pl.pallas_call(
        body, name="inproj_bwd_scatter" if ex.n else "inproj_bwd", grid=(lp // tm,),
        in_specs=[_rows(tm, D_MODEL), _rows(tm, D_MODEL), pl.BlockSpec((1, D_MODEL), lambda i: (0, 0)),
                  _resident((IN_PAD, D_MODEL))] + [_rows(tm, wd) for wd in IN_WIDTHS] + ex.specs,
        out_specs=[_rows(tm, D_MODEL), _rows(tm, D_MODEL), pl.BlockSpec((1, D_MODEL), lambda i: (0, 0))] + ex.specs,
        out_shape=[jax.ShapeDtypeStruct((lp, D_MODEL), F32), jax.ShapeDtypeStruct((lp, D_MODEL), BF16),
                   jax.ShapeDtypeStruct((1, D_MODEL), F32)] + ex.out_shape,
        scratch_shapes=ex.sems if ex.n else [],
        compiler_params=_cp(("arbitrary",)),
    )(dres, x, gain, w_pad, *dparts, *scatter)
    return dprev, h, dgain, pieces


def _weight_grads(a, bs, name, transposed=False):
    lp, m = a.shape
    tk = _row_tile(lp)
    nb = len(bs)
    shapes = [(b.shape[1], m) if transposed else (m, b.shape[1]) for b in bs]

    def body(a_ref, *rest):
        b_refs, o_refs = rest[:nb], rest[nb:]

        @pl.when(pl.program_id(0) == 0)
        def _():
            for o_ref in o_refs:
                o_ref[...] = jnp.zeros(o_ref.shape, F32)

        ab = a_ref[...].astype(BF16)
        for b_ref, o_ref in zip(b_refs, o_refs):
            bb = b_ref[...].astype(BF16)
            o_ref[...] += _tn(bb, ab) if transposed else _tn(ab, bb)

    return pl.pallas_call(
        body, name=name, grid=(lp // tk,),
        in_specs=[_rows(tk, m)] + [_rows(tk, b.shape[1]) for b in bs],
        out_specs=[pl.BlockSpec(s, lambda i: (0, 0)) for s in shapes],
        out_shape=[jax.ShapeDtypeStruct(s, F32) for s in shapes],
        compiler_params=_cp(("arbitrary",)),
    )(a, *bs)


HBM_SPEC = pl.BlockSpec(memory_space=pltpu.HBM)


def _my_place():
    return lax.axis_index("x"), lax.axis_index("y"), lax.axis_index("c")


def _other_chips(x, y):
    return [(1 - x, y), (x, 1 - y), (1 - x, 1 - y)]


class _ChipExchange:
    def __init__(self, arrs, scatter):
        self.n = len(arrs)
        self.scatter = scatter
        self.out_shape = [jax.ShapeDtypeStruct(a.shape if scatter else (N_CHIPS,) + a.shape, a.dtype) for a in arrs]
        self.specs = [HBM_SPEC] * self.n
        self.sems = [pltpu.SemaphoreType.DMA((3 * self.n,)), pltpu.SemaphoreType.DMA((3 * self.n,)),
                     pltpu.SemaphoreType.DMA((self.n,))]

    def _copies(self, ins, outs, sems):
        send_sems, recv_sems, local_sems = sems
        x, y, c = _my_place()
        me = 2 * x + y
        chips = _other_chips(x, y)
        mine = lambda a: ins[a].at[me] if self.scatter else ins[a]

        def remote(a, k, arriving):
            px, py = chips[k]
            there = 2 * px + py
            return pltpu.make_async_remote_copy(
                src_ref=mine(a) if arriving or not self.scatter else ins[a].at[there],
                dst_ref=outs[a].at[there if arriving else me],
                send_sem=send_sems.at[a * 3 + k], recv_sem=recv_sems.at[a * 3 + k],
                device_id=(px, py, c), device_id_type=MESH)

        pairs = [(a, k) for a in range(self.n) for k in range(3)]
        local = [pltpu.make_async_copy(mine(a), outs[a].at[me], local_sems.at[a]) for a in range(self.n)]
        return local, [remote(a, k, False) for a, k in pairs], [remote(a, k, True) for a, k in pairs]

    def start(self, ins, outs, sems):
        local, sends, _ = self._copies(ins, outs, sems)
        for cp in local + sends:
            cp.start()

    def wait(self, ins, outs, sems):
        local, sends, arrivals = self._copies(ins, outs, sems)
        for cp in arrivals:
            cp.wait_recv()
        for cp in sends:
            cp.wait_send()
        for cp in local:
            cp.wait()


def _chip_exchange(arrs, scatter, name):
    ex = _ChipExchange(arrs, scatter)

    def body(*refs):
        ins, outs, sems = refs[:ex.n], refs[ex.n:2 * ex.n], refs[2 * ex.n:]
        ex.start(ins, outs, sems)
        ex.wait(ins, outs, sems)

    return pl.pallas_call(body, name=name, in_specs=ex.specs, out_specs=ex.specs, out_shape=ex.out_shape,
                          scratch_shapes=ex.sems)(*arrs)


def _sibling_exchange(arrs, name):
    n = len(arrs)

    def body(*refs):
        ins, outs = refs[:n], refs[n:2 * n]
        send_sems, recv_sems = refs[2 * n:]
        x, y, c = _my_place()
        cps = [pltpu.make_async_remote_copy(src_ref=ins[a], dst_ref=outs[a], send_sem=send_sems.at[a],
                                            recv_sem=recv_sems.at[a], device_id=(x, y, 1 - c), device_id_type=MESH)
               for a in range(n)]
        for cp in cps:
            cp.start()
        for cp in cps:
            cp.wait_recv()
        for cp in cps:
            cp.wait_send()

    return pl.pallas_call(
        body, name=name, in_specs=[HBM_SPEC] * n, out_specs=[HBM_SPEC] * n,
        out_shape=[jax.ShapeDtypeStruct(a.shape, a.dtype) for a in arrs],
        scratch_shapes=[pltpu.SemaphoreType.DMA((n,)), pltpu.SemaphoreType.DMA((n,))],
    )(*arrs)


def _all_reduce_small(pack):
    rows = pack.shape[0]

    def body(p_ref, o_ref, g_scr, send_sems, recv_sems):
        x, y, c = _my_place()
        me = 4 * x + 2 * y + c
        flips = [(dx, dy, dc) for dx in (0, 1) for dy in (0, 1) for dc in (0, 1) if (dx, dy, dc) != (0, 0, 0)]

        def peer(f):
            return (x if f[0] == 0 else 1 - x, y if f[1] == 0 else 1 - y, c if f[2] == 0 else 1 - c)

        def copy(k, slot):
            return pltpu.make_async_remote_copy(src_ref=p_ref, dst_ref=g_scr.at[slot], send_sem=send_sems.at[k],
                                                recv_sem=recv_sems.at[k], device_id=peer(flips[k]), device_id_type=MESH)

        sends = [copy(k, me) for k in range(len(flips))]
        for cp in sends:
            cp.start()
        g_scr[me] = p_ref[...]
        for k, f in enumerate(flips):
            px, py, pc = peer(f)
            copy(k, 4 * px + 2 * py + pc).wait_recv()
        for cp in sends:
            cp.wait_send()
        acc = g_scr[0]
        for d in range(1, N_DEV):
            acc = acc + g_scr[d]
        o_ref[...] = acc

    vm = pl.BlockSpec(memory_space=pltpu.VMEM)
    return pl.pallas_call(
        body, name="all_reduce_small", in_specs=[vm], out_specs=vm,
        out_shape=jax.ShapeDtypeStruct(pack.shape, F32),
        scratch_shapes=[pltpu.VMEM((N_DEV, rows, 128), F32), pltpu.SemaphoreType.DMA((N_DEV - 1,)),
                        pltpu.SemaphoreType.DMA((N_DEV - 1,))],
        compiler_params=_cp(),
    )(pack)


def _as3d(a):
    return a.reshape((-1,) + a.shape[-2:])


def _row_block(r, sublanes=8, cap=512):
    fits = [t for t in range(sublanes, min(r, cap) + 1, sublanes) if r % t == 0]
    return fits[-1] if fits else r


def _sum_pieces(pieces, name):
    _, na, r, c = pieces.shape
    rt = _row_block(r, sublanes=16)

    def body(p_ref, o_ref):
        acc = p_ref[0, 0].astype(F32)
        for s in range(1, N_CHIPS):
            acc = acc + p_ref[s, 0].astype(F32)
        o_ref[0] = acc

    return pl.pallas_call(
        body, name=name, grid=(na, r // rt),
        in_specs=[pl.BlockSpec((N_CHIPS, 1, rt, c), lambda a, i: (0, a, i, 0))],
        out_specs=pl.BlockSpec((1, rt, c), lambda a, i: (a, i, 0)),
        out_shape=jax.ShapeDtypeStruct((na, r, c), F32),
        compiler_params=_cp(("parallel", "parallel")),
    )(pieces)


def _adamw(w, g_parts, m, v, name):
    na, r, c = w.shape
    rt = _row_block(r)
    ng = len(g_parts)

    def body(w_ref, *rest):
        g_refs = rest[:ng]
        m_ref, v_ref, g_out, d_out, m_out, v_out = rest[ng:]
        g = g_refs[0][...]
        for gr in g_refs[1:]:
            g = g + gr[...]
        m_new = ADAM_B1 * m_ref[...] + (1.0 - ADAM_B1) * g
        v_new = ADAM_B2 * v_ref[...] + (1.0 - ADAM_B2) * (g * g)
        m_hat = m_new / (1.0 - ADAM_B1 ** ADAM_STEP)
        v_hat = v_new / (1.0 - ADAM_B2 ** ADAM_STEP)
        g_out[...] = g
        d_out[...] = -ADAM_LR * (m_hat / (jnp.sqrt(v_hat) + ADAM_EPS) + ADAM_WD * w_ref[...])
        m_out[...] = m_new
        v_out[...] = v_new

    spec = pl.BlockSpec((1, rt, c), lambda a, i: (a, i, 0))
    out = jax.ShapeDtypeStruct((na, r, c), F32)
    return pl.pallas_call(
        body, name=name, grid=(na, r // rt), in_specs=[spec] * (3 + ng), out_specs=[spec] * 4, out_shape=[out] * 4,
        compiler_params=_cp(("parallel", "parallel")),
    )(w, *g_parts, m, v)


def _cols_from_shards(g):
    g = jnp.moveaxis(g, 0, -2)
    return g.reshape(g.shape[:-2] + (g.shape[-2] * g.shape[-1],))


def _rows_from_shards(g):
    g = jnp.moveaxis(g, 0, -3)
    return g.reshape(g.shape[:-3] + (g.shape[-3] * g.shape[-2], g.shape[-1]))


def _cols_to_shards(w):
    w = w.reshape(w.shape[:-1] + (N_CHIPS, w.shape[-1] // N_CHIPS))
    return jnp.moveaxis(w, -2, 0)


def _rows_to_shards(w):
    w = w.reshape(w.shape[:-2] + (N_CHIPS, w.shape[-2] // N_CHIPS, w.shape[-1]))
    return jnp.moveaxis(w, -3, 0)


def _pad_w_in(wt):
    z = lambda n: jnp.zeros((n, wt.shape[1]), wt.dtype)
    return jnp.concatenate([wt[:2048], wt[2080:4640], z(KR_LANE0), wt[2048:2080], z(HEAD_PAD - QK_DIM)], axis=0)


def _unpad_w_in(parts):
    u, zp, cq, ckv, zm, gp, gm, kr = parts
    return jnp.concatenate([u, zp, cq, ckv, kr[KR_LANE0:QK_DIM], zm, gp, gm], axis=0)


def _pad_heads(w, real):
    w = w.reshape(w.shape[:-1] + (N_HEADS, real))
    w = jnp.pad(w, [(0, 0)] * (w.ndim - 1) + [(0, HEAD_PAD - real)])
    return w.reshape(w.shape[:-2] + (N_HEADS * HEAD_PAD,))


def _flat_rows(a):
    a = a.reshape(-1)
    return jnp.pad(a, (0, (-a.shape[0]) % (8 * 128))).reshape(-1, 128)


def kernel(x, positions, meta_tokens, norm_gain, w_in, pool_w_group, pool_scale, pool_w_up, q_a_norm_gain, kv_a_norm_gain, w_q_b, w_kv_b, q_norm_gain, k_norm_gain, mla_w_up, w_out, loss_target, m_meta_tokens, m_norm_gain, m_w_in, m_pool_w_group, m_pool_scale, m_pool_w_up, m_q_a_norm_gain, m_kv_a_norm_gain, m_w_q_b, m_w_kv_b, m_q_norm_gain, m_k_norm_gain, m_mla_w_up, m_w_out, v_meta_tokens, v_norm_gain, v_w_in, v_pool_w_group, v_pool_scale, v_pool_w_up, v_q_a_norm_gain, v_kv_a_norm_gain, v_w_q_b, v_w_kv_b, v_q_norm_gain, v_k_norm_gain, v_mla_w_up, v_w_out):
    seq = x.shape[1]
    lp = -(-(ROW0 + seq) // ATTN_TILE) * ATTN_TILE
    pad_back = lp - ROW0 - seq
    chip = 2 * lax.axis_index("x") + lax.axis_index("y")

    tr = lambda a: jnp.swapaxes(a, 1, 2)
    big = dict(w_in=tr(w_in), pool_w_up=pool_w_up, w_q_b=w_q_b, w_kv_b=w_kv_b, mla_w_up=mla_w_up, w_out=w_out)
    row_sharded = ("w_in", "w_q_b", "w_out")
    names = list(big)
    shards = [[big[n][l].astype(BF16) for n in names] for l in range(DEPTH)]
    from_shards = lambda n: _rows_from_shards if n in row_sharded else _cols_from_shards
    to_shards = lambda n: _rows_to_shards if n in row_sharded else _cols_to_shards

    def in_weights(g_w_in):
        return dict(w_pad=_pad_w_in(from_shards("w_in")(g_w_in)))

    def rest_weights(gathered):
        w = {n: from_shards(n)(g) for n, g in zip(names[1:], gathered)}
        wkv = w["w_kv_b"].reshape(KV_RANK, N_HEADS, NOPE + V_DIM)
        return dict(wq=_pad_heads(w["w_q_b"], QK_DIM),
                    wkn=_pad_heads(wkv[..., :NOPE].reshape(KV_RANK, N_HEADS * NOPE), NOPE),
                    wv=wkv[..., NOPE:].reshape(KV_RANK, MLA_WIDTH),
                    wpu=w["pool_w_up"], wmu=w["mla_w_up"], wout=w["w_out"])

    g_in0, meta_g = _chip_exchange([shards[0][0], meta_tokens], scatter=False, name="gather_layer0")
    weights = [in_weights(g_in0)]
    meta_full = _cols_from_shards(meta_g)
    wg = pool_w_group.astype(BF16)
    gqn = jnp.pad(q_norm_gain, ((0, 0), (0, HEAD_PAD - QK_DIM)))
    gkn = jnp.pad(k_norm_gain, ((0, 0), (0, HEAD_PAD - QK_DIM)))

    x_pad = jnp.concatenate([jnp.zeros((PAD_FRONT, D_MODEL), F32), meta_full, x[0], jnp.zeros((pad_back, D_MODEL), F32)], axis=0)
    pos_pad = jnp.concatenate([jnp.zeros((PAD_FRONT,), jnp.int32), jnp.arange(N_META, dtype=jnp.int32),
                               positions[0] + N_META, jnp.zeros((pad_back,), jnp.int32)])
    half = ROPE // 2
    inv_freq = (ROPE_THETA ** (-np.arange(half, dtype=np.float32) / half)).astype(np.float32)
    freq_row = np.zeros((1, HEAD_PAD), np.float32)
    freq_row[0, NOPE:NOPE + half] = inv_freq
    freq_row[0, NOPE + half:QK_DIM] = inv_freq
    tabs = _rope_tables(pos_pad[:, None], jnp.asarray(freq_row))

    row = lambda a, l: a[l][None, :]

    saved = []
    h_res = x_pad
    for l in range(DEPTH):
        w = weights[l]
        (u, zp, cq, ckv, zm, gp, gm, kr), rest0 = _inproj_fwd(h_res, row(norm_gain, l), w["w_pad"],
                                                              gather=shards[0][1:] if l == 0 else ())
        if rest0:
            w.update(rest_weights(rest0))
        a_pool = _pool_fwd(u, zp, wg[l], row(pool_scale, l))
        q, k, v = _mla_prep_fwd(cq, ckv, kr, tabs, row(q_a_norm_gain, l), row(kv_a_norm_gain, l), row(gqn, l), row(gkn, l),
                                w["wq"], w["wkn"], w["wv"])
        o, lse, nxt = _flash_fwd(q, k, v, gather=shards[l + 1] if l + 1 < DEPTH else ())
        if nxt:
            weights.append({**in_weights(nxt[0]), **rest_weights(nxt[1:])})
        h_next, yp, ym = _merge_fwd(h_res, a_pool, o, zm, gp, gm, w["wpu"], w["wmu"], w["wout"])
        saved.append(dict(x=h_res, u=u, zp=zp, cq=cq, ckv=ckv, zm=zm, gp=gp, gm=gm, kr=kr, a_pool=a_pool, q=q, k=k, v=v,
                          o=o, lse=lse, yp=yp, ym=ym))
        h_res = h_next
    dres, loss_blk = _loss_head(h_res, loss_target[0])

    gw = {n: [None] * DEPTH for n in names}
    pieces = [None] * DEPTH
    grad_stacks = lambda l, which=names: [to_shards(n)(gw[n][l]).astype(BF16) for n in which]
    gs = {n: [None] * DEPTH for n in ("norm_gain", "pool_w_group", "pool_scale", "q_a", "kv_a", "q_norm", "k_norm")}
    for l in reversed(range(DEPTH)):
        s, w = saved[l], weights[l]
        merged, dyp, dym, dgp, dgm, dap, amla, do, dzm, delta = _merge_bwd(
            dres, s["yp"], s["ym"], s["gp"], s["gm"], s["o"], s["zm"], w["wout"], w["wpu"], w["wmu"])
        (gw["w_out"][l],) = _weight_grads(merged, [dres], "grad_w_out")
        (gw["pool_w_up"][l],) = _weight_grads(s["a_pool"], [dyp], "grad_pool_w_up")
        (gw["mla_w_up"][l],) = _weight_grads(amla, [dym], "grad_mla_w_up")
        dq, dk, dv, got = _flash_bwd(s["q"], s["k"], s["v"], do, s["lse"], delta,
                                     scatter=grad_stacks(l + 1) if l + 1 < DEPTH else ())
        if got:
            pieces[l + 1] = got
        dcq, dckv, dkr, dwq, dwkn, dwv, gs["q_a"][l], gs["kv_a"][l], dgqn, dgkn = _mla_prep_bwd(
            dq, dk, dv, s["cq"], s["ckv"], s["kr"], tabs, row(q_a_norm_gain, l), row(kv_a_norm_gain, l), row(gqn, l), row(gkn, l),
            w["wq"], w["wkn"], w["wv"])
        gs["q_norm"][l] = dgqn[:, :QK_DIM]
        gs["k_norm"][l] = dgkn[:, :QK_DIM]
        gw["w_q_b"][l] = dwq.reshape(Q_RANK, N_HEADS, HEAD_PAD)[..., :QK_DIM].reshape(Q_RANK, N_HEADS * QK_DIM)
        gw["w_kv_b"][l] = jnp.concatenate([dwkn.reshape(KV_RANK, N_HEADS, HEAD_PAD)[..., :NOPE],
                                           dwv.reshape(KV_RANK, N_HEADS, V_DIM)], axis=-1).reshape(KV_RANK, N_HEADS * (NOPE + V_DIM))
        du, dzp, gs["pool_w_group"][l], gs["pool_scale"][l] = _pool_bwd(dap, s["u"], s["zp"], wg[l], row(pool_scale, l))
        dparts = [du, dzp, dcq, dckv, dzm, dgp, dgm, dkr]
        dres, h, gs["norm_gain"][l], rest0 = _inproj_bwd(dres, s["x"], row(norm_gain, l), w["w_pad"], dparts,
                                                         scatter=grad_stacks(0, names[1:]) if l == 0 else ())
        ga = _weight_grads(h, [du, dzp, dcq, dckv, dkr], "grad_w_in_a", transposed=True)
        gb = _weight_grads(h, [dzm, dgp, dgm], "grad_w_in_b", transposed=True)
        gw["w_in"][l] = _unpad_w_in([ga[0], ga[1], ga[2], ga[3], gb[0], gb[1], gb[2], ga[4]])
    grad_x = dres[ROW0:ROW0 + seq][None]

    pieces[0] = list(_chip_exchange(grad_stacks(0, names[:1]), scatter=True, name="scatter_layer0")) + list(rest0)
    sums = [_sum_pieces(jnp.stack([pieces[l][a] for l in range(DEPTH)], axis=1), "sum_" + n) for a, n in enumerate(names)]
    other = _sibling_exchange(sums, name="swap_core_sums")
    moments = dict(w_in=(tr(m_w_in), tr(v_w_in)), pool_w_up=(m_pool_w_up, v_pool_w_up), w_q_b=(m_w_q_b, v_w_q_b),
                   w_kv_b=(m_w_kv_b, v_w_kv_b), mla_w_up=(m_mla_w_up, v_mla_w_up), w_out=(m_w_out, v_w_out))
    big_out = {n: _adamw(big[n], [sm, ot], moments[n][0], moments[n][1], "adamw_" + n)
               for n, sm, ot in zip(names, sums, other)}

    small_names = ("norm_gain", "pool_w_group", "pool_scale", "q_a", "kv_a", "q_norm", "k_norm")
    small_w = dict(norm_gain=(norm_gain, m_norm_gain, v_norm_gain), pool_w_group=(pool_w_group, m_pool_w_group, v_pool_w_group),
                   pool_scale=(pool_scale, m_pool_scale, v_pool_scale), q_a=(q_a_norm_gain, m_q_a_norm_gain, v_q_a_norm_gain),
                   kv_a=(kv_a_norm_gain, m_kv_a_norm_gain, v_kv_a_norm_gain), q_norm=(q_norm_gain, m_q_norm_gain, v_q_norm_gain),
                   k_norm=(k_norm_gain, m_k_norm_gain, v_k_norm_gain))
    small_g = {n: jnp.stack(gs[n]).reshape(small_w[n][0].shape) for n in small_names}
    blocks = [_flat_rows(small_g[n]) for n in small_names]
    n_rows = [b.shape[0] for b in blocks]
    meta_rows = N_META * D_MODEL // 128
    pack = jnp.concatenate(blocks + [dres[PAD_FRONT:ROW0].reshape(meta_rows, 128), loss_blk], axis=0)
    pack = jnp.pad(pack, ((0, (-pack.shape[0]) % 8), (0, 0)))
    total = _all_reduce_small(pack)
    n_small = sum(n_rows)
    loss = total[n_small + meta_rows, 0]
    gmeta = lax.dynamic_slice_in_dim(total[n_small:n_small + meta_rows].reshape(N_META, D_MODEL), chip * (D_MODEL // N_CHIPS),
                                     D_MODEL // N_CHIPS, axis=1)

    def packed(idx, meta_part):
        p = jnp.concatenate([_flat_rows(small_w[n][idx]) for n in small_names] + [_flat_rows(meta_part)], axis=0)
        return jnp.pad(p, ((0, (-p.shape[0]) % 8), (0, 0)))[None]

    g_pack = jnp.concatenate([total[:n_small], _flat_rows(gmeta)], axis=0)
    g_pack = jnp.pad(g_pack, ((0, (-g_pack.shape[0]) % 8), (0, 0)))[None]
    small_out = _adamw(packed(0, meta_tokens), [g_pack], packed(1, m_meta_tokens), packed(2, v_meta_tokens), "adamw_small")

    def unpack(p):
        res, r0 = {}, 0
        for n, nr in zip(small_names, n_rows):
            shape = small_w[n][0].shape
            res[n] = p[0, r0:r0 + nr].reshape(-1)[:math.prod(shape)].reshape(shape)
            r0 += nr
        res["meta"] = p[0, r0:r0 + N_META * (D_MODEL // N_CHIPS) // 128].reshape(N_META, D_MODEL // N_CHIPS)
        return res

    small_res = [unpack(p) for p in small_out]

    def leaf(kind, name):
        key = {"meta_tokens": "meta", "q_a_norm_gain": "q_a", "kv_a_norm_gain": "kv_a", "q_norm_gain": "q_norm",
               "k_norm_gain": "k_norm"}.get(name, name)
        if name in big_out:
            return tr(big_out[name][kind]) if name == "w_in" else big_out[name][kind]
        return small_res[kind][key]

    order = ("meta_tokens", "norm_gain", "w_in", "pool_w_group", "pool_scale", "pool_w_up", "q_a_norm_gain", "kv_a_norm_gain",
             "w_q_b", "w_kv_b", "q_norm_gain", "k_norm_gain", "mla_w_up", "w_out")
    outs = [loss, grad_x]
    for kind in range(4):
        outs += [leaf(kind, n) for n in order]
    return tuple(outs)
```

```python
import functools
import math

import numpy as np
import jax
import jax.numpy as jnp
from jax import lax
from jax.experimental import pallas as pl
from jax.experimental.pallas import tpu as pltpu

F32 = jnp.float32
BF16 = jnp.bfloat16
MESH = pl.DeviceIdType.MESH

D_MODEL = 1024
DEPTH = 4
N_META = 16
POOL_WIDTH = 512
POOL_WINDOWS = (2, 4, 8, 16)
POOL_GROUPS = 4
GROUP_DIM = 128
N_HEADS = 8
NOPE = 64
ROPE = 32
QK_DIM = 96
V_DIM = 64
MLA_WIDTH = 512
KV_RANK = 256
Q_RANK = 768
ROPE_THETA = 10000.0
EPS = 1e-6
MASK_VALUE = -1e30
ATTN_BLOCK = 128
PAD_FRONT = (-N_META) % ATTN_BLOCK
ROW0 = PAD_FRONT + N_META
HEAD_PAD = 128
HALO = 16
N_CHIPS = 4
N_DEV = 8

IN_NAMES = ("u", "zp", "cq", "ckv", "zm", "gp", "gm", "kr")
IN_WIDTHS = (512, 512, 768, 256, 512, 1024, 1024, 128)
IN_OFFS = tuple(int(v) for v in np.cumsum((0,) + IN_WIDTHS[:-1]))
IN_PAD = sum(IN_WIDTHS)
KR_LANE0 = NOPE

ADAM_LR = 0.001
ADAM_B1 = 0.9
ADAM_B2 = 0.999
ADAM_EPS = 1e-08
ADAM_WD = 0.01
ADAM_STEP = 10

VMEM_LIMIT = 56 * 1024 * 1024
ATTN_TILE = 768
ROW_TILES = (768, 384)
ROW_TILES_HEAVY = (384,)
ROW_BLOCK = 32
FWD_PAIRS = 2
BWD_PAIRS = 2
LOG2E = 1.4426950408889634
LN2 = 0.6931471805599453
Q_PRESCALE = LOG2E / math.sqrt(QK_DIM)


def _cp(sem=None, vmem=VMEM_LIMIT):
    kw = dict(vmem_limit_bytes=vmem)
    if sem is not None:
        kw["dimension_semantics"] = sem
    return pltpu.CompilerParams(**kw)


def _row_tile(n_rows, prefs=None):
    for t in prefs or ROW_TILES:
        if n_rows % t == 0:
            return t
    raise ValueError(f"no row tile for {n_rows}")


def _nt(a, b):
    return lax.dot_general(a, b, (((1,), (1,)), ((), ())), preferred_element_type=F32)


def _tn(a, b):
    return lax.dot_general(a, b, (((0,), (0,)), ((), ())), preferred_element_type=F32)


def _mm(a, b):
    return jnp.dot(a, b, preferred_element_type=F32)


def _sigmoid(x):
    return 0.5 * jnp.tanh(0.5 * x) + 0.5


def _resident(shape):
    nd = len(shape)
    return pl.BlockSpec(shape, lambda *_: (0,) * nd, pipeline_mode=pl.Buffered(1))


def _rows(tm, width):
    return pl.BlockSpec((tm, width), lambda i: (i, 0))


def _rope_tables(pos_col, inv_freq_row):
    lp = pos_col.shape[0]
    tm = _row_tile(lp)

    def body(p_ref, f_ref, c_ref, s1_ref, s2_ref):
        ang = p_ref[...].astype(F32) * f_ref[...]
        lane = lax.broadcasted_iota(jnp.int32, ang.shape, 1)
        cs = jnp.cos(ang)
        sn = jnp.sin(ang)
        c_ref[...] = jnp.where(lane < NOPE, 1.0, jnp.where(lane < QK_DIM, cs, 0.0))
        s1_ref[...] = jnp.where((lane >= NOPE) & (lane < NOPE + ROPE // 2), -sn, 0.0)
        s2_ref[...] = jnp.where((lane >= NOPE + ROPE // 2) & (lane < QK_DIM), sn, 0.0)

    out = jax.ShapeDtypeStruct((lp, HEAD_PAD), F32)
    return pl.pallas_call(
        body, name="rope_tables", grid=(lp // tm,),
        in_specs=[pl.BlockSpec((tm, 1), lambda i: (i, 0)), pl.BlockSpec((1, HEAD_PAD), lambda i: (0, 0))],
        out_specs=[_rows(tm, HEAD_PAD)] * 3, out_shape=[out] * 3,
        compiler_params=_cp(("parallel",)),
    )(pos_col, inv_freq_row)


def _rope(y, c, s1, s2):
    return y * c + pltpu.roll(y, HEAD_PAD - ROPE // 2, 1) * s1 + pltpu.roll(y, ROPE // 2, 1) * s2


def _rope_t(g, c, s1, s2):
    return g * c + pltpu.roll(g * s1, ROPE // 2, 1) + pltpu.roll(g * s2, HEAD_PAD - ROPE // 2, 1)


def _inproj_fwd(x, gain, w_pad, gather=()):
    lp = x.shape[0]
    tm = _row_tile(lp)
    n_out = len(IN_WIDTHS)
    ex = _ChipExchange(list(gather), scatter=False)

    def body(x_ref, g_ref, w_ref, *rest):
        ex_in, outs, ex_out, ex_sems = rest[:ex.n], rest[ex.n:ex.n + n_out], rest[ex.n + n_out:2 * ex.n + n_out], rest[2 * ex.n + n_out:]
        if ex.n:
            pl.when(pl.program_id(0) == 0)(lambda: ex.start(ex_in, ex_out, ex_sems))
        xf = x_ref[...]
        inv = lax.rsqrt(jnp.mean(xf * xf, axis=-1, keepdims=True) + EPS)
        h = (xf * inv * g_ref[...]).astype(BF16)
        for o_ref, off, wd in zip(outs, IN_OFFS, IN_WIDTHS):
            o_ref[...] = _nt(h, w_ref[off:off + wd, :]).astype(o_ref.dtype)
        if ex.n:
            pl.when(pl.program_id(0) == lp // tm - 1)(lambda: ex.wait(ex_in, ex_out, ex_sems))

    res = pl.pallas_call(
        body, name="inproj_fwd_gather" if ex.n else "inproj_fwd", grid=(lp // tm,),
        in_specs=[_rows(tm, D_MODEL), pl.BlockSpec((1, D_MODEL), lambda i: (0, 0)), _resident((IN_PAD, D_MODEL))] + ex.specs,
        out_specs=[_rows(tm, wd) for wd in IN_WIDTHS] + ex.specs,
        out_shape=[jax.ShapeDtypeStruct((lp, wd), BF16) for wd in IN_WIDTHS] + ex.out_shape,
        scratch_shapes=ex.sems if ex.n else [],
        compiler_params=_cp(("arbitrary",)),
    )(x, gain, w_pad, *gather)
    return res[:n_out], res[n_out:]


def _inv_counts(tile_idx, tm):
    row = tile_idx * tm + lax.broadcasted_iota(jnp.int32, (tm, 1), 0)
    t1 = jnp.maximum(row - PAD_FRONT + 1, 1).astype(F32)
    return [1.0 / jnp.minimum(t1, float(w)) for w in POOL_WINDOWS]


def _trailing_sums(e):
    s2 = e + pltpu.roll(e, 1, 0)
    s4 = s2 + pltpu.roll(s2, 2, 0)
    s8 = s4 + pltpu.roll(s4, 4, 0)
    s16 = s8 + pltpu.roll(s8, 8, 0)
    return (s2, s4, s8, s16)


def _leading_sums(e):
    n = e.shape[0]
    s2 = e + pltpu.roll(e, n - 1, 0)
    s4 = s2 + pltpu.roll(s2, n - 2, 0)
    s8 = s4 + pltpu.roll(s4, n - 4, 0)
    s16 = s8 + pltpu.roll(s8, n - 8, 0)
    return (s2, s4, s8, s16)


def _pool_fwd(u, zp, wg, scale):
    lp = u.shape[0]
    tm = _row_tile(lp)

    def body(u_ref, z_ref, wg_ref, sc_ref, a_ref, ext_ref):
        i = pl.program_id(0)

        @pl.when(i == 0)
        def _():
            ext_ref[0:HALO, :] = jnp.zeros((HALO, POOL_WIDTH), F32)

        ext_ref[HALO:HALO + tm, :] = u_ref[...].astype(F32)
        e = ext_ref[...]
        sums = _trailing_sums(e)
        ext_ref[0:HALO, :] = e[tm:tm + HALO, :]
        inv_cnt = _inv_counts(i, tm)
        for g in range(POOL_GROUPS):
            cols = slice(g * GROUP_DIM, (g + 1) * GROUP_DIM)
            mixed = sums[g][HALO:, cols] * inv_cnt[g] - e[HALO:, cols]
            y = _mm(mixed.astype(BF16), wg_ref[g]) * sc_ref[:, cols]
            zf = z_ref[:, cols].astype(F32)
            a_ref[:, cols] = (y * (zf * _sigmoid(zf))).astype(a_ref.dtype)

    return pl.pallas_call(
        body, name="pool_fwd", grid=(lp // tm,),
        in_specs=[_rows(tm, POOL_WIDTH), _rows(tm, POOL_WIDTH),
                  pl.BlockSpec((POOL_GROUPS, GROUP_DIM, GROUP_DIM), lambda i: (0, 0, 0)),
                  pl.BlockSpec((1, POOL_WIDTH), lambda i: (0, 0))],
        out_specs=_rows(tm, POOL_WIDTH), out_shape=jax.ShapeDtypeStruct((lp, POOL_WIDTH), BF16),
        scratch_shapes=[pltpu.VMEM((HALO + tm, POOL_WIDTH), F32)],
        compiler_params=_cp(("arbitrary",)),
    )(u, zp, wg, scale)


def _rms_fwd(xf, gain):
    inv = lax.rsqrt(jnp.mean(xf * xf, axis=-1, keepdims=True) + EPS)
    xhat = xf * inv
    return inv, xhat, xhat * gain


def _rms_bwd(dy, inv, xhat, gain):
    dgain = jnp.sum(dy * xhat, axis=0, keepdims=True)
    dyg = dy * gain
    dx = inv * (dyg - xhat * jnp.mean(dyg * xhat, axis=-1, keepdims=True))
    return dx, dgain


def _head_norm_fwd(xh, gain128):
    inv = lax.rsqrt(jnp.sum(xh * xh, axis=-1, keepdims=True) * (1.0 / QK_DIM) + EPS)
    xhat = xh * inv
    return inv, xhat, xhat * gain128


def _head_norm_bwd(dy, inv, xhat, gain128):
    dyg = dy * gain128
    return inv * (dyg - xhat * (jnp.sum(dyg * xhat, axis=-1, keepdims=True) * (1.0 / QK_DIM)))


def _mla_prep_fwd(cq, ckv, kr, tabs, gqa, gkva, gqn, gkn, wq, wkn, wv):
    lp = cq.shape[0]
    tm = _row_tile(lp)

    def body(cq_ref, ckv_ref, kr_ref, c_ref, s1_ref, s2_ref, gqa_ref, gkva_ref, gqn_ref, gkn_ref,
             wq_ref, wkn_ref, wv_ref, q_ref, k_ref, v_ref):
        c, s1, s2 = c_ref[...], s1_ref[...], s2_ref[...]
        _, _, cqn = _rms_fwd(cq_ref[...].astype(F32), gqa_ref[...])
        qraw = _mm(cqn.astype(BF16), wq_ref[...])
        for h in range(N_HEADS):
            hb = slice(h * HEAD_PAD, (h + 1) * HEAD_PAD)
            _, _, yh = _head_norm_fwd(qraw[:, hb], gqn_ref[...])
            q_ref[:, hb] = (_rope(yh, c, s1, s2) * Q_PRESCALE).astype(q_ref.dtype)
        _, _, ckvn = _rms_fwd(ckv_ref[...].astype(F32), gkva_ref[...])
        ckvn_b = ckvn.astype(BF16)
        knraw = _mm(ckvn_b, wkn_ref[...])
        krs = kr_ref[...].astype(F32)
        for h in range(N_HEADS):
            hb = slice(h * HEAD_PAD, (h + 1) * HEAD_PAD)
            _, _, yh = _head_norm_fwd(knraw[:, hb] + krs, gkn_ref[...])
            k_ref[:, hb] = _rope(yh, c, s1, s2).astype(k_ref.dtype)
        v_ref[...] = _mm(ckvn_b, wv_ref[...]).astype(v_ref.dtype)

    hw = N_HEADS * HEAD_PAD
    vec = lambda n: pl.BlockSpec((1, n), lambda i: (0, 0))
    return pl.pallas_call(
        body, name="mla_prep_fwd", grid=(lp // tm,),
        in_specs=[_rows(tm, Q_RANK), _rows(tm, KV_RANK), _rows(tm, HEAD_PAD)] + [_rows(tm, HEAD_PAD)] * 3
        + [vec(Q_RANK), vec(KV_RANK), vec(HEAD_PAD), vec(HEAD_PAD),
           _resident((Q_RANK, hw)), _resident((KV_RANK, hw)), _resident((KV_RANK, MLA_WIDTH))],
        out_specs=[_rows(tm, hw), _rows(tm, hw), _rows(tm, MLA_WIDTH)],
        out_shape=[jax.ShapeDtypeStruct((lp, hw), BF16), jax.ShapeDtypeStruct((lp, hw), BF16),
                   jax.ShapeDtypeStruct((lp, MLA_WIDTH), BF16)],
        compiler_params=_cp(("parallel",)),
    )(cq, ckv, kr, *tabs, gqa, gkva, gqn, gkn, wq, wkn, wv)


def _causal_mask(s, q0, k0):
    qi = q0 + lax.broadcasted_iota(jnp.int32, s.shape, 0)
    ki = k0 + lax.broadcasted_iota(jnp.int32, s.shape, 1)
    return jnp.where((ki <= qi) & (ki >= PAD_FRONT), s, MASK_VALUE)


def _score_chunks(kind, r, tk):
    if kind == "inner":
        return [(c0, False) for c0 in range(0, tk, ATTN_BLOCK)]
    if kind == "first":
        return [(c0, c0 == 0) for c0 in range(0, tk, ATTN_BLOCK)]
    return [(c0, True) for c0 in range(0, min(tk, (r + 1) * ROW_BLOCK), ATTN_BLOCK)]


def _tile_kinds(i, t):
    return (("diag", t == i), ("first", (t == 0) & (i > 0)), ("inner", (t > 0) & (t < i)))


def _lanes(col, width=HEAD_PAD):
    return jnp.broadcast_to(col, (col.shape[0], width))


def _flash_fwd(q, k, v, gather=()):
    lp = q.shape[0]
    tq = tk = ATTN_TILE
    nq = lp // tq
    nj = N_HEADS // (2 * FWD_PAIRS)
    heads = 2 * FWD_PAIRS
    n_blocks = tq // ROW_BLOCK
    ex = _ChipExchange(list(gather), scatter=False)

    pairs = [(i, t) for i in range(nq) for t in range(i + 1)]
    i_tab = jnp.asarray([p[0] for p in pairs], jnp.int32)
    t_tab = jnp.asarray([p[1] for p in pairs], jnp.int32)

    def body(i_tab_ref, t_tab_ref, q_ref, k_ref, v_ref, *rest):
        ex_in, (o_ref, lse_ref), ex_out = rest[:ex.n], rest[ex.n:ex.n + 2], rest[ex.n + 2:2 * ex.n + 2]
        m_scr, acc_scr, s_scr, p_scr, part_scr, vext_scr = rest[2 * ex.n + 2:2 * ex.n + 8]
        ex_sems = rest[2 * ex.n + 8:]
        j, step_no = pl.program_id(0), pl.program_id(1)
        i, t = i_tab_ref[step_no], t_tab_ref[step_no]
        if ex.n:
            pl.when((j == 0) & (step_no == 0))(lambda: ex.start(ex_in, ex_out, ex_sems))

        @pl.when(t == 0)
        def _():
            m_scr[...] = jnp.full(m_scr.shape, MASK_VALUE, F32)
            acc_scr[...] = jnp.zeros(acc_scr.shape, F32)

        def step(kind):
            def scores(hh, r, c0, masked):
                s = s_scr[hh, r * ROW_BLOCK:(r + 1) * ROW_BLOCK, c0:c0 + ATTN_BLOCK]
                return _causal_mask(s, i * tq + r * ROW_BLOCK, t * tk + c0) if masked else s

            for pp in range(FWD_PAIRS):
                vext_scr[pp, :, 0:HEAD_PAD] = v_ref[:, pp * HEAD_PAD:(pp + 1) * HEAD_PAD]
                vext_scr[pp, :, HEAD_PAD:2 * HEAD_PAD] = jnp.ones((tk, HEAD_PAD), BF16)
            for hh in range(heads):
                hb = slice(hh * HEAD_PAD, (hh + 1) * HEAD_PAD)
                s_scr[hh] = _nt(q_ref[:, hb], k_ref[:, hb])
            for hh in range(heads):
                for r in range(n_blocks):
                    part = None
                    for c0, masked in _score_chunks(kind, r, tk):
                        s = scores(hh, r, c0, masked)
                        part = s if part is None else jnp.maximum(part, s)
                    part_scr[r * ROW_BLOCK:(r + 1) * ROW_BLOCK, :] = part
                m_prev = m_scr[hh]
                m_new = jnp.maximum(m_prev, _lanes(jnp.max(part_scr[...], axis=-1, keepdims=True)))
                alpha = jnp.exp2(m_prev - m_new)
                m_scr[hh] = m_new
                for r in range(n_blocks):
                    rows = slice(r * ROW_BLOCK, (r + 1) * ROW_BLOCK)
                    m_r = m_scr[hh, rows, :]
                    chunks = _score_chunks(kind, r, tk)
                    for c0, masked in chunks:
                        p_scr[hh, rows, c0:c0 + ATTN_BLOCK] = jnp.exp2((scores(hh, r, c0, masked) - m_r).astype(BF16))
                    done = chunks[-1][0] + ATTN_BLOCK
                    if done < tk:
                        p_scr[hh, rows, done:tk] = jnp.zeros((ROW_BLOCK, tk - done), BF16)
                acc_scr[hh] = jnp.concatenate([alpha, alpha], axis=1) * acc_scr[hh] + _mm(p_scr[hh], vext_scr[hh // 2])

        for kind, pred in _tile_kinds(i, t):
            pl.when(pred)(functools.partial(step, kind))

        @pl.when(t == i)
        def _():
            lane = lax.broadcasted_iota(jnp.int32, (tq, HEAD_PAD), 1)
            for pp in range(FWD_PAIRS):
                pb = slice(pp * HEAD_PAD, (pp + 1) * HEAD_PAD)
                h0, h1 = 2 * pp, 2 * pp + 1
                l0, l1 = acc_scr[h0, :, HEAD_PAD:2 * HEAD_PAD], acc_scr[h1, :, HEAD_PAD:2 * HEAD_PAD]
                o = jnp.where(lane < V_DIM, acc_scr[h0, :, 0:HEAD_PAD] / l0, acc_scr[h1, :, 0:HEAD_PAD] / l1)
                o_ref[:, pb] = o.astype(o_ref.dtype)
                lse_ref[:, pb] = jnp.where(lane < V_DIM, m_scr[h0] + jnp.log(l0) * LOG2E, m_scr[h1] + jnp.log(l1) * LOG2E)

        if ex.n:
            pl.when((j == nj - 1) & (step_no == len(pairs) - 1))(lambda: ex.wait(ex_in, ex_out, ex_sems))

    q_idx = lambda j, p, it, tt: (it[p], j)
    kv_idx = lambda j, p, it, tt: (tt[p], j)
    o, lse, *gathered = pl.pallas_call(
        body, name="flash_fwd_gather" if ex.n else "flash_fwd",
        grid_spec=pltpu.PrefetchScalarGridSpec(
            num_scalar_prefetch=2, grid=(nj, len(pairs)),
            in_specs=[pl.BlockSpec((tq, heads * HEAD_PAD), q_idx), pl.BlockSpec((tk, heads * HEAD_PAD), kv_idx),
                      pl.BlockSpec((tk, FWD_PAIRS * HEAD_PAD), kv_idx)] + ex.specs,
            out_specs=[pl.BlockSpec((tq, FWD_PAIRS * HEAD_PAD), q_idx)] * 2 + ex.specs,
            scratch_shapes=[pltpu.VMEM((heads, tq, HEAD_PAD), F32), pltpu.VMEM((heads, tq, 2 * HEAD_PAD), F32),
                            pltpu.VMEM((heads, tq, tk), F32), pltpu.VMEM((heads, tq, tk), BF16),
                            pltpu.VMEM((tq, HEAD_PAD), F32), pltpu.VMEM((FWD_PAIRS, tk, 2 * HEAD_PAD), BF16)]
            + (ex.sems if ex.n else [])),
        out_shape=[jax.ShapeDtypeStruct((lp, MLA_WIDTH), BF16), jax.ShapeDtypeStruct((lp, MLA_WIDTH), F32)] + ex.out_shape,
        compiler_params=_cp(("arbitrary",) * 2),
    )(i_tab, t_tab, q, k, v, *gather)
    return o, lse, gathered


def _merge_fwd(x, a_pool, o, zm, gp, gm, wpu, wmu, wout):
    lp = x.shape[0]
    tm = _row_tile(lp)

    def body(x_ref, ap_ref, o_ref, zm_ref, gp_ref, gm_ref, wpu_ref, wmu_ref, wout_ref, xn_ref, yp_ref, ym_ref):
        yp = _mm(ap_ref[...], wpu_ref[...])
        zf = zm_ref[...].astype(F32)
        amla = o_ref[...].astype(F32) * (zf * _sigmoid(zf))
        ym = _mm(amla.astype(BF16), wmu_ref[...])
        merged = _sigmoid(gp_ref[...].astype(F32)) * yp + _sigmoid(gm_ref[...].astype(F32)) * ym
        xn_ref[...] = x_ref[...] + _mm(merged.astype(BF16), wout_ref[...])
        yp_ref[...] = yp.astype(yp_ref.dtype)
        ym_ref[...] = ym.astype(ym_ref.dtype)

    return pl.pallas_call(
        body, name="merge_fwd", grid=(lp // tm,),
        in_specs=[_rows(tm, D_MODEL), _rows(tm, POOL_WIDTH), _rows(tm, MLA_WIDTH), _rows(tm, MLA_WIDTH),
                  _rows(tm, D_MODEL), _rows(tm, D_MODEL),
                  _resident((POOL_WIDTH, D_MODEL)), _resident((MLA_WIDTH, D_MODEL)), _resident((D_MODEL, D_MODEL))],
        out_specs=[_rows(tm, D_MODEL)] * 3,
        out_shape=[jax.ShapeDtypeStruct((lp, D_MODEL), F32), jax.ShapeDtypeStruct((lp, D_MODEL), BF16),
                   jax.ShapeDtypeStruct((lp, D_MODEL), BF16)],
        compiler_params=_cp(("parallel",)),
    )(x, a_pool, o, zm, gp, gm, wpu, wmu, wout)


def _loss_head(y, target):
    lp = y.shape[0]
    tm = ROW0
    n_real = target.shape[0] // tm

    def body(y_ref, t_ref, d_ref, l_ref):
        i = pl.program_id(0)

        @pl.when(i == 0)
        def _():
            l_ref[...] = jnp.zeros(l_ref.shape, F32)

        real = (i >= 1) & (i <= n_real)
        err = jnp.where(real, y_ref[...] - t_ref[...], 0.0)
        d_ref[...] = err * (1.0 / D_MODEL)
        l_ref[...] += jnp.sum(err * err) * (0.5 / D_MODEL)

    return pl.pallas_call(
        body, name="loss_head", grid=(lp // tm,),
        in_specs=[_rows(tm, D_MODEL), pl.BlockSpec((tm, D_MODEL), lambda i: (jnp.clip(i - 1, 0, n_real - 1), 0))],
        out_specs=[_rows(tm, D_MODEL), pl.BlockSpec((8, 128), lambda i: (0, 0))],
        out_shape=[jax.ShapeDtypeStruct((lp, D_MODEL), F32), jax.ShapeDtypeStruct((8, 128), F32)],
        compiler_params=_cp(("arbitrary",)),
    )(y, target)


def _pair_rowsum(prod):
    lane = lax.broadcasted_iota(jnp.int32, prod.shape, 1)
    lo = jnp.sum(jnp.where(lane < V_DIM, prod, 0.0), axis=-1, keepdims=True)
    hi = jnp.sum(jnp.where(lane < V_DIM, 0.0, prod), axis=-1, keepdims=True)
    return jnp.where(lane < V_DIM, lo, hi)


def _merge_bwd(dres, yp, ym, gp, gm, o, zm, wout, wpu, wmu):
    lp = dres.shape[0]
    tm = _row_tile(lp, ROW_TILES_HEAVY)

    def body(dres_ref, yp_ref, ym_ref, gp_ref, gm_ref, o_ref, zm_ref, wout_ref, wpu_ref, wmu_ref,
             merged_ref, dyp_ref, dym_ref, dgp_ref, dgm_ref, dap_ref, amla_ref, do_ref, dzm_ref, delta_ref):
        dmerged = _nt(dres_ref[...].astype(BF16), wout_ref[...])
        sp = _sigmoid(gp_ref[...].astype(F32))
        sm = _sigmoid(gm_ref[...].astype(F32))
        ypf = yp_ref[...].astype(F32)
        ymf = ym_ref[...].astype(F32)
        merged_ref[...] = (sp * ypf + sm * ymf).astype(merged_ref.dtype)
        dyp = (dmerged * sp).astype(BF16)
        dym = (dmerged * sm).astype(BF16)
        dyp_ref[...] = dyp
        dym_ref[...] = dym
        dgp_ref[...] = (dmerged * ypf * sp * (1.0 - sp)).astype(dgp_ref.dtype)
        dgm_ref[...] = (dmerged * ymf * sm * (1.0 - sm)).astype(dgm_ref.dtype)
        dap_ref[...] = _nt(dyp, wpu_ref[...]).astype(dap_ref.dtype)
        dam = _nt(dym, wmu_ref[...])
        zf = zm_ref[...].astype(F32)
        sg = _sigmoid(zf)
        si = zf * sg
        of = o_ref[...].astype(F32)
        amla_ref[...] = (of * si).astype(amla_ref.dtype)
        do = dam * si
        do_ref[...] = do.astype(do_ref.dtype)
        dzm_ref[...] = (dam * of * (sg * (1.0 + zf * (1.0 - sg)))).astype(dzm_ref.dtype)
        prod = do * of
        for j in range(N_HEADS // 2):
            hb = slice(j * HEAD_PAD, (j + 1) * HEAD_PAD)
            delta_ref[:, hb] = _pair_rowsum(prod[:, hb])

    bf = lambda w: jax.ShapeDtypeStruct((lp, w), BF16)
    return pl.pallas_call(
        body, name="merge_bwd", grid=(lp // tm,),
        in_specs=[_rows(tm, D_MODEL)] * 5 + [_rows(tm, MLA_WIDTH)] * 2
        + [_resident((D_MODEL, D_MODEL)), _resident((POOL_WIDTH, D_MODEL)), _resident((MLA_WIDTH, D_MODEL))],
        out_specs=[_rows(tm, D_MODEL)] * 5 + [_rows(tm, POOL_WIDTH)] + [_rows(tm, MLA_WIDTH)] * 4,
        out_shape=[bf(D_MODEL)] * 5 + [bf(POOL_WIDTH)] + [bf(MLA_WIDTH)] * 3 + [jax.ShapeDtypeStruct((lp, MLA_WIDTH), F32)],
        compiler_params=_cp(("parallel",)),
    )(dres, yp, ym, gp, gm, o, zm, wout, wpu, wmu)


def _flash_bwd(q, k, v, do, lse, delta, scatter=()):
    lp = q.shape[0]
    tq = tk = ATTN_TILE
    nq = lp // tq
    heads = 2 * BWD_PAIRS
    nj = N_HEADS // heads
    qk_w, v_w = heads * HEAD_PAD, BWD_PAIRS * HEAD_PAD
    scale = 1.0 / math.sqrt(QK_DIM)
    ex = _ChipExchange(list(scatter), scatter=True)

    pairs = [(t, i) for t in range(nq) for i in range(t, nq)]
    t_tab = jnp.asarray([p[0] for p in pairs], jnp.int32)
    i_tab = jnp.asarray([p[1] for p in pairs], jnp.int32)

    def body(t_tab_ref, i_tab_ref, q_ref, k_ref, v_ref, do_ref, lse_ref, dl_ref, *rest):
        ex_in, (dq_hbm, dk_ref, dv_ref), ex_out = rest[:ex.n], rest[ex.n:ex.n + 3], rest[ex.n + 3:2 * ex.n + 3]
        (dq_acc, dk_acc, dv_acc, s_scr, dp_scr, p_scr, ds_scr, doh_scr, stat_scr, stage_scr,
         stage_sem) = rest[2 * ex.n + 3:2 * ex.n + 14]
        ex_sems = rest[2 * ex.n + 14:]
        j, step_no = pl.program_id(0), pl.program_id(1)
        t, i = t_tab_ref[step_no], i_tab_ref[step_no]
        if ex.n:
            pl.when((j == 0) & (step_no == 0))(lambda: ex.start(ex_in, ex_out, ex_sems))

        @pl.when(step_no == 0)
        def _():
            dq_acc[...] = jnp.zeros(dq_acc.shape, F32)

        @pl.when(i == t)
        def _():
            dk_acc[...] = jnp.zeros(dk_acc.shape, F32)
            dv_acc[...] = jnp.zeros(dv_acc.shape, F32)

        def step(kind):
            lane = lax.broadcasted_iota(jnp.int32, (tq, HEAD_PAD), 1)
            for pp in range(BWD_PAIRS):
                pb = slice(pp * HEAD_PAD, (pp + 1) * HEAD_PAD)
                for hh in range(2):
                    hb = slice((2 * pp + hh) * HEAD_PAD, (2 * pp + hh + 1) * HEAD_PAD)
                    mine = (lane < V_DIM) if hh == 0 else (lane >= V_DIM)
                    doh_scr[hh] = jnp.where(mine, do_ref[:, pb], jnp.zeros((tq, HEAD_PAD), BF16))
                    s_scr[hh] = _nt(q_ref[:, hb], k_ref[:, hb])
                    dp_scr[hh] = _nt(doh_scr[hh], v_ref[:, pb])
                for hh in range(2):
                    head = 2 * pp + hh
                    hb = slice(head * HEAD_PAD, (head + 1) * HEAD_PAD)
                    col = slice(pp * HEAD_PAD + hh * V_DIM, pp * HEAD_PAD + hh * V_DIM + 1)
                    stat_scr[0] = _lanes(lse_ref[:, col])
                    stat_scr[1] = _lanes(dl_ref[:, col])
                    for r in range(tq // ROW_BLOCK):
                        rows = slice(r * ROW_BLOCK, (r + 1) * ROW_BLOCK)
                        lse_r = stat_scr[0, rows, :]
                        dl_r = stat_scr[1, rows, :]
                        chunks = _score_chunks(kind, r, tk)
                        for c0, masked in chunks:
                            cols = slice(c0, c0 + ATTN_BLOCK)
                            s = s_scr[hh, rows, cols]
                            if masked:
                                s = _causal_mask(s, i * tq + r * ROW_BLOCK, t * tk + c0)
                            p = jnp.exp2(s - lse_r)
                            p_scr[hh, rows, cols] = p.astype(BF16)
                            ds_scr[hh, rows, cols] = (p * (dp_scr[hh, rows, cols] - dl_r)).astype(BF16)
                        done = chunks[-1][0] + ATTN_BLOCK
                        if done < tk:
                            zeros = jnp.zeros((ROW_BLOCK, tk - done), BF16)
                            p_scr[hh, rows, done:tk] = zeros
                            ds_scr[hh, rows, done:tk] = zeros
                    dv_acc[pp] += _tn(doh_scr[hh], p_scr[hh])
                    dk_acc[head] += _tn(q_ref[:, hb], ds_scr[hh])
                    dq_acc[i, hb, :] += lax.dot_general(k_ref[:, hb], ds_scr[hh], (((0,), (1,)), ((), ())),
                                                        preferred_element_type=F32)

        for kind, pred in _tile_kinds(i, t):
            pl.when(pred)(functools.partial(step, kind))

        @pl.when(i == nq - 1)
        def _():
            for head in range(heads):
                hb = slice(head * HEAD_PAD, (head + 1) * HEAD_PAD)
                dk_ref[:, hb] = (dk_acc[head].T * LN2).astype(dk_ref.dtype)
            for pp in range(BWD_PAIRS):
                dv_ref[:, pp * HEAD_PAD:(pp + 1) * HEAD_PAD] = dv_acc[pp].T.astype(dv_ref.dtype)

        @pl.when(step_no == len(pairs) - 1)
        def _():
            my_cols = pl.ds(pl.multiple_of(j * qk_w, qk_w), qk_w)
            for qi in range(nq):
                for head in range(heads):
                    hb = slice(head * HEAD_PAD, (head + 1) * HEAD_PAD)
                    stage_scr[:, hb] = (dq_acc[qi, hb, :].T * scale).astype(BF16)
                out = pltpu.make_async_copy(stage_scr, dq_hbm.at[pl.ds(qi * tq, tq), my_cols], stage_sem)
                out.start()
                out.wait()

        if ex.n:
            pl.when((j == nj - 1) & (step_no == len(pairs) - 1))(lambda: ex.wait(ex_in, ex_out, ex_sems))

    q_idx = lambda j, p, tt, it: (it[p], j)
    kv_idx = lambda j, p, tt, it: (tt[p], j)
    hw = N_HEADS * HEAD_PAD
    dq, dk, dv, *pieces = pl.pallas_call(
        body, name="flash_bwd_scatter" if ex.n else "flash_bwd",
        grid_spec=pltpu.PrefetchScalarGridSpec(
            num_scalar_prefetch=2, grid=(nj, len(pairs)),
            in_specs=[pl.BlockSpec((tq, qk_w), q_idx), pl.BlockSpec((tk, qk_w), kv_idx),
                      pl.BlockSpec((tk, v_w), kv_idx), pl.BlockSpec((tq, v_w), q_idx),
                      pl.BlockSpec((tq, v_w), q_idx), pl.BlockSpec((tq, v_w), q_idx)] + ex.specs,
            out_specs=[HBM_SPEC, pl.BlockSpec((tk, qk_w), kv_idx), pl.BlockSpec((tk, v_w), kv_idx)] + ex.specs,
            scratch_shapes=[pltpu.VMEM((nq, qk_w, tq), F32), pltpu.VMEM((heads, HEAD_PAD, tk), F32),
                            pltpu.VMEM((BWD_PAIRS, HEAD_PAD, tk), F32),
                            pltpu.VMEM((2, tq, tk), F32), pltpu.VMEM((2, tq, tk), F32),
                            pltpu.VMEM((2, tq, tk), BF16), pltpu.VMEM((2, tq, tk), BF16),
                            pltpu.VMEM((2, tq, HEAD_PAD), BF16), pltpu.VMEM((2, tq, HEAD_PAD), F32),
                            pltpu.VMEM((tq, qk_w), BF16), pltpu.SemaphoreType.DMA(())]
            + (ex.sems if ex.n else [])),
        out_shape=[jax.ShapeDtypeStruct((lp, hw), BF16), jax.ShapeDtypeStruct((lp, hw), BF16),
                   jax.ShapeDtypeStruct((lp, MLA_WIDTH), BF16)] + ex.out_shape,
        compiler_params=_cp(("arbitrary",) * 2),
    )(t_tab, i_tab, q, k, v, do, lse, delta, *scatter)
    return dq, dk, dv, pieces


def _mla_prep_bwd(dq, dk, dv, cq, ckv, kr, tabs, gqa, gkva, gqn, gkn, wq, wkn, wv):
    lp = cq.shape[0]
    tm = _row_tile(lp, ROW_TILES_HEAVY)
    hw = N_HEADS * HEAD_PAD

    def body(dq_ref, dk_ref, dv_ref, cq_ref, ckv_ref, kr_ref, c_ref, s1_ref, s2_ref, gqa_ref, gkva_ref, gqn_ref,
             gkn_ref, wq_ref, wkn_ref, wv_ref, dcq_ref, dckv_ref, dkr_ref, dwq_ref, dwkn_ref, dwv_ref,
             dgqa_ref, dgkva_ref, dgqn_ref, dgkn_ref, draw_scr):
        @pl.when(pl.program_id(0) == 0)
        def _():
            for r in (dwq_ref, dwkn_ref, dwv_ref, dgqa_ref, dgkva_ref, dgqn_ref, dgkn_ref):
                r[...] = jnp.zeros(r.shape, F32)

        c, s1, s2 = c_ref[...], s1_ref[...], s2_ref[...]
        lane = lax.broadcasted_iota(jnp.int32, (tm, HEAD_PAD), 1)

        inv_q, xhat_q, cqn = _rms_fwd(cq_ref[...].astype(F32), gqa_ref[...])
        cqn_b = cqn.astype(BF16)
        qraw = _mm(cqn_b, wq_ref[...])
        dgqn = jnp.zeros((1, HEAD_PAD), F32)
        for h in range(N_HEADS):
            hb = slice(h * HEAD_PAD, (h + 1) * HEAD_PAD)
            inv, xhat, _ = _head_norm_fwd(qraw[:, hb], gqn_ref[...])
            dy = _rope_t(dq_ref[:, hb].astype(F32), c, s1, s2)
            dgqn += jnp.sum(dy * xhat, axis=0, keepdims=True)
            draw_scr[:, hb] = _head_norm_bwd(dy, inv, xhat, gqn_ref[...]).astype(BF16)
        dgqn_ref[...] += dgqn
        dqraw = draw_scr[...]
        dwq_ref[...] += _tn(cqn_b, dqraw)
        dcq, dgqa = _rms_bwd(_nt(dqraw, wq_ref[...]), inv_q, xhat_q, gqa_ref[...])
        dcq_ref[...] = dcq.astype(dcq_ref.dtype)
        dgqa_ref[...] += dgqa

        inv_kv, xhat_kv, ckvn = _rms_fwd(ckv_ref[...].astype(F32), gkva_ref[...])
        ckvn_b = ckvn.astype(BF16)
        knraw = _mm(ckvn_b, wkn_ref[...])
        krs = kr_ref[...].astype(F32)
        dgkn = jnp.zeros((1, HEAD_PAD), F32)
        dkr = jnp.zeros((tm, HEAD_PAD), F32)
        for h in range(N_HEADS):
            hb = slice(h * HEAD_PAD, (h + 1) * HEAD_PAD)
            inv, xhat, _ = _head_norm_fwd(knraw[:, hb] + krs, gkn_ref[...])
            dy = _rope_t(dk_ref[:, hb].astype(F32), c, s1, s2)
            dgkn += jnp.sum(dy * xhat, axis=0, keepdims=True)
            dxh = _head_norm_bwd(dy, inv, xhat, gkn_ref[...])
            dkr += dxh
            draw_scr[:, hb] = jnp.where(lane < NOPE, dxh, 0.0).astype(BF16)
        dgkn_ref[...] += dgkn
        dkr_ref[...] = jnp.where((lane >= KR_LANE0) & (lane < QK_DIM), dkr, 0.0).astype(dkr_ref.dtype)
        dknraw = draw_scr[...]
        dvb = dv_ref[...]
        dwkn_ref[...] += _tn(ckvn_b, dknraw)
        dwv_ref[...] += _tn(ckvn_b, dvb)
        dckvn = _nt(dknraw, wkn_ref[...]) + _nt(dvb, wv_ref[...])
        dckv, dgkva = _rms_bwd(dckvn, inv_kv, xhat_kv, gkva_ref[...])
        dckv_ref[...] = dckv.astype(dckv_ref.dtype)
        dgkva_ref[...] += dgkva

    vec = lambda n: pl.BlockSpec((1, n), lambda i: (0, 0))
    whole = lambda r, c: pl.BlockSpec((r, c), lambda i: (0, 0))
    f = lambda r, c: jax.ShapeDtypeStruct((r, c), F32)
    return pl.pallas_call(
        body, name="mla_prep_bwd", grid=(lp // tm,),
        in_specs=[_rows(tm, hw), _rows(tm, hw), _rows(tm, MLA_WIDTH), _rows(tm, Q_RANK), _rows(tm, KV_RANK),
                  _rows(tm, HEAD_PAD)] + [_rows(tm, HEAD_PAD)] * 3
        + [vec(Q_RANK), vec(KV_RANK), vec(HEAD_PAD), vec(HEAD_PAD),
           _resident((Q_RANK, hw)), _resident((KV_RANK, hw)), _resident((KV_RANK, MLA_WIDTH))],
        out_specs=[_rows(tm, Q_RANK), _rows(tm, KV_RANK), _rows(tm, HEAD_PAD),
                   whole(Q_RANK, hw), whole(KV_RANK, hw), whole(KV_RANK, MLA_WIDTH),
                   vec(Q_RANK), vec(KV_RANK), vec(HEAD_PAD), vec(HEAD_PAD)],
        out_shape=[jax.ShapeDtypeStruct((lp, Q_RANK), BF16), jax.ShapeDtypeStruct((lp, KV_RANK), BF16),
                   jax.ShapeDtypeStruct((lp, HEAD_PAD), BF16),
                   f(Q_RANK, hw), f(KV_RANK, hw), f(KV_RANK, MLA_WIDTH),
                   f(1, Q_RANK), f(1, KV_RANK), f(1, HEAD_PAD), f(1, HEAD_PAD)],
        scratch_shapes=[pltpu.VMEM((tm, hw), BF16)],
        compiler_params=_cp(("arbitrary",)),
    )(dq, dk, dv, cq, ckv, kr, *tabs, gqa, gkva, gqn, gkn, wq, wkn, wv)


def _pool_bwd(dap, u, zp, wg, scale):
    lp = u.shape[0]
    tm = _row_tile(lp)
    n = lp // tm
    per = tm // HALO

    def body(dap_ref, u_ref, uh_ref, z_ref, wg_ref, sc_ref, du_ref, dz_ref, dwg_ref, dsc_ref, ext_u, ext_d):
        i = pl.program_id(0)
        r = n - 1 - i

        @pl.when(i == 0)
        def _():
            dwg_ref[...] = jnp.zeros(dwg_ref.shape, F32)
            dsc_ref[...] = jnp.zeros(dsc_ref.shape, F32)
            ext_d[tm:tm + HALO, :] = jnp.zeros((HALO, POOL_WIDTH), F32)

        ext_u[0:HALO, :] = jnp.where(r == 0, 0.0, uh_ref[...].astype(F32))
        ext_u[HALO:HALO + tm, :] = u_ref[...].astype(F32)
        e = ext_u[...]
        sums = _trailing_sums(e)
        inv_cnt = _inv_counts(r, tm)
        dmixed = []
        for g in range(POOL_GROUPS):
            cols = slice(g * GROUP_DIM, (g + 1) * GROUP_DIM)
            mixed_b = (sums[g][HALO:, cols] * inv_cnt[g] - e[HALO:, cols]).astype(BF16)
            yg = _mm(mixed_b, wg_ref[g])
            zf = z_ref[:, cols].astype(F32)
            sg = _sigmoid(zf)
            da = dap_ref[:, cols].astype(F32)
            dy = da * (zf * sg)
            dz_ref[:, cols] = (da * (yg * sc_ref[:, cols]) * (sg * (1.0 + zf * (1.0 - sg)))).astype(dz_ref.dtype)
            dsc_ref[:, cols] += jnp.sum(dy * yg, axis=0, keepdims=True)
            dyg = (dy * sc_ref[:, cols]).astype(BF16)
            dwg_ref[g] += _tn(mixed_b, dyg)
            dm = _nt(dyg, wg_ref[g])
            dmixed.append(dm)
            ext_d[0:tm, cols] = dm * inv_cnt[g]
        ed = ext_d[...]
        lead = _leading_sums(ed)
        ext_d[tm:tm + HALO, :] = ed[0:HALO, :]
        for g in range(POOL_GROUPS):
            cols = slice(g * GROUP_DIM, (g + 1) * GROUP_DIM)
            du_ref[:, cols] = (lead[g][0:tm, cols] - dmixed[g]).astype(du_ref.dtype)

    rev = lambda i: (n - 1 - i, 0)
    return pl.pallas_call(
        body, name="pool_bwd", grid=(n,),
        in_specs=[pl.BlockSpec((tm, POOL_WIDTH), rev), pl.BlockSpec((tm, POOL_WIDTH), rev),
                  pl.BlockSpec((HALO, POOL_WIDTH), lambda i: (jnp.maximum((n - 1 - i) * per - 1, 0), 0)),
                  pl.BlockSpec((tm, POOL_WIDTH), rev),
                  pl.BlockSpec((POOL_GROUPS, GROUP_DIM, GROUP_DIM), lambda i: (0, 0, 0)),
                  pl.BlockSpec((1, POOL_WIDTH), lambda i: (0, 0))],
        out_specs=[pl.BlockSpec((tm, POOL_WIDTH), rev), pl.BlockSpec((tm, POOL_WIDTH), rev),
                   pl.BlockSpec((POOL_GROUPS, GROUP_DIM, GROUP_DIM), lambda i: (0, 0, 0)),
                   pl.BlockSpec((1, POOL_WIDTH), lambda i: (0, 0))],
        out_shape=[jax.ShapeDtypeStruct((lp, POOL_WIDTH), BF16), jax.ShapeDtypeStruct((lp, POOL_WIDTH), BF16),
                   jax.ShapeDtypeStruct((POOL_GROUPS, GROUP_DIM, GROUP_DIM), F32),
                   jax.ShapeDtypeStruct((1, POOL_WIDTH), F32)],
        scratch_shapes=[pltpu.VMEM((HALO + tm, POOL_WIDTH), F32), pltpu.VMEM((tm + HALO, POOL_WIDTH), F32)],
        compiler_params=_cp(("arbitrary",)),
    )(dap, u, u, zp, wg, scale)


def _inproj_bwd(dres, x, gain, w_pad, dparts, scatter=()):
    lp = x.shape[0]
    tm = _row_tile(lp, ROW_TILES_HEAVY)
    n_dp = len(IN_WIDTHS)
    ex = _ChipExchange(list(scatter), scatter=True)

    def body(dres_ref, x_ref, g_ref, w_ref, *rest):
        dps, ex_in = rest[:n_dp], rest[n_dp:n_dp + ex.n]
        dprev_ref, h_ref, dg_ref = rest[n_dp + ex.n:n_dp + ex.n + 3]
        ex_out, ex_sems = rest[n_dp + ex.n + 3:n_dp + 2 * ex.n + 3], rest[n_dp + 2 * ex.n + 3:]
        if ex.n:
            pl.when(pl.program_id(0) == 0)(lambda: ex.start(ex_in, ex_out, ex_sems))

        @pl.when(pl.program_id(0) == 0)
        def _():
            dg_ref[...] = jnp.zeros(dg_ref.shape, F32)

        dh = jnp.zeros((tm, D_MODEL), F32)
        for dp_ref, off, wd in zip(dps, IN_OFFS, IN_WIDTHS):
            dh += _mm(dp_ref[...], w_ref[off:off + wd, :])
        inv, xhat, hn = _rms_fwd(x_ref[...], g_ref[...])
        h_ref[...] = hn.astype(h_ref.dtype)
        dx, dgain = _rms_bwd(dh, inv, xhat, g_ref[...])
        dg_ref[...] += dgain
        dprev_ref[...] = dres_ref[...] + dx
        if ex.n:
            pl.when(pl.program_id(0) == lp // tm - 1)(lambda: ex.wait(ex_in, ex_out, ex_sems))

    dprev, h, dgain, *pieces = pl.pallas_call(
        body, name="inproj_bwd_scatter" if ex.n else "inproj_bwd", grid=(lp // tm,),
        in_specs=[_rows(tm, D_MODEL), _rows(tm, D_MODEL), pl.BlockSpec((1, D_MODEL), lambda i: (0, 0)),
                  _resident((IN_PAD, D_MODEL))] + [_rows(tm, wd) for wd in IN_WIDTHS] + ex.specs,
        out_specs=[_rows(tm, D_MODEL), _rows(tm, D_MODEL), pl.BlockSpec((1, D_MODEL), lambda i: (0, 0))] + ex.specs,
        out_shape=[jax.ShapeDtypeStruct((lp, D_MODEL), F32), jax.ShapeDtypeStruct((lp, D_MODEL), BF16),
                   jax.ShapeDtypeStruct((1, D_MODEL), F32)] + ex.out_shape,
        scratch_shapes=ex.sems if ex.n else [],
        compiler_params=_cp(("arbitrary",)),
    )(dres, x, gain, w_pad, *dparts, *scatter)
    return dprev, h, dgain, pieces


def _weight_grads(a, bs, name, transposed=False):
    lp, m = a.shape
    tk = _row_tile(lp)
    nb = len(bs)
    shapes = [(b.shape[1], m) if transposed else (m, b.shape[1]) for b in bs]

    def body(a_ref, *rest):
        b_refs, o_refs = rest[:nb], rest[nb:]

        @pl.when(pl.program_id(0) == 0)
        def _():
            for o_ref in o_refs:
                o_ref[...] = jnp.zeros(o_ref.shape, F32)

        ab = a_ref[...].astype(BF16)
        for b_ref, o_ref in zip(b_refs, o_refs):
            bb = b_ref[...].astype(BF16)
            o_ref[...] += _tn(bb, ab) if transposed else _tn(ab, bb)

    return pl.pallas_call(
        body, name=name, grid=(lp // tk,),
        in_specs=[_rows(tk, m)] + [_rows(tk, b.shape[1]) for b in bs],
        out_specs=[pl.BlockSpec(s, lambda i: (0, 0)) for s in shapes],
        out_shape=[jax.ShapeDtypeStruct(s, F32) for s in shapes],
        compiler_params=_cp(("arbitrary",)),
    )(a, *bs)


HBM_SPEC = pl.BlockSpec(memory_space=pltpu.HBM)


def _my_place():
    return lax.axis_index("x"), lax.axis_index("y"), lax.axis_index("c")


def _other_chips(x, y):
    return [(1 - x, y), (x, 1 - y), (1 - x, 1 - y)]


class _ChipExchange:
    def __init__(self, arrs, scatter):
        self.n = len(arrs)
        self.scatter = scatter
        self.out_shape = [jax.ShapeDtypeStruct(a.shape if scatter else (N_CHIPS,) + a.shape, a.dtype) for a in arrs]
        self.specs = [HBM_SPEC] * self.n
        self.sems = [pltpu.SemaphoreType.DMA((3 * self.n,)), pltpu.SemaphoreType.DMA((3 * self.n,)),
                     pltpu.SemaphoreType.DMA((self.n,))]

    def _copies(self, ins, outs, sems):
        send_sems, recv_sems, local_sems = sems
        x, y, c = _my_place()
        me = 2 * x + y
        chips = _other_chips(x, y)
        mine = lambda a: ins[a].at[me] if self.scatter else ins[a]

        def remote(a, k, arriving):
            px, py = chips[k]
            there = 2 * px + py
            return pltpu.make_async_remote_copy(
                src_ref=mine(a) if arriving or not self.scatter else ins[a].at[there],
                dst_ref=outs[a].at[there if arriving else me],
                send_sem=send_sems.at[a * 3 + k], recv_sem=recv_sems.at[a * 3 + k],
                device_id=(px, py, c), device_id_type=MESH)

        pairs = [(a, k) for a in range(self.n) for k in range(3)]
        local = [pltpu.make_async_copy(mine(a), outs[a].at[me], local_sems.at[a]) for a in range(self.n)]
        return local, [remote(a, k, False) for a, k in pairs], [remote(a, k, True) for a, k in pairs]

    def start(self, ins, outs, sems):
        local, sends, _ = self._copies(ins, outs, sems)
        for cp in local + sends:
            cp.start()

    def wait(self, ins, outs, sems):
        local, sends, arrivals = self._copies(ins, outs, sems)
        for cp in arrivals:
            cp.wait_recv()
        for cp in sends:
            cp.wait_send()
        for cp in local:
            cp.wait()


def _chip_exchange(arrs, scatter, name):
    ex = _ChipExchange(arrs, scatter)

    def body(*refs):
        ins, outs, sems = refs[:ex.n], refs[ex.n:2 * ex.n], refs[2 * ex.n:]
        ex.start(ins, outs, sems)
        ex.wait(ins, outs, sems)

    return pl.pallas_call(body, name=name, in_specs=ex.specs, out_specs=ex.specs, out_shape=ex.out_shape,
                          scratch_shapes=ex.sems)(*arrs)


def _sibling_exchange(arrs, name):
    n = len(arrs)

    def body(*refs):
        ins, outs = refs[:n], refs[n:2 * n]
        send_sems, recv_sems = refs[2 * n:]
        x, y, c = _my_place()
        cps = [pltpu.make_async_remote_copy(src_ref=ins[a], dst_ref=outs[a], send_sem=send_sems.at[a],
                                            recv_sem=recv_sems.at[a], device_id=(x, y, 1 - c), device_id_type=MESH)
               for a in range(n)]
        for cp in cps:
            cp.start()
        for cp in cps:
            cp.wait_recv()
        for cp in cps:
            cp.wait_send()

    return pl.pallas_call(
        body, name=name, in_specs=[HBM_SPEC] * n, out_specs=[HBM_SPEC] * n,
        out_shape=[jax.ShapeDtypeStruct(a.shape, a.dtype) for a in arrs],
        scratch_shapes=[pltpu.SemaphoreType.DMA((n,)), pltpu.SemaphoreType.DMA((n,))],
    )(*arrs)


def _all_reduce_small(pack):
    rows = pack.shape[0]

    def body(p_ref, o_ref, g_scr, send_sems, recv_sems):
        x, y, c = _my_place()
        me = 4 * x + 2 * y + c
        flips = [(dx, dy, dc) for dx in (0, 1) for dy in (0, 1) for dc in (0, 1) if (dx, dy, dc) != (0, 0, 0)]

        def peer(f):
            return (x if f[0] == 0 else 1 - x, y if f[1] == 0 else 1 - y, c if f[2] == 0 else 1 - c)

        def copy(k, slot):
            return pltpu.make_async_remote_copy(src_ref=p_ref, dst_ref=g_scr.at[slot], send_sem=send_sems.at[k],
                                                recv_sem=recv_sems.at[k], device_id=peer(flips[k]), device_id_type=MESH)

        sends = [copy(k, me) for k in range(len(flips))]
        for cp in sends:
            cp.start()
        g_scr[me] = p_ref[...]
        for k, f in enumerate(flips):
            px, py, pc = peer(f)
            copy(k, 4 * px + 2 * py + pc).wait_recv()
        for cp in sends:
            cp.wait_send()
        acc = g_scr[0]
        for d in range(1, N_DEV):
            acc = acc + g_scr[d]
        o_ref[...] = acc

    vm = pl.BlockSpec(memory_space=pltpu.VMEM)
    return pl.pallas_call(
        body, name="all_reduce_small", in_specs=[vm], out_specs=vm,
        out_shape=jax.ShapeDtypeStruct(pack.shape, F32),
        scratch_shapes=[pltpu.VMEM((N_DEV, rows, 128), F32), pltpu.SemaphoreType.DMA((N_DEV - 1,)),
                        pltpu.SemaphoreType.DMA((N_DEV - 1,))],
        compiler_params=_cp(),
    )(pack)


def _as3d(a):
    return a.reshape((-1,) + a.shape[-2:])


def _row_block(r, sublanes=8, cap=512):
    fits = [t for t in range(sublanes, min(r, cap) + 1, sublanes) if r % t == 0]
    return fits[-1] if fits else r


def _sum_pieces(pieces, name):
    _, na, r, c = pieces.shape
    rt = _row_block(r, sublanes=16)

    def body(p_ref, o_ref):
        acc = p_ref[0, 0].astype(F32)
        for s in range(1, N_CHIPS):
            acc = acc + p_ref[s, 0].astype(F32)
        o_ref[0] = acc

    return pl.pallas_call(
        body, name=name, grid=(na, r // rt),
        in_specs=[pl.BlockSpec((N_CHIPS, 1, rt, c), lambda a, i: (0, a, i, 0))],
        out_specs=pl.BlockSpec((1, rt, c), lambda a, i: (a, i, 0)),
        out_shape=jax.ShapeDtypeStruct((na, r, c), F32),
        compiler_params=_cp(("parallel", "parallel")),
    )(pieces)


def _adamw(w, g_parts, m, v, name):
    na, r, c = w.shape
    rt = _row_block(r)
    ng = len(g_parts)

    def body(w_ref, *rest):
        g_refs = rest[:ng]
        m_ref, v_ref, g_out, d_out, m_out, v_out = rest[ng:]
        g = g_refs[0][...]
        for gr in g_refs[1:]:
            g = g + gr[...]
        m_new = ADAM_B1 * m_ref[...] + (1.0 - ADAM_B1) * g
        v_new = ADAM_B2 * v_ref[...] + (1.0 - ADAM_B2) * (g * g)
        m_hat = m_new / (1.0 - ADAM_B1 ** ADAM_STEP)
        v_hat = v_new / (1.0 - ADAM_B2 ** ADAM_STEP)
        g_out[...] = g
        d_out[...] = -ADAM_LR * (m_hat / (jnp.sqrt(v_hat) + ADAM_EPS) + ADAM_WD * w_ref[...])
        m_out[...] = m_new
        v_out[...] = v_new

    spec = pl.BlockSpec((1, rt, c), lambda a, i: (a, i, 0))
    out = jax.ShapeDtypeStruct((na, r, c), F32)
    return pl.pallas_call(
        body, name=name, grid=(na, r // rt), in_specs=[spec] * (3 + ng), out_specs=[spec] * 4, out_shape=[out] * 4,
        compiler_params=_cp(("parallel", "parallel")),
    )(w, *g_parts, m, v)


def _cols_from_shards(g):
    g = jnp.moveaxis(g, 0, -2)
    return g.reshape(g.shape[:-2] + (g.shape[-2] * g.shape[-1],))


def _rows_from_shards(g):
    g = jnp.moveaxis(g, 0, -3)
    return g.reshape(g.shape[:-3] + (g.shape[-3] * g.shape[-2], g.shape[-1]))


def _cols_to_shards(w):
    w = w.reshape(w.shape[:-1] + (N_CHIPS, w.shape[-1] // N_CHIPS))
    return jnp.moveaxis(w, -2, 0)


def _rows_to_shards(w):
    w = w.reshape(w.shape[:-2] + (N_CHIPS, w.shape[-2] // N_CHIPS, w.shape[-1]))
    return jnp.moveaxis(w, -3, 0)


def _pad_w_in(wt):
    z = lambda n: jnp.zeros((n, wt.shape[1]), wt.dtype)
    return jnp.concatenate([wt[:2048], wt[2080:4640], z(KR_LANE0), wt[2048:2080], z(HEAD_PAD - QK_DIM)], axis=0)


def _unpad_w_in(parts):
    u, zp, cq, ckv, zm, gp, gm, kr = parts
    return jnp.concatenate([u, zp, cq, ckv, kr[KR_LANE0:QK_DIM], zm, gp, gm], axis=0)


def _pad_heads(w, real):
    w = w.reshape(w.shape[:-1] + (N_HEADS, real))
    w = jnp.pad(w, [(0, 0)] * (w.ndim - 1) + [(0, HEAD_PAD - real)])
    return w.reshape(w.shape[:-2] + (N_HEADS * HEAD_PAD,))


def _flat_rows(a):
    a = a.reshape(-1)
    return jnp.pad(a, (0, (-a.shape[0]) % (8 * 128))).reshape(-1, 128)


def kernel(x, positions, meta_tokens, norm_gain, w_in, pool_w_group, pool_scale, pool_w_up, q_a_norm_gain, kv_a_norm_gain, w_q_b, w_kv_b, q_norm_gain, k_norm_gain, mla_w_up, w_out, loss_target, m_meta_tokens, m_norm_gain, m_w_in, m_pool_w_group, m_pool_scale, m_pool_w_up, m_q_a_norm_gain, m_kv_a_norm_gain, m_w_q_b, m_w_kv_b, m_q_norm_gain, m_k_norm_gain, m_mla_w_up, m_w_out, v_meta_tokens, v_norm_gain, v_w_in, v_pool_w_group, v_pool_scale, v_pool_w_up, v_q_a_norm_gain, v_kv_a_norm_gain, v_w_q_b, v_w_kv_b, v_q_norm_gain, v_k_norm_gain, v_mla_w_up, v_w_out):
    seq = x.shape[1]
    lp = -(-(ROW0 + seq) // ATTN_TILE) * ATTN_TILE
    pad_back = lp - ROW0 - seq
    chip = 2 * lax.axis_index("x") + lax.axis_index("y")

    tr = lambda a: jnp.swapaxes(a, 1, 2)
    big = dict(w_in=tr(w_in), pool_w_up=pool_w_up, w_q_b=w_q_b, w_kv_b=w_kv_b, mla_w_up=mla_w_up, w_out=w_out)
    row_sharded = ("w_in", "w_q_b", "w_out")
    names = list(big)
    shards = [[big[n][l].astype(BF16) for n in names] for l in range(DEPTH)]
    from_shards = lambda n: _rows_from_shards if n in row_sharded else _cols_from_shards
    to_shards = lambda n: _rows_to_shards if n in row_sharded else _cols_to_shards

    def in_weights(g_w_in):
        return dict(w_pad=_pad_w_in(from_shards("w_in")(g_w_in)))

    def rest_weights(gathered):
        w = {n: from_shards(n)(g) for n, g in zip(names[1:], gathered)}
        wkv = w["w_kv_b"].reshape(KV_RANK, N_HEADS, NOPE + V_DIM)
        return dict(wq=_pad_heads(w["w_q_b"], QK_DIM),
                    wkn=_pad_heads(wkv[..., :NOPE].reshape(KV_RANK, N_HEADS * NOPE), NOPE),
                    wv=wkv[..., NOPE:].reshape(KV_RANK, MLA_WIDTH),
                    wpu=w["pool_w_up"], wmu=w["mla_w_up"], wout=w["w_out"])

    g_in0, meta_g = _chip_exchange([shards[0][0], meta_tokens], scatter=False, name="gather_layer0")
    weights = [in_weights(g_in0)]
    meta_full = _cols_from_shards(meta_g)
    wg = pool_w_group.astype(BF16)
    gqn = jnp.pad(q_norm_gain, ((0, 0), (0, HEAD_PAD - QK_DIM)))
    gkn = jnp.pad(k_norm_gain, ((0, 0), (0, HEAD_PAD - QK_DIM)))

    x_pad = jnp.concatenate([jnp.zeros((PAD_FRONT, D_MODEL), F32), meta_full, x[0], jnp.zeros((pad_back, D_MODEL), F32)], axis=0)
    pos_pad = jnp.concatenate([jnp.zeros((PAD_FRONT,), jnp.int32), jnp.arange(N_META, dtype=jnp.int32),
                               positions[0] + N_META, jnp.zeros((pad_back,), jnp.int32)])
    half = ROPE // 2
    inv_freq = (ROPE_THETA ** (-np.arange(half, dtype=np.float32) / half)).astype(np.float32)
    freq_row = np.zeros((1, HEAD_PAD), np.float32)
    freq_row[0, NOPE:NOPE + half] = inv_freq
    freq_row[0, NOPE + half:QK_DIM] = inv_freq
    tabs = _rope_tables(pos_pad[:, None], jnp.asarray(freq_row))

    row = lambda a, l: a[l][None, :]

    saved = []
    h_res = x_pad
    for l in range(DEPTH):
        w = weights[l]
        (u, zp, cq, ckv, zm, gp, gm, kr), rest0 = _inproj_fwd(h_res, row(norm_gain, l), w["w_pad"],
                                                              gather=shards[0][1:] if l == 0 else ())
        if rest0:
            w.update(rest_weights(rest0))
        a_pool = _pool_fwd(u, zp, wg[l], row(pool_scale, l))
        q, k, v = _mla_prep_fwd(cq, ckv, kr, tabs, row(q_a_norm_gain, l), row(kv_a_norm_gain, l), row(gqn, l), row(gkn, l),
                                w["wq"], w["wkn"], w["wv"])
        o, lse, nxt = _flash_fwd(q, k, v, gather=shards[l + 1] if l + 1 < DEPTH else ())
        if nxt:
            weights.append({**in_weights(nxt[0]), **rest_weights(nxt[1:])})
        h_next, yp, ym = _merge_fwd(h_res, a_pool, o, zm, gp, gm, w["wpu"], w["wmu"], w["wout"])
        saved.append(dict(x=h_res, u=u, zp=zp, cq=cq, ckv=ckv, zm=zm, gp=gp, gm=gm, kr=kr, a_pool=a_pool, q=q, k=k, v=v,
                          o=o, lse=lse, yp=yp, ym=ym))
        h_res = h_next
    dres, loss_blk = _loss_head(h_res, loss_target[0])

    gw = {n: [None] * DEPTH for n in names}
    pieces = [None] * DEPTH
    grad_stacks = lambda l, which=names: [to_shards(n)(gw[n][l]).astype(BF16) for n in which]
    gs = {n: [None] * DEPTH for n in ("norm_gain", "pool_w_group", "pool_scale", "q_a", "kv_a", "q_norm", "k_norm")}
    for l in reversed(range(DEPTH)):
        s, w = saved[l], weights[l]
        merged, dyp, dym, dgp, dgm, dap, amla, do, dzm, delta = _merge_bwd(
            dres, s["yp"], s["ym"], s["gp"], s["gm"], s["o"], s["zm"], w["wout"], w["wpu"], w["wmu"])
        (gw["w_out"][l],) = _weight_grads(merged, [dres], "grad_w_out")
        (gw["pool_w_up"][l],) = _weight_grads(s["a_pool"], [dyp], "grad_pool_w_up")
        (gw["mla_w_up"][l],) = _weight_grads(amla, [dym], "grad_mla_w_up")
        dq, dk, dv, got = _flash_bwd(s["q"], s["k"], s["v"], do, s["lse"], delta,
                                     scatter=grad_stacks(l + 1) if l + 1 < DEPTH else ())
        if got:
            pieces[l + 1] = got
        dcq, dckv, dkr, dwq, dwkn, dwv, gs["q_a"][l], gs["kv_a"][l], dgqn, dgkn = _mla_prep_bwd(
            dq, dk, dv, s["cq"], s["ckv"], s["kr"], tabs, row(q_a_norm_gain, l), row(kv_a_norm_gain, l), row(gqn, l), row(gkn, l),
            w["wq"], w["wkn"], w["wv"])
        gs["q_norm"][l] = dgqn[:, :QK_DIM]
        gs["k_norm"][l] = dgkn[:, :QK_DIM]
        gw["w_q_b"][l] = dwq.reshape(Q_RANK, N_HEADS, HEAD_PAD)[..., :QK_DIM].reshape(Q_RANK, N_HEADS * QK_DIM)
        gw["w_kv_b"][l] = jnp.concatenate([dwkn.reshape(KV_RANK, N_HEADS, HEAD_PAD)[..., :NOPE],
                                           dwv.reshape(KV_RANK, N_HEADS, V_DIM)], axis=-1).reshape(KV_RANK, N_HEADS * (NOPE + V_DIM))
        du, dzp, gs["pool_w_group"][l], gs["pool_scale"][l] = _pool_bwd(dap, s["u"], s["zp"], wg[l], row(pool_scale, l))
        dparts = [du, dzp, dcq, dckv, dzm, dgp, dgm, dkr]
        dres, h, gs["norm_gain"][l], rest0 = _inproj_bwd(dres, s["x"], row(norm_gain, l), w["w_pad"], dparts,
                                                         scatter=grad_stacks(0, names[1:]) if l == 0 else ())
        ga = _weight_grads(h, [du, dzp, dcq, dckv, dkr], "grad_w_in_a", transposed=True)
        gb = _weight_grads(h, [dzm, dgp, dgm], "grad_w_in_b", transposed=True)
        gw["w_in"][l] = _unpad_w_in([ga[0], ga[1], ga[2], ga[3], gb[0], gb[1], gb[2], ga[4]])
    grad_x = dres[ROW0:ROW0 + seq][None]

    pieces[0] = list(_chip_exchange(grad_stacks(0, names[:1]), scatter=True, name="scatter_layer0")) + list(rest0)
    sums = [_sum_pieces(jnp.stack([pieces[l][a] for l in range(DEPTH)], axis=1), "sum_" + n) for a, n in enumerate(names)]
    other = _sibling_exchange(sums, name="swap_core_sums")
    moments = dict(w_in=(tr(m_w_in), tr(v_w_in)), pool_w_up=(m_pool_w_up, v_pool_w_up), w_q_b=(m_w_q_b, v_w_q_b),
                   w_kv_b=(m_w_kv_b, v_w_kv_b), mla_w_up=(m_mla_w_up, v_mla_w_up), w_out=(m_w_out, v_w_out))
    big_out = {n: _adamw(big[n], [sm, ot], moments[n][0], moments[n][1], "adamw_" + n)
               for n, sm, ot in zip(names, sums, other)}

    small_names = ("norm_gain", "pool_w_group", "pool_scale", "q_a", "kv_a", "q_norm", "k_norm")
    small_w = dict(norm_gain=(norm_gain, m_norm_gain, v_norm_gain), pool_w_group=(pool_w_group, m_pool_w_group, v_pool_w_group),
                   pool_scale=(pool_scale, m_pool_scale, v_pool_scale), q_a=(q_a_norm_gain, m_q_a_norm_gain, v_q_a_norm_gain),
                   kv_a=(kv_a_norm_gain, m_kv_a_norm_gain, v_kv_a_norm_gain), q_norm=(q_norm_gain, m_q_norm_gain, v_q_norm_gain),
                   k_norm=(k_norm_gain, m_k_norm_gain, v_k_norm_gain))
    small_g = {n: jnp.stack(gs[n]).reshape(small_w[n][0].shape) for n in small_names}
    blocks = [_flat_rows(small_g[n]) for n in small_names]
    n_rows = [b.shape[0] for b in blocks]
    meta_rows = N_META * D_MODEL // 128
    pack = jnp.concatenate(blocks + [dres[PAD_FRONT:ROW0].reshape(meta_rows, 128), loss_blk], axis=0)
    pack = jnp.pad(pack, ((0, (-pack.shape[0]) % 8), (0, 0)))
    total = _all_reduce_small(pack)
    n_small = sum(n_rows)
    loss = total[n_small + meta_rows, 0]
    gmeta = lax.dynamic_slice_in_dim(total[n_small:n_small + meta_rows].reshape(N_META, D_MODEL), chip * (D_MODEL // N_CHIPS),
                                     D_MODEL // N_CHIPS, axis=1)

    def packed(idx, meta_part):
        p = jnp.concatenate([_flat_rows(small_w[n][idx]) for n in small_names] + [_flat_rows(meta_part)], axis=0)
        return jnp.pad(p, ((0, (-p.shape[0]) % 8), (0, 0)))[None]

    g_pack = jnp.concatenate([total[:n_small], _flat_rows(gmeta)], axis=0)
    g_pack = jnp.pad(g_pack, ((0, (-g_pack.shape[0]) % 8), (0, 0)))[None]
    small_out = _adamw(packed(0, meta_tokens), [g_pack], packed(1, m_meta_tokens), packed(2, v_meta_tokens), "adamw_small")

    def unpack(p):
        res, r0 = {}, 0
        for n, nr in zip(small_names, n_rows):
            shape = small_w[n][0].shape
            res[n] = p[0, r0:r0 + nr].reshape(-1)[:math.prod(shape)].reshape(shape)
            r0 += nr
        res["meta"] = p[0, r0:r0 + N_META * (D_MODEL // N_CHIPS) // 128].reshape(N_META, D_MODEL // N_CHIPS)
        return res

    small_res = [unpack(p) for p in small_out]

    def leaf(kind, name):
        key = {"meta_tokens": "meta", "q_a_norm_gain": "q_a", "kv_a_norm_gain": "kv_a", "q_norm_gain": "q_norm",
               "k_norm_gain": "k_norm"}.get(name, name)
        if name in big_out:
            return tr(big_out[name][kind]) if name == "w_in" else big_out[name][kind]
        return small_res[kind][key]

    order = ("meta_tokens", "norm_gain", "w_in", "pool_w_group", "pool_scale", "pool_w_up", "q_a_norm_gain", "kv_a_norm_gain",
             "w_q_b", "w_kv_b", "q_norm_gain", "k_norm_gain", "mla_w_up", "w_out")
    outs = [loss, grad_x]
    for kind in range(4):
        outs += [leaf(kind, n) for n in order]
    return tuple(outs)
```

```python
import functools
import math

import numpy as np
import jax
import jax.numpy as jnp
from jax import lax
from jax.experimental import pallas as pl
from jax.experimental.pallas import tpu as pltpu

F32 = jnp.float32
BF16 = jnp.bfloat16
MESH = pl.DeviceIdType.MESH

D_MODEL = 1024
DEPTH = 4
N_META = 16
POOL_WIDTH = 512
POOL_WINDOWS = (2, 4, 8, 16)
POOL_GROUPS = 4
GROUP_DIM = 128
N_HEADS = 8
NOPE = 64
ROPE = 32
QK_DIM = 96
V_DIM = 64
MLA_WIDTH = 512
KV_RANK = 256
Q_RANK = 768
ROPE_THETA = 10000.0
EPS = 1e-6
MASK_VALUE = -1e30
ATTN_BLOCK = 128
PAD_FRONT = (-N_META) % ATTN_BLOCK
ROW0 = PAD_FRONT + N_META
HEAD_PAD = 128
HALO = 16
N_CHIPS = 4
N_DEV = 8

IN_NAMES = ("u", "zp", "cq", "ckv", "zm", "gp", "gm", "kr")
IN_WIDTHS = (512, 512, 768, 256, 512, 1024, 1024, 128)
IN_OFFS = tuple(int(v) for v in np.cumsum((0,) + IN_WIDTHS[:-1]))
IN_PAD = sum(IN_WIDTHS)
KR_LANE0 = NOPE

ADAM_LR = 0.001
ADAM_B1 = 0.9
ADAM_B2 = 0.999
ADAM_EPS = 1e-08
ADAM_WD = 0.01
ADAM_STEP = 10

VMEM_LIMIT = 56 * 1024 * 1024
ATTN_TILE = 768
ROW_TILES = (768, 384)
ROW_TILES_HEAVY = (384,)
ROW_BLOCK = 32
FWD_PAIRS = 2
BWD_PAIRS = 2
LOG2E = 1.4426950408889634
LN2 = 0.6931471805599453
Q_PRESCALE = LOG2E / math.sqrt(QK_DIM)


def _cp(sem=None, vmem=VMEM_LIMIT):
    kw = dict(vmem_limit_bytes=vmem)
    if sem is not None:
        kw["dimension_semantics"] = sem
    return pltpu.CompilerParams(**kw)


def _row_tile(n_rows, prefs=None):
    for t in prefs or ROW_TILES:
        if n_rows % t == 0:
            return t
    raise ValueError(f"no row tile for {n_rows}")


def _nt(a, b):
    return lax.dot_general(a, b, (((1,), (1,)), ((), ())), preferred_element_type=F32)


def _tn(a, b):
    return lax.dot_general(a, b, (((0,), (0,)), ((), ())), preferred_element_type=F32)


def _mm(a, b):
    return jnp.dot(a, b, preferred_element_type=F32)


def _sigmoid(x):
    return 0.5 * jnp.tanh(0.5 * x) + 0.5


def _resident(shape):
    nd = len(shape)
    return pl.BlockSpec(shape, lambda *_: (0,) * nd, pipeline_mode=pl.Buffered(1))


def _rows(tm, width):
    return pl.BlockSpec((tm, width), lambda i: (i, 0))


def _rope_tables(pos_col, inv_freq_row):
    lp = pos_col.shape[0]
    tm = _row_tile(lp)

    def body(p_ref, f_ref, c_ref, s1_ref, s2_ref):
        ang = p_ref[...].astype(F32) * f_ref[...]
        lane = lax.broadcasted_iota(jnp.int32, ang.shape, 1)
        cs = jnp.cos(ang)
        sn = jnp.sin(ang)
        c_ref[...] = jnp.where(lane < NOPE, 1.0, jnp.where(lane < QK_DIM, cs, 0.0))
        s1_ref[...] = jnp.where((lane >= NOPE) & (lane < NOPE + ROPE // 2), -sn, 0.0)
        s2_ref[...] = jnp.where((lane >= NOPE + ROPE // 2) & (lane < QK_DIM), sn, 0.0)

    out = jax.ShapeDtypeStruct((lp, HEAD_PAD), F32)
    return pl.pallas_call(
        body, name="rope_tables", grid=(lp // tm,),
        in_specs=[pl.BlockSpec((tm, 1), lambda i: (i, 0)), pl.BlockSpec((1, HEAD_PAD), lambda i: (0, 0))],
        out_specs=[_rows(tm, HEAD_PAD)] * 3, out_shape=[out] * 3,
        compiler_params=_cp(("parallel",)),
    )(pos_col, inv_freq_row)


def _rope(y, c, s1, s2):
    return y * c + pltpu.roll(y, HEAD_PAD - ROPE // 2, 1) * s1 + pltpu.roll(y, ROPE // 2, 1) * s2


def _rope_t(g, c, s1, s2):
    return g * c + pltpu.roll(g * s1, ROPE // 2, 1) + pltpu.roll(g * s2, HEAD_PAD - ROPE // 2, 1)


def _inproj_fwd(x, gain, w_pad, gather=()):
    lp = x.shape[0]
    tm = _row_tile(lp)
    n_out = len(IN_WIDTHS)
    ex = _ChipExchange(list(gather), scatter=False)

    def body(x_ref, g_ref, w_ref, *rest):
        ex_in, outs, ex_out, ex_sems = rest[:ex.n], rest[ex.n:ex.n + n_out], rest[ex.n + n_out:2 * ex.n + n_out], rest[2 * ex.n + n_out:]
        if ex.n:
            pl.when(pl.program_id(0) == 0)(lambda: ex.start(ex_in, ex_out, ex_sems))
        xf = x_ref[...]
        inv = lax.rsqrt(jnp.mean(xf * xf, axis=-1, keepdims=True) + EPS)
        h = (xf * inv * g_ref[...]).astype(BF16)
        for o_ref, off, wd in zip(outs, IN_OFFS, IN_WIDTHS):
            o_ref[...] = _nt(h, w_ref[off:off + wd, :]).astype(o_ref.dtype)
        if ex.n:
            pl.when(pl.program_id(0) == lp // tm - 1)(lambda: ex.wait(ex_in, ex_out, ex_sems))

    res = pl.pallas_call(
        body, name="inproj_fwd_gather" if ex.n else "inproj_fwd", grid=(lp // tm,),
        in_specs=[_rows(tm, D_MODEL), pl.BlockSpec((1, D_MODEL), lambda i: (0, 0)), _resident((IN_PAD, D_MODEL))] + ex.specs,
        out_specs=[_rows(tm, wd) for wd in IN_WIDTHS] + ex.specs,
        out_shape=[jax.ShapeDtypeStruct((lp, wd), BF16) for wd in IN_WIDTHS] + ex.out_shape,
        scratch_shapes=ex.sems if ex.n else [],
        compiler_params=_cp(("arbitrary",)),
    )(x, gain, w_pad, *gather)
    return res[:n_out], res[n_out:]


def _inv_counts(tile_idx, tm):
    row = tile_idx * tm + lax.broadcasted_iota(jnp.int32, (tm, 1), 0)
    t1 = jnp.maximum(row - PAD_FRONT + 1, 1).astype(F32)
    return [1.0 / jnp.minimum(t1, float(w)) for w in POOL_WINDOWS]


def _trailing_sums(e):
    s2 = e + pltpu.roll(e, 1, 0)
    s4 = s2 + pltpu.roll(s2, 2, 0)
    s8 = s4 + pltpu.roll(s4, 4, 0)
    s16 = s8 + pltpu.roll(s8, 8, 0)
    return (s2, s4, s8, s16)


def _leading_sums(e):
    n = e.shape[0]
    s2 = e + pltpu.roll(e, n - 1, 0)
    s4 = s2 + pltpu.roll(s2, n - 2, 0)
    s8 = s4 + pltpu.roll(s4, n - 4, 0)
    s16 = s8 + pltpu.roll(s8, n - 8, 0)
    return (s2, s4, s8, s16)


def _pool_fwd(u, zp, wg, scale):
    lp = u.shape[0]
    tm = _row_tile(lp)

    def body(u_ref, z_ref, wg_ref, sc_ref, a_ref, ext_ref):
        i = pl.program_id(0)

        @pl.when(i == 0)
        def _():
            ext_ref[0:HALO, :] = jnp.zeros((HALO, POOL_WIDTH), F32)

        ext_ref[HALO:HALO + tm, :] = u_ref[...].astype(F32)
        e = ext_ref[...]
        sums = _trailing_sums(e)
        ext_ref[0:HALO, :] = e[tm:tm + HALO, :]
        inv_cnt = _inv_counts(i, tm)
        for g in range(POOL_GROUPS):
            cols = slice(g * GROUP_DIM, (g + 1) * GROUP_DIM)
            mixed = sums[g][HALO:, cols] * inv_cnt[g] - e[HALO:, cols]
            y = _mm(mixed.astype(BF16), wg_ref[g]) * sc_ref[:, cols]
            zf = z_ref[:, cols].astype(F32)
            a_ref[:, cols] = (y * (zf * _sigmoid(zf))).astype(a_ref.dtype)

    return pl.pallas_call(
        body, name="pool_fwd", grid=(lp // tm,),
        in_specs=[_rows(tm, POOL_WIDTH), _rows(tm, POOL_WIDTH),
                  pl.BlockSpec((POOL_GROUPS, GROUP_DIM, GROUP_DIM), lambda i: (0, 0, 0)),
                  pl.BlockSpec((1, POOL_WIDTH), lambda i: (0, 0))],
        out_specs=_rows(tm, POOL_WIDTH), out_shape=jax.ShapeDtypeStruct((lp, POOL_WIDTH), BF16),
        scratch_shapes=[pltpu.VMEM((HALO + tm, POOL_WIDTH), F32)],
        compiler_params=_cp(("arbitrary",)),
    )(u, zp, wg, scale)


def _rms_fwd(xf, gain):
    inv = lax.rsqrt(jnp.mean(xf * xf, axis=-1, keepdims=True) + EPS)
    xhat = xf * inv
    return inv, xhat, xhat * gain


def _rms_bwd(dy, inv, xhat, gain):
    dgain = jnp.sum(dy * xhat, axis=0, keepdims=True)
    dyg = dy * gain
    dx = inv * (dyg - xhat * jnp.mean(dyg * xhat, axis=-1, keepdims=True))
    return dx, dgain


def _head_norm_fwd(xh, gain128):
    inv = lax.rsqrt(jnp.sum(xh * xh, axis=-1, keepdims=True) * (1.0 / QK_DIM) + EPS)
    xhat = xh * inv
    return inv, xhat, xhat * gain128


def _head_norm_bwd(dy, inv, xhat, gain128):
    dyg = dy * gain128
    return inv * (dyg - xhat * (jnp.sum(dyg * xhat, axis=-1, keepdims=True) * (1.0 / QK_DIM)))


def _mla_prep_fwd(cq, ckv, kr, tabs, gqa, gkva, gqn, gkn, wq, wkn, wv):
    lp = cq.shape[0]
    tm = _row_tile(lp)

    def body(cq_ref, ckv_ref, kr_ref, c_ref, s1_ref, s2_ref, gqa_ref, gkva_ref, gqn_ref, gkn_ref,
             wq_ref, wkn_ref, wv_ref, q_ref, k_ref, v_ref):
        c, s1, s2 = c_ref[...], s1_ref[...], s2_ref[...]
        _, _, cqn = _rms_fwd(cq_ref[...].astype(F32), gqa_ref[...])
        qraw = _mm(cqn.astype(BF16), wq_ref[...])
        for h in range(N_HEADS):
            hb = slice(h * HEAD_PAD, (h + 1) * HEAD_PAD)
            _, _, yh = _head_norm_fwd(qraw[:, hb], gqn_ref[...])
            q_ref[:, hb] = (_rope(yh, c, s1, s2) * Q_PRESCALE).astype(q_ref.dtype)
        _, _, ckvn = _rms_fwd(ckv_ref[...].astype(F32), gkva_ref[...])
        ckvn_b = ckvn.astype(BF16)
        knraw = _mm(ckvn_b, wkn_ref[...])
        krs = kr_ref[...].astype(F32)
        for h in range(N_HEADS):
            hb = slice(h * HEAD_PAD, (h + 1) * HEAD_PAD)
            _, _, yh = _head_norm_fwd(knraw[:, hb] + krs, gkn_ref[...])
            k_ref[:, hb] = _rope(yh, c, s1, s2).astype(k_ref.dtype)
        v_ref[...] = _mm(ckvn_b, wv_ref[...]).astype(v_ref.dtype)

    hw = N_HEADS * HEAD_PAD
    vec = lambda n: pl.BlockSpec((1, n), lambda i: (0, 0))
    return pl.pallas_call(
        body, name="mla_prep_fwd", grid=(lp // tm,),
        in_specs=[_rows(tm, Q_RANK), _rows(tm, KV_RANK), _rows(tm, HEAD_PAD)] + [_rows(tm, HEAD_PAD)] * 3
        + [vec(Q_RANK), vec(KV_RANK), vec(HEAD_PAD), vec(HEAD_PAD),
           _resident((Q_RANK, hw)), _resident((KV_RANK, hw)), _resident((KV_RANK, MLA_WIDTH))],
        out_specs=[_rows(tm, hw), _rows(tm, hw), _rows(tm, MLA_WIDTH)],
        out_shape=[jax.ShapeDtypeStruct((lp, hw), BF16), jax.ShapeDtypeStruct((lp, hw), BF16),
                   jax.ShapeDtypeStruct((lp, MLA_WIDTH), BF16)],
        compiler_params=_cp(("parallel",)),
    )(cq, ckv, kr, *tabs, gqa, gkva, gqn, gkn, wq, wkn, wv)


def _causal_mask(s, q0, k0):
    qi = q0 + lax.broadcasted_iota(jnp.int32, s.shape, 0)
    ki = k0 + lax.broadcasted_iota(jnp.int32, s.shape, 1)
    return jnp.where((ki <= qi) & (ki >= PAD_FRONT), s, MASK_VALUE)


def _score_chunks(kind, r, tk):
    if kind == "inner":
        return [(c0, False) for c0 in range(0, tk, ATTN_BLOCK)]
    if kind == "first":
        return [(c0, c0 == 0) for c0 in range(0, tk, ATTN_BLOCK)]
    return [(c0, True) for c0 in range(0, min(tk, (r + 1) * ROW_BLOCK), ATTN_BLOCK)]


def _tile_kinds(i, t):
    return (("diag", t == i), ("first", (t == 0) & (i > 0)), ("inner", (t > 0) & (t < i)))


def _lanes(col, width=HEAD_PAD):
    return jnp.broadcast_to(col, (col.shape[0], width))


def _flash_fwd(q, k, v, gather=()):
    lp = q.shape[0]
    tq = tk = ATTN_TILE
    nq = lp // tq
    nj = N_HEADS // (2 * FWD_PAIRS)
    heads = 2 * FWD_PAIRS
    n_blocks = tq // ROW_BLOCK
    ex = _ChipExchange(list(gather), scatter=False)

    pairs = [(i, t) for i in range(nq) for t in range(i + 1)]
    i_tab = jnp.asarray([p[0] for p in pairs], jnp.int32)
    t_tab = jnp.asarray([p[1] for p in pairs], jnp.int32)

    def body(i_tab_ref, t_tab_ref, q_ref, k_ref, v_ref, *rest):
        ex_in, (o_ref, lse_ref), ex_out = rest[:ex.n], rest[ex.n:ex.n + 2], rest[ex.n + 2:2 * ex.n + 2]
        m_scr, acc_scr, s_scr, p_scr, part_scr, vext_scr = rest[2 * ex.n + 2:2 * ex.n + 8]
        ex_sems = rest[2 * ex.n + 8:]
        j, step_no = pl.program_id(0), pl.program_id(1)
        i, t = i_tab_ref[step_no], t_tab_ref[step_no]
        if ex.n:
            pl.when((j == 0) & (step_no == 0))(lambda: ex.start(ex_in, ex_out, ex_sems))

        @pl.when(t == 0)
        def _():
            m_scr[...] = jnp.full(m_scr.shape, MASK_VALUE, F32)
            acc_scr[...] = jnp.zeros(acc_scr.shape, F32)

        def step(kind):
            def scores(hh, r, c0, masked):
                s = s_scr[hh, r * ROW_BLOCK:(r + 1) * ROW_BLOCK, c0:c0 + ATTN_BLOCK]
                return _causal_mask(s, i * tq + r * ROW_BLOCK, t * tk + c0) if masked else s

            for pp in range(FWD_PAIRS):
                vext_scr[pp, :, 0:HEAD_PAD] = v_ref[:, pp * HEAD_PAD:(pp + 1) * HEAD_PAD]
                vext_scr[pp, :, HEAD_PAD:2 * HEAD_PAD] = jnp.ones((tk, HEAD_PAD), BF16)
            half = tq // 2
            for hh in range(heads):
                hb = slice(hh * HEAD_PAD, (hh + 1) * HEAD_PAD)
                if kind == "diag":
                    s_scr[hh, 0:half, 0:half] = _nt(q_ref[0:half, hb], k_ref[0:half, hb])
                    s_scr[hh, half:tq, :] = _nt(q_ref[half:tq, hb], k_ref[:, hb])
                else:
                    s_scr[hh] = _nt(q_ref[:, hb], k_ref[:, hb])
            for hh in range(heads):
                for r in range(n_blocks):
                    part = None
                    for c0, masked in _score_chunks(kind, r, tk):
                        s = scores(hh, r, c0, masked)
                        part = s if part is None else jnp.maximum(part, s)
                    part_scr[r * ROW_BLOCK:(r + 1) * ROW_BLOCK, :] = part
                m_prev = m_scr[hh]
                m_new = jnp.maximum(m_prev, _lanes(jnp.max(part_scr[...], axis=-1, keepdims=True)))
                alpha = jnp.exp2(m_prev - m_new)
                m_scr[hh] = m_new
                for r in range(n_blocks):
                    rows = slice(r * ROW_BLOCK, (r + 1) * ROW_BLOCK)
                    m_r = m_scr[hh, rows, :]
                    chunks = _score_chunks(kind, r, tk)
                    for c0, masked in chunks:
                        p_scr[hh, rows, c0:c0 + ATTN_BLOCK] = jnp.exp2((scores(hh, r, c0, masked) - m_r).astype(BF16))
                    done = chunks[-1][0] + ATTN_BLOCK
                    if done < tk:
                        p_scr[hh, rows, done:tk] = jnp.zeros((ROW_BLOCK, tk - done), BF16)
                alpha2 = jnp.concatenate([alpha, alpha], axis=1)
                if kind == "diag":
                    acc_scr[hh, 0:half, :] = (alpha2[0:half] * acc_scr[hh, 0:half, :]
                                              + _mm(p_scr[hh, 0:half, 0:half], vext_scr[hh // 2, 0:half, :]))
                    acc_scr[hh, half:tq, :] = (alpha2[half:tq] * acc_scr[hh, half:tq, :]
                                               + _mm(p_scr[hh, half:tq, :], vext_scr[hh // 2]))
                else:
                    acc_scr[hh] = alpha2 * acc_scr[hh] + _mm(p_scr[hh], vext_scr[hh // 2])

        for kind, pred in _tile_kinds(i, t):
            pl.when(pred)(functools.partial(step, kind))

        @pl.when(t == i)
        def _():
            lane = lax.broadcasted_iota(jnp.int32, (tq, HEAD_PAD), 1)
            for pp in range(FWD_PAIRS):
                pb = slice(pp * HEAD_PAD, (pp + 1) * HEAD_PAD)
                h0, h1 = 2 * pp, 2 * pp + 1
                l0, l1 = acc_scr[h0, :, HEAD_PAD:2 * HEAD_PAD], acc_scr[h1, :, HEAD_PAD:2 * HEAD_PAD]
                o = jnp.where(lane < V_DIM, acc_scr[h0, :, 0:HEAD_PAD] / l0, acc_scr[h1, :, 0:HEAD_PAD] / l1)
                o_ref[:, pb] = o.astype(o_ref.dtype)
                lse_ref[:, pb] = jnp.where(lane < V_DIM, m_scr[h0] + jnp.log(l0) * LOG2E, m_scr[h1] + jnp.log(l1) * LOG2E)

        if ex.n:
            pl.when((j == nj - 1) & (step_no == len(pairs) - 1))(lambda: ex.wait(ex_in, ex_out, ex_sems))

    q_idx = lambda j, p, it, tt: (it[p], j)
    kv_idx = lambda j, p, it, tt: (tt[p], j)
    o, lse, *gathered = pl.pallas_call(
        body, name="flash_fwd_gather" if ex.n else "flash_fwd",
        grid_spec=pltpu.PrefetchScalarGridSpec(
            num_scalar_prefetch=2, grid=(nj, len(pairs)),
            in_specs=[pl.BlockSpec((tq, heads * HEAD_PAD), q_idx), pl.BlockSpec((tk, heads * HEAD_PAD), kv_idx),
                      pl.BlockSpec((tk, FWD_PAIRS * HEAD_PAD), kv_idx)] + ex.specs,
            out_specs=[pl.BlockSpec((tq, FWD_PAIRS * HEAD_PAD), q_idx)] * 2 + ex.specs,
            scratch_shapes=[pltpu.VMEM((heads, tq, HEAD_PAD), F32), pltpu.VMEM((heads, tq, 2 * HEAD_PAD), F32),
                            pltpu.VMEM((heads, tq, tk), F32), pltpu.VMEM((heads, tq, tk), BF16),
                            pltpu.VMEM((tq, HEAD_PAD), F32), pltpu.VMEM((FWD_PAIRS, tk, 2 * HEAD_PAD), BF16)]
            + (ex.sems if ex.n else [])),
        out_shape=[jax.ShapeDtypeStruct((lp, MLA_WIDTH), BF16), jax.ShapeDtypeStruct((lp, MLA_WIDTH), F32)] + ex.out_shape,
        compiler_params=_cp(("arbitrary",) * 2),
    )(i_tab, t_tab, q, k, v, *gather)
    return o, lse, gathered


def _merge_fwd(x, a_pool, o, zm, gp, gm, wpu, wmu, wout):
    lp = x.shape[0]
    tm = _row_tile(lp)

    def body(x_ref, ap_ref, o_ref, zm_ref, gp_ref, gm_ref, wpu_ref, wmu_ref, wout_ref, xn_ref, yp_ref, ym_ref):
        yp = _mm(ap_ref[...], wpu_ref[...])
        zf = zm_ref[...].astype(F32)
        amla = o_ref[...].astype(F32) * (zf * _sigmoid(zf))
        ym = _mm(amla.astype(BF16), wmu_ref[...])
        merged = _sigmoid(gp_ref[...].astype(F32)) * yp + _sigmoid(gm_ref[...].astype(F32)) * ym
        xn_ref[...] = x_ref[...] + _mm(merged.astype(BF16), wout_ref[...])
        yp_ref[...] = yp.astype(yp_ref.dtype)
        ym_ref[...] = ym.astype(ym_ref.dtype)

    return pl.pallas_call(
        body, name="merge_fwd", grid=(lp // tm,),
        in_specs=[_rows(tm, D_MODEL), _rows(tm, POOL_WIDTH), _rows(tm, MLA_WIDTH), _rows(tm, MLA_WIDTH),
                  _rows(tm, D_MODEL), _rows(tm, D_MODEL),
                  _resident((POOL_WIDTH, D_MODEL)), _resident((MLA_WIDTH, D_MODEL)), _resident((D_MODEL, D_MODEL))],
        out_specs=[_rows(tm, D_MODEL)] * 3,
        out_shape=[jax.ShapeDtypeStruct((lp, D_MODEL), F32), jax.ShapeDtypeStruct((lp, D_MODEL), BF16),
                   jax.ShapeDtypeStruct((lp, D_MODEL), BF16)],
        compiler_params=_cp(("parallel",)),
    )(x, a_pool, o, zm, gp, gm, wpu, wmu, wout)


def _loss_head(y, target):
    lp = y.shape[0]
    tm = ROW0
    n_real = target.shape[0] // tm

    def body(y_ref, t_ref, d_ref, l_ref):
        i = pl.program_id(0)

        @pl.when(i == 0)
        def _():
            l_ref[...] = jnp.zeros(l_ref.shape, F32)

        real = (i >= 1) & (i <= n_real)
        err = jnp.where(real, y_ref[...] - t_ref[...], 0.0)
        d_ref[...] = err * (1.0 / D_MODEL)
        l_ref[...] += jnp.sum(err * err) * (0.5 / D_MODEL)

    return pl.pallas_call(
        body, name="loss_head", grid=(lp // tm,),
        in_specs=[_rows(tm, D_MODEL), pl.BlockSpec((tm, D_MODEL), lambda i: (jnp.clip(i - 1, 0, n_real - 1), 0))],
        out_specs=[_rows(tm, D_MODEL), pl.BlockSpec((8, 128), lambda i: (0, 0))],
        out_shape=[jax.ShapeDtypeStruct((lp, D_MODEL), F32), jax.ShapeDtypeStruct((8, 128), F32)],
        compiler_params=_cp(("arbitrary",)),
    )(y, target)


def _pair_rowsum(prod):
    lane = lax.broadcasted_iota(jnp.int32, prod.shape, 1)
    lo = jnp.sum(jnp.where(lane < V_DIM, prod, 0.0), axis=-1, keepdims=True)
    hi = jnp.sum(jnp.where(lane < V_DIM, 0.0, prod), axis=-1, keepdims=True)
    return jnp.where(lane < V_DIM, lo, hi)


def _merge_bwd(dres, yp, ym, gp, gm, o, zm, wout, wpu, wmu):
    lp = dres.shape[0]
    tm = _row_tile(lp, ROW_TILES_HEAVY)

    def body(dres_ref, yp_ref, ym_ref, gp_ref, gm_ref, o_ref, zm_ref, wout_ref, wpu_ref, wmu_ref,
             merged_ref, dyp_ref, dym_ref, dgp_ref, dgm_ref, dap_ref, amla_ref, do_ref, dzm_ref, delta_ref):
        dmerged = _nt(dres_ref[...].astype(BF16), wout_ref[...])
        sp = _sigmoid(gp_ref[...].astype(F32))
        sm = _sigmoid(gm_ref[...].astype(F32))
        ypf = yp_ref[...].astype(F32)
        ymf = ym_ref[...].astype(F32)
        merged_ref[...] = (sp * ypf + sm * ymf).astype(merged_ref.dtype)
        dyp = (dmerged * sp).astype(BF16)
        dym = (dmerged * sm).astype(BF16)
        dyp_ref[...] = dyp
        dym_ref[...] = dym
        dgp_ref[...] = (dmerged * ypf * sp * (1.0 - sp)).astype(dgp_ref.dtype)
        dgm_ref[...] = (dmerged * ymf * sm * (1.0 - sm)).astype(dgm_ref.dtype)
        dap_ref[...] = _nt(dyp, wpu_ref[...]).astype(dap_ref.dtype)
        dam = _nt(dym, wmu_ref[...])
        zf = zm_ref[...].astype(F32)
        sg = _sigmoid(zf)
        si = zf * sg
        of = o_ref[...].astype(F32)
        amla_ref[...] = (of * si).astype(amla_ref.dtype)
        do = dam * si
        do_ref[...] = do.astype(do_ref.dtype)
        dzm_ref[...] = (dam * of * (sg * (1.0 + zf * (1.0 - sg)))).astype(dzm_ref.dtype)
        prod = do * of
        for j in range(N_HEADS // 2):
            hb = slice(j * HEAD_PAD, (j + 1) * HEAD_PAD)
            delta_ref[:, hb] = _pair_rowsum(prod[:, hb])

    bf = lambda w: jax.ShapeDtypeStruct((lp, w), BF16)
    return pl.pallas_call(
        body, name="merge_bwd", grid=(lp // tm,),
        in_specs=[_rows(tm, D_MODEL)] * 5 + [_rows(tm, MLA_WIDTH)] * 2
        + [_resident((D_MODEL, D_MODEL)), _resident((POOL_WIDTH, D_MODEL)), _resident((MLA_WIDTH, D_MODEL))],
        out_specs=[_rows(tm, D_MODEL)] * 5 + [_rows(tm, POOL_WIDTH)] + [_rows(tm, MLA_WIDTH)] * 4,
        out_shape=[bf(D_MODEL)] * 5 + [bf(POOL_WIDTH)] + [bf(MLA_WIDTH)] * 3 + [jax.ShapeDtypeStruct((lp, MLA_WIDTH), F32)],
        compiler_params=_cp(("parallel",)),
    )(dres, yp, ym, gp, gm, o, zm, wout, wpu, wmu)


def _flash_bwd(q, k, v, do, lse, delta, scatter=()):
    lp = q.shape[0]
    tq = tk = ATTN_TILE
    nq = lp // tq
    heads = 2 * BWD_PAIRS
    nj = N_HEADS // heads
    qk_w, v_w = heads * HEAD_PAD, BWD_PAIRS * HEAD_PAD
    scale = 1.0 / math.sqrt(QK_DIM)
    ex = _ChipExchange(list(scatter), scatter=True)

    pairs = [(t, i) for t in range(nq) for i in range(t, nq)]
    t_tab = jnp.asarray([p[0] for p in pairs], jnp.int32)
    i_tab = jnp.asarray([p[1] for p in pairs], jnp.int32)

    def body(t_tab_ref, i_tab_ref, q_ref, k_ref, v_ref, do_ref, lse_ref, dl_ref, *rest):
        ex_in, (dq_hbm, dk_ref, dv_ref), ex_out = rest[:ex.n], rest[ex.n:ex.n + 3], rest[ex.n + 3:2 * ex.n + 3]
        (dq_acc, dk_acc, dv_acc, s_scr, dp_scr, p_scr, ds_scr, doh_scr, stat_scr, stage_scr,
         stage_sem) = rest[2 * ex.n + 3:2 * ex.n + 14]
        ex_sems = rest[2 * ex.n + 14:]
        j, step_no = pl.program_id(0), pl.program_id(1)
        t, i = t_tab_ref[step_no], i_tab_ref[step_no]
        if ex.n:
            pl.when((j == 0) & (step_no == 0))(lambda: ex.start(ex_in, ex_out, ex_sems))

        @pl.when(step_no == 0)
        def _():
            dq_acc[...] = jnp.zeros(dq_acc.shape, F32)

        @pl.when(i == t)
        def _():
            dk_acc[...] = jnp.zeros(dk_acc.shape, F32)
            dv_acc[...] = jnp.zeros(dv_acc.shape, F32)

        def step(kind):
            lane = lax.broadcasted_iota(jnp.int32, (tq, HEAD_PAD), 1)
            half = tq // 2
            for pp in range(BWD_PAIRS):
                pb = slice(pp * HEAD_PAD, (pp + 1) * HEAD_PAD)
                for hh in range(2):
                    hb = slice((2 * pp + hh) * HEAD_PAD, (2 * pp + hh + 1) * HEAD_PAD)
                    mine = (lane < V_DIM) if hh == 0 else (lane >= V_DIM)
                    doh_scr[hh] = jnp.where(mine, do_ref[:, pb], jnp.zeros((tq, HEAD_PAD), BF16))
                    if kind == "diag":
                        s_scr[hh, 0:half, 0:half] = _nt(q_ref[0:half, hb], k_ref[0:half, hb])
                        s_scr[hh, half:tq, :] = _nt(q_ref[half:tq, hb], k_ref[:, hb])
                        dp_scr[hh, 0:half, 0:half] = _nt(doh_scr[hh, 0:half, :], v_ref[0:half, pb])
                        dp_scr[hh, half:tq, :] = _nt(doh_scr[hh, half:tq, :], v_ref[:, pb])
                    else:
                        s_scr[hh] = _nt(q_ref[:, hb], k_ref[:, hb])
                        dp_scr[hh] = _nt(doh_scr[hh], v_ref[:, pb])
                for hh in range(2):
                    head = 2 * pp + hh
                    hb = slice(head * HEAD_PAD, (head + 1) * HEAD_PAD)
                    col = slice(pp * HEAD_PAD + hh * V_DIM, pp * HEAD_PAD + hh * V_DIM + 1)
                    stat_scr[0] = _lanes(lse_ref[:, col])
                    stat_scr[1] = _lanes(dl_ref[:, col])
                    for r in range(tq // ROW_BLOCK):
                        rows = slice(r * ROW_BLOCK, (r + 1) * ROW_BLOCK)
                        lse_r = stat_scr[0, rows, :]
                        dl_r = stat_scr[1, rows, :]
                        chunks = _score_chunks(kind, r, tk)
                        for c0, masked in chunks:
                            cols = slice(c0, c0 + ATTN_BLOCK)
                            s = s_scr[hh, rows, cols]
                            if masked:
                                s = _causal_mask(s, i * tq + r * ROW_BLOCK, t * tk + c0)
                            p = jnp.exp2(s - lse_r)
                            p_scr[hh, rows, cols] = p.astype(BF16)
                            ds_scr[hh, rows, cols] = (p * (dp_scr[hh, rows, cols] - dl_r)).astype(BF16)
                        done = chunks[-1][0] + ATTN_BLOCK
                        if done < tk:
                            zeros = jnp.zeros((ROW_BLOCK, tk - done), BF16)
                            p_scr[hh, rows, done:tk] = zeros
                            ds_scr[hh, rows, done:tk] = zeros
                    tnt = lambda a, b: lax.dot_general(a, b, (((0,), (1,)), ((), ())), preferred_element_type=F32)
                    if kind == "diag":
                        dv_acc[pp, :, 0:half] += _tn(doh_scr[hh], p_scr[hh, :, 0:half])
                        dv_acc[pp, :, half:tk] += _tn(doh_scr[hh, half:tq, :], p_scr[hh, half:tq, half:tk])
                        dk_acc[head, :, 0:half] += _tn(q_ref[:, hb], ds_scr[hh, :, 0:half])
                        dk_acc[head, :, half:tk] += _tn(q_ref[half:tq, hb], ds_scr[hh, half:tq, half:tk])
                        dq_acc[i, hb, 0:half] += tnt(k_ref[0:half, hb], ds_scr[hh, 0:half, 0:half])
                        dq_acc[i, hb, half:tq] += tnt(k_ref[:, hb], ds_scr[hh, half:tq, :])
                    else:
                        dv_acc[pp] += _tn(doh_scr[hh], p_scr[hh])
                        dk_acc[head] += _tn(q_ref[:, hb], ds_scr[hh])
                        dq_acc[i, hb, :] += tnt(k_ref[:, hb], ds_scr[hh])

        for kind, pred in _tile_kinds(i, t):
            pl.when(pred)(functools.partial(step, kind))

        @pl.when(i == nq - 1)
        def _():
            for head in range(heads):
                hb = slice(head * HEAD_PAD, (head + 1) * HEAD_PAD)
                dk_ref[:, hb] = (dk_acc[head].T * LN2).astype(dk_ref.dtype)
            for pp in range(BWD_PAIRS):
                dv_ref[:, pp * HEAD_PAD:(pp + 1) * HEAD_PAD] = dv_acc[pp].T.astype(dv_ref.dtype)

        @pl.when(step_no == len(pairs) - 1)
        def _():
            my_cols = pl.ds(pl.multiple_of(j * qk_w, qk_w), qk_w)
            for qi in range(nq):
                for head in range(heads):
                    hb = slice(head * HEAD_PAD, (head + 1) * HEAD_PAD)
                    stage_scr[:, hb] = (dq_acc[qi, hb, :].T * scale).astype(BF16)
                out = pltpu.make_async_copy(stage_scr, dq_hbm.at[pl.ds(qi * tq, tq), my_cols], stage_sem)
                out.start()
                out.wait()

        if ex.n:
            pl.when((j == nj - 1) & (step_no == len(pairs) - 1))(lambda: ex.wait(ex_in, ex_out, ex_sems))

    q_idx = lambda j, p, tt, it: (it[p], j)
    kv_idx = lambda j, p, tt, it: (tt[p], j)
    hw = N_HEADS * HEAD_PAD
    dq, dk, dv, *pieces = pl.pallas_call(
        body, name="flash_bwd_scatter" if ex.n else "flash_bwd",
        grid_spec=pltpu.PrefetchScalarGridSpec(
            num_scalar_prefetch=2, grid=(nj, len(pairs)),
            in_specs=[pl.BlockSpec((tq, qk_w), q_idx), pl.BlockSpec((tk, qk_w), kv_idx),
                      pl.BlockSpec((tk, v_w), kv_idx), pl.BlockSpec((tq, v_w), q_idx),
                      pl.BlockSpec((tq, v_w), q_idx), pl.BlockSpec((tq, v_w), q_idx)] + ex.specs,
            out_specs=[HBM_SPEC, pl.BlockSpec((tk, qk_w), kv_idx), pl.BlockSpec((tk, v_w), kv_idx)] + ex.specs,
            scratch_shapes=[pltpu.VMEM((nq, qk_w, tq), F32), pltpu.VMEM((heads, HEAD_PAD, tk), F32),
                            pltpu.VMEM((BWD_PAIRS, HEAD_PAD, tk), F32),
                            pltpu.VMEM((2, tq, tk), F32), pltpu.VMEM((2, tq, tk), F32),
                            pltpu.VMEM((2, tq, tk), BF16), pltpu.VMEM((2, tq, tk), BF16),
                            pltpu.VMEM((2, tq, HEAD_PAD), BF16), pltpu.VMEM((2, tq, HEAD_PAD), F32),
                            pltpu.VMEM((tq, qk_w), BF16), pltpu.SemaphoreType.DMA(())]
            + (ex.sems if ex.n else [])),
        out_shape=[jax.ShapeDtypeStruct((lp, hw), BF16), jax.ShapeDtypeStruct((lp, hw), BF16),
                   jax.ShapeDtypeStruct((lp, MLA_WIDTH), BF16)] + ex.out_shape,
        compiler_params=_cp(("arbitrary",) * 2),
    )(t_tab, i_tab, q, k, v, do, lse, delta, *scatter)
    return dq, dk, dv, pieces


def _mla_prep_bwd(dq, dk, dv, cq, ckv, kr, tabs, gqa, gkva, gqn, gkn, wq, wkn, wv):
    lp = cq.shape[0]
    tm = _row_tile(lp, ROW_TILES_HEAVY)
    hw = N_HEADS * HEAD_PAD

    def body(dq_ref, dk_ref, dv_ref, cq_ref, ckv_ref, kr_ref, c_ref, s1_ref, s2_ref, gqa_ref, gkva_ref, gqn_ref,
             gkn_ref, wq_ref, wkn_ref, wv_ref, dcq_ref, dckv_ref, dkr_ref, dwq_ref, dwkn_ref, dwv_ref,
             dgqa_ref, dgkva_ref, dgqn_ref, dgkn_ref, draw_scr):
        @pl.when(pl.program_id(0) == 0)
        def _():
            for r in (dwq_ref, dwkn_ref, dwv_ref, dgqa_ref, dgkva_ref, dgqn_ref, dgkn_ref):
                r[...] = jnp.zeros(r.shape, F32)

        c, s1, s2 = c_ref[...], s1_ref[...], s2_ref[...]
        lane = lax.broadcasted_iota(jnp.int32, (tm, HEAD_PAD), 1)

        inv_q, xhat_q, cqn = _rms_fwd(cq_ref[...].astype(F32), gqa_ref[...])
        cqn_b = cqn.astype(BF16)
        qraw = _mm(cqn_b, wq_ref[...])
        dgqn = jnp.zeros((1, HEAD_PAD), F32)
        for h in range(N_HEADS):
            hb = slice(h * HEAD_PAD, (h + 1) * HEAD_PAD)
            inv, xhat, _ = _head_norm_fwd(qraw[:, hb], gqn_ref[...])
            dy = _rope_t(dq_ref[:, hb].astype(F32), c, s1, s2)
            dgqn += jnp.sum(dy * xhat, axis=0, keepdims=True)
            draw_scr[:, hb] = _head_norm_bwd(dy, inv, xhat, gqn_ref[...]).astype(BF16)
        dgqn_ref[...] += dgqn
        dqraw = draw_scr[...]
        dwq_ref[...] += _tn(cqn_b, dqraw)
        dcq, dgqa = _rms_bwd(_nt(dqraw, wq_ref[...]), inv_q, xhat_q, gqa_ref[...])
        dcq_ref[...] = dcq.astype(dcq_ref.dtype)
        dgqa_ref[...] += dgqa

        inv_kv, xhat_kv, ckvn = _rms_fwd(ckv_ref[...].astype(F32), gkva_ref[...])
        ckvn_b = ckvn.astype(BF16)
        knraw = _mm(ckvn_b, wkn_ref[...])
        krs = kr_ref[...].astype(F32)
        dgkn = jnp.zeros((1, HEAD_PAD), F32)
        dkr = jnp.zeros((tm, HEAD_PAD), F32)
        for h in range(N_HEADS):
            hb = slice(h * HEAD_PAD, (h + 1) * HEAD_PAD)
            inv, xhat, _ = _head_norm_fwd(knraw[:, hb] + krs, gkn_ref[...])
            dy = _rope_t(dk_ref[:, hb].astype(F32), c, s1, s2)
            dgkn += jnp.sum(dy * xhat, axis=0, keepdims=True)
            dxh = _head_norm_bwd(dy, inv, xhat, gkn_ref[...])
            dkr += dxh
            draw_scr[:, hb] = jnp.where(lane < NOPE, dxh, 0.0).astype(BF16)
        dgkn_ref[...] += dgkn
        dkr_ref[...] = jnp.where((lane >= KR_LANE0) & (lane < QK_DIM), dkr, 0.0).astype(dkr_ref.dtype)
        dknraw = draw_scr[...]
        dvb = dv_ref[...]
        dwkn_ref[...] += _tn(ckvn_b, dknraw)
        dwv_ref[...] += _tn(ckvn_b, dvb)
        dckvn = _nt(dknraw, wkn_ref[...]) + _nt(dvb, wv_ref[...])
        dckv, dgkva = _rms_bwd(dckvn, inv_kv, xhat_kv, gkva_ref[...])
        dckv_ref[...] = dckv.astype(dckv_ref.dtype)
        dgkva_ref[...] += dgkva

    vec = lambda n: pl.BlockSpec((1, n), lambda i: (0, 0))
    whole = lambda r, c: pl.BlockSpec((r, c), lambda i: (0, 0))
    f = lambda r, c: jax.ShapeDtypeStruct((r, c), F32)
    return pl.pallas_call(
        body, name="mla_prep_bwd", grid=(lp // tm,),
        in_specs=[_rows(tm, hw), _rows(tm, hw), _rows(tm, MLA_WIDTH), _rows(tm, Q_RANK), _rows(tm, KV_RANK),
                  _rows(tm, HEAD_PAD)] + [_rows(tm, HEAD_PAD)] * 3
        + [vec(Q_RANK), vec(KV_RANK), vec(HEAD_PAD), vec(HEAD_PAD),
           _resident((Q_RANK, hw)), _resident((KV_RANK, hw)), _resident((KV_RANK, MLA_WIDTH))],
        out_specs=[_rows(tm, Q_RANK), _rows(tm, KV_RANK), _rows(tm, HEAD_PAD),
                   whole(Q_RANK, hw), whole(KV_RANK, hw), whole(KV_RANK, MLA_WIDTH),
                   vec(Q_RANK), vec(KV_RANK), vec(HEAD_PAD), vec(HEAD_PAD)],
        out_shape=[jax.ShapeDtypeStruct((lp, Q_RANK), BF16), jax.ShapeDtypeStruct((lp, KV_RANK), BF16),
                   jax.ShapeDtypeStruct((lp, HEAD_PAD), BF16),
                   f(Q_RANK, hw), f(KV_RANK, hw), f(KV_RANK, MLA_WIDTH),
                   f(1, Q_RANK), f(1, KV_RANK), f(1, HEAD_PAD), f(1, HEAD_PAD)],
        scratch_shapes=[pltpu.VMEM((tm, hw), BF16)],
        compiler_params=_cp(("arbitrary",)),
    )(dq, dk, dv, cq, ckv, kr, *tabs, gqa, gkva, gqn, gkn, wq, wkn, wv)


def _pool_bwd(dap, u, zp, wg, scale):
    lp = u.shape[0]
    tm = _row_tile(lp)
    n = lp // tm
    per = tm // HALO

    def body(dap_ref, u_ref, uh_ref, z_ref, wg_ref, sc_ref, du_ref, dz_ref, dwg_ref, dsc_ref, ext_u, ext_d):
        i = pl.program_id(0)
        r = n - 1 - i

        @pl.when(i == 0)
        def _():
            dwg_ref[...] = jnp.zeros(dwg_ref.shape, F32)
            dsc_ref[...] = jnp.zeros(dsc_ref.shape, F32)
            ext_d[tm:tm + HALO, :] = jnp.zeros((HALO, POOL_WIDTH), F32)

        ext_u[0:HALO, :] = jnp.where(r == 0, 0.0, uh_ref[...].astype(F32))
        ext_u[HALO:HALO + tm, :] = u_ref[...].astype(F32)
        e = ext_u[...]
        sums = _trailing_sums(e)
        inv_cnt = _inv_counts(r, tm)
        dmixed = []
        for g in range(POOL_GROUPS):
            cols = slice(g * GROUP_DIM, (g + 1) * GROUP_DIM)
            mixed_b = (sums[g][HALO:, cols] * inv_cnt[g] - e[HALO:, cols]).astype(BF16)
            yg = _mm(mixed_b, wg_ref[g])
            zf = z_ref[:, cols].astype(F32)
            sg = _sigmoid(zf)
            da = dap_ref[:, cols].astype(F32)
            dy = da * (zf * sg)
            dz_ref[:, cols] = (da * (yg * sc_ref[:, cols]) * (sg * (1.0 + zf * (1.0 - sg)))).astype(dz_ref.dtype)
            dsc_ref[:, cols] += jnp.sum(dy * yg, axis=0, keepdims=True)
            dyg = (dy * sc_ref[:, cols]).astype(BF16)
            dwg_ref[g] += _tn(mixed_b, dyg)
            dm = _nt(dyg, wg_ref[g])
            dmixed.append(dm)
            ext_d[0:tm, cols] = dm * inv_cnt[g]
        ed = ext_d[...]
        lead = _leading_sums(ed)
        ext_d[tm:tm + HALO, :] = ed[0:HALO, :]
        for g in range(POOL_GROUPS):
            cols = slice(g * GROUP_DIM, (g + 1) * GROUP_DIM)
            du_ref[:, cols] = (lead[g][0:tm, cols] - dmixed[g]).astype(du_ref.dtype)

    rev = lambda i: (n - 1 - i, 0)
    return pl.pallas_call(
        body, name="pool_bwd", grid=(n,),
        in_specs=[pl.BlockSpec((tm, POOL_WIDTH), rev), pl.BlockSpec((tm, POOL_WIDTH), rev),
                  pl.BlockSpec((HALO, POOL_WIDTH), lambda i: (jnp.maximum((n - 1 - i) * per - 1, 0), 0)),
                  pl.BlockSpec((tm, POOL_WIDTH), rev),
                  pl.BlockSpec((POOL_GROUPS, GROUP_DIM, GROUP_DIM), lambda i: (0, 0, 0)),
                  pl.BlockSpec((1, POOL_WIDTH), lambda i: (0, 0))],
        out_specs=[pl.BlockSpec((tm, POOL_WIDTH), rev), pl.BlockSpec((tm, POOL_WIDTH), rev),
                   pl.BlockSpec((POOL_GROUPS, GROUP_DIM, GROUP_DIM), lambda i: (0, 0, 0)),
                   pl.BlockSpec((1, POOL_WIDTH), lambda i: (0, 0))],
        out_shape=[jax.ShapeDtypeStruct((lp, POOL_WIDTH), BF16), jax.ShapeDtypeStruct((lp, POOL_WIDTH), BF16),
                   jax.ShapeDtypeStruct((POOL_GROUPS, GROUP_DIM, GROUP_DIM), F32),
                   jax.ShapeDtypeStruct((1, POOL_WIDTH), F32)],
        scratch_shapes=[pltpu.VMEM((HALO + tm, POOL_WIDTH), F32), pltpu.VMEM((tm + HALO, POOL_WIDTH), F32)],
        compiler_params=_cp(("arbitrary",)),
    )(dap, u, u, zp, wg, scale)


def _inproj_bwd(dres, x, gain, w_pad, dparts, scatter=()):
    lp = x.shape[0]
    tm = _row_tile(lp, ROW_TILES_HEAVY)
    n_dp = len(IN_WIDTHS)
    ex = _ChipExchange(list(scatter), scatter=True)

    def body(dres_ref, x_ref, g_ref, w_ref, *rest):
        dps, ex_in = rest[:n_dp], rest[n_dp:n_dp + ex.n]
        dprev_ref, h_ref, dg_ref = rest[n_dp + ex.n:n_dp + ex.n + 3]
        ex_out, ex_sems = rest[n_dp + ex.n + 3:n_dp + 2 * ex.n + 3], rest[n_dp + 2 * ex.n + 3:]
        if ex.n:
            pl.when(pl.program_id(0) == 0)(lambda: ex.start(ex_in, ex_out, ex_sems))

        @pl.when(pl.program_id(0) == 0)
        def _():
            dg_ref[...] = jnp.zeros(dg_ref.shape, F32)

        dh = jnp.zeros((tm, D_MODEL), F32)
        for dp_ref, off, wd in zip(dps, IN_OFFS, IN_WIDTHS):
            dh += _mm(dp_ref[...], w_ref[off:off + wd, :])
        inv, xhat, hn = _rms_fwd(x_ref[...], g_ref[...])
        h_ref[...] = hn.astype(h_ref.dtype)
        dx, dgain = _rms_bwd(dh, inv, xhat, g_ref[...])
        dg_ref[...] += dgain
        dprev_ref[...] = dres_ref[...] + dx
        if ex.n:
            pl.when(pl.program_id(0) == lp // tm - 1)(lambda: ex.wait(ex_in, ex_out, ex_sems))

    dprev, h, dgain, *pieces = pl.pallas_call(
        body, name="inproj_bwd_scatter" if ex.n else "inproj_bwd", grid=(lp // tm,),
        in_specs=[_rows(tm, D_MODEL), _rows(tm, D_MODEL), pl.BlockSpec((1, D_MODEL), lambda i: (0, 0)),
                  _resident((IN_PAD, D_MODEL))] + [_rows(tm, wd) for wd in IN_WIDTHS] + ex.specs,
        out_specs=[_rows(tm, D_MODEL), _rows(tm, D_MODEL), pl.BlockSpec((1, D_MODEL), lambda i: (0, 0))] + ex.specs,
        out_shape=[jax.ShapeDtypeStruct((lp, D_MODEL), F32), jax.ShapeDtypeStruct((lp, D_MODEL), BF16),
                   jax.ShapeDtypeStruct((1, D_MODEL), F32)] + ex.out_shape,
        scratch_shapes=ex.sems if ex.n else [],
        compiler_params=_cp(("arbitrary",)),
    )(dres, x, gain, w_pad, *dparts, *scatter)
    return dprev, h, dgain, pieces


def _weight_grads(a, bs, name, transposed=False):
    lp, m = a.shape
    tk = _row_tile(lp)
    nb = len(bs)
    shapes = [(b.shape[1], m) if transposed else (m, b.shape[1]) for b in bs]

    def body(a_ref, *rest):
        b_refs, o_refs = rest[:nb], rest[nb:]

        @pl.when(pl.program_id(0) == 0)
        def _():
            for o_ref in o_refs:
                o_ref[...] = jnp.zeros(o_ref.shape, F32)

        ab = a_ref[...].astype(BF16)
        for b_ref, o_ref in zip(b_refs, o_refs):
            bb = b_ref[...].astype(BF16)
            o_ref[...] += _tn(bb, ab) if transposed else _tn(ab, bb)

    return pl.pallas_call(
        body, name=name, grid=(lp // tk,),
        in_specs=[_rows(tk, m)] + [_rows(tk, b.shape[1]) for b in bs],
        out_specs=[pl.BlockSpec(s, lambda i: (0, 0)) for s in shapes],
        out_shape=[jax.ShapeDtypeStruct(s, F32) for s in shapes],
        compiler_params=_cp(("arbitrary",)),
    )(a, *bs)


HBM_SPEC = pl.BlockSpec(memory_space=pltpu.HBM)


def _my_place():
    return lax.axis_index("x"), lax.axis_index("y"), lax.axis_index("c")


def _other_chips(x, y):
    return [(1 - x, y), (x, 1 - y), (1 - x, 1 - y)]


class _ChipExchange:
    def __init__(self, arrs, scatter):
        self.n = len(arrs)
        self.scatter = scatter
        self.out_shape = [jax.ShapeDtypeStruct(a.shape if scatter else (N_CHIPS,) + a.shape, a.dtype) for a in arrs]
        self.specs = [HBM_SPEC] * self.n
        self.sems = [pltpu.SemaphoreType.DMA((3 * self.n,)), pltpu.SemaphoreType.DMA((3 * self.n,)),
                     pltpu.SemaphoreType.DMA((self.n,))]

    def _copies(self, ins, outs, sems):
        send_sems, recv_sems, local_sems = sems
        x, y, c = _my_place()
        me = 2 * x + y
        chips = _other_chips(x, y)
        mine = lambda a: ins[a].at[me] if self.scatter else ins[a]

        def remote(a, k, arriving):
            px, py = chips[k]
            there = 2 * px + py
            return pltpu.make_async_remote_copy(
                src_ref=mine(a) if arriving or not self.scatter else ins[a].at[there],
                dst_ref=outs[a].at[there if arriving else me],
                send_sem=send_sems.at[a * 3 + k], recv_sem=recv_sems.at[a * 3 + k],
                device_id=(px, py, c), device_id_type=MESH)

        pairs = [(a, k) for a in range(self.n) for k in range(3)]
        local = [pltpu.make_async_copy(mine(a), outs[a].at[me], local_sems.at[a]) for a in range(self.n)]
        return local, [remote(a, k, False) for a, k in pairs], [remote(a, k, True) for a, k in pairs]

    def start(self, ins, outs, sems):
        local, sends, _ = self._copies(ins, outs, sems)
        for cp in local + sends:
            cp.start()

    def wait(self, ins, outs, sems):
        local, sends, arrivals = self._copies(ins, outs, sems)
        for cp in arrivals:
            cp.wait_recv()
        for cp in sends:
            cp.wait_send()
        for cp in local:
            cp.wait()


def _chip_exchange(arrs, scatter, name):
    ex = _ChipExchange(arrs, scatter)

    def body(*refs):
        ins, outs, sems = refs[:ex.n], refs[ex.n:2 * ex.n], refs[2 * ex.n:]
        ex.start(ins, outs, sems)
        ex.wait(ins, outs, sems)

    return pl.pallas_call(body, name=name, in_specs=ex.specs, out_specs=ex.specs, out_shape=ex.out_shape,
                          scratch_shapes=ex.sems)(*arrs)


def _sibling_exchange(arrs, name):
    n = len(arrs)

    def body(*refs):
        ins, outs = refs[:n], refs[n:2 * n]
        send_sems, recv_sems = refs[2 * n:]
        x, y, c = _my_place()
        cps = [pltpu.make_async_remote_copy(src_ref=ins[a], dst_ref=outs[a], send_sem=send_sems.at[a],
                                            recv_sem=recv_sems.at[a], device_id=(x, y, 1 - c), device_id_type=MESH)
               for a in range(n)]
        for cp in cps:
            cp.start()
        for cp in cps:
            cp.wait_recv()
        for cp in cps:
            cp.wait_send()

    return pl.pallas_call(
        body, name=name, in_specs=[HBM_SPEC] * n, out_specs=[HBM_SPEC] * n,
        out_shape=[jax.ShapeDtypeStruct(a.shape, a.dtype) for a in arrs],
        scratch_shapes=[pltpu.SemaphoreType.DMA((n,)), pltpu.SemaphoreType.DMA((n,))],
    )(*arrs)


def _all_reduce_small(pack):
    rows = pack.shape[0]

    def body(p_ref, o_ref, g_scr, send_sems, recv_sems):
        x, y, c = _my_place()
        me = 4 * x + 2 * y + c
        flips = [(dx, dy, dc) for dx in (0, 1) for dy in (0, 1) for dc in (0, 1) if (dx, dy, dc) != (0, 0, 0)]

        def peer(f):
            return (x if f[0] == 0 else 1 - x, y if f[1] == 0 else 1 - y, c if f[2] == 0 else 1 - c)

        def copy(k, slot):
            return pltpu.make_async_remote_copy(src_ref=p_ref, dst_ref=g_scr.at[slot], send_sem=send_sems.at[k],
                                                recv_sem=recv_sems.at[k], device_id=peer(flips[k]), device_id_type=MESH)

        sends = [copy(k, me) for k in range(len(flips))]
        for cp in sends:
            cp.start()
        g_scr[me] = p_ref[...]
        for k, f in enumerate(flips):
            px, py, pc = peer(f)
            copy(k, 4 * px + 2 * py + pc).wait_recv()
        for cp in sends:
            cp.wait_send()
        acc = g_scr[0]
        for d in range(1, N_DEV):
            acc = acc + g_scr[d]
        o_ref[...] = acc

    vm = pl.BlockSpec(memory_space=pltpu.VMEM)
    return pl.pallas_call(
        body, name="all_reduce_small", in_specs=[vm], out_specs=vm,
        out_shape=jax.ShapeDtypeStruct(pack.shape, F32),
        scratch_shapes=[pltpu.VMEM((N_DEV, rows, 128), F32), pltpu.SemaphoreType.DMA((N_DEV - 1,)),
                        pltpu.SemaphoreType.DMA((N_DEV - 1,))],
        compiler_params=_cp(),
    )(pack)


def _as3d(a):
    return a.reshape((-1,) + a.shape[-2:])


def _row_block(r, sublanes=8, cap=512):
    fits = [t for t in range(sublanes, min(r, cap) + 1, sublanes) if r % t == 0]
    return fits[-1] if fits else r


def _sum_pieces(pieces, name):
    _, na, r, c = pieces.shape
    rt = _row_block(r, sublanes=16)

    def body(p_ref, o_ref):
        acc = p_ref[0, 0].astype(F32)
        for s in range(1, N_CHIPS):
            acc = acc + p_ref[s, 0].astype(F32)
        o_ref[0] = acc

    return pl.pallas_call(
        body, name=name, grid=(na, r // rt),
        in_specs=[pl.BlockSpec((N_CHIPS, 1, rt, c), lambda a, i: (0, a, i, 0))],
        out_specs=pl.BlockSpec((1, rt, c), lambda a, i: (a, i, 0)),
        out_shape=jax.ShapeDtypeStruct((na, r, c), F32),
        compiler_params=_cp(("parallel", "parallel")),
    )(pieces)


def _adamw(w, g_parts, m, v, name):
    na, r, c = w.shape
    rt = _row_block(r)
    ng = len(g_parts)

    def body(w_ref, *rest):
        g_refs = rest[:ng]
        m_ref, v_ref, g_out, d_out, m_out, v_out = rest[ng:]
        g = g_refs[0][...]
        for gr in g_refs[1:]:
            g = g + gr[...]
        m_new = ADAM_B1 * m_ref[...] + (1.0 - ADAM_B1) * g
        v_new = ADAM_B2 * v_ref[...] + (1.0 - ADAM_B2) * (g * g)
        m_hat = m_new / (1.0 - ADAM_B1 ** ADAM_STEP)
        v_hat = v_new / (1.0 - ADAM_B2 ** ADAM_STEP)
        g_out[...] = g
        d_out[...] = -ADAM_LR * (m_hat / (jnp.sqrt(v_hat) + ADAM_EPS) + ADAM_WD * w_ref[...])
        m_out[...] = m_new
        v_out[...] = v_new

    spec = pl.BlockSpec((1, rt, c), lambda a, i: (a, i, 0))
    out = jax.ShapeDtypeStruct((na, r, c), F32)
    return pl.pallas_call(
        body, name=name, grid=(na, r // rt), in_specs=[spec] * (3 + ng), out_specs=[spec] * 4, out_shape=[out] * 4,
        compiler_params=_cp(("parallel", "parallel")),
    )(w, *g_parts, m, v)


def _cols_from_shards(g):
    g = jnp.moveaxis(g, 0, -2)
    return g.reshape(g.shape[:-2] + (g.shape[-2] * g.shape[-1],))


def _rows_from_shards(g):
    g = jnp.moveaxis(g, 0, -3)
    return g.reshape(g.shape[:-3] + (g.shape[-3] * g.shape[-2], g.shape[-1]))


def _cols_to_shards(w):
    w = w.reshape(w.shape[:-1] + (N_CHIPS, w.shape[-1] // N_CHIPS))
    return jnp.moveaxis(w, -2, 0)


def _rows_to_shards(w):
    w = w.reshape(w.shape[:-2] + (N_CHIPS, w.shape[-2] // N_CHIPS, w.shape[-1]))
    return jnp.moveaxis(w, -3, 0)


def _pad_w_in(wt):
    z = lambda n: jnp.zeros((n, wt.shape[1]), wt.dtype)
    return jnp.concatenate([wt[:2048], wt[2080:4640], z(KR_LANE0), wt[2048:2080], z(HEAD_PAD - QK_DIM)], axis=0)


def _unpad_w_in(parts):
    u, zp, cq, ckv, zm, gp, gm, kr = parts
    return jnp.concatenate([u, zp, cq, ckv, kr[KR_LANE0:QK_DIM], zm, gp, gm], axis=0)


def _pad_heads(w, real):
    w = w.reshape(w.shape[:-1] + (N_HEADS, real))
    w = jnp.pad(w, [(0, 0)] * (w.ndim - 1) + [(0, HEAD_PAD - real)])
    return w.reshape(w.shape[:-2] + (N_HEADS * HEAD_PAD,))


def _flat_rows(a):
    a = a.reshape(-1)
    return jnp.pad(a, (0, (-a.shape[0]) % (8 * 128))).reshape(-1, 128)


def kernel(x, positions, meta_tokens, norm_gain, w_in, pool_w_group, pool_scale, pool_w_up, q_a_norm_gain, kv_a_norm_gain, w_q_b, w_kv_b, q_norm_gain, k_norm_gain, mla_w_up, w_out, loss_target, m_meta_tokens, m_norm_gain, m_w_in, m_pool_w_group, m_pool_scale, m_pool_w_up, m_q_a_norm_gain, m_kv_a_norm_gain, m_w_q_b, m_w_kv_b, m_q_norm_gain, m_k_norm_gain, m_mla_w_up, m_w_out, v_meta_tokens, v_norm_gain, v_w_in, v_pool_w_group, v_pool_scale, v_pool_w_up, v_q_a_norm_gain, v_kv_a_norm_gain, v_w_q_b, v_w_kv_b, v_q_norm_gain, v_k_norm_gain, v_mla_w_up, v_w_out):
    seq = x.shape[1]
    lp = -(-(ROW0 + seq) // ATTN_TILE) * ATTN_TILE
    pad_back = lp - ROW0 - seq
    chip = 2 * lax.axis_index("x") + lax.axis_index("y")

    tr = lambda a: jnp.swapaxes(a, 1, 2)
    big = dict(w_in=tr(w_in), pool_w_up=pool_w_up, w_q_b=w_q_b, w_kv_b=w_kv_b, mla_w_up=mla_w_up, w_out=w_out)
    row_sharded = ("w_in", "w_q_b", "w_out")
    names = list(big)
    shards = [[big[n][l].astype(BF16) for n in names] for l in range(DEPTH)]
    from_shards = lambda n: _rows_from_shards if n in row_sharded else _cols_from_shards
    to_shards = lambda n: _rows_to_shards if n in row_sharded else _cols_to_shards

    def in_weights(g_w_in):
        return dict(w_pad=_pad_w_in(from_shards("w_in")(g_w_in)))

    def rest_weights(gathered):
        w = {n: from_shards(n)(g) for n, g in zip(names[1:], gathered)}
        wkv = w["w_kv_b"].reshape(KV_RANK, N_HEADS, NOPE + V_DIM)
        return dict(wq=_pad_heads(w["w_q_b"], QK_DIM),
                    wkn=_pad_heads(wkv[..., :NOPE].reshape(KV_RANK, N_HEADS * NOPE), NOPE),
                    wv=wkv[..., NOPE:].reshape(KV_RANK, MLA_WIDTH),
                    wpu=w["pool_w_up"], wmu=w["mla_w_up"], wout=w["w_out"])

    g_in0, meta_g = _chip_exchange([shards[0][0], meta_tokens], scatter=False, name="gather_layer0")
    weights = [in_weights(g_in0)]
    meta_full = _cols_from_shards(meta_g)
    wg = pool_w_group.astype(BF16)
    gqn = jnp.pad(q_norm_gain, ((0, 0), (0, HEAD_PAD - QK_DIM)))
    gkn = jnp.pad(k_norm_gain, ((0, 0), (0, HEAD_PAD - QK_DIM)))

    x_pad = jnp.concatenate([jnp.zeros((PAD_FRONT, D_MODEL), F32), meta_full, x[0], jnp.zeros((pad_back, D_MODEL), F32)], axis=0)
    pos_pad = jnp.concatenate([jnp.zeros((PAD_FRONT,), jnp.int32), jnp.arange(N_META, dtype=jnp.int32),
                               positions[0] + N_META, jnp.zeros((pad_back,), jnp.int32)])
    half = ROPE // 2
    inv_freq = (ROPE_THETA ** (-np.arange(half, dtype=np.float32) / half)).astype(np.float32)
    freq_row = np.zeros((1, HEAD_PAD), np.float32)
    freq_row[0, NOPE:NOPE + half] = inv_freq
    freq_row[0, NOPE + half:QK_DIM] = inv_freq
    tabs = _rope_tables(pos_pad[:, None], jnp.asarray(freq_row))

    row = lambda a, l: a[l][None, :]

    saved = []
    h_res = x_pad
    for l in range(DEPTH):
        w = weights[l]
        (u, zp, cq, ckv, zm, gp, gm, kr), rest0 = _inproj_fwd(h_res, row(norm_gain, l), w["w_pad"],
                                                              gather=shards[0][1:] if l == 0 else ())
        if rest0:
            w.update(rest_weights(rest0))
        a_pool = _pool_fwd(u, zp, wg[l], row(pool_scale, l))
        q, k, v = _mla_prep_fwd(cq, ckv, kr, tabs, row(q_a_norm_gain, l), row(kv_a_norm_gain, l), row(gqn, l), row(gkn, l),
                                w["wq"], w["wkn"], w["wv"])
        o, lse, nxt = _flash_fwd(q, k, v, gather=shards[l + 1] if l + 1 < DEPTH else ())
        if nxt:
            weights.append({**in_weights(nxt[0]), **rest_weights(nxt[1:])})
        h_next, yp, ym = _merge_fwd(h_res, a_pool, o, zm, gp, gm, w["wpu"], w["wmu"], w["wout"])
        saved.append(dict(x=h_res, u=u, zp=zp, cq=cq, ckv=ckv, zm=zm, gp=gp, gm=gm, kr=kr, a_pool=a_pool, q=q, k=k, v=v,
                          o=o, lse=lse, yp=yp, ym=ym))
        h_res = h_next
    dres, loss_blk = _loss_head(h_res, loss_target[0])

    gw = {n: [None] * DEPTH for n in names}
    pieces = [None] * DEPTH
    grad_stacks = lambda l, which=names: [to_shards(n)(gw[n][l]).astype(BF16) for n in which]
    gs = {n: [None] * DEPTH for n in ("norm_gain", "pool_w_group", "pool_scale", "q_a", "kv_a", "q_norm", "k_norm")}
    for l in reversed(range(DEPTH)):
        s, w = saved[l], weights[l]
        merged, dyp, dym, dgp, dgm, dap, amla, do, dzm, delta = _merge_bwd(
            dres, s["yp"], s["ym"], s["gp"], s["gm"], s["o"], s["zm"], w["wout"], w["wpu"], w["wmu"])
        (gw["w_out"][l],) = _weight_grads(merged, [dres], "grad_w_out")
        (gw["pool_w_up"][l],) = _weight_grads(s["a_pool"], [dyp], "grad_pool_w_up")
        (gw["mla_w_up"][l],) = _weight_grads(amla, [dym], "grad_mla_w_up")
        dq, dk, dv, got = _flash_bwd(s["q"], s["k"], s["v"], do, s["lse"], delta,
                                     scatter=grad_stacks(l + 1) if l + 1 < DEPTH else ())
        if got:
            pieces[l + 1] = got
        dcq, dckv, dkr, dwq, dwkn, dwv, gs["q_a"][l], gs["kv_a"][l], dgqn, dgkn = _mla_prep_bwd(
            dq, dk, dv, s["cq"], s["ckv"], s["kr"], tabs, row(q_a_norm_gain, l), row(kv_a_norm_gain, l), row(gqn, l), row(gkn, l),
            w["wq"], w["wkn"], w["wv"])
        gs["q_norm"][l] = dgqn[:, :QK_DIM]
        gs["k_norm"][l] = dgkn[:, :QK_DIM]
        gw["w_q_b"][l] = dwq.reshape(Q_RANK, N_HEADS, HEAD_PAD)[..., :QK_DIM].reshape(Q_RANK, N_HEADS * QK_DIM)
        gw["w_kv_b"][l] = jnp.concatenate([dwkn.reshape(KV_RANK, N_HEADS, HEAD_PAD)[..., :NOPE],
                                           dwv.reshape(KV_RANK, N_HEADS, V_DIM)], axis=-1).reshape(KV_RANK, N_HEADS * (NOPE + V_DIM))
        du, dzp, gs["pool_w_group"][l], gs["pool_scale"][l] = _pool_bwd(dap, s["u"], s["zp"], wg[l], row(pool_scale, l))
        dparts = [du, dzp, dcq, dckv, dzm, dgp, dgm, dkr]
        dres, h, gs["norm_gain"][l], rest0 = _inproj_bwd(dres, s["x"], row(norm_gain, l), w["w_pad"], dparts,
                                                         scatter=grad_stacks(0, names[1:]) if l == 0 else ())
        ga = _weight_grads(h, [du, dzp, dcq, dckv, dkr], "grad_w_in_a", transposed=True)
        gb = _weight_grads(h, [dzm, dgp, dgm], "grad_w_in_b", transposed=True)
        gw["w_in"][l] = _unpad_w_in([ga[0], ga[1], ga[2], ga[3], gb[0], gb[1], gb[2], ga[4]])
    grad_x = dres[ROW0:ROW0 + seq][None]

    pieces[0] = list(_chip_exchange(grad_stacks(0, names[:1]), scatter=True, name="scatter_layer0")) + list(rest0)
    sums = [_sum_pieces(jnp.stack([pieces[l][a] for l in range(DEPTH)], axis=1), "sum_" + n) for a, n in enumerate(names)]
    other = _sibling_exchange(sums, name="swap_core_sums")
    moments = dict(w_in=(tr(m_w_in), tr(v_w_in)), pool_w_up=(m_pool_w_up, v_pool_w_up), w_q_b=(m_w_q_b, v_w_q_b),
                   w_kv_b=(m_w_kv_b, v_w_kv_b), mla_w_up=(m_mla_w_up, v_mla_w_up), w_out=(m_w_out, v_w_out))
    big_out = {n: _adamw(big[n], [sm, ot], moments[n][0], moments[n][1], "adamw_" + n)
               for n, sm, ot in zip(names, sums, other)}

    small_names = ("norm_gain", "pool_w_group", "pool_scale", "q_a", "kv_a", "q_norm", "k_norm")
    small_w = dict(norm_gain=(norm_gain, m_norm_gain, v_norm_gain), pool_w_group=(pool_w_group, m_pool_w_group, v_pool_w_group),
                   pool_scale=(pool_scale, m_pool_scale, v_pool_scale), q_a=(q_a_norm_gain, m_q_a_norm_gain, v_q_a_norm_gain),
                   kv_a=(kv_a_norm_gain, m_kv_a_norm_gain, v_kv_a_norm_gain), q_norm=(q_norm_gain, m_q_norm_gain, v_q_norm_gain),
                   k_norm=(k_norm_gain, m_k_norm_gain, v_k_norm_gain))
    small_g = {n: jnp.stack(gs[n]).reshape(small_w[n][0].shape) for n in small_names}
    blocks = [_flat_rows(small_g[n]) for n in small_names]
    n_rows = [b.shape[0] for b in blocks]
    meta_rows = N_META * D_MODEL // 128
    pack = jnp.concatenate(blocks + [dres[PAD_FRONT:ROW0].reshape(meta_rows, 128), loss_blk], axis=0)
    pack = jnp.pad(pack, ((0, (-pack.shape[0]) % 8), (0, 0)))
    total = _all_reduce_small(pack)
    n_small = sum(n_rows)
    loss = total[n_small + meta_rows, 0]
    gmeta = lax.dynamic_slice_in_dim(total[n_small:n_small + meta_rows].reshape(N_META, D_MODEL), chip * (D_MODEL // N_CHIPS),
                                     D_MODEL // N_CHIPS, axis=1)

    def packed(idx, meta_part):
        p = jnp.concatenate([_flat_rows(small_w[n][idx]) for n in small_names] + [_flat_rows(meta_part)], axis=0)
        return jnp.pad(p, ((0, (-p.shape[0]) % 8), (0, 0)))[None]

    g_pack = jnp.concatenate([total[:n_small], _flat_rows(gmeta)], axis=0)
    g_pack = jnp.pad(g_pack, ((0, (-g_pack.shape[0]) % 8), (0, 0)))[None]
    small_out = _adamw(packed(0, meta_tokens), [g_pack], packed(1, m_meta_tokens), packed(2, v_meta_tokens), "adamw_small")

    def unpack(p):
        res, r0 = {}, 0
        for n, nr in zip(small_names, n_rows):
            shape = small_w[n][0].shape
            res[n] = p[0, r0:r0 + nr].reshape(-1)[:math.prod(shape)].reshape(shape)
            r0 += nr
        res["meta"] = p[0, r0:r0 + N_META * (D_MODEL // N_CHIPS) // 128].reshape(N_META, D_MODEL // N_CHIPS)
        return res

    small_res = [unpack(p) for p in small_out]

    def leaf(kind, name):
        key = {"meta_tokens": "meta", "q_a_norm_gain": "q_a", "kv_a_norm_gain": "kv_a", "q_norm_gain": "q_norm",
               "k_norm_gain": "k_norm"}.get(name, name)
        if name in big_out:
            return tr(big_out[name][kind]) if name == "w_in" else big_out[name][kind]
        return small_res[kind][key]

    order = ("meta_tokens", "norm_gain", "w_in", "pool_w_group", "pool_scale", "pool_w_up", "q_a_norm_gain", "kv_a_norm_gain",
             "w_q_b", "w_kv_b", "q_norm_gain", "k_norm_gain", "mla_w_up", "w_out")
    outs = [loss, grad_x]
    for kind in range(4):
        outs += [leaf(kind, n) for n in order]
    return tuple(outs)
```

```python
import functools
import math

import numpy as np
import jax
import jax.numpy as jnp
from jax import lax
from jax.experimental import pallas as pl
from jax.experimental.pallas import tpu as pltpu

F32 = jnp.float32
BF16 = jnp.bfloat16
MESH = pl.DeviceIdType.MESH

D_MODEL = 1024
DEPTH = 4
N_META = 16
POOL_WIDTH = 512
POOL_WINDOWS = (2, 4, 8, 16)
POOL_GROUPS = 4
GROUP_DIM = 128
N_HEADS = 8
NOPE = 64
ROPE = 32
QK_DIM = 96
V_DIM = 64
MLA_WIDTH = 512
KV_RANK = 256
Q_RANK = 768
ROPE_THETA = 10000.0
EPS = 1e-6
MASK_VALUE = -1e30
ATTN_BLOCK = 128
PAD_FRONT = (-N_META) % ATTN_BLOCK
ROW0 = PAD_FRONT + N_META
HEAD_PAD = 128
HALO = 16
N_CHIPS = 4
N_DEV = 8

IN_NAMES = ("u", "zp", "cq", "ckv", "zm", "gp", "gm", "kr")
IN_WIDTHS = (512, 512, 768, 256, 512, 1024, 1024, 128)
IN_OFFS = tuple(int(v) for v in np.cumsum((0,) + IN_WIDTHS[:-1]))
IN_PAD = sum(IN_WIDTHS)
KR_LANE0 = NOPE

ADAM_LR = 0.001
ADAM_B1 = 0.9
ADAM_B2 = 0.999
ADAM_EPS = 1e-08
ADAM_WD = 0.01
ADAM_STEP = 10

VMEM_LIMIT = 56 * 1024 * 1024
ATTN_TILE = 768
ROW_TILES = (768, 384)
ROW_TILES_HEAVY = (384,)
ROW_BLOCK = 32
LOSS_GROUP = 3
FWD_PAIRS = 4
SCORE_BUFS = 4
BWD_PAIRS = 2
LOG2E = 1.4426950408889634
LN2 = 0.6931471805599453
Q_PRESCALE = LOG2E / math.sqrt(QK_DIM)


def _cp(sem=None, vmem=VMEM_LIMIT):
    kw = dict(vmem_limit_bytes=vmem)
    if sem is not None:
        kw["dimension_semantics"] = sem
    return pltpu.CompilerParams(**kw)


def _row_tile(n_rows, prefs=None):
    for t in prefs or ROW_TILES:
        if n_rows % t == 0:
            return t
    raise ValueError(f"no row tile for {n_rows}")


def _nt(a, b):
    return lax.dot_general(a, b, (((1,), (1,)), ((), ())), preferred_element_type=F32)


def _tn(a, b):
    return lax.dot_general(a, b, (((0,), (0,)), ((), ())), preferred_element_type=F32)


def _mm(a, b):
    return jnp.dot(a, b, preferred_element_type=F32)


def _sigmoid(x):
    return 0.5 * jnp.tanh(0.5 * x) + 0.5


def _resident(shape):
    nd = len(shape)
    return pl.BlockSpec(shape, lambda *_: (0,) * nd, pipeline_mode=pl.Buffered(1))


def _rows(tm, width):
    return pl.BlockSpec((tm, width), lambda i: (i, 0))


def _rope_tables(pos_col, inv_freq_row):
    lp = pos_col.shape[0]
    tm = _row_tile(lp)

    def body(p_ref, f_ref, c_ref, s1_ref, s2_ref):
        ang = p_ref[...].astype(F32) * f_ref[...]
        lane = lax.broadcasted_iota(jnp.int32, ang.shape, 1)
        cs = jnp.cos(ang)
        sn = jnp.sin(ang)
        c_ref[...] = jnp.where(lane < NOPE, 1.0, jnp.where(lane < QK_DIM, cs, 0.0))
        s1_ref[...] = jnp.where((lane >= NOPE) & (lane < NOPE + ROPE // 2), -sn, 0.0)
        s2_ref[...] = jnp.where((lane >= NOPE + ROPE // 2) & (lane < QK_DIM), sn, 0.0)

    out = jax.ShapeDtypeStruct((lp, HEAD_PAD), F32)
    return pl.pallas_call(
        body, name="rope_tables", grid=(lp // tm,),
        in_specs=[pl.BlockSpec((tm, 1), lambda i: (i, 0)), pl.BlockSpec((1, HEAD_PAD), lambda i: (0, 0))],
        out_specs=[_rows(tm, HEAD_PAD)] * 3, out_shape=[out] * 3,
        compiler_params=_cp(("parallel",)),
    )(pos_col, inv_freq_row)


def _rope(y, c, s1, s2):
    return y * c + pltpu.roll(y, HEAD_PAD - ROPE // 2, 1) * s1 + pltpu.roll(y, ROPE // 2, 1) * s2


def _rope_t(g, c, s1, s2):
    return g * c + pltpu.roll(g * s1, ROPE // 2, 1) + pltpu.roll(g * s2, HEAD_PAD - ROPE // 2, 1)


def _inproj_fwd(x, gain, w_pad, gather=()):
    lp = x.shape[0]
    tm = _row_tile(lp)
    n_out = len(IN_WIDTHS)
    ex = _ChipExchange(list(gather), scatter=False)

    def body(x_ref, g_ref, w_ref, *rest):
        ex_in, outs, ex_out, ex_sems = rest[:ex.n], rest[ex.n:ex.n + n_out], rest[ex.n + n_out:2 * ex.n + n_out], rest[2 * ex.n + n_out:]
        if ex.n:
            pl.when(pl.program_id(0) == 0)(lambda: ex.start(ex_in, ex_out, ex_sems))
        xf = x_ref[...]
        inv = lax.rsqrt(jnp.mean(xf * xf, axis=-1, keepdims=True) + EPS)
        h = (xf * inv * g_ref[...]).astype(BF16)
        for o_ref, off, wd in zip(outs, IN_OFFS, IN_WIDTHS):
            o_ref[...] = _nt(h, w_ref[off:off + wd, :]).astype(o_ref.dtype)
        if ex.n:
            pl.when(pl.program_id(0) == lp // tm - 1)(lambda: ex.wait(ex_in, ex_out, ex_sems))

    res = pl.pallas_call(
        body, name="inproj_fwd_gather" if ex.n else "inproj_fwd", grid=(lp // tm,),
        in_specs=[_rows(tm, D_MODEL), pl.BlockSpec((1, D_MODEL), lambda i: (0, 0)), _resident((IN_PAD, D_MODEL))] + ex.specs,
        out_specs=[_rows(tm, wd) for wd in IN_WIDTHS] + ex.specs,
        out_shape=[jax.ShapeDtypeStruct((lp, wd), BF16) for wd in IN_WIDTHS] + ex.out_shape,
        scratch_shapes=ex.sems if ex.n else [],
        compiler_params=_cp(("arbitrary",)),
    )(x, gain, w_pad, *gather)
    return res[:n_out], res[n_out:]


def _inv_counts(tile_idx, tm):
    row = tile_idx * tm + lax.broadcasted_iota(jnp.int32, (tm, 1), 0)
    t1 = jnp.maximum(row - PAD_FRONT + 1, 1).astype(F32)
    return [1.0 / jnp.minimum(t1, float(w)) for w in POOL_WINDOWS]


def _trailing_sums(e):
    s2 = e + pltpu.roll(e, 1, 0)
    s4 = s2 + pltpu.roll(s2, 2, 0)
    s8 = s4 + pltpu.roll(s4, 4, 0)
    s16 = s8 + pltpu.roll(s8, 8, 0)
    return (s2, s4, s8, s16)


def _leading_sums(e):
    n = e.shape[0]
    s2 = e + pltpu.roll(e, n - 1, 0)
    s4 = s2 + pltpu.roll(s2, n - 2, 0)
    s8 = s4 + pltpu.roll(s4, n - 4, 0)
    s16 = s8 + pltpu.roll(s8, n - 8, 0)
    return (s2, s4, s8, s16)


def _pool_fwd(u, zp, wg, scale):
    lp = u.shape[0]
    tm = _row_tile(lp)

    def body(u_ref, z_ref, wg_ref, sc_ref, a_ref, ext_ref):
        i = pl.program_id(0)

        @pl.when(i == 0)
        def _():
            ext_ref[0:HALO, :] = jnp.zeros((HALO, POOL_WIDTH), F32)

        ext_ref[HALO:HALO + tm, :] = u_ref[...].astype(F32)
        e = ext_ref[...]
        sums = _trailing_sums(e)
        ext_ref[0:HALO, :] = e[tm:tm + HALO, :]
        inv_cnt = _inv_counts(i, tm)
        for g in range(POOL_GROUPS):
            cols = slice(g * GROUP_DIM, (g + 1) * GROUP_DIM)
            mixed = sums[g][HALO:, cols] * inv_cnt[g] - e[HALO:, cols]
            y = _mm(mixed.astype(BF16), wg_ref[g]) * sc_ref[:, cols]
            zf = z_ref[:, cols].astype(F32)
            a_ref[:, cols] = (y * (zf * _sigmoid(zf))).astype(a_ref.dtype)

    return pl.pallas_call(
        body, name="pool_fwd", grid=(lp // tm,),
        in_specs=[_rows(tm, POOL_WIDTH), _rows(tm, POOL_WIDTH),
                  pl.BlockSpec((POOL_GROUPS, GROUP_DIM, GROUP_DIM), lambda i: (0, 0, 0)),
                  pl.BlockSpec((1, POOL_WIDTH), lambda i: (0, 0))],
        out_specs=_rows(tm, POOL_WIDTH), out_shape=jax.ShapeDtypeStruct((lp, POOL_WIDTH), BF16),
        scratch_shapes=[pltpu.VMEM((HALO + tm, POOL_WIDTH), F32)],
        compiler_params=_cp(("arbitrary",)),
    )(u, zp, wg, scale)


def _rms_fwd(xf, gain):
    inv = lax.rsqrt(jnp.mean(xf * xf, axis=-1, keepdims=True) + EPS)
    xhat = xf * inv
    return inv, xhat, xhat * gain


def _rms_bwd(dy, inv, xhat, gain):
    dgain = jnp.sum(dy * xhat, axis=0, keepdims=True)
    dyg = dy * gain
    dx = inv * (dyg - xhat * jnp.mean(dyg * xhat, axis=-1, keepdims=True))
    return dx, dgain


def _head_norm_fwd(xh, gain128):
    inv = lax.rsqrt(jnp.sum(xh * xh, axis=-1, keepdims=True) * (1.0 / QK_DIM) + EPS)
    xhat = xh * inv
    return inv, xhat, xhat * gain128


def _head_norm_bwd(dy, inv, xhat, gain128):
    dyg = dy * gain128
    return inv * (dyg - xhat * (jnp.sum(dyg * xhat, axis=-1, keepdims=True) * (1.0 / QK_DIM)))


def _mla_prep_fwd(cq, ckv, kr, tabs, gqa, gkva, gqn, gkn, wq, wkn, wv):
    lp = cq.shape[0]
    tm = _row_tile(lp)

    def body(cq_ref, ckv_ref, kr_ref, c_ref, s1_ref, s2_ref, gqa_ref, gkva_ref, gqn_ref, gkn_ref,
             wq_ref, wkn_ref, wv_ref, q_ref, k_ref, v_ref):
        c, s1, s2 = c_ref[...], s1_ref[...], s2_ref[...]
        _, _, cqn = _rms_fwd(cq_ref[...].astype(F32), gqa_ref[...])
        qraw = _mm(cqn.astype(BF16), wq_ref[...])
        for h in range(N_HEADS):
            hb = slice(h * HEAD_PAD, (h + 1) * HEAD_PAD)
            _, _, yh = _head_norm_fwd(qraw[:, hb], gqn_ref[...])
            q_ref[:, hb] = (_rope(yh, c, s1, s2) * Q_PRESCALE).astype(q_ref.dtype)
        _, _, ckvn = _rms_fwd(ckv_ref[...].astype(F32), gkva_ref[...])
        ckvn_b = ckvn.astype(BF16)
        knraw = _mm(ckvn_b, wkn_ref[...])
        krs = kr_ref[...].astype(F32)
        for h in range(N_HEADS):
            hb = slice(h * HEAD_PAD, (h + 1) * HEAD_PAD)
            _, _, yh = _head_norm_fwd(knraw[:, hb] + krs, gkn_ref[...])
            k_ref[:, hb] = _rope(yh, c, s1, s2).astype(k_ref.dtype)
        v_ref[...] = _mm(ckvn_b, wv_ref[...]).astype(v_ref.dtype)

    hw = N_HEADS * HEAD_PAD
    vec = lambda n: pl.BlockSpec((1, n), lambda i: (0, 0))
    return pl.pallas_call(
        body, name="mla_prep_fwd", grid=(lp // tm,),
        in_specs=[_rows(tm, Q_RANK), _rows(tm, KV_RANK), _rows(tm, HEAD_PAD)] + [_rows(tm, HEAD_PAD)] * 3
        + [vec(Q_RANK), vec(KV_RANK), vec(HEAD_PAD), vec(HEAD_PAD),
           _resident((Q_RANK, hw)), _resident((KV_RANK, hw)), _resident((KV_RANK, MLA_WIDTH))],
        out_specs=[_rows(tm, hw), _rows(tm, hw), _rows(tm, MLA_WIDTH)],
        out_shape=[jax.ShapeDtypeStruct((lp, hw), BF16), jax.ShapeDtypeStruct((lp, hw), BF16),
                   jax.ShapeDtypeStruct((lp, MLA_WIDTH), BF16)],
        compiler_params=_cp(("parallel",)),
    )(cq, ckv, kr, *tabs, gqa, gkva, gqn, gkn, wq, wkn, wv)


def _causal_mask(s, q0, k0):
    qi = q0 + lax.broadcasted_iota(jnp.int32, s.shape, 0)
    ki = k0 + lax.broadcasted_iota(jnp.int32, s.shape, 1)
    return jnp.where((ki <= qi) & (ki >= PAD_FRONT), s, MASK_VALUE)


def _score_chunks(kind, r, tk):
    if kind == "inner":
        return [(c0, False) for c0 in range(0, tk, ATTN_BLOCK)]
    if kind == "first":
        return [(c0, c0 == 0) for c0 in range(0, tk, ATTN_BLOCK)]
    return [(c0, True) for c0 in range(0, min(tk, (r + 1) * ROW_BLOCK), ATTN_BLOCK)]


def _tile_kinds(i, t):
    return (("diag", t == i), ("first", (t == 0) & (i > 0)), ("inner", (t > 0) & (t < i)))


def _lanes(col, width=HEAD_PAD):
    return jnp.broadcast_to(col, (col.shape[0], width))


def _flash_fwd(q, k, v, gather=()):
    lp = q.shape[0]
    tq = tk = ATTN_TILE
    nq = lp // tq
    nj = N_HEADS // (2 * FWD_PAIRS)
    heads = 2 * FWD_PAIRS
    n_blocks = tq // ROW_BLOCK
    ex = _ChipExchange(list(gather), scatter=False)

    pairs = [(i, t) for i in range(nq) for t in range(i + 1)]
    i_tab = jnp.asarray([p[0] for p in pairs], jnp.int32)
    t_tab = jnp.asarray([p[1] for p in pairs], jnp.int32)

    def body(i_tab_ref, t_tab_ref, q_ref, k_ref, v_ref, *rest):
        ex_in, (o_ref, lse_ref), ex_out = rest[:ex.n], rest[ex.n:ex.n + 2], rest[ex.n + 2:2 * ex.n + 2]
        m_scr, acc_scr, s_scr, p_scr, part_scr, vext_scr = rest[2 * ex.n + 2:2 * ex.n + 8]
        ex_sems = rest[2 * ex.n + 8:]
        j, step_no = pl.program_id(0), pl.program_id(1)
        i, t = i_tab_ref[step_no], t_tab_ref[step_no]
        if ex.n:
            pl.when((j == 0) & (step_no == 0))(lambda: ex.start(ex_in, ex_out, ex_sems))

        @pl.when(t == 0)
        def _():
            m_scr[...] = jnp.full(m_scr.shape, MASK_VALUE, F32)
            acc_scr[...] = jnp.zeros(acc_scr.shape, F32)

        def step(kind):
            def scores(hh, r, c0, masked):
                s = s_scr[hh % SCORE_BUFS, r * ROW_BLOCK:(r + 1) * ROW_BLOCK, c0:c0 + ATTN_BLOCK]
                return _causal_mask(s, i * tq + r * ROW_BLOCK, t * tk + c0) if masked else s

            for pp in range(FWD_PAIRS):
                vext_scr[pp, :, 0:HEAD_PAD] = v_ref[:, pp * HEAD_PAD:(pp + 1) * HEAD_PAD]
                vext_scr[pp, :, HEAD_PAD:2 * HEAD_PAD] = jnp.ones((tk, HEAD_PAD), BF16)
            half = tq // 2
            order = []
            for g0 in range(0, heads, SCORE_BUFS):
                group = range(g0, min(g0 + SCORE_BUFS, heads))
                order += [("scores", hh) for hh in group] + [("softmax", hh) for hh in group]
            for what, hh in order:
                sb = hh % SCORE_BUFS
                if what == "scores":
                    hb = slice(hh * HEAD_PAD, (hh + 1) * HEAD_PAD)
                    if kind == "diag":
                        s_scr[sb, 0:half, 0:half] = _nt(q_ref[0:half, hb], k_ref[0:half, hb])
                        s_scr[sb, half:tq, :] = _nt(q_ref[half:tq, hb], k_ref[:, hb])
                    else:
                        s_scr[sb] = _nt(q_ref[:, hb], k_ref[:, hb])
                    continue
                for r in range(n_blocks):
                    part = None
                    for c0, masked in _score_chunks(kind, r, tk):
                        s = scores(hh, r, c0, masked)
                        part = s if part is None else jnp.maximum(part, s)
                    part_scr[r * ROW_BLOCK:(r + 1) * ROW_BLOCK, :] = part
                m_prev = m_scr[hh]
                m_new = jnp.maximum(m_prev, _lanes(jnp.max(part_scr[...], axis=-1, keepdims=True)))
                alpha = jnp.exp2(m_prev - m_new)
                m_scr[hh] = m_new
                for r in range(n_blocks):
                    rows = slice(r * ROW_BLOCK, (r + 1) * ROW_BLOCK)
                    m_r = m_scr[hh, rows, :]
                    chunks = _score_chunks(kind, r, tk)
                    for c0, masked in chunks:
                        p_scr[sb, rows, c0:c0 + ATTN_BLOCK] = jnp.exp2((scores(hh, r, c0, masked) - m_r).astype(BF16))
                    done = chunks[-1][0] + ATTN_BLOCK
                    if done < tk:
                        p_scr[sb, rows, done:tk] = jnp.zeros((ROW_BLOCK, tk - done), BF16)
                alpha2 = jnp.concatenate([alpha, alpha], axis=1)
                if kind == "diag":
                    acc_scr[hh, 0:half, :] = (alpha2[0:half] * acc_scr[hh, 0:half, :]
                                              + _mm(p_scr[sb, 0:half, 0:half], vext_scr[hh // 2, 0:half, :]))
                    acc_scr[hh, half:tq, :] = (alpha2[half:tq] * acc_scr[hh, half:tq, :]
                                               + _mm(p_scr[sb, half:tq, :], vext_scr[hh // 2]))
                else:
                    acc_scr[hh] = alpha2 * acc_scr[hh] + _mm(p_scr[sb], vext_scr[hh // 2])

        for kind, pred in _tile_kinds(i, t):
            pl.when(pred)(functools.partial(step, kind))

        @pl.when(t == i)
        def _():
            lane = lax.broadcasted_iota(jnp.int32, (tq, HEAD_PAD), 1)
            for pp in range(FWD_PAIRS):
                pb = slice(pp * HEAD_PAD, (pp + 1) * HEAD_PAD)
                h0, h1 = 2 * pp, 2 * pp + 1
                l0, l1 = acc_scr[h0, :, HEAD_PAD:2 * HEAD_PAD], acc_scr[h1, :, HEAD_PAD:2 * HEAD_PAD]
                o = jnp.where(lane < V_DIM, acc_scr[h0, :, 0:HEAD_PAD] / l0, acc_scr[h1, :, 0:HEAD_PAD] / l1)
                o_ref[:, pb] = o.astype(o_ref.dtype)
                lse_ref[:, pb] = jnp.where(lane < V_DIM, m_scr[h0] + jnp.log(l0) * LOG2E, m_scr[h1] + jnp.log(l1) * LOG2E)

        if ex.n:
            pl.when((j == nj - 1) & (step_no == len(pairs) - 1))(lambda: ex.wait(ex_in, ex_out, ex_sems))

    q_idx = lambda j, p, it, tt: (it[p], j)
    kv_idx = lambda j, p, it, tt: (tt[p], j)
    o, lse, *gathered = pl.pallas_call(
        body, name="flash_fwd_gather" if ex.n else "flash_fwd",
        grid_spec=pltpu.PrefetchScalarGridSpec(
            num_scalar_prefetch=2, grid=(nj, len(pairs)),
            in_specs=[pl.BlockSpec((tq, heads * HEAD_PAD), q_idx), pl.BlockSpec((tk, heads * HEAD_PAD), kv_idx),
                      pl.BlockSpec((tk, FWD_PAIRS * HEAD_PAD), kv_idx)] + ex.specs,
            out_specs=[pl.BlockSpec((tq, FWD_PAIRS * HEAD_PAD), q_idx)] * 2 + ex.specs,
            scratch_shapes=[pltpu.VMEM((heads, tq, HEAD_PAD), F32), pltpu.VMEM((heads, tq, 2 * HEAD_PAD), F32),
                            pltpu.VMEM((SCORE_BUFS, tq, tk), F32), pltpu.VMEM((SCORE_BUFS, tq, tk), BF16),
                            pltpu.VMEM((tq, HEAD_PAD), F32), pltpu.VMEM((FWD_PAIRS, tk, 2 * HEAD_PAD), BF16)]
            + (ex.sems if ex.n else [])),
        out_shape=[jax.ShapeDtypeStruct((lp, MLA_WIDTH), BF16), jax.ShapeDtypeStruct((lp, MLA_WIDTH), F32)] + ex.out_shape,
        compiler_params=_cp(("arbitrary",) * 2),
    )(i_tab, t_tab, q, k, v, *gather)
    return o, lse, gathered


def _merge_fwd(x, a_pool, o, zm, gp, gm, wpu, wmu, wout):
    lp = x.shape[0]
    tm = _row_tile(lp)

    def body(x_ref, ap_ref, o_ref, zm_ref, gp_ref, gm_ref, wpu_ref, wmu_ref, wout_ref, xn_ref, yp_ref, ym_ref):
        yp = _mm(ap_ref[...], wpu_ref[...])
        zf = zm_ref[...].astype(F32)
        amla = o_ref[...].astype(F32) * (zf * _sigmoid(zf))
        ym = _mm(amla.astype(BF16), wmu_ref[...])
        merged = _sigmoid(gp_ref[...].astype(F32)) * yp + _sigmoid(gm_ref[...].astype(F32)) * ym
        xn_ref[...] = x_ref[...] + _mm(merged.astype(BF16), wout_ref[...])
        yp_ref[...] = yp.astype(yp_ref.dtype)
        ym_ref[...] = ym.astype(ym_ref.dtype)

    return pl.pallas_call(
        body, name="merge_fwd", grid=(lp // tm,),
        in_specs=[_rows(tm, D_MODEL), _rows(tm, POOL_WIDTH), _rows(tm, MLA_WIDTH), _rows(tm, MLA_WIDTH),
                  _rows(tm, D_MODEL), _rows(tm, D_MODEL),
                  _resident((POOL_WIDTH, D_MODEL)), _resident((MLA_WIDTH, D_MODEL)), _resident((D_MODEL, D_MODEL))],
        out_specs=[_rows(tm, D_MODEL)] * 3,
        out_shape=[jax.ShapeDtypeStruct((lp, D_MODEL), F32), jax.ShapeDtypeStruct((lp, D_MODEL), BF16),
                   jax.ShapeDtypeStruct((lp, D_MODEL), BF16)],
        compiler_params=_cp(("parallel",)),
    )(x, a_pool, o, zm, gp, gm, wpu, wmu, wout)


def _loss_head(y, target):
    lp = y.shape[0]
    blk = ROW0
    n_real = target.shape[0] // blk
    assert (lp // blk) % LOSS_GROUP == 0

    def body(y_ref, *rest):
        t_refs, (d_ref, l_ref) = rest[:LOSS_GROUP], rest[LOSS_GROUP:]
        i = pl.program_id(0)

        @pl.when(i == 0)
        def _():
            l_ref[...] = jnp.zeros(l_ref.shape, F32)

        total = jnp.zeros((), F32)
        for g in range(LOSS_GROUP):
            b = i * LOSS_GROUP + g
            rows = slice(g * blk, (g + 1) * blk)
            err = jnp.where((b >= 1) & (b <= n_real), y_ref[rows, :] - t_refs[g][...], 0.0)
            d_ref[rows, :] = err * (1.0 / D_MODEL)
            total = total + jnp.sum(err * err)
        l_ref[...] += total * (0.5 / D_MODEL)

    def target_block(g):
        return pl.BlockSpec((blk, D_MODEL), lambda i: (jnp.clip(i * LOSS_GROUP + g - 1, 0, n_real - 1), 0))

    return pl.pallas_call(
        body, name="loss_head", grid=(lp // (blk * LOSS_GROUP),),
        in_specs=[_rows(blk * LOSS_GROUP, D_MODEL)] + [target_block(g) for g in range(LOSS_GROUP)],
        out_specs=[_rows(blk * LOSS_GROUP, D_MODEL), pl.BlockSpec((8, 128), lambda i: (0, 0))],
        out_shape=[jax.ShapeDtypeStruct((lp, D_MODEL), F32), jax.ShapeDtypeStruct((8, 128), F32)],
        compiler_params=_cp(("arbitrary",)),
    )(y, *([target] * LOSS_GROUP))


def _pair_rowsum(prod):
    lane = lax.broadcasted_iota(jnp.int32, prod.shape, 1)
    lo = jnp.sum(jnp.where(lane < V_DIM, prod, 0.0), axis=-1, keepdims=True)
    hi = jnp.sum(jnp.where(lane < V_DIM, 0.0, prod), axis=-1, keepdims=True)
    return jnp.where(lane < V_DIM, lo, hi)


def _merge_bwd(dres, yp, ym, gp, gm, o, zm, wout, wpu, wmu):
    lp = dres.shape[0]
    tm = _row_tile(lp, ROW_TILES_HEAVY)

    def body(dres_ref, yp_ref, ym_ref, gp_ref, gm_ref, o_ref, zm_ref, wout_ref, wpu_ref, wmu_ref,
             merged_ref, dyp_ref, dym_ref, dgp_ref, dgm_ref, dap_ref, amla_ref, do_ref, dzm_ref, delta_ref):
        dmerged = _nt(dres_ref[...].astype(BF16), wout_ref[...])
        sp = _sigmoid(gp_ref[...].astype(F32))
        sm = _sigmoid(gm_ref[...].astype(F32))
        ypf = yp_ref[...].astype(F32)
        ymf = ym_ref[...].astype(F32)
        merged_ref[...] = (sp * ypf + sm * ymf).astype(merged_ref.dtype)
        dyp = (dmerged * sp).astype(BF16)
        dym = (dmerged * sm).astype(BF16)
        dyp_ref[...] = dyp
        dym_ref[...] = dym
        dgp_ref[...] = (dmerged * ypf * sp * (1.0 - sp)).astype(dgp_ref.dtype)
        dgm_ref[...] = (dmerged * ymf * sm * (1.0 - sm)).astype(dgm_ref.dtype)
        dap_ref[...] = _nt(dyp, wpu_ref[...]).astype(dap_ref.dtype)
        dam = _nt(dym, wmu_ref[...])
        zf = zm_ref[...].astype(F32)
        sg = _sigmoid(zf)
        si = zf * sg
        of = o_ref[...].astype(F32)
        amla_ref[...] = (of * si).astype(amla_ref.dtype)
        do = dam * si
        do_ref[...] = do.astype(do_ref.dtype)
        dzm_ref[...] = (dam * of * (sg * (1.0 + zf * (1.0 - sg)))).astype(dzm_ref.dtype)
        prod = do * of
        for j in range(N_HEADS // 2):
            hb = slice(j * HEAD_PAD, (j + 1) * HEAD_PAD)
            delta_ref[:, hb] = _pair_rowsum(prod[:, hb])

    bf = lambda w: jax.ShapeDtypeStruct((lp, w), BF16)
    return pl.pallas_call(
        body, name="merge_bwd", grid=(lp // tm,),
        in_specs=[_rows(tm, D_MODEL)] * 5 + [_rows(tm, MLA_WIDTH)] * 2
        + [_resident((D_MODEL, D_MODEL)), _resident((POOL_WIDTH, D_MODEL)), _resident((MLA_WIDTH, D_MODEL))],
        out_specs=[_rows(tm, D_MODEL)] * 5 + [_rows(tm, POOL_WIDTH)] + [_rows(tm, MLA_WIDTH)] * 4,
        out_shape=[bf(D_MODEL)] * 5 + [bf(POOL_WIDTH)] + [bf(MLA_WIDTH)] * 3 + [jax.ShapeDtypeStruct((lp, MLA_WIDTH), F32)],
        compiler_params=_cp(("parallel",)),
    )(dres, yp, ym, gp, gm, o, zm, wout, wpu, wmu)


def _flash_bwd(q, k, v, do, lse, delta, scatter=()):
    lp = q.shape[0]
    tq = tk = ATTN_TILE
    nq = lp // tq
    heads = 2 * BWD_PAIRS
    nj = N_HEADS // heads
    qk_w, v_w = heads * HEAD_PAD, BWD_PAIRS * HEAD_PAD
    scale = 1.0 / math.sqrt(QK_DIM)
    ex = _ChipExchange(list(scatter), scatter=True)

    pairs = [(t, i) for t in range(nq) for i in range(t, nq)]
    t_tab = jnp.asarray([p[0] for p in pairs], jnp.int32)
    i_tab = jnp.asarray([p[1] for p in pairs], jnp.int32)

    def body(t_tab_ref, i_tab_ref, q_ref, k_ref, v_ref, do_ref, lse_ref, dl_ref, *rest):
        ex_in, (dq_hbm, dk_ref, dv_ref), ex_out = rest[:ex.n], rest[ex.n:ex.n + 3], rest[ex.n + 3:2 * ex.n + 3]
        (dq_acc, dk_acc, dv_acc, s_scr, dp_scr, p_scr, ds_scr, doh_scr, stat_scr, stage_scr,
         stage_sem) = rest[2 * ex.n + 3:2 * ex.n + 14]
        ex_sems = rest[2 * ex.n + 14:]
        j, step_no = pl.program_id(0), pl.program_id(1)
        t, i = t_tab_ref[step_no], i_tab_ref[step_no]
        if ex.n:
            pl.when((j == 0) & (step_no == 0))(lambda: ex.start(ex_in, ex_out, ex_sems))

        @pl.when(step_no == 0)
        def _():
            dq_acc[...] = jnp.zeros(dq_acc.shape, F32)

        @pl.when(i == t)
        def _():
            dk_acc[...] = jnp.zeros(dk_acc.shape, F32)
            dv_acc[...] = jnp.zeros(dv_acc.shape, F32)

        def step(kind):
            lane = lax.broadcasted_iota(jnp.int32, (tq, HEAD_PAD), 1)
            half = tq // 2
            for pp in range(BWD_PAIRS):
                pb = slice(pp * HEAD_PAD, (pp + 1) * HEAD_PAD)
                for hh in range(2):
                    hb = slice((2 * pp + hh) * HEAD_PAD, (2 * pp + hh + 1) * HEAD_PAD)
                    mine = (lane < V_DIM) if hh == 0 else (lane >= V_DIM)
                    doh_scr[hh] = jnp.where(mine, do_ref[:, pb], jnp.zeros((tq, HEAD_PAD), BF16))
                    if kind == "diag":
                        s_scr[hh, 0:half, 0:half] = _nt(q_ref[0:half, hb], k_ref[0:half, hb])
                        s_scr[hh, half:tq, :] = _nt(q_ref[half:tq, hb], k_ref[:, hb])
                        dp_scr[hh, 0:half, 0:half] = _nt(doh_scr[hh, 0:half, :], v_ref[0:half, pb])
                        dp_scr[hh, half:tq, :] = _nt(doh_scr[hh, half:tq, :], v_ref[:, pb])
                    else:
                        s_scr[hh] = _nt(q_ref[:, hb], k_ref[:, hb])
                        dp_scr[hh] = _nt(doh_scr[hh], v_ref[:, pb])
                for hh in range(2):
                    head = 2 * pp + hh
                    hb = slice(head * HEAD_PAD, (head + 1) * HEAD_PAD)
                    col = slice(pp * HEAD_PAD + hh * V_DIM, pp * HEAD_PAD + hh * V_DIM + 1)
                    stat_scr[0] = _lanes(lse_ref[:, col])
                    stat_scr[1] = _lanes(dl_ref[:, col])
                    for r in range(tq // ROW_BLOCK):
                        rows = slice(r * ROW_BLOCK, (r + 1) * ROW_BLOCK)
                        lse_r = stat_scr[0, rows, :]
                        dl_r = stat_scr[1, rows, :]
                        chunks = _score_chunks(kind, r, tk)
                        for c0, masked in chunks:
                            cols = slice(c0, c0 + ATTN_BLOCK)
                            s = s_scr[hh, rows, cols]
                            if masked:
                                s = _causal_mask(s, i * tq + r * ROW_BLOCK, t * tk + c0)
                            p = jnp.exp2(s - lse_r)
                            p_scr[hh, rows, cols] = p.astype(BF16)
                            ds_scr[hh, rows, cols] = (p * (dp_scr[hh, rows, cols] - dl_r)).astype(BF16)
                        done = chunks[-1][0] + ATTN_BLOCK
                        if done < tk:
                            zeros = jnp.zeros((ROW_BLOCK, tk - done), BF16)
                            p_scr[hh, rows, done:tk] = zeros
                            ds_scr[hh, rows, done:tk] = zeros
                    tnt = lambda a, b: lax.dot_general(a, b, (((0,), (1,)), ((), ())), preferred_element_type=F32)
                    if kind == "diag":
                        dv_acc[pp, :, 0:half] += _tn(doh_scr[hh], p_scr[hh, :, 0:half])
                        dv_acc[pp, :, half:tk] += _tn(doh_scr[hh, half:tq, :], p_scr[hh, half:tq, half:tk])
                        dk_acc[head, :, 0:half] += _tn(q_ref[:, hb], ds_scr[hh, :, 0:half])
                        dk_acc[head, :, half:tk] += _tn(q_ref[half:tq, hb], ds_scr[hh, half:tq, half:tk])
                        dq_acc[i, hb, 0:half] += tnt(k_ref[0:half, hb], ds_scr[hh, 0:half, 0:half])
                        dq_acc[i, hb, half:tq] += tnt(k_ref[:, hb], ds_scr[hh, half:tq, :])
                    else:
                        dv_acc[pp] += _tn(doh_scr[hh], p_scr[hh])
                        dk_acc[head] += _tn(q_ref[:, hb], ds_scr[hh])
                        dq_acc[i, hb, :] += tnt(k_ref[:, hb], ds_scr[hh])

        for kind, pred in _tile_kinds(i, t):
            pl.when(pred)(functools.partial(step, kind))

        @pl.when(i == nq - 1)
        def _():
            for head in range(heads):
                hb = slice(head * HEAD_PAD, (head + 1) * HEAD_PAD)
                dk_ref[:, hb] = (dk_acc[head].T * LN2).astype(dk_ref.dtype)
            for pp in range(BWD_PAIRS):
                dv_ref[:, pp * HEAD_PAD:(pp + 1) * HEAD_PAD] = dv_acc[pp].T.astype(dv_ref.dtype)

        @pl.when(step_no == len(pairs) - 1)
        def _():
            my_cols = pl.ds(pl.multiple_of(j * qk_w, qk_w), qk_w)
            for qi in range(nq):
                for head in range(heads):
                    hb = slice(head * HEAD_PAD, (head + 1) * HEAD_PAD)
                    stage_scr[:, hb] = (dq_acc[qi, hb, :].T * scale).astype(BF16)
                out = pltpu.make_async_copy(stage_scr, dq_hbm.at[pl.ds(qi * tq, tq), my_cols], stage_sem)
                out.start()
                out.wait()

        if ex.n:
            pl.when((j == nj - 1) & (step_no == len(pairs) - 1))(lambda: ex.wait(ex_in, ex_out, ex_sems))

    q_idx = lambda j, p, tt, it: (it[p], j)
    kv_idx = lambda j, p, tt, it: (tt[p], j)
    hw = N_HEADS * HEAD_PAD
    dq, dk, dv, *pieces = pl.pallas_call(
        body, name="flash_bwd_scatter" if ex.n else "flash_bwd",
        grid_spec=pltpu.PrefetchScalarGridSpec(
            num_scalar_prefetch=2, grid=(nj, len(pairs)),
            in_specs=[pl.BlockSpec((tq, qk_w), q_idx), pl.BlockSpec((tk, qk_w), kv_idx),
                      pl.BlockSpec((tk, v_w), kv_idx), pl.BlockSpec((tq, v_w), q_idx),
                      pl.BlockSpec((tq, v_w), q_idx), pl.BlockSpec((tq, v_w), q_idx)] + ex.specs,
            out_specs=[HBM_SPEC, pl.BlockSpec((tk, qk_w), kv_idx), pl.BlockSpec((tk, v_w), kv_idx)] + ex.specs,
            scratch_shapes=[pltpu.VMEM((nq, qk_w, tq), F32), pltpu.VMEM((heads, HEAD_PAD, tk), F32),
                            pltpu.VMEM((BWD_PAIRS, HEAD_PAD, tk), F32),
                            pltpu.VMEM((2, tq, tk), F32), pltpu.VMEM((2, tq, tk), F32),
                            pltpu.VMEM((2, tq, tk), BF16), pltpu.VMEM((2, tq, tk), BF16),
                            pltpu.VMEM((2, tq, HEAD_PAD), BF16), pltpu.VMEM((2, tq, HEAD_PAD), F32),
                            pltpu.VMEM((tq, qk_w), BF16), pltpu.SemaphoreType.DMA(())]
            + (ex.sems if ex.n else [])),
        out_shape=[jax.ShapeDtypeStruct((lp, hw), BF16), jax.ShapeDtypeStruct((lp, hw), BF16),
                   jax.ShapeDtypeStruct((lp, MLA_WIDTH), BF16)] + ex.out_shape,
        compiler_params=_cp(("arbitrary",) * 2),
    )(t_tab, i_tab, q, k, v, do, lse, delta, *scatter)
    return dq, dk, dv, pieces


def _mla_prep_bwd(dq, dk, dv, cq, ckv, kr, tabs, gqa, gkva, gqn, gkn, wq, wkn, wv):
    lp = cq.shape[0]
    tm = _row_tile(lp, ROW_TILES_HEAVY)
    hw = N_HEADS * HEAD_PAD

    def body(dq_ref, dk_ref, dv_ref, cq_ref, ckv_ref, kr_ref, c_ref, s1_ref, s2_ref, gqa_ref, gkva_ref, gqn_ref,
             gkn_ref, wq_ref, wkn_ref, wv_ref, dcq_ref, dckv_ref, dkr_ref, dwq_ref, dwkn_ref, dwv_ref,
             dgqa_ref, dgkva_ref, dgqn_ref, dgkn_ref, draw_scr):
        @pl.when(pl.program_id(0) == 0)
        def _():
            for r in (dwq_ref, dwkn_ref, dwv_ref, dgqa_ref, dgkva_ref, dgqn_ref, dgkn_ref):
                r[...] = jnp.zeros(r.shape, F32)

        c, s1, s2 = c_ref[...], s1_ref[...], s2_ref[...]
        lane = lax.broadcasted_iota(jnp.int32, (tm, HEAD_PAD), 1)

        inv_q, xhat_q, cqn = _rms_fwd(cq_ref[...].astype(F32), gqa_ref[...])
        cqn_b = cqn.astype(BF16)
        qraw = _mm(cqn_b, wq_ref[...])
        dgqn = jnp.zeros((1, HEAD_PAD), F32)
        for h in range(N_HEADS):
            hb = slice(h * HEAD_PAD, (h + 1) * HEAD_PAD)
            inv, xhat, _ = _head_norm_fwd(qraw[:, hb], gqn_ref[...])
            dy = _rope_t(dq_ref[:, hb].astype(F32), c, s1, s2)
            dgqn += jnp.sum(dy * xhat, axis=0, keepdims=True)
            draw_scr[:, hb] = _head_norm_bwd(dy, inv, xhat, gqn_ref[...]).astype(BF16)
        dgqn_ref[...] += dgqn
        dqraw = draw_scr[...]
        dwq_ref[...] += _tn(cqn_b, dqraw)
        dcq, dgqa = _rms_bwd(_nt(dqraw, wq_ref[...]), inv_q, xhat_q, gqa_ref[...])
        dcq_ref[...] = dcq.astype(dcq_ref.dtype)
        dgqa_ref[...] += dgqa

        inv_kv, xhat_kv, ckvn = _rms_fwd(ckv_ref[...].astype(F32), gkva_ref[...])
        ckvn_b = ckvn.astype(BF16)
        knraw = _mm(ckvn_b, wkn_ref[...])
        krs = kr_ref[...].astype(F32)
        dgkn = jnp.zeros((1, HEAD_PAD), F32)
        dkr = jnp.zeros((tm, HEAD_PAD), F32)
        for h in range(N_HEADS):
            hb = slice(h * HEAD_PAD, (h + 1) * HEAD_PAD)
            inv, xhat, _ = _head_norm_fwd(knraw[:, hb] + krs, gkn_ref[...])
            dy = _rope_t(dk_ref[:, hb].astype(F32), c, s1, s2)
            dgkn += jnp.sum(dy * xhat, axis=0, keepdims=True)
            dxh = _head_norm_bwd(dy, inv, xhat, gkn_ref[...])
            dkr += dxh
            draw_scr[:, hb] = jnp.where(lane < NOPE, dxh, 0.0).astype(BF16)
        dgkn_ref[...] += dgkn
        dkr_ref[...] = jnp.where((lane >= KR_LANE0) & (lane < QK_DIM), dkr, 0.0).astype(dkr_ref.dtype)
        dknraw = draw_scr[...]
        dvb = dv_ref[...]
        dwkn_ref[...] += _tn(ckvn_b, dknraw)
        dwv_ref[...] += _tn(ckvn_b, dvb)
        dckvn = _nt(dknraw, wkn_ref[...]) + _nt(dvb, wv_ref[...])
        dckv, dgkva = _rms_bwd(dckvn, inv_kv, xhat_kv, gkva_ref[...])
        dckv_ref[...] = dckv.astype(dckv_ref.dtype)
        dgkva_ref[...] += dgkva

    vec = lambda n: pl.BlockSpec((1, n), lambda i: (0, 0))
    whole = lambda r, c: pl.BlockSpec((r, c), lambda i: (0, 0))
    f = lambda r, c: jax.ShapeDtypeStruct((r, c), F32)
    return pl.pallas_call(
        body, name="mla_prep_bwd", grid=(lp // tm,),
        in_specs=[_rows(tm, hw), _rows(tm, hw), _rows(tm, MLA_WIDTH), _rows(tm, Q_RANK), _rows(tm, KV_RANK),
                  _rows(tm, HEAD_PAD)] + [_rows(tm, HEAD_PAD)] * 3
        + [vec(Q_RANK), vec(KV_RANK), vec(HEAD_PAD), vec(HEAD_PAD),
           _resident((Q_RANK, hw)), _resident((KV_RANK, hw)), _resident((KV_RANK, MLA_WIDTH))],
        out_specs=[_rows(tm, Q_RANK), _rows(tm, KV_RANK), _rows(tm, HEAD_PAD),
                   whole(Q_RANK, hw), whole(KV_RANK, hw), whole(KV_RANK, MLA_WIDTH),
                   vec(Q_RANK), vec(KV_RANK), vec(HEAD_PAD), vec(HEAD_PAD)],
        out_shape=[jax.ShapeDtypeStruct((lp, Q_RANK), BF16), jax.ShapeDtypeStruct((lp, KV_RANK), BF16),
                   jax.ShapeDtypeStruct((lp, HEAD_PAD), BF16),
                   f(Q_RANK, hw), f(KV_RANK, hw), f(KV_RANK, MLA_WIDTH),
                   f(1, Q_RANK), f(1, KV_RANK), f(1, HEAD_PAD), f(1, HEAD_PAD)],
        scratch_shapes=[pltpu.VMEM((tm, hw), BF16)],
        compiler_params=_cp(("arbitrary",)),
    )(dq, dk, dv, cq, ckv, kr, *tabs, gqa, gkva, gqn, gkn, wq, wkn, wv)


def _pool_bwd(dap, u, zp, wg, scale):
    lp = u.shape[0]
    tm = _row_tile(lp)
    n = lp // tm
    per = tm // HALO

    def body(dap_ref, u_ref, uh_ref, z_ref, wg_ref, sc_ref, du_ref, dz_ref, dwg_ref, dsc_ref, ext_u, ext_d):
        i = pl.program_id(0)
        r = n - 1 - i

        @pl.when(i == 0)
        def _():
            dwg_ref[...] = jnp.zeros(dwg_ref.shape, F32)
            dsc_ref[...] = jnp.zeros(dsc_ref.shape, F32)
            ext_d[tm:tm + HALO, :] = jnp.zeros((HALO, POOL_WIDTH), F32)

        ext_u[0:HALO, :] = jnp.where(r == 0, 0.0, uh_ref[...].astype(F32))
        ext_u[HALO:HALO + tm, :] = u_ref[...].astype(F32)
        e = ext_u[...]
        sums = _trailing_sums(e)
        inv_cnt = _inv_counts(r, tm)
        dmixed = []
        for g in range(POOL_GROUPS):
            cols = slice(g * GROUP_DIM, (g + 1) * GROUP_DIM)
            mixed_b = (sums[g][HALO:, cols] * inv_cnt[g] - e[HALO:, cols]).astype(BF16)
            yg = _mm(mixed_b, wg_ref[g])
            zf = z_ref[:, cols].astype(F32)
            sg = _sigmoid(zf)
            da = dap_ref[:, cols].astype(F32)
            dy = da * (zf * sg)
            dz_ref[:, cols] = (da * (yg * sc_ref[:, cols]) * (sg * (1.0 + zf * (1.0 - sg)))).astype(dz_ref.dtype)
            dsc_ref[:, cols] += jnp.sum(dy * yg, axis=0, keepdims=True)
            dyg = (dy * sc_ref[:, cols]).astype(BF16)
            dwg_ref[g] += _tn(mixed_b, dyg)
            dm = _nt(dyg, wg_ref[g])
            dmixed.append(dm)
            ext_d[0:tm, cols] = dm * inv_cnt[g]
        ed = ext_d[...]
        lead = _leading_sums(ed)
        ext_d[tm:tm + HALO, :] = ed[0:HALO, :]
        for g in range(POOL_GROUPS):
            cols = slice(g * GROUP_DIM, (g + 1) * GROUP_DIM)
            du_ref[:, cols] = (lead[g][0:tm, cols] - dmixed[g]).astype(du_ref.dtype)

    rev = lambda i: (n - 1 - i, 0)
    return pl.pallas_call(
        body, name="pool_bwd", grid=(n,),
        in_specs=[pl.BlockSpec((tm, POOL_WIDTH), rev), pl.BlockSpec((tm, POOL_WIDTH), rev),
                  pl.BlockSpec((HALO, POOL_WIDTH), lambda i: (jnp.maximum((n - 1 - i) * per - 1, 0), 0)),
                  pl.BlockSpec((tm, POOL_WIDTH), rev),
                  pl.BlockSpec((POOL_GROUPS, GROUP_DIM, GROUP_DIM), lambda i: (0, 0, 0)),
                  pl.BlockSpec((1, POOL_WIDTH), lambda i: (0, 0))],
        out_specs=[pl.BlockSpec((tm, POOL_WIDTH), rev), pl.BlockSpec((tm, POOL_WIDTH), rev),
                   pl.BlockSpec((POOL_GROUPS, GROUP_DIM, GROUP_DIM), lambda i: (0, 0, 0)),
                   pl.BlockSpec((1, POOL_WIDTH), lambda i: (0, 0))],
        out_shape=[jax.ShapeDtypeStruct((lp, POOL_WIDTH), BF16), jax.ShapeDtypeStruct((lp, POOL_WIDTH), BF16),
                   jax.ShapeDtypeStruct((POOL_GROUPS, GROUP_DIM, GROUP_DIM), F32),
                   jax.ShapeDtypeStruct((1, POOL_WIDTH), F32)],
        scratch_shapes=[pltpu.VMEM((HALO + tm, POOL_WIDTH), F32), pltpu.VMEM((tm + HALO, POOL_WIDTH), F32)],
        compiler_params=_cp(("arbitrary",)),
    )(dap, u, u, zp, wg, scale)


def _inproj_bwd(dres, x, gain, w_pad, dparts, scatter=()):
    lp = x.shape[0]
    tm = _row_tile(lp, ROW_TILES_HEAVY)
    n_dp = len(IN_WIDTHS)
    ex = _ChipExchange(list(scatter), scatter=True)

    def body(dres_ref, x_ref, g_ref, w_ref, *rest):
        dps, ex_in = rest[:n_dp], rest[n_dp:n_dp + ex.n]
        dprev_ref, h_ref, dg_ref = rest[n_dp + ex.n:n_dp + ex.n + 3]
        ex_out, ex_sems = rest[n_dp + ex.n + 3:n_dp + 2 * ex.n + 3], rest[n_dp + 2 * ex.n + 3:]
        if ex.n:
            pl.when(pl.program_id(0) == 0)(lambda: ex.start(ex_in, ex_out, ex_sems))

        @pl.when(pl.program_id(0) == 0)
        def _():
            dg_ref[...] = jnp.zeros(dg_ref.shape, F32)

        dh = jnp.zeros((tm, D_MODEL), F32)
        for dp_ref, off, wd in zip(dps, IN_OFFS, IN_WIDTHS):
            dh += _mm(dp_ref[...], w_ref[off:off + wd, :])
        inv, xhat, hn = _rms_fwd(x_ref[...], g_ref[...])
        h_ref[...] = hn.astype(h_ref.dtype)
        dx, dgain = _rms_bwd(dh, inv, xhat, g_ref[...])
        dg_ref[...] += dgain
        dprev_ref[...] = dres_ref[...] + dx
        if ex.n:
            pl.when(pl.program_id(0) == lp // tm - 1)(lambda: ex.wait(ex_in, ex_out, ex_sems))

    dprev, h, dgain, *pieces = pl.pallas_call(
        body, name="inproj_bwd_scatter" if ex.n else "inproj_bwd", grid=(lp // tm,),
        in_specs=[_rows(tm, D_MODEL), _rows(tm, D_MODEL), pl.BlockSpec((1, D_MODEL), lambda i: (0, 0)),
                  _resident((IN_PAD, D_MODEL))] + [_rows(tm, wd) for wd in IN_WIDTHS] + ex.specs,
        out_specs=[_rows(tm, D_MODEL), _rows(tm, D_MODEL), pl.BlockSpec((1, D_MODEL), lambda i: (0, 0))] + ex.specs,
        out_shape=[jax.ShapeDtypeStruct((lp, D_MODEL), F32), jax.ShapeDtypeStruct((lp, D_MODEL), BF16),
                   jax.ShapeDtypeStruct((1, D_MODEL), F32)] + ex.out_shape,
        scratch_shapes=ex.sems if ex.n else [],
        compiler_params=_cp(("arbitrary",)),
    )(dres, x, gain, w_pad, *dparts, *scatter)
    return dprev, h, dgain, pieces


def _weight_grads(a, bs, name, transposed=False):
    lp, m = a.shape
    tk = _row_tile(lp)
    nb = len(bs)
    shapes = [(b.shape[1], m) if transposed else (m, b.shape[1]) for b in bs]

    def body(a_ref, *rest):
        b_refs, o_refs = rest[:nb], rest[nb:]

        @pl.when(pl.program_id(0) == 0)
        def _():
            for o_ref in o_refs:
                o_ref[...] = jnp.zeros(o_ref.shape, F32)

        ab = a_ref[...].astype(BF16)
        for b_ref, o_ref in zip(b_refs, o_refs):
            bb = b_ref[...].astype(BF16)
            o_ref[...] += _tn(bb, ab) if transposed else _tn(ab, bb)

    return pl.pallas_call(
        body, name=name, grid=(lp // tk,),
        in_specs=[_rows(tk, m)] + [_rows(tk, b.shape[1]) for b in bs],
        out_specs=[pl.BlockSpec(s, lambda i: (0, 0)) for s in shapes],
        out_shape=[jax.ShapeDtypeStruct(s, F32) for s in shapes],
        compiler_params=_cp(("arbitrary",)),
    )(a, *bs)


HBM_SPEC = pl.BlockSpec(memory_space=pltpu.HBM)


def _my_place():
    return lax.axis_index("x"), lax.axis_index("y"), lax.axis_index("c")


def _other_chips(x, y):
    return [(1 - x, y), (x, 1 - y), (1 - x, 1 - y)]


class _ChipExchange:
    def __init__(self, arrs, scatter):
        self.n = len(arrs)
        self.scatter = scatter
        self.out_shape = [jax.ShapeDtypeStruct(a.shape if scatter else (N_CHIPS,) + a.shape, a.dtype) for a in arrs]
        self.specs = [HBM_SPEC] * self.n
        self.sems = [pltpu.SemaphoreType.DMA((3 * self.n,)), pltpu.SemaphoreType.DMA((3 * self.n,)),
                     pltpu.SemaphoreType.DMA((self.n,))]

    def _copies(self, ins, outs, sems):
        send_sems, recv_sems, local_sems = sems
        x, y, c = _my_place()
        me = 2 * x + y
        chips = _other_chips(x, y)
        mine = lambda a: ins[a].at[me] if self.scatter else ins[a]

        def remote(a, k, arriving):
            px, py = chips[k]
            there = 2 * px + py
            return pltpu.make_async_remote_copy(
                src_ref=mine(a) if arriving or not self.scatter else ins[a].at[there],
                dst_ref=outs[a].at[there if arriving else me],
                send_sem=send_sems.at[a * 3 + k], recv_sem=recv_sems.at[a * 3 + k],
                device_id=(px, py, c), device_id_type=MESH)

        pairs = [(a, k) for a in range(self.n) for k in range(3)]
        local = [pltpu.make_async_copy(mine(a), outs[a].at[me], local_sems.at[a]) for a in range(self.n)]
        return local, [remote(a, k, False) for a, k in pairs], [remote(a, k, True) for a, k in pairs]

    def start(self, ins, outs, sems):
        local, sends, _ = self._copies(ins, outs, sems)
        for cp in local + sends:
            cp.start()

    def wait(self, ins, outs, sems):
        local, sends, arrivals = self._copies(ins, outs, sems)
        for cp in arrivals:
            cp.wait_recv()
        for cp in sends:
            cp.wait_send()
        for cp in local:
            cp.wait()


def _chip_exchange(arrs, scatter, name):
    ex = _ChipExchange(arrs, scatter)

    def body(*refs):
        ins, outs, sems = refs[:ex.n], refs[ex.n:2 * ex.n], refs[2 * ex.n:]
        ex.start(ins, outs, sems)
        ex.wait(ins, outs, sems)

    return pl.pallas_call(body, name=name, in_specs=ex.specs, out_specs=ex.specs, out_shape=ex.out_shape,
                          scratch_shapes=ex.sems)(*arrs)


def _sibling_exchange(arrs, name):
    n = len(arrs)

    def body(*refs):
        ins, outs = refs[:n], refs[n:2 * n]
        send_sems, recv_sems = refs[2 * n:]
        x, y, c = _my_place()
        cps = [pltpu.make_async_remote_copy(src_ref=ins[a], dst_ref=outs[a], send_sem=send_sems.at[a],
                                            recv_sem=recv_sems.at[a], device_id=(x, y, 1 - c), device_id_type=MESH)
               for a in range(n)]
        for cp in cps:
            cp.start()
        for cp in cps:
            cp.wait_recv()
        for cp in cps:
            cp.wait_send()

    return pl.pallas_call(
        body, name=name, in_specs=[HBM_SPEC] * n, out_specs=[HBM_SPEC] * n,
        out_shape=[jax.ShapeDtypeStruct(a.shape, a.dtype) for a in arrs],
        scratch_shapes=[pltpu.SemaphoreType.DMA((n,)), pltpu.SemaphoreType.DMA((n,))],
    )(*arrs)


def _all_reduce_small(pack):
    rows = pack.shape[0]

    def body(p_ref, o_ref, g_scr, send_sems, recv_sems):
        x, y, c = _my_place()
        me = 4 * x + 2 * y + c
        flips = [(dx, dy, dc) for dx in (0, 1) for dy in (0, 1) for dc in (0, 1) if (dx, dy, dc) != (0, 0, 0)]

        def peer(f):
            return (x if f[0] == 0 else 1 - x, y if f[1] == 0 else 1 - y, c if f[2] == 0 else 1 - c)

        def copy(k, slot):
            return pltpu.make_async_remote_copy(src_ref=p_ref, dst_ref=g_scr.at[slot], send_sem=send_sems.at[k],
                                                recv_sem=recv_sems.at[k], device_id=peer(flips[k]), device_id_type=MESH)

        sends = [copy(k, me) for k in range(len(flips))]
        for cp in sends:
            cp.start()
        g_scr[me] = p_ref[...]
        for k, f in enumerate(flips):
            px, py, pc = peer(f)
            copy(k, 4 * px + 2 * py + pc).wait_recv()
        for cp in sends:
            cp.wait_send()
        acc = g_scr[0]
        for d in range(1, N_DEV):
            acc = acc + g_scr[d]
        o_ref[...] = acc

    vm = pl.BlockSpec(memory_space=pltpu.VMEM)
    return pl.pallas_call(
        body, name="all_reduce_small", in_specs=[vm], out_specs=vm,
        out_shape=jax.ShapeDtypeStruct(pack.shape, F32),
        scratch_shapes=[pltpu.VMEM((N_DEV, rows, 128), F32), pltpu.SemaphoreType.DMA((N_DEV - 1,)),
                        pltpu.SemaphoreType.DMA((N_DEV - 1,))],
        compiler_params=_cp(),
    )(pack)


def _as3d(a):
    return a.reshape((-1,) + a.shape[-2:])


def _row_block(r, sublanes=8, cap=512):
    fits = [t for t in range(sublanes, min(r, cap) + 1, sublanes) if r % t == 0]
    return fits[-1] if fits else r


def _sum_pieces(pieces, name):
    _, na, r, c = pieces.shape
    rt = _row_block(r, sublanes=16)

    def body(p_ref, o_ref):
        acc = p_ref[0, 0].astype(F32)
        for s in range(1, N_CHIPS):
            acc = acc + p_ref[s, 0].astype(F32)
        o_ref[0] = acc

    return pl.pallas_call(
        body, name=name, grid=(na, r // rt),
        in_specs=[pl.BlockSpec((N_CHIPS, 1, rt, c), lambda a, i: (0, a, i, 0))],
        out_specs=pl.BlockSpec((1, rt, c), lambda a, i: (a, i, 0)),
        out_shape=jax.ShapeDtypeStruct((na, r, c), F32),
        compiler_params=_cp(("parallel", "parallel")),
    )(pieces)


def _adamw(w, g_parts, m, v, name):
    na, r, c = w.shape
    rt = _row_block(r)
    ng = len(g_parts)

    def body(w_ref, *rest):
        g_refs = rest[:ng]
        m_ref, v_ref, g_out, d_out, m_out, v_out = rest[ng:]
        g = g_refs[0][...]
        for gr in g_refs[1:]:
            g = g + gr[...]
        m_new = ADAM_B1 * m_ref[...] + (1.0 - ADAM_B1) * g
        v_new = ADAM_B2 * v_ref[...] + (1.0 - ADAM_B2) * (g * g)
        m_hat = m_new / (1.0 - ADAM_B1 ** ADAM_STEP)
        v_hat = v_new / (1.0 - ADAM_B2 ** ADAM_STEP)
        g_out[...] = g
        d_out[...] = -ADAM_LR * (m_hat / (jnp.sqrt(v_hat) + ADAM_EPS) + ADAM_WD * w_ref[...])
        m_out[...] = m_new
        v_out[...] = v_new

    spec = pl.BlockSpec((1, rt, c), lambda a, i: (a, i, 0))
    out = jax.ShapeDtypeStruct((na, r, c), F32)
    return pl.pallas_call(
        body, name=name, grid=(na, r // rt), in_specs=[spec] * (3 + ng), out_specs=[spec] * 4, out_shape=[out] * 4,
        compiler_params=_cp(("parallel", "parallel")),
    )(w, *g_parts, m, v)


def _cols_from_shards(g):
    g = jnp.moveaxis(g, 0, -2)
    return g.reshape(g.shape[:-2] + (g.shape[-2] * g.shape[-1],))


def _rows_from_shards(g):
    g = jnp.moveaxis(g, 0, -3)
    return g.reshape(g.shape[:-3] + (g.shape[-3] * g.shape[-2], g.shape[-1]))


def _cols_to_shards(w):
    w = w.reshape(w.shape[:-1] + (N_CHIPS, w.shape[-1] // N_CHIPS))
    return jnp.moveaxis(w, -2, 0)


def _rows_to_shards(w):
    w = w.reshape(w.shape[:-2] + (N_CHIPS, w.shape[-2] // N_CHIPS, w.shape[-1]))
    return jnp.moveaxis(w, -3, 0)


def _pad_w_in(wt):
    z = lambda n: jnp.zeros((n, wt.shape[1]), wt.dtype)
    return jnp.concatenate([wt[:2048], wt[2080:4640], z(KR_LANE0), wt[2048:2080], z(HEAD_PAD - QK_DIM)], axis=0)


def _unpad_w_in(parts):
    u, zp, cq, ckv, zm, gp, gm, kr = parts
    return jnp.concatenate([u, zp, cq, ckv, kr[KR_LANE0:QK_DIM], zm, gp, gm], axis=0)


def _pad_heads(w, real):
    w = w.reshape(w.shape[:-1] + (N_HEADS, real))
    w = jnp.pad(w, [(0, 0)] * (w.ndim - 1) + [(0, HEAD_PAD - real)])
    return w.reshape(w.shape[:-2] + (N_HEADS * HEAD_PAD,))


def _flat_rows(a):
    a = a.reshape(-1)
    return jnp.pad(a, (0, (-a.shape[0]) % (8 * 128))).reshape(-1, 128)


def kernel(x, positions, meta_tokens, norm_gain, w_in, pool_w_group, pool_scale, pool_w_up, q_a_norm_gain, kv_a_norm_gain, w_q_b, w_kv_b, q_norm_gain, k_norm_gain, mla_w_up, w_out, loss_target, m_meta_tokens, m_norm_gain, m_w_in, m_pool_w_group, m_pool_scale, m_pool_w_up, m_q_a_norm_gain, m_kv_a_norm_gain, m_w_q_b, m_w_kv_b, m_q_norm_gain, m_k_norm_gain, m_mla_w_up, m_w_out, v_meta_tokens, v_norm_gain, v_w_in, v_pool_w_group, v_pool_scale, v_pool_w_up, v_q_a_norm_gain, v_kv_a_norm_gain, v_w_q_b, v_w_kv_b, v_q_norm_gain, v_k_norm_gain, v_mla_w_up, v_w_out):
    seq = x.shape[1]
    lp = -(-(ROW0 + seq) // ATTN_TILE) * ATTN_TILE
    pad_back = lp - ROW0 - seq
    chip = 2 * lax.axis_index("x") + lax.axis_index("y")

    tr = lambda a: jnp.swapaxes(a, 1, 2)
    big = dict(w_in=tr(w_in), pool_w_up=pool_w_up, w_q_b=w_q_b, w_kv_b=w_kv_b, mla_w_up=mla_w_up, w_out=w_out)
    row_sharded = ("w_in", "w_q_b", "w_out")
    names = list(big)
    shards = [[big[n][l].astype(BF16) for n in names] for l in range(DEPTH)]
    from_shards = lambda n: _rows_from_shards if n in row_sharded else _cols_from_shards
    to_shards = lambda n: _rows_to_shards if n in row_sharded else _cols_to_shards

    def in_weights(g_w_in):
        return dict(w_pad=_pad_w_in(from_shards("w_in")(g_w_in)))

    def rest_weights(gathered):
        w = {n: from_shards(n)(g) for n, g in zip(names[1:], gathered)}
        wkv = w["w_kv_b"].reshape(KV_RANK, N_HEADS, NOPE + V_DIM)
        return dict(wq=_pad_heads(w["w_q_b"], QK_DIM),
                    wkn=_pad_heads(wkv[..., :NOPE].reshape(KV_RANK, N_HEADS * NOPE), NOPE),
                    wv=wkv[..., NOPE:].reshape(KV_RANK, MLA_WIDTH),
                    wpu=w["pool_w_up"], wmu=w["mla_w_up"], wout=w["w_out"])

    g_in0, meta_g = _chip_exchange([shards[0][0], meta_tokens], scatter=False, name="gather_layer0")
    weights = [in_weights(g_in0)]
    meta_full = _cols_from_shards(meta_g)
    wg = pool_w_group.astype(BF16)
    gqn = jnp.pad(q_norm_gain, ((0, 0), (0, HEAD_PAD - QK_DIM)))
    gkn = jnp.pad(k_norm_gain, ((0, 0), (0, HEAD_PAD - QK_DIM)))

    x_pad = jnp.concatenate([jnp.zeros((PAD_FRONT, D_MODEL), F32), meta_full, x[0], jnp.zeros((pad_back, D_MODEL), F32)], axis=0)
    pos_pad = jnp.concatenate([jnp.zeros((PAD_FRONT,), jnp.int32), jnp.arange(N_META, dtype=jnp.int32),
                               positions[0] + N_META, jnp.zeros((pad_back,), jnp.int32)])
    half = ROPE // 2
    inv_freq = (ROPE_THETA ** (-np.arange(half, dtype=np.float32) / half)).astype(np.float32)
    freq_row = np.zeros((1, HEAD_PAD), np.float32)
    freq_row[0, NOPE:NOPE + half] = inv_freq
    freq_row[0, NOPE + half:QK_DIM] = inv_freq
    tabs = _rope_tables(pos_pad[:, None], jnp.asarray(freq_row))

    row = lambda a, l: a[l][None, :]

    saved = []
    h_res = x_pad
    for l in range(DEPTH):
        w = weights[l]
        (u, zp, cq, ckv, zm, gp, gm, kr), rest0 = _inproj_fwd(h_res, row(norm_gain, l), w["w_pad"],
                                                              gather=shards[0][1:] if l == 0 else ())
        if rest0:
            w.update(rest_weights(rest0))
        a_pool = _pool_fwd(u, zp, wg[l], row(pool_scale, l))
        q, k, v = _mla_prep_fwd(cq, ckv, kr, tabs, row(q_a_norm_gain, l), row(kv_a_norm_gain, l), row(gqn, l), row(gkn, l),
                                w["wq"], w["wkn"], w["wv"])
        o, lse, nxt = _flash_fwd(q, k, v, gather=shards[l + 1] if l + 1 < DEPTH else ())
        if nxt:
            weights.append({**in_weights(nxt[0]), **rest_weights(nxt[1:])})
        h_next, yp, ym = _merge_fwd(h_res, a_pool, o, zm, gp, gm, w["wpu"], w["wmu"], w["wout"])
        saved.append(dict(x=h_res, u=u, zp=zp, cq=cq, ckv=ckv, zm=zm, gp=gp, gm=gm, kr=kr, a_pool=a_pool, q=q, k=k, v=v,
                          o=o, lse=lse, yp=yp, ym=ym))
        h_res = h_next
    dres, loss_blk = _loss_head(h_res, loss_target[0])

    gw = {n: [None] * DEPTH for n in names}
    pieces = [None] * DEPTH
    grad_stacks = lambda l, which=names: [to_shards(n)(gw[n][l]).astype(BF16) for n in which]
    gs = {n: [None] * DEPTH for n in ("norm_gain", "pool_w_group", "pool_scale", "q_a", "kv_a", "q_norm", "k_norm")}
    for l in reversed(range(DEPTH)):
        s, w = saved[l], weights[l]
        merged, dyp, dym, dgp, dgm, dap, amla, do, dzm, delta = _merge_bwd(
            dres, s["yp"], s["ym"], s["gp"], s["gm"], s["o"], s["zm"], w["wout"], w["wpu"], w["wmu"])
        (gw["w_out"][l],) = _weight_grads(merged, [dres], "grad_w_out")
        (gw["pool_w_up"][l],) = _weight_grads(s["a_pool"], [dyp], "grad_pool_w_up")
        (gw["mla_w_up"][l],) = _weight_grads(amla, [dym], "grad_mla_w_up")
        dq, dk, dv, got = _flash_bwd(s["q"], s["k"], s["v"], do, s["lse"], delta,
                                     scatter=grad_stacks(l + 1) if l + 1 < DEPTH else ())
        if got:
            pieces[l + 1] = got
        dcq, dckv, dkr, dwq, dwkn, dwv, gs["q_a"][l], gs["kv_a"][l], dgqn, dgkn = _mla_prep_bwd(
            dq, dk, dv, s["cq"], s["ckv"], s["kr"], tabs, row(q_a_norm_gain, l), row(kv_a_norm_gain, l), row(gqn, l), row(gkn, l),
            w["wq"], w["wkn"], w["wv"])
        gs["q_norm"][l] = dgqn[:, :QK_DIM]
        gs["k_norm"][l] = dgkn[:, :QK_DIM]
        gw["w_q_b"][l] = dwq.reshape(Q_RANK, N_HEADS, HEAD_PAD)[..., :QK_DIM].reshape(Q_RANK, N_HEADS * QK_DIM)
        gw["w_kv_b"][l] = jnp.concatenate([dwkn.reshape(KV_RANK, N_HEADS, HEAD_PAD)[..., :NOPE],
                                           dwv.reshape(KV_RANK, N_HEADS, V_DIM)], axis=-1).reshape(KV_RANK, N_HEADS * (NOPE + V_DIM))
        du, dzp, gs["pool_w_group"][l], gs["pool_scale"][l] = _pool_bwd(dap, s["u"], s["zp"], wg[l], row(pool_scale, l))
        dparts = [du, dzp, dcq, dckv, dzm, dgp, dgm, dkr]
        dres, h, gs["norm_gain"][l], rest0 = _inproj_bwd(dres, s["x"], row(norm_gain, l), w["w_pad"], dparts,
                                                         scatter=grad_stacks(0, names[1:]) if l == 0 else ())
        ga = _weight_grads(h, [du, dzp, dcq, dckv, dkr], "grad_w_in_a", transposed=True)
        gb = _weight_grads(h, [dzm, dgp, dgm], "grad_w_in_b", transposed=True)
        gw["w_in"][l] = _unpad_w_in([ga[0], ga[1], ga[2], ga[3], gb[0], gb[1], gb[2], ga[4]])
    grad_x = dres[ROW0:ROW0 + seq][None]

    pieces[0] = list(_chip_exchange(grad_stacks(0, names[:1]), scatter=True, name="scatter_layer0")) + list(rest0)
    sums = [_sum_pieces(jnp.stack([pieces[l][a] for l in range(DEPTH)], axis=1), "sum_" + n) for a, n in enumerate(names)]
    other = _sibling_exchange(sums, name="swap_core_sums")
    moments = dict(w_in=(tr(m_w_in), tr(v_w_in)), pool_w_up=(m_pool_w_up, v_pool_w_up), w_q_b=(m_w_q_b, v_w_q_b),
                   w_kv_b=(m_w_kv_b, v_w_kv_b), mla_w_up=(m_mla_w_up, v_mla_w_up), w_out=(m_w_out, v_w_out))
    big_out = {n: _adamw(big[n], [sm, ot], moments[n][0], moments[n][1], "adamw_" + n)
               for n, sm, ot in zip(names, sums, other)}

    small_names = ("norm_gain", "pool_w_group", "pool_scale", "q_a", "kv_a", "q_norm", "k_norm")
    small_w = dict(norm_gain=(norm_gain, m_norm_gain, v_norm_gain), pool_w_group=(pool_w_group, m_pool_w_group, v_pool_w_group),
                   pool_scale=(pool_scale, m_pool_scale, v_pool_scale), q_a=(q_a_norm_gain, m_q_a_norm_gain, v_q_a_norm_gain),
                   kv_a=(kv_a_norm_gain, m_kv_a_norm_gain, v_kv_a_norm_gain), q_norm=(q_norm_gain, m_q_norm_gain, v_q_norm_gain),
                   k_norm=(k_norm_gain, m_k_norm_gain, v_k_norm_gain))
    small_g = {n: jnp.stack(gs[n]).reshape(small_w[n][0].shape) for n in small_names}
    blocks = [_flat_rows(small_g[n]) for n in small_names]
    n_rows = [b.shape[0] for b in blocks]
    meta_rows = N_META * D_MODEL // 128
    pack = jnp.concatenate(blocks + [dres[PAD_FRONT:ROW0].reshape(meta_rows, 128), loss_blk], axis=0)
    pack = jnp.pad(pack, ((0, (-pack.shape[0]) % 8), (0, 0)))
    total = _all_reduce_small(pack)
    n_small = sum(n_rows)
    loss = total[n_small + meta_rows, 0]
    gmeta = lax.dynamic_slice_in_dim(total[n_small:n_small + meta_rows].reshape(N_META, D_MODEL), chip * (D_MODEL // N_CHIPS),
                                     D_MODEL // N_CHIPS, axis=1)

    def packed(idx, meta_part):
        p = jnp.concatenate([_flat_rows(small_w[n][idx]) for n in small_names] + [_flat_rows(meta_part)], axis=0)
        return jnp.pad(p, ((0, (-p.shape[0]) % 8), (0, 0)))[None]

    g_pack = jnp.concatenate([total[:n_small], _flat_rows(gmeta)], axis=0)
    g_pack = jnp.pad(g_pack, ((0, (-g_pack.shape[0]) % 8), (0, 0)))[None]
    small_out = _adamw(packed(0, meta_tokens), [g_pack], packed(1, m_meta_tokens), packed(2, v_meta_tokens), "adamw_small")

    def unpack(p):
        res, r0 = {}, 0
        for n, nr in zip(small_names, n_rows):
            shape = small_w[n][0].shape
            res[n] = p[0, r0:r0 + nr].reshape(-1)[:math.prod(shape)].reshape(shape)
            r0 += nr
        res["meta"] = p[0, r0:r0 + N_META * (D_MODEL // N_CHIPS) // 128].reshape(N_META, D_MODEL // N_CHIPS)
        return res

    small_res = [unpack(p) for p in small_out]

    def leaf(kind, name):
        key = {"meta_tokens": "meta", "q_a_norm_gain": "q_a", "kv_a_norm_gain": "kv_a", "q_norm_gain": "q_norm",
               "k_norm_gain": "k_norm"}.get(name, name)
        if name in big_out:
            return tr(big_out[name][kind]) if name == "w_in" else big_out[name][kind]
        return small_res[kind][key]

    order = ("meta_tokens", "norm_gain", "w_in", "pool_w_group", "pool_scale", "pool_w_up", "q_a_norm_gain", "kv_a_norm_gain",
             "w_q_b", "w_kv_b", "q_norm_gain", "k_norm_gain", "mla_w_up", "w_out")
    outs = [loss, grad_x]
    for kind in range(4):
        outs += [leaf(kind, n) for n in order]
    return tuple(outs)
```

```python
import functools
import math

import numpy as np
import jax
import jax.numpy as jnp
from jax import lax
from jax.experimental import pallas as pl
from jax.experimental.pallas import tpu as pltpu

F32 = jnp.float32
BF16 = jnp.bfloat16
MESH = pl.DeviceIdType.MESH

D_MODEL = 1024
DEPTH = 4
N_META = 16
POOL_WIDTH = 512
POOL_WINDOWS = (2, 4, 8, 16)
POOL_GROUPS = 4
GROUP_DIM = 128
N_HEADS = 8
NOPE = 64
ROPE = 32
QK_DIM = 96
V_DIM = 64
MLA_WIDTH = 512
KV_RANK = 256
Q_RANK = 768
ROPE_THETA = 10000.0
EPS = 1e-6
MASK_VALUE = -1e30
ATTN_BLOCK = 128
PAD_FRONT = (-N_META) % ATTN_BLOCK
ROW0 = PAD_FRONT + N_META
HEAD_PAD = 128
HALO = 16
N_CHIPS = 4
N_DEV = 8

IN_NAMES = ("u", "zp", "cq", "ckv", "zm", "gp", "gm", "kr")
IN_WIDTHS = (512, 512, 768, 256, 512, 1024, 1024, 128)
IN_OFFS = tuple(int(v) for v in np.cumsum((0,) + IN_WIDTHS[:-1]))
IN_PAD = sum(IN_WIDTHS)
KR_LANE0 = NOPE

ADAM_LR = 0.001
ADAM_B1 = 0.9
ADAM_B2 = 0.999
ADAM_EPS = 1e-08
ADAM_WD = 0.01
ADAM_STEP = 10

VMEM_LIMIT = 56 * 1024 * 1024
ATTN_TILE = 768
ROW_TILES = (768, 384)
ROW_TILES_HEAVY = (384,)
ROW_BLOCK = 32
LOSS_GROUP = 3
FWD_PAIRS = 4
SCORE_BUFS = 4
BWD_PAIRS = 2
LOG2E = 1.4426950408889634
LN2 = 0.6931471805599453
Q_PRESCALE = LOG2E / math.sqrt(QK_DIM)


def _cp(sem=None, vmem=VMEM_LIMIT):
    kw = dict(vmem_limit_bytes=vmem)
    if sem is not None:
        kw["dimension_semantics"] = sem
    return pltpu.CompilerParams(**kw)


def _row_tile(n_rows, prefs=None):
    for t in prefs or ROW_TILES:
        if n_rows % t == 0:
            return t
    raise ValueError(f"no row tile for {n_rows}")


def _nt(a, b):
    return lax.dot_general(a, b, (((1,), (1,)), ((), ())), preferred_element_type=F32)


def _tn(a, b):
    return lax.dot_general(a, b, (((0,), (0,)), ((), ())), preferred_element_type=F32)


def _mm(a, b):
    return jnp.dot(a, b, preferred_element_type=F32)


def _sigmoid(x):
    return 0.5 * jnp.tanh(0.5 * x) + 0.5


def _resident(shape):
    nd = len(shape)
    return pl.BlockSpec(shape, lambda *_: (0,) * nd, pipeline_mode=pl.Buffered(1))


def _rows(tm, width):
    return pl.BlockSpec((tm, width), lambda i: (i, 0))


def _rope_tables(pos_col, inv_freq_row):
    lp = pos_col.shape[0]
    tm = _row_tile(lp)

    def body(p_ref, f_ref, c_ref, s1_ref, s2_ref):
        ang = p_ref[...].astype(F32) * f_ref[...]
        lane = lax.broadcasted_iota(jnp.int32, ang.shape, 1)
        cs = jnp.cos(ang)
        sn = jnp.sin(ang)
        c_ref[...] = jnp.where(lane < NOPE, 1.0, jnp.where(lane < QK_DIM, cs, 0.0))
        s1_ref[...] = jnp.where((lane >= NOPE) & (lane < NOPE + ROPE // 2), -sn, 0.0)
        s2_ref[...] = jnp.where((lane >= NOPE + ROPE // 2) & (lane < QK_DIM), sn, 0.0)

    out = jax.ShapeDtypeStruct((lp, HEAD_PAD), F32)
    return pl.pallas_call(
        body, name="rope_tables", grid=(lp // tm,),
        in_specs=[pl.BlockSpec((tm, 1), lambda i: (i, 0)), pl.BlockSpec((1, HEAD_PAD), lambda i: (0, 0))],
        out_specs=[_rows(tm, HEAD_PAD)] * 3, out_shape=[out] * 3,
        compiler_params=_cp(("parallel",)),
    )(pos_col, inv_freq_row)


def _rope(y, c, s1, s2):
    return y * c + pltpu.roll(y, HEAD_PAD - ROPE // 2, 1) * s1 + pltpu.roll(y, ROPE // 2, 1) * s2


def _rope_t(g, c, s1, s2):
    return g * c + pltpu.roll(g * s1, ROPE // 2, 1) + pltpu.roll(g * s2, HEAD_PAD - ROPE // 2, 1)


def _inproj_fwd(x, gain, w_pad, gather=()):
    lp = x.shape[0]
    tm = _row_tile(lp)
    n_out = len(IN_WIDTHS)
    ex = _ChipExchange(list(gather), scatter=False)

    def body(x_ref, g_ref, w_ref, *rest):
        ex_in, outs, ex_out, ex_sems = rest[:ex.n], rest[ex.n:ex.n + n_out], rest[ex.n + n_out:2 * ex.n + n_out], rest[2 * ex.n + n_out:]
        if ex.n:
            pl.when(pl.program_id(0) == 0)(lambda: ex.start(ex_in, ex_out, ex_sems))
        xf = x_ref[...]
        inv = lax.rsqrt(jnp.mean(xf * xf, axis=-1, keepdims=True) + EPS)
        h = (xf * inv * g_ref[...]).astype(BF16)
        for o_ref, off, wd in zip(outs, IN_OFFS, IN_WIDTHS):
            o_ref[...] = _nt(h, w_ref[off:off + wd, :]).astype(o_ref.dtype)
        if ex.n:
            pl.when(pl.program_id(0) == lp // tm - 1)(lambda: ex.wait(ex_in, ex_out, ex_sems))

    res = pl.pallas_call(
        body, name="inproj_fwd_gather" if ex.n else "inproj_fwd", grid=(lp // tm,),
        in_specs=[_rows(tm, D_MODEL), pl.BlockSpec((1, D_MODEL), lambda i: (0, 0)), _resident((IN_PAD, D_MODEL))] + ex.specs,
        out_specs=[_rows(tm, wd) for wd in IN_WIDTHS] + ex.specs,
        out_shape=[jax.ShapeDtypeStruct((lp, wd), BF16) for wd in IN_WIDTHS] + ex.out_shape,
        scratch_shapes=ex.sems if ex.n else [],
        compiler_params=_cp(("arbitrary",)),
    )(x, gain, w_pad, *gather)
    return res[:n_out], res[n_out:]


def _inv_counts(tile_idx, tm):
    row = tile_idx * tm + lax.broadcasted_iota(jnp.int32, (tm, 1), 0)
    t1 = jnp.maximum(row - PAD_FRONT + 1, 1).astype(F32)
    return [1.0 / jnp.minimum(t1, float(w)) for w in POOL_WINDOWS]


def _window_sums(e, shift):
    s2 = e + shift(e, 1)
    s4 = s2[:, GROUP_DIM:] + shift(s2[:, GROUP_DIM:], 2)
    s8 = s4[:, GROUP_DIM:] + shift(s4[:, GROUP_DIM:], 4)
    s16 = s8[:, GROUP_DIM:] + shift(s8[:, GROUP_DIM:], 8)
    return (s2[:, :GROUP_DIM], s4[:, :GROUP_DIM], s8[:, :GROUP_DIM], s16)


def _trailing_sums(e):
    return _window_sums(e, lambda x, s: pltpu.roll(x, s, 0))


def _leading_sums(e):
    return _window_sums(e, lambda x, s: pltpu.roll(x, x.shape[0] - s, 0))


def _pool_fwd(u, zp, wg, scale):
    lp = u.shape[0]
    tm = _row_tile(lp)

    def body(u_ref, z_ref, wg_ref, sc_ref, a_ref, ext_ref):
        i = pl.program_id(0)

        @pl.when(i == 0)
        def _():
            ext_ref[0:HALO, :] = jnp.zeros((HALO, POOL_WIDTH), F32)

        ext_ref[HALO:HALO + tm, :] = u_ref[...].astype(F32)
        e = ext_ref[...]
        sums = _trailing_sums(e)
        ext_ref[0:HALO, :] = e[tm:tm + HALO, :]
        inv_cnt = _inv_counts(i, tm)
        for g in range(POOL_GROUPS):
            cols = slice(g * GROUP_DIM, (g + 1) * GROUP_DIM)
            mixed = sums[g][HALO:, :] * inv_cnt[g] - e[HALO:, cols]
            y = _mm(mixed.astype(BF16), wg_ref[g]) * sc_ref[:, cols]
            zf = z_ref[:, cols].astype(F32)
            a_ref[:, cols] = (y * (zf * _sigmoid(zf))).astype(a_ref.dtype)

    return pl.pallas_call(
        body, name="pool_fwd", grid=(lp // tm,),
        in_specs=[_rows(tm, POOL_WIDTH), _rows(tm, POOL_WIDTH),
                  pl.BlockSpec((POOL_GROUPS, GROUP_DIM, GROUP_DIM), lambda i: (0, 0, 0)),
                  pl.BlockSpec((1, POOL_WIDTH), lambda i: (0, 0))],
        out_specs=_rows(tm, POOL_WIDTH), out_shape=jax.ShapeDtypeStruct((lp, POOL_WIDTH), BF16),
        scratch_shapes=[pltpu.VMEM((HALO + tm, POOL_WIDTH), F32)],
        compiler_params=_cp(("arbitrary",)),
    )(u, zp, wg, scale)


def _rms_fwd(xf, gain):
    inv = lax.rsqrt(jnp.mean(xf * xf, axis=-1, keepdims=True) + EPS)
    xhat = xf * inv
    return inv, xhat, xhat * gain


def _rms_bwd(dy, inv, xhat, gain):
    dgain = jnp.sum(dy * xhat, axis=0, keepdims=True)
    dyg = dy * gain
    dx = inv * (dyg - xhat * jnp.mean(dyg * xhat, axis=-1, keepdims=True))
    return dx, dgain


def _head_norm_fwd(xh, gain128):
    inv = lax.rsqrt(jnp.sum(xh * xh, axis=-1, keepdims=True) * (1.0 / QK_DIM) + EPS)
    xhat = xh * inv
    return inv, xhat, xhat * gain128


def _head_norm_bwd(dy, inv, xhat, gain128):
    dyg = dy * gain128
    return inv * (dyg - xhat * (jnp.sum(dyg * xhat, axis=-1, keepdims=True) * (1.0 / QK_DIM)))


def _mla_prep_fwd(cq, ckv, kr, tabs, gqa, gkva, gqn, gkn, wq, wkn, wv):
    lp = cq.shape[0]
    tm = _row_tile(lp)

    def body(cq_ref, ckv_ref, kr_ref, c_ref, s1_ref, s2_ref, gqa_ref, gkva_ref, gqn_ref, gkn_ref,
             wq_ref, wkn_ref, wv_ref, q_ref, k_ref, v_ref):
        c, s1, s2 = c_ref[...], s1_ref[...], s2_ref[...]
        _, _, cqn = _rms_fwd(cq_ref[...].astype(F32), gqa_ref[...])
        qraw = _mm(cqn.astype(BF16), wq_ref[...])
        for h in range(N_HEADS):
            hb = slice(h * HEAD_PAD, (h + 1) * HEAD_PAD)
            _, _, yh = _head_norm_fwd(qraw[:, hb], gqn_ref[...])
            q_ref[:, hb] = (_rope(yh, c, s1, s2) * Q_PRESCALE).astype(q_ref.dtype)
        _, _, ckvn = _rms_fwd(ckv_ref[...].astype(F32), gkva_ref[...])
        ckvn_b = ckvn.astype(BF16)
        knraw = _mm(ckvn_b, wkn_ref[...])
        krs = kr_ref[...].astype(F32)
        for h in range(N_HEADS):
            hb = slice(h * HEAD_PAD, (h + 1) * HEAD_PAD)
            _, _, yh = _head_norm_fwd(knraw[:, hb] + krs, gkn_ref[...])
            k_ref[:, hb] = _rope(yh, c, s1, s2).astype(k_ref.dtype)
        v_ref[...] = _mm(ckvn_b, wv_ref[...]).astype(v_ref.dtype)

    hw = N_HEADS * HEAD_PAD
    vec = lambda n: pl.BlockSpec((1, n), lambda i: (0, 0))
    return pl.pallas_call(
        body, name="mla_prep_fwd", grid=(lp // tm,),
        in_specs=[_rows(tm, Q_RANK), _rows(tm, KV_RANK), _rows(tm, HEAD_PAD)] + [_rows(tm, HEAD_PAD)] * 3
        + [vec(Q_RANK), vec(KV_RANK), vec(HEAD_PAD), vec(HEAD_PAD),
           _resident((Q_RANK, hw)), _resident((KV_RANK, hw)), _resident((KV_RANK, MLA_WIDTH))],
        out_specs=[_rows(tm, hw), _rows(tm, hw), _rows(tm, MLA_WIDTH)],
        out_shape=[jax.ShapeDtypeStruct((lp, hw), BF16), jax.ShapeDtypeStruct((lp, hw), BF16),
                   jax.ShapeDtypeStruct((lp, MLA_WIDTH), BF16)],
        compiler_params=_cp(("parallel",)),
    )(cq, ckv, kr, *tabs, gqa, gkva, gqn, gkn, wq, wkn, wv)


def _causal_mask(s, q0, k0):
    qi = q0 + lax.broadcasted_iota(jnp.int32, s.shape, 0)
    ki = k0 + lax.broadcasted_iota(jnp.int32, s.shape, 1)
    return jnp.where((ki <= qi) & (ki >= PAD_FRONT), s, MASK_VALUE)


def _score_chunks(kind, r, tk):
    if kind == "inner":
        return [(c0, False) for c0 in range(0, tk, ATTN_BLOCK)]
    if kind == "first":
        return [(c0, c0 == 0) for c0 in range(0, tk, ATTN_BLOCK)]
    return [(c0, True) for c0 in range(0, min(tk, (r + 1) * ROW_BLOCK), ATTN_BLOCK)]


def _tile_kinds(i, t):
    return (("diag", t == i), ("first", (t == 0) & (i > 0)), ("inner", (t > 0) & (t < i)))


def _lanes(col, width=HEAD_PAD):
    return jnp.broadcast_to(col, (col.shape[0], width))


def _flash_fwd(q, k, v, gather=()):
    lp = q.shape[0]
    tq = tk = ATTN_TILE
    nq = lp // tq
    nj = N_HEADS // (2 * FWD_PAIRS)
    heads = 2 * FWD_PAIRS
    n_blocks = tq // ROW_BLOCK
    ex = _ChipExchange(list(gather), scatter=False)

    pairs = [(i, t) for i in range(nq) for t in range(i + 1)]
    i_tab = jnp.asarray([p[0] for p in pairs], jnp.int32)
    t_tab = jnp.asarray([p[1] for p in pairs], jnp.int32)

    def body(i_tab_ref, t_tab_ref, q_ref, k_ref, v_ref, *rest):
        ex_in, (o_ref, lse_ref), ex_out = rest[:ex.n], rest[ex.n:ex.n + 2], rest[ex.n + 2:2 * ex.n + 2]
        m_scr, acc_scr, s_scr, p_scr, part_scr, vext_scr = rest[2 * ex.n + 2:2 * ex.n + 8]
        ex_sems = rest[2 * ex.n + 8:]
        j, step_no = pl.program_id(0), pl.program_id(1)
        i, t = i_tab_ref[step_no], t_tab_ref[step_no]
        if ex.n:
            pl.when((j == 0) & (step_no == 0))(lambda: ex.start(ex_in, ex_out, ex_sems))

        @pl.when(t == 0)
        def _():
            m_scr[...] = jnp.full(m_scr.shape, MASK_VALUE, F32)
            acc_scr[...] = jnp.zeros(acc_scr.shape, F32)

        def step(kind):
            def scores(hh, r, c0, masked):
                s = s_scr[hh % SCORE_BUFS, r * ROW_BLOCK:(r + 1) * ROW_BLOCK, c0:c0 + ATTN_BLOCK]
                return _causal_mask(s, i * tq + r * ROW_BLOCK, t * tk + c0) if masked else s

            for pp in range(FWD_PAIRS):
                vext_scr[pp, :, 0:HEAD_PAD] = v_ref[:, pp * HEAD_PAD:(pp + 1) * HEAD_PAD]
                vext_scr[pp, :, HEAD_PAD:2 * HEAD_PAD] = jnp.ones((tk, HEAD_PAD), BF16)
            half = tq // 2
            order = []
            for g0 in range(0, heads, SCORE_BUFS):
                group = range(g0, min(g0 + SCORE_BUFS, heads))
                order += [("scores", hh) for hh in group] + [("softmax", hh) for hh in group]
            for what, hh in order:
                sb = hh % SCORE_BUFS
                if what == "scores":
                    hb = slice(hh * HEAD_PAD, (hh + 1) * HEAD_PAD)
                    if kind == "diag":
                        s_scr[sb, 0:half, 0:half] = _nt(q_ref[0:half, hb], k_ref[0:half, hb])
                        s_scr[sb, half:tq, :] = _nt(q_ref[half:tq, hb], k_ref[:, hb])
                    else:
                        s_scr[sb] = _nt(q_ref[:, hb], k_ref[:, hb])
                    continue
                for r in range(n_blocks):
                    part = None
                    for c0, masked in _score_chunks(kind, r, tk):
                        s = scores(hh, r, c0, masked)
                        part = s if part is None else jnp.maximum(part, s)
                    part_scr[r * ROW_BLOCK:(r + 1) * ROW_BLOCK, :] = part
                m_prev = m_scr[hh]
                m_new = jnp.maximum(m_prev, _lanes(jnp.max(part_scr[...], axis=-1, keepdims=True)))
                alpha = jnp.exp2(m_prev - m_new)
                m_scr[hh] = m_new
                for r in range(n_blocks):
                    rows = slice(r * ROW_BLOCK, (r + 1) * ROW_BLOCK)
                    m_r = m_scr[hh, rows, :]
                    chunks = _score_chunks(kind, r, tk)
                    for c0, masked in chunks:
                        p_scr[sb, rows, c0:c0 + ATTN_BLOCK] = jnp.exp2((scores(hh, r, c0, masked) - m_r).astype(BF16))
                    done = chunks[-1][0] + ATTN_BLOCK
                    if done < tk:
                        p_scr[sb, rows, done:tk] = jnp.zeros((ROW_BLOCK, tk - done), BF16)
                alpha2 = jnp.concatenate([alpha, alpha], axis=1)
                if kind == "diag":
                    acc_scr[hh, 0:half, :] = (alpha2[0:half] * acc_scr[hh, 0:half, :]
                                              + _mm(p_scr[sb, 0:half, 0:half], vext_scr[hh // 2, 0:half, :]))
                    acc_scr[hh, half:tq, :] = (alpha2[half:tq] * acc_scr[hh, half:tq, :]
                                               + _mm(p_scr[sb, half:tq, :], vext_scr[hh // 2]))
                else:
                    acc_scr[hh] = alpha2 * acc_scr[hh] + _mm(p_scr[sb], vext_scr[hh // 2])

        for kind, pred in _tile_kinds(i, t):
            pl.when(pred)(functools.partial(step, kind))

        @pl.when(t == i)
        def _():
            lane = lax.broadcasted_iota(jnp.int32, (tq, HEAD_PAD), 1)
            for pp in range(FWD_PAIRS):
                pb = slice(pp * HEAD_PAD, (pp + 1) * HEAD_PAD)
                h0, h1 = 2 * pp, 2 * pp + 1
                l0, l1 = acc_scr[h0, :, HEAD_PAD:2 * HEAD_PAD], acc_scr[h1, :, HEAD_PAD:2 * HEAD_PAD]
                o = jnp.where(lane < V_DIM, acc_scr[h0, :, 0:HEAD_PAD] / l0, acc_scr[h1, :, 0:HEAD_PAD] / l1)
                o_ref[:, pb] = o.astype(o_ref.dtype)
                lse_ref[:, pb] = jnp.where(lane < V_DIM, m_scr[h0] + jnp.log(l0) * LOG2E, m_scr[h1] + jnp.log(l1) * LOG2E)

        if ex.n:
            pl.when((j == nj - 1) & (step_no == len(pairs) - 1))(lambda: ex.wait(ex_in, ex_out, ex_sems))

    q_idx = lambda j, p, it, tt: (it[p], j)
    kv_idx = lambda j, p, it, tt: (tt[p], j)
    o, lse, *gathered = pl.pallas_call(
        body, name="flash_fwd_gather" if ex.n else "flash_fwd",
        grid_spec=pltpu.PrefetchScalarGridSpec(
            num_scalar_prefetch=2, grid=(nj, len(pairs)),
            in_specs=[pl.BlockSpec((tq, heads * HEAD_PAD), q_idx), pl.BlockSpec((tk, heads * HEAD_PAD), kv_idx),
                      pl.BlockSpec((tk, FWD_PAIRS * HEAD_PAD), kv_idx)] + ex.specs,
            out_specs=[pl.BlockSpec((tq, FWD_PAIRS * HEAD_PAD), q_idx)] * 2 + ex.specs,
            scratch_shapes=[pltpu.VMEM((heads, tq, HEAD_PAD), F32), pltpu.VMEM((heads, tq, 2 * HEAD_PAD), F32),
                            pltpu.VMEM((SCORE_BUFS, tq, tk), F32), pltpu.VMEM((SCORE_BUFS, tq, tk), BF16),
                            pltpu.VMEM((tq, HEAD_PAD), F32), pltpu.VMEM((FWD_PAIRS, tk, 2 * HEAD_PAD), BF16)]
            + (ex.sems if ex.n else [])),
        out_shape=[jax.ShapeDtypeStruct((lp, MLA_WIDTH), BF16), jax.ShapeDtypeStruct((lp, MLA_WIDTH), F32)] + ex.out_shape,
        compiler_params=_cp(("arbitrary",) * 2),
    )(i_tab, t_tab, q, k, v, *gather)
    return o, lse, gathered


def _merge_fwd(x, a_pool, o, zm, gp, gm, wpu, wmu, wout):
    lp = x.shape[0]
    tm = _row_tile(lp)

    def body(x_ref, ap_ref, o_ref, zm_ref, gp_ref, gm_ref, wpu_ref, wmu_ref, wout_ref, xn_ref, yp_ref, ym_ref):
        yp = _mm(ap_ref[...], wpu_ref[...])
        zf = zm_ref[...].astype(F32)
        amla = o_ref[...].astype(F32) * (zf * _sigmoid(zf))
        ym = _mm(amla.astype(BF16), wmu_ref[...])
        merged = _sigmoid(gp_ref[...].astype(F32)) * yp + _sigmoid(gm_ref[...].astype(F32)) * ym
        xn_ref[...] = x_ref[...] + _mm(merged.astype(BF16), wout_ref[...])
        yp_ref[...] = yp.astype(yp_ref.dtype)
        ym_ref[...] = ym.astype(ym_ref.dtype)

    return pl.pallas_call(
        body, name="merge_fwd", grid=(lp // tm,),
        in_specs=[_rows(tm, D_MODEL), _rows(tm, POOL_WIDTH), _rows(tm, MLA_WIDTH), _rows(tm, MLA_WIDTH),
                  _rows(tm, D_MODEL), _rows(tm, D_MODEL),
                  _resident((POOL_WIDTH, D_MODEL)), _resident((MLA_WIDTH, D_MODEL)), _resident((D_MODEL, D_MODEL))],
        out_specs=[_rows(tm, D_MODEL)] * 3,
        out_shape=[jax.ShapeDtypeStruct((lp, D_MODEL), F32), jax.ShapeDtypeStruct((lp, D_MODEL), BF16),
                   jax.ShapeDtypeStruct((lp, D_MODEL), BF16)],
        compiler_params=_cp(("parallel",)),
    )(x, a_pool, o, zm, gp, gm, wpu, wmu, wout)


def _loss_head(y, target):
    lp = y.shape[0]
    blk = ROW0
    n_real = target.shape[0] // blk
    assert (lp // blk) % LOSS_GROUP == 0

    def body(y_ref, *rest):
        t_refs, (d_ref, l_ref) = rest[:LOSS_GROUP], rest[LOSS_GROUP:]
        i = pl.program_id(0)

        @pl.when(i == 0)
        def _():
            l_ref[...] = jnp.zeros(l_ref.shape, F32)

        total = jnp.zeros((), F32)
        for g in range(LOSS_GROUP):
            b = i * LOSS_GROUP + g
            rows = slice(g * blk, (g + 1) * blk)
            err = jnp.where((b >= 1) & (b <= n_real), y_ref[rows, :] - t_refs[g][...], 0.0)
            d_ref[rows, :] = err * (1.0 / D_MODEL)
            total = total + jnp.sum(err * err)
        l_ref[...] += total * (0.5 / D_MODEL)

    def target_block(g):
        return pl.BlockSpec((blk, D_MODEL), lambda i: (jnp.clip(i * LOSS_GROUP + g - 1, 0, n_real - 1), 0))

    return pl.pallas_call(
        body, name="loss_head", grid=(lp // (blk * LOSS_GROUP),),
        in_specs=[_rows(blk * LOSS_GROUP, D_MODEL)] + [target_block(g) for g in range(LOSS_GROUP)],
        out_specs=[_rows(blk * LOSS_GROUP, D_MODEL), pl.BlockSpec((8, 128), lambda i: (0, 0))],
        out_shape=[jax.ShapeDtypeStruct((lp, D_MODEL), F32), jax.ShapeDtypeStruct((8, 128), F32)],
        compiler_params=_cp(("arbitrary",)),
    )(y, *([target] * LOSS_GROUP))


def _pair_rowsum(prod):
    lane = lax.broadcasted_iota(jnp.int32, prod.shape, 1)
    lo = jnp.sum(jnp.where(lane < V_DIM, prod, 0.0), axis=-1, keepdims=True)
    hi = jnp.sum(jnp.where(lane < V_DIM, 0.0, prod), axis=-1, keepdims=True)
    return jnp.where(lane < V_DIM, lo, hi)


def _merge_bwd(dres, yp, ym, gp, gm, o, zm, wout, wpu, wmu):
    lp = dres.shape[0]
    tm = _row_tile(lp, ROW_TILES_HEAVY)

    def body(dres_ref, yp_ref, ym_ref, gp_ref, gm_ref, o_ref, zm_ref, wout_ref, wpu_ref, wmu_ref,
             merged_ref, dyp_ref, dym_ref, dgp_ref, dgm_ref, dap_ref, amla_ref, do_ref, dzm_ref, delta_ref):
        dmerged = _nt(dres_ref[...].astype(BF16), wout_ref[...])
        sp = _sigmoid(gp_ref[...].astype(F32))
        sm = _sigmoid(gm_ref[...].astype(F32))
        ypf = yp_ref[...].astype(F32)
        ymf = ym_ref[...].astype(F32)
        merged_ref[...] = (sp * ypf + sm * ymf).astype(merged_ref.dtype)
        dyp = (dmerged * sp).astype(BF16)
        dym = (dmerged * sm).astype(BF16)
        dyp_ref[...] = dyp
        dym_ref[...] = dym
        dgp_ref[...] = (dmerged * ypf * sp * (1.0 - sp)).astype(dgp_ref.dtype)
        dgm_ref[...] = (dmerged * ymf * sm * (1.0 - sm)).astype(dgm_ref.dtype)
        dap_ref[...] = _nt(dyp, wpu_ref[...]).astype(dap_ref.dtype)
        dam = _nt(dym, wmu_ref[...])
        zf = zm_ref[...].astype(F32)
        sg = _sigmoid(zf)
        si = zf * sg
        of = o_ref[...].astype(F32)
        amla_ref[...] = (of * si).astype(amla_ref.dtype)
        do = dam * si
        do_ref[...] = do.astype(do_ref.dtype)
        dzm_ref[...] = (dam * of * (sg * (1.0 + zf * (1.0 - sg)))).astype(dzm_ref.dtype)
        prod = do * of
        for j in range(N_HEADS // 2):
            hb = slice(j * HEAD_PAD, (j + 1) * HEAD_PAD)
            delta_ref[:, hb] = _pair_rowsum(prod[:, hb])

    bf = lambda w: jax.ShapeDtypeStruct((lp, w), BF16)
    return pl.pallas_call(
        body, name="merge_bwd", grid=(lp // tm,),
        in_specs=[_rows(tm, D_MODEL)] * 5 + [_rows(tm, MLA_WIDTH)] * 2
        + [_resident((D_MODEL, D_MODEL)), _resident((POOL_WIDTH, D_MODEL)), _resident((MLA_WIDTH, D_MODEL))],
        out_specs=[_rows(tm, D_MODEL)] * 5 + [_rows(tm, POOL_WIDTH)] + [_rows(tm, MLA_WIDTH)] * 4,
        out_shape=[bf(D_MODEL)] * 5 + [bf(POOL_WIDTH)] + [bf(MLA_WIDTH)] * 3 + [jax.ShapeDtypeStruct((lp, MLA_WIDTH), F32)],
        compiler_params=_cp(("parallel",)),
    )(dres, yp, ym, gp, gm, o, zm, wout, wpu, wmu)


def _flash_bwd(q, k, v, do, lse, delta, scatter=()):
    lp = q.shape[0]
    tq = tk = ATTN_TILE
    nq = lp // tq
    heads = 2 * BWD_PAIRS
    nj = N_HEADS // heads
    qk_w, v_w = heads * HEAD_PAD, BWD_PAIRS * HEAD_PAD
    scale = 1.0 / math.sqrt(QK_DIM)
    ex = _ChipExchange(list(scatter), scatter=True)

    pairs = [(t, i) for t in range(nq) for i in range(t, nq)]
    t_tab = jnp.asarray([p[0] for p in pairs], jnp.int32)
    i_tab = jnp.asarray([p[1] for p in pairs], jnp.int32)

    def body(t_tab_ref, i_tab_ref, q_ref, k_ref, v_ref, do_ref, lse_ref, dl_ref, *rest):
        ex_in, (dq_hbm, dk_ref, dv_ref), ex_out = rest[:ex.n], rest[ex.n:ex.n + 3], rest[ex.n + 3:2 * ex.n + 3]
        (dq_acc, dk_acc, dv_acc, s_scr, dp_scr, p_scr, ds_scr, doh_scr, stat_scr, stage_scr,
         stage_sem) = rest[2 * ex.n + 3:2 * ex.n + 14]
        ex_sems = rest[2 * ex.n + 14:]
        j, step_no = pl.program_id(0), pl.program_id(1)
        t, i = t_tab_ref[step_no], i_tab_ref[step_no]
        if ex.n:
            pl.when((j == 0) & (step_no == 0))(lambda: ex.start(ex_in, ex_out, ex_sems))

        @pl.when(step_no == 0)
        def _():
            dq_acc[...] = jnp.zeros(dq_acc.shape, F32)

        @pl.when(i == t)
        def _():
            dk_acc[...] = jnp.zeros(dk_acc.shape, F32)
            dv_acc[...] = jnp.zeros(dv_acc.shape, F32)

        def step(kind):
            lane = lax.broadcasted_iota(jnp.int32, (tq, HEAD_PAD), 1)
            half = tq // 2
            for pp in range(BWD_PAIRS):
                pb = slice(pp * HEAD_PAD, (pp + 1) * HEAD_PAD)
                for hh in range(2):
                    hb = slice((2 * pp + hh) * HEAD_PAD, (2 * pp + hh + 1) * HEAD_PAD)
                    mine = (lane < V_DIM) if hh == 0 else (lane >= V_DIM)
                    doh_scr[hh] = jnp.where(mine, do_ref[:, pb], jnp.zeros((tq, HEAD_PAD), BF16))
                    if kind == "diag":
                        s_scr[hh, 0:half, 0:half] = _nt(q_ref[0:half, hb], k_ref[0:half, hb])
                        s_scr[hh, half:tq, :] = _nt(q_ref[half:tq, hb], k_ref[:, hb])
                        dp_scr[hh, 0:half, 0:half] = _nt(doh_scr[hh, 0:half, :], v_ref[0:half, pb])
                        dp_scr[hh, half:tq, :] = _nt(doh_scr[hh, half:tq, :], v_ref[:, pb])
                    else:
                        s_scr[hh] = _nt(q_ref[:, hb], k_ref[:, hb])
                        dp_scr[hh] = _nt(doh_scr[hh], v_ref[:, pb])
                for hh in range(2):
                    head = 2 * pp + hh
                    hb = slice(head * HEAD_PAD, (head + 1) * HEAD_PAD)
                    col = slice(pp * HEAD_PAD + hh * V_DIM, pp * HEAD_PAD + hh * V_DIM + 1)
                    stat_scr[0] = _lanes(lse_ref[:, col])
                    stat_scr[1] = _lanes(dl_ref[:, col])
                    for r in range(tq // ROW_BLOCK):
                        rows = slice(r * ROW_BLOCK, (r + 1) * ROW_BLOCK)
                        lse_r = stat_scr[0, rows, :]
                        dl_r = stat_scr[1, rows, :]
                        chunks = _score_chunks(kind, r, tk)
                        for c0, masked in chunks:
                            cols = slice(c0, c0 + ATTN_BLOCK)
                            s = s_scr[hh, rows, cols]
                            if masked:
                                s = _causal_mask(s, i * tq + r * ROW_BLOCK, t * tk + c0)
                            p = jnp.exp2(s - lse_r)
                            p_scr[hh, rows, cols] = p.astype(BF16)
                            ds_scr[hh, rows, cols] = (p * (dp_scr[hh, rows, cols] - dl_r)).astype(BF16)
                        done = chunks[-1][0] + ATTN_BLOCK
                        if done < tk:
                            zeros = jnp.zeros((ROW_BLOCK, tk - done), BF16)
                            p_scr[hh, rows, done:tk] = zeros
                            ds_scr[hh, rows, done:tk] = zeros
                    tnt = lambda a, b: lax.dot_general(a, b, (((0,), (1,)), ((), ())), preferred_element_type=F32)
                    if kind == "diag":
                        dv_acc[pp, :, 0:half] += _tn(doh_scr[hh], p_scr[hh, :, 0:half])
                        dv_acc[pp, :, half:tk] += _tn(doh_scr[hh, half:tq, :], p_scr[hh, half:tq, half:tk])
                        dk_acc[head, :, 0:half] += _tn(q_ref[:, hb], ds_scr[hh, :, 0:half])
                        dk_acc[head, :, half:tk] += _tn(q_ref[half:tq, hb], ds_scr[hh, half:tq, half:tk])
                        dq_acc[i, hb, 0:half] += tnt(k_ref[0:half, hb], ds_scr[hh, 0:half, 0:half])
                        dq_acc[i, hb, half:tq] += tnt(k_ref[:, hb], ds_scr[hh, half:tq, :])
                    else:
                        dv_acc[pp] += _tn(doh_scr[hh], p_scr[hh])
                        dk_acc[head] += _tn(q_ref[:, hb], ds_scr[hh])
                        dq_acc[i, hb, :] += tnt(k_ref[:, hb], ds_scr[hh])

        for kind, pred in _tile_kinds(i, t):
            pl.when(pred)(functools.partial(step, kind))

        @pl.when(i == nq - 1)
        def _():
            for head in range(heads):
                hb = slice(head * HEAD_PAD, (head + 1) * HEAD_PAD)
                dk_ref[:, hb] = (dk_acc[head].T * LN2).astype(dk_ref.dtype)
            for pp in range(BWD_PAIRS):
                dv_ref[:, pp * HEAD_PAD:(pp + 1) * HEAD_PAD] = dv_acc[pp].T.astype(dv_ref.dtype)

        @pl.when(step_no == len(pairs) - 1)
        def _():
            my_cols = pl.ds(pl.multiple_of(j * qk_w, qk_w), qk_w)
            for qi in range(nq):
                for head in range(heads):
                    hb = slice(head * HEAD_PAD, (head + 1) * HEAD_PAD)
                    stage_scr[:, hb] = (dq_acc[qi, hb, :].T * scale).astype(BF16)
                out = pltpu.make_async_copy(stage_scr, dq_hbm.at[pl.ds(qi * tq, tq), my_cols], stage_sem)
                out.start()
                out.wait()

        if ex.n:
            pl.when((j == nj - 1) & (step_no == len(pairs) - 1))(lambda: ex.wait(ex_in, ex_out, ex_sems))

    q_idx = lambda j, p, tt, it: (it[p], j)
    kv_idx = lambda j, p, tt, it: (tt[p], j)
    hw = N_HEADS * HEAD_PAD
    dq, dk, dv, *pieces = pl.pallas_call(
        body, name="flash_bwd_scatter" if ex.n else "flash_bwd",
        grid_spec=pltpu.PrefetchScalarGridSpec(
            num_scalar_prefetch=2, grid=(nj, len(pairs)),
            in_specs=[pl.BlockSpec((tq, qk_w), q_idx), pl.BlockSpec((tk, qk_w), kv_idx),
                      pl.BlockSpec((tk, v_w), kv_idx), pl.BlockSpec((tq, v_w), q_idx),
                      pl.BlockSpec((tq, v_w), q_idx), pl.BlockSpec((tq, v_w), q_idx)] + ex.specs,
            out_specs=[HBM_SPEC, pl.BlockSpec((tk, qk_w), kv_idx), pl.BlockSpec((tk, v_w), kv_idx)] + ex.specs,
            scratch_shapes=[pltpu.VMEM((nq, qk_w, tq), F32), pltpu.VMEM((heads, HEAD_PAD, tk), F32),
                            pltpu.VMEM((BWD_PAIRS, HEAD_PAD, tk), F32),
                            pltpu.VMEM((2, tq, tk), F32), pltpu.VMEM((2, tq, tk), F32),
                            pltpu.VMEM((2, tq, tk), BF16), pltpu.VMEM((2, tq, tk), BF16),
                            pltpu.VMEM((2, tq, HEAD_PAD), BF16), pltpu.VMEM((2, tq, HEAD_PAD), F32),
                            pltpu.VMEM((tq, qk_w), BF16), pltpu.SemaphoreType.DMA(())]
            + (ex.sems if ex.n else [])),
        out_shape=[jax.ShapeDtypeStruct((lp, hw), BF16), jax.ShapeDtypeStruct((lp, hw), BF16),
                   jax.ShapeDtypeStruct((lp, MLA_WIDTH), BF16)] + ex.out_shape,
        compiler_params=_cp(("arbitrary",) * 2),
    )(t_tab, i_tab, q, k, v, do, lse, delta, *scatter)
    return dq, dk, dv, pieces


def _mla_prep_bwd(dq, dk, dv, cq, ckv, kr, tabs, gqa, gkva, gqn, gkn, wq, wkn, wv):
    lp = cq.shape[0]
    tm = _row_tile(lp)
    hw = N_HEADS * HEAD_PAD

    def body(dq_ref, dk_ref, dv_ref, cq_ref, ckv_ref, kr_ref, c_ref, s1_ref, s2_ref, gqa_ref, gkva_ref, gqn_ref,
             gkn_ref, wq_ref, wkn_ref, wv_ref, dcq_ref, dckv_ref, dkr_ref, dwq_ref, dwkn_ref, dwv_ref,
             dgqa_ref, dgkva_ref, dgqn_ref, dgkn_ref, draw_scr):
        @pl.when(pl.program_id(0) == 0)
        def _():
            for r in (dwq_ref, dwkn_ref, dwv_ref, dgqa_ref, dgkva_ref, dgqn_ref, dgkn_ref):
                r[...] = jnp.zeros(r.shape, F32)

        c, s1, s2 = c_ref[...], s1_ref[...], s2_ref[...]
        lane = lax.broadcasted_iota(jnp.int32, (tm, HEAD_PAD), 1)

        inv_q, xhat_q, cqn = _rms_fwd(cq_ref[...].astype(F32), gqa_ref[...])
        cqn_b = cqn.astype(BF16)
        qraw = _mm(cqn_b, wq_ref[...])
        dgqn = jnp.zeros((1, HEAD_PAD), F32)
        for h in range(N_HEADS):
            hb = slice(h * HEAD_PAD, (h + 1) * HEAD_PAD)
            inv, xhat, _ = _head_norm_fwd(qraw[:, hb], gqn_ref[...])
            dy = _rope_t(dq_ref[:, hb].astype(F32), c, s1, s2)
            dgqn += jnp.sum(dy * xhat, axis=0, keepdims=True)
            draw_scr[:, hb] = _head_norm_bwd(dy, inv, xhat, gqn_ref[...]).astype(BF16)
        dgqn_ref[...] += dgqn
        dqraw = draw_scr[...]
        dwq_ref[...] += _tn(cqn_b, dqraw)
        dcq, dgqa = _rms_bwd(_nt(dqraw, wq_ref[...]), inv_q, xhat_q, gqa_ref[...])
        dcq_ref[...] = dcq.astype(dcq_ref.dtype)
        dgqa_ref[...] += dgqa

        inv_kv, xhat_kv, ckvn = _rms_fwd(ckv_ref[...].astype(F32), gkva_ref[...])
        ckvn_b = ckvn.astype(BF16)
        knraw = _mm(ckvn_b, wkn_ref[...])
        krs = kr_ref[...].astype(F32)
        dgkn = jnp.zeros((1, HEAD_PAD), F32)
        dkr = jnp.zeros((tm, HEAD_PAD), F32)
        for h in range(N_HEADS):
            hb = slice(h * HEAD_PAD, (h + 1) * HEAD_PAD)
            inv, xhat, _ = _head_norm_fwd(knraw[:, hb] + krs, gkn_ref[...])
            dy = _rope_t(dk_ref[:, hb].astype(F32), c, s1, s2)
            dgkn += jnp.sum(dy * xhat, axis=0, keepdims=True)
            dxh = _head_norm_bwd(dy, inv, xhat, gkn_ref[...])
            dkr += dxh
            draw_scr[:, hb] = jnp.where(lane < NOPE, dxh, 0.0).astype(BF16)
        dgkn_ref[...] += dgkn
        dkr_ref[...] = jnp.where((lane >= KR_LANE0) & (lane < QK_DIM), dkr, 0.0).astype(dkr_ref.dtype)
        dknraw = draw_scr[...]
        dvb = dv_ref[...]
        dwkn_ref[...] += _tn(ckvn_b, dknraw)
        dwv_ref[...] += _tn(ckvn_b, dvb)
        dckvn = _nt(dknraw, wkn_ref[...]) + _nt(dvb, wv_ref[...])
        dckv, dgkva = _rms_bwd(dckvn, inv_kv, xhat_kv, gkva_ref[...])
        dckv_ref[...] = dckv.astype(dckv_ref.dtype)
        dgkva_ref[...] += dgkva

    vec = lambda n: pl.BlockSpec((1, n), lambda i: (0, 0))
    whole = lambda r, c: pl.BlockSpec((r, c), lambda i: (0, 0))
    f = lambda r, c: jax.ShapeDtypeStruct((r, c), F32)
    return pl.pallas_call(
        body, name="mla_prep_bwd", grid=(lp // tm,),
        in_specs=[_rows(tm, hw), _rows(tm, hw), _rows(tm, MLA_WIDTH), _rows(tm, Q_RANK), _rows(tm, KV_RANK),
                  _rows(tm, HEAD_PAD)] + [_rows(tm, HEAD_PAD)] * 3
        + [vec(Q_RANK), vec(KV_RANK), vec(HEAD_PAD), vec(HEAD_PAD),
           _resident((Q_RANK, hw)), _resident((KV_RANK, hw)), _resident((KV_RANK, MLA_WIDTH))],
        out_specs=[_rows(tm, Q_RANK), _rows(tm, KV_RANK), _rows(tm, HEAD_PAD),
                   whole(Q_RANK, hw), whole(KV_RANK, hw), whole(KV_RANK, MLA_WIDTH),
                   vec(Q_RANK), vec(KV_RANK), vec(HEAD_PAD), vec(HEAD_PAD)],
        out_shape=[jax.ShapeDtypeStruct((lp, Q_RANK), BF16), jax.ShapeDtypeStruct((lp, KV_RANK), BF16),
                   jax.ShapeDtypeStruct((lp, HEAD_PAD), BF16),
                   f(Q_RANK, hw), f(KV_RANK, hw), f(KV_RANK, MLA_WIDTH),
                   f(1, Q_RANK), f(1, KV_RANK), f(1, HEAD_PAD), f(1, HEAD_PAD)],
        scratch_shapes=[pltpu.VMEM((tm, hw), BF16)],
        compiler_params=_cp(("arbitrary",)),
    )(dq, dk, dv, cq, ckv, kr, *tabs, gqa, gkva, gqn, gkn, wq, wkn, wv)


def _pool_bwd(dap, u, zp, wg, scale):
    lp = u.shape[0]
    tm = _row_tile(lp)
    n = lp // tm
    per = tm // HALO

    def body(dap_ref, u_ref, uh_ref, z_ref, wg_ref, sc_ref, du_ref, dz_ref, dwg_ref, dsc_ref, ext_u, ext_d):
        i = pl.program_id(0)
        r = n - 1 - i

        @pl.when(i == 0)
        def _():
            dwg_ref[...] = jnp.zeros(dwg_ref.shape, F32)
            dsc_ref[...] = jnp.zeros(dsc_ref.shape, F32)
            ext_d[tm:tm + HALO, :] = jnp.zeros((HALO, POOL_WIDTH), F32)

        ext_u[0:HALO, :] = jnp.where(r == 0, 0.0, uh_ref[...].astype(F32))
        ext_u[HALO:HALO + tm, :] = u_ref[...].astype(F32)
        e = ext_u[...]
        sums = _trailing_sums(e)
        inv_cnt = _inv_counts(r, tm)
        dmixed = []
        for g in range(POOL_GROUPS):
            cols = slice(g * GROUP_DIM, (g + 1) * GROUP_DIM)
            mixed_b = (sums[g][HALO:, :] * inv_cnt[g] - e[HALO:, cols]).astype(BF16)
            yg = _mm(mixed_b, wg_ref[g])
            zf = z_ref[:, cols].astype(F32)
            sg = _sigmoid(zf)
            da = dap_ref[:, cols].astype(F32)
            dy = da * (zf * sg)
            dz_ref[:, cols] = (da * (yg * sc_ref[:, cols]) * (sg * (1.0 + zf * (1.0 - sg)))).astype(dz_ref.dtype)
            dsc_ref[:, cols] += jnp.sum(dy * yg, axis=0, keepdims=True)
            dyg = (dy * sc_ref[:, cols]).astype(BF16)
            dwg_ref[g] += _tn(mixed_b, dyg)
            dm = _nt(dyg, wg_ref[g])
            dmixed.append(dm)
            ext_d[0:tm, cols] = dm * inv_cnt[g]
        ed = ext_d[...]
        lead = _leading_sums(ed)
        ext_d[tm:tm + HALO, :] = ed[0:HALO, :]
        for g in range(POOL_GROUPS):
            cols = slice(g * GROUP_DIM, (g + 1) * GROUP_DIM)
            du_ref[:, cols] = (lead[g][0:tm, :] - dmixed[g]).astype(du_ref.dtype)

    rev = lambda i: (n - 1 - i, 0)
    return pl.pallas_call(
        body, name="pool_bwd", grid=(n,),
        in_specs=[pl.BlockSpec((tm, POOL_WIDTH), rev), pl.BlockSpec((tm, POOL_WIDTH), rev),
                  pl.BlockSpec((HALO, POOL_WIDTH), lambda i: (jnp.maximum((n - 1 - i) * per - 1, 0), 0)),
                  pl.BlockSpec((tm, POOL_WIDTH), rev),
                  pl.BlockSpec((POOL_GROUPS, GROUP_DIM, GROUP_DIM), lambda i: (0, 0, 0)),
                  pl.BlockSpec((1, POOL_WIDTH), lambda i: (0, 0))],
        out_specs=[pl.BlockSpec((tm, POOL_WIDTH), rev), pl.BlockSpec((tm, POOL_WIDTH), rev),
                   pl.BlockSpec((POOL_GROUPS, GROUP_DIM, GROUP_DIM), lambda i: (0, 0, 0)),
                   pl.BlockSpec((1, POOL_WIDTH), lambda i: (0, 0))],
        out_shape=[jax.ShapeDtypeStruct((lp, POOL_WIDTH), BF16), jax.ShapeDtypeStruct((lp, POOL_WIDTH), BF16),
                   jax.ShapeDtypeStruct((POOL_GROUPS, GROUP_DIM, GROUP_DIM), F32),
                   jax.ShapeDtypeStruct((1, POOL_WIDTH), F32)],
        scratch_shapes=[pltpu.VMEM((HALO + tm, POOL_WIDTH), F32), pltpu.VMEM((tm + HALO, POOL_WIDTH), F32)],
        compiler_params=_cp(("arbitrary",)),
    )(dap, u, u, zp, wg, scale)


def _inproj_bwd(dres, x, gain, w_pad, dparts, scatter=()):
    lp = x.shape[0]
    tm = _row_tile(lp, ROW_TILES_HEAVY)
    n_dp = len(IN_WIDTHS)
    ex = _ChipExchange(list(scatter), scatter=True)

    def body(dres_ref, x_ref, g_ref, w_ref, *rest):
        dps, ex_in = rest[:n_dp], rest[n_dp:n_dp + ex.n]
        dprev_ref, h_ref, dg_ref = rest[n_dp + ex.n:n_dp + ex.n + 3]
        ex_out, ex_sems = rest[n_dp + ex.n + 3:n_dp + 2 * ex.n + 3], rest[n_dp + 2 * ex.n + 3:]
        if ex.n:
            pl.when(pl.program_id(0) == 0)(lambda: ex.start(ex_in, ex_out, ex_sems))

        @pl.when(pl.program_id(0) == 0)
        def _():
            dg_ref[...] = jnp.zeros(dg_ref.shape, F32)

        dh = jnp.zeros((tm, D_MODEL), F32)
        for dp_ref, off, wd in zip(dps, IN_OFFS, IN_WIDTHS):
            dh += _mm(dp_ref[...], w_ref[off:off + wd, :])
        inv, xhat, hn = _rms_fwd(x_ref[...], g_ref[...])
        h_ref[...] = hn.astype(h_ref.dtype)
        dx, dgain = _rms_bwd(dh, inv, xhat, g_ref[...])
        dg_ref[...] += dgain
        dprev_ref[...] = dres_ref[...] + dx
        if ex.n:
            pl.when(pl.program_id(0) == lp // tm - 1)(lambda: ex.wait(ex_in, ex_out, ex_sems))

    dprev, h, dgain, *pieces = pl.pallas_call(
        body, name="inproj_bwd_scatter" if ex.n else "inproj_bwd", grid=(lp // tm,),
        in_specs=[_rows(tm, D_MODEL), _rows(tm, D_MODEL), pl.BlockSpec((1, D_MODEL), lambda i: (0, 0)),
                  _resident((IN_PAD, D_MODEL))] + [_rows(tm, wd) for wd in IN_WIDTHS] + ex.specs,
        out_specs=[_rows(tm, D_MODEL), _rows(tm, D_MODEL), pl.BlockSpec((1, D_MODEL), lambda i: (0, 0))] + ex.specs,
        out_shape=[jax.ShapeDtypeStruct((lp, D_MODEL), F32), jax.ShapeDtypeStruct((lp, D_MODEL), BF16),
                   jax.ShapeDtypeStruct((1, D_MODEL), F32)] + ex.out_shape,
        scratch_shapes=ex.sems if ex.n else [],
        compiler_params=_cp(("arbitrary",)),
    )(dres, x, gain, w_pad, *dparts, *scatter)
    return dprev, h, dgain, pieces


def _weight_grads(a, bs, name, transposed=False):
    lp, m = a.shape
    tk = _row_tile(lp)
    nb = len(bs)
    shapes = [(b.shape[1], m) if transposed else (m, b.shape[1]) for b in bs]

    def body(a_ref, *rest):
        b_refs, o_refs = rest[:nb], rest[nb:]

        @pl.when(pl.program_id(0) == 0)
        def _():
            for o_ref in o_refs:
                o_ref[...] = jnp.zeros(o_ref.shape, F32)

        ab = a_ref[...].astype(BF16)
        for b_ref, o_ref in zip(b_refs, o_refs):
            bb = b_ref[...].astype(BF16)
            o_ref[...] += _tn(bb, ab) if transposed else _tn(ab, bb)

    return pl.pallas_call(
        body, name=name, grid=(lp // tk,),
        in_specs=[_rows(tk, m)] + [_rows(tk, b.shape[1]) for b in bs],
        out_specs=[pl.BlockSpec(s, lambda i: (0, 0)) for s in shapes],
        out_shape=[jax.ShapeDtypeStruct(s, F32) for s in shapes],
        compiler_params=_cp(("arbitrary",)),
    )(a, *bs)


HBM_SPEC = pl.BlockSpec(memory_space=pltpu.HBM)


def _my_place():
    return lax.axis_index("x"), lax.axis_index("y"), lax.axis_index("c")


def _other_chips(x, y):
    return [(1 - x, y), (x, 1 - y), (1 - x, 1 - y)]


class _ChipExchange:
    def __init__(self, arrs, scatter):
        self.n = len(arrs)
        self.scatter = scatter
        self.out_shape = [jax.ShapeDtypeStruct(a.shape if scatter else (N_CHIPS,) + a.shape, a.dtype) for a in arrs]
        self.specs = [HBM_SPEC] * self.n
        self.sems = [pltpu.SemaphoreType.DMA((3 * self.n,)), pltpu.SemaphoreType.DMA((3 * self.n,)),
                     pltpu.SemaphoreType.DMA((self.n,))]

    def _copies(self, ins, outs, sems):
        send_sems, recv_sems, local_sems = sems
        x, y, c = _my_place()
        me = 2 * x + y
        chips = _other_chips(x, y)
        mine = lambda a: ins[a].at[me] if self.scatter else ins[a]

        def remote(a, k, arriving):
            px, py = chips[k]
            there = 2 * px + py
            return pltpu.make_async_remote_copy(
                src_ref=mine(a) if arriving or not self.scatter else ins[a].at[there],
                dst_ref=outs[a].at[there if arriving else me],
                send_sem=send_sems.at[a * 3 + k], recv_sem=recv_sems.at[a * 3 + k],
                device_id=(px, py, c), device_id_type=MESH)

        pairs = [(a, k) for a in range(self.n) for k in range(3)]
        local = [pltpu.make_async_copy(mine(a), outs[a].at[me], local_sems.at[a]) for a in range(self.n)]
        return local, [remote(a, k, False) for a, k in pairs], [remote(a, k, True) for a, k in pairs]

    def start(self, ins, outs, sems):
        local, sends, _ = self._copies(ins, outs, sems)
        for cp in local + sends:
            cp.start()

    def wait(self, ins, outs, sems):
        local, sends, arrivals = self._copies(ins, outs, sems)
        for cp in arrivals:
            cp.wait_recv()
        for cp in sends:
            cp.wait_send()
        for cp in local:
            cp.wait()


def _chip_exchange(arrs, scatter, name):
    ex = _ChipExchange(arrs, scatter)

    def body(*refs):
        ins, outs, sems = refs[:ex.n], refs[ex.n:2 * ex.n], refs[2 * ex.n:]
        ex.start(ins, outs, sems)
        ex.wait(ins, outs, sems)

    return pl.pallas_call(body, name=name, in_specs=ex.specs, out_specs=ex.specs, out_shape=ex.out_shape,
                          scratch_shapes=ex.sems)(*arrs)


def _sibling_exchange(arrs, name):
    n = len(arrs)

    def body(*refs):
        ins, outs = refs[:n], refs[n:2 * n]
        send_sems, recv_sems = refs[2 * n:]
        x, y, c = _my_place()
        cps = [pltpu.make_async_remote_copy(src_ref=ins[a], dst_ref=outs[a], send_sem=send_sems.at[a],
                                            recv_sem=recv_sems.at[a], device_id=(x, y, 1 - c), device_id_type=MESH)
               for a in range(n)]
        for cp in cps:
            cp.start()
        for cp in cps:
            cp.wait_recv()
        for cp in cps:
            cp.wait_send()

    return pl.pallas_call(
        body, name=name, in_specs=[HBM_SPEC] * n, out_specs=[HBM_SPEC] * n,
        out_shape=[jax.ShapeDtypeStruct(a.shape, a.dtype) for a in arrs],
        scratch_shapes=[pltpu.SemaphoreType.DMA((n,)), pltpu.SemaphoreType.DMA((n,))],
    )(*arrs)


def _all_reduce_small(pack):
    rows = pack.shape[0]

    def body(p_ref, o_ref, g_scr, send_sems, recv_sems):
        x, y, c = _my_place()
        me = 4 * x + 2 * y + c
        flips = [(dx, dy, dc) for dx in (0, 1) for dy in (0, 1) for dc in (0, 1) if (dx, dy, dc) != (0, 0, 0)]

        def peer(f):
            return (x if f[0] == 0 else 1 - x, y if f[1] == 0 else 1 - y, c if f[2] == 0 else 1 - c)

        def copy(k, slot):
            return pltpu.make_async_remote_copy(src_ref=p_ref, dst_ref=g_scr.at[slot], send_sem=send_sems.at[k],
                                                recv_sem=recv_sems.at[k], device_id=peer(flips[k]), device_id_type=MESH)

        sends = [copy(k, me) for k in range(len(flips))]
        for cp in sends:
            cp.start()
        g_scr[me] = p_ref[...]
        for k, f in enumerate(flips):
            px, py, pc = peer(f)
            copy(k, 4 * px + 2 * py + pc).wait_recv()
        for cp in sends:
            cp.wait_send()
        acc = g_scr[0]
        for d in range(1, N_DEV):
            acc = acc + g_scr[d]
        o_ref[...] = acc

    vm = pl.BlockSpec(memory_space=pltpu.VMEM)
    return pl.pallas_call(
        body, name="all_reduce_small", in_specs=[vm], out_specs=vm,
        out_shape=jax.ShapeDtypeStruct(pack.shape, F32),
        scratch_shapes=[pltpu.VMEM((N_DEV, rows, 128), F32), pltpu.SemaphoreType.DMA((N_DEV - 1,)),
                        pltpu.SemaphoreType.DMA((N_DEV - 1,))],
        compiler_params=_cp(),
    )(pack)


def _as3d(a):
    return a.reshape((-1,) + a.shape[-2:])


def _row_block(r, sublanes=8, cap=512):
    fits = [t for t in range(sublanes, min(r, cap) + 1, sublanes) if r % t == 0]
    return fits[-1] if fits else r


def _sum_pieces(pieces, name):
    _, na, r, c = pieces.shape
    rt = _row_block(r, sublanes=16)

    def body(p_ref, o_ref):
        acc = p_ref[0, 0].astype(F32)
        for s in range(1, N_CHIPS):
            acc = acc + p_ref[s, 0].astype(F32)
        o_ref[0] = acc

    return pl.pallas_call(
        body, name=name, grid=(na, r // rt),
        in_specs=[pl.BlockSpec((N_CHIPS, 1, rt, c), lambda a, i: (0, a, i, 0))],
        out_specs=pl.BlockSpec((1, rt, c), lambda a, i: (a, i, 0)),
        out_shape=jax.ShapeDtypeStruct((na, r, c), F32),
        compiler_params=_cp(("parallel", "parallel")),
    )(pieces)


def _adamw(w, g_parts, m, v, name):
    na, r, c = w.shape
    rt = _row_block(r)
    ng = len(g_parts)

    def body(w_ref, *rest):
        g_refs = rest[:ng]
        m_ref, v_ref, g_out, d_out, m_out, v_out = rest[ng:]
        g = g_refs[0][...]
        for gr in g_refs[1:]:
            g = g + gr[...]
        m_new = ADAM_B1 * m_ref[...] + (1.0 - ADAM_B1) * g
        v_new = ADAM_B2 * v_ref[...] + (1.0 - ADAM_B2) * (g * g)
        m_hat = m_new / (1.0 - ADAM_B1 ** ADAM_STEP)
        v_hat = v_new / (1.0 - ADAM_B2 ** ADAM_STEP)
        g_out[...] = g
        d_out[...] = -ADAM_LR * (m_hat / (jnp.sqrt(v_hat) + ADAM_EPS) + ADAM_WD * w_ref[...])
        m_out[...] = m_new
        v_out[...] = v_new

    spec = pl.BlockSpec((1, rt, c), lambda a, i: (a, i, 0))
    out = jax.ShapeDtypeStruct((na, r, c), F32)
    return pl.pallas_call(
        body, name=name, grid=(na, r // rt), in_specs=[spec] * (3 + ng), out_specs=[spec] * 4, out_shape=[out] * 4,
        compiler_params=_cp(("parallel", "parallel")),
    )(w, *g_parts, m, v)


def _cols_from_shards(g):
    g = jnp.moveaxis(g, 0, -2)
    return g.reshape(g.shape[:-2] + (g.shape[-2] * g.shape[-1],))


def _rows_from_shards(g):
    g = jnp.moveaxis(g, 0, -3)
    return g.reshape(g.shape[:-3] + (g.shape[-3] * g.shape[-2], g.shape[-1]))


def _cols_to_shards(w):
    w = w.reshape(w.shape[:-1] + (N_CHIPS, w.shape[-1] // N_CHIPS))
    return jnp.moveaxis(w, -2, 0)


def _rows_to_shards(w):
    w = w.reshape(w.shape[:-2] + (N_CHIPS, w.shape[-2] // N_CHIPS, w.shape[-1]))
    return jnp.moveaxis(w, -3, 0)


def _pad_w_in(wt):
    z = lambda n: jnp.zeros((n, wt.shape[1]), wt.dtype)
    return jnp.concatenate([wt[:2048], wt[2080:4640], z(KR_LANE0), wt[2048:2080], z(HEAD_PAD - QK_DIM)], axis=0)


def _unpad_w_in(parts):
    u, zp, cq, ckv, zm, gp, gm, kr = parts
    return jnp.concatenate([u, zp, cq, ckv, kr[KR_LANE0:QK_DIM], zm, gp, gm], axis=0)


def _pad_heads(w, real):
    w = w.reshape(w.shape[:-1] + (N_HEADS, real))
    w = jnp.pad(w, [(0, 0)] * (w.ndim - 1) + [(0, HEAD_PAD - real)])
    return w.reshape(w.shape[:-2] + (N_HEADS * HEAD_PAD,))


def _flat_rows(a):
    a = a.reshape(-1)
    return jnp.pad(a, (0, (-a.shape[0]) % (8 * 128))).reshape(-1, 128)


def kernel(x, positions, meta_tokens, norm_gain, w_in, pool_w_group, pool_scale, pool_w_up, q_a_norm_gain, kv_a_norm_gain, w_q_b, w_kv_b, q_norm_gain, k_norm_gain, mla_w_up, w_out, loss_target, m_meta_tokens, m_norm_gain, m_w_in, m_pool_w_group, m_pool_scale, m_pool_w_up, m_q_a_norm_gain, m_kv_a_norm_gain, m_w_q_b, m_w_kv_b, m_q_norm_gain, m_k_norm_gain, m_mla_w_up, m_w_out, v_meta_tokens, v_norm_gain, v_w_in, v_pool_w_group, v_pool_scale, v_pool_w_up, v_q_a_norm_gain, v_kv_a_norm_gain, v_w_q_b, v_w_kv_b, v_q_norm_gain, v_k_norm_gain, v_mla_w_up, v_w_out):
    seq = x.shape[1]
    lp = -(-(ROW0 + seq) // ATTN_TILE) * ATTN_TILE
    pad_back = lp - ROW0 - seq
    chip = 2 * lax.axis_index("x") + lax.axis_index("y")

    tr = lambda a: jnp.swapaxes(a, 1, 2)
    big = dict(w_in=tr(w_in), pool_w_up=pool_w_up, w_q_b=w_q_b, w_kv_b=w_kv_b, mla_w_up=mla_w_up, w_out=w_out)
    row_sharded = ("w_in", "w_q_b", "w_out")
    names = list(big)
    shards = [[big[n][l].astype(BF16) for n in names] for l in range(DEPTH)]
    from_shards = lambda n: _rows_from_shards if n in row_sharded else _cols_from_shards
    to_shards = lambda n: _rows_to_shards if n in row_sharded else _cols_to_shards

    def in_weights(g_w_in):
        return dict(w_pad=_pad_w_in(from_shards("w_in")(g_w_in)))

    def rest_weights(gathered):
        w = {n: from_shards(n)(g) for n, g in zip(names[1:], gathered)}
        wkv = w["w_kv_b"].reshape(KV_RANK, N_HEADS, NOPE + V_DIM)
        return dict(wq=_pad_heads(w["w_q_b"], QK_DIM),
                    wkn=_pad_heads(wkv[..., :NOPE].reshape(KV_RANK, N_HEADS * NOPE), NOPE),
                    wv=wkv[..., NOPE:].reshape(KV_RANK, MLA_WIDTH),
                    wpu=w["pool_w_up"], wmu=w["mla_w_up"], wout=w["w_out"])

    g_in0, meta_g = _chip_exchange([shards[0][0], meta_tokens], scatter=False, name="gather_layer0")
    weights = [in_weights(g_in0)]
    meta_full = _cols_from_shards(meta_g)
    wg = pool_w_group.astype(BF16)
    gqn = jnp.pad(q_norm_gain, ((0, 0), (0, HEAD_PAD - QK_DIM)))
    gkn = jnp.pad(k_norm_gain, ((0, 0), (0, HEAD_PAD - QK_DIM)))

    x_pad = jnp.concatenate([jnp.zeros((PAD_FRONT, D_MODEL), F32), meta_full, x[0], jnp.zeros((pad_back, D_MODEL), F32)], axis=0)
    pos_pad = jnp.concatenate([jnp.zeros((PAD_FRONT,), jnp.int32), jnp.arange(N_META, dtype=jnp.int32),
                               positions[0] + N_META, jnp.zeros((pad_back,), jnp.int32)])
    half = ROPE // 2
    inv_freq = (ROPE_THETA ** (-np.arange(half, dtype=np.float32) / half)).astype(np.float32)
    freq_row = np.zeros((1, HEAD_PAD), np.float32)
    freq_row[0, NOPE:NOPE + half] = inv_freq
    freq_row[0, NOPE + half:QK_DIM] = inv_freq
    tabs = _rope_tables(pos_pad[:, None], jnp.asarray(freq_row))

    row = lambda a, l: a[l][None, :]

    saved = []
    h_res = x_pad
    for l in range(DEPTH):
        w = weights[l]
        (u, zp, cq, ckv, zm, gp, gm, kr), rest0 = _inproj_fwd(h_res, row(norm_gain, l), w["w_pad"],
                                                              gather=shards[0][1:] if l == 0 else ())
        if rest0:
            w.update(rest_weights(rest0))
        a_pool = _pool_fwd(u, zp, wg[l], row(pool_scale, l))
        q, k, v = _mla_prep_fwd(cq, ckv, kr, tabs, row(q_a_norm_gain, l), row(kv_a_norm_gain, l), row(gqn, l), row(gkn, l),
                                w["wq"], w["wkn"], w["wv"])
        o, lse, nxt = _flash_fwd(q, k, v, gather=shards[l + 1] if l + 1 < DEPTH else ())
        if nxt:
            weights.append({**in_weights(nxt[0]), **rest_weights(nxt[1:])})
        h_next, yp, ym = _merge_fwd(h_res, a_pool, o, zm, gp, gm, w["wpu"], w["wmu"], w["wout"])
        saved.append(dict(x=h_res, u=u, zp=zp, cq=cq, ckv=ckv, zm=zm, gp=gp, gm=gm, kr=kr, a_pool=a_pool, q=q, k=k, v=v,
                          o=o, lse=lse, yp=yp, ym=ym))
        h_res = h_next
    dres, loss_blk = _loss_head(h_res, loss_target[0])

    gw = {n: [None] * DEPTH for n in names}
    pieces = [None] * DEPTH
    grad_stacks = lambda l, which=names: [to_shards(n)(gw[n][l]).astype(BF16) for n in which]
    gs = {n: [None] * DEPTH for n in ("norm_gain", "pool_w_group", "pool_scale", "q_a", "kv_a", "q_norm", "k_norm")}
    for l in reversed(range(DEPTH)):
        s, w = saved[l], weights[l]
        merged, dyp, dym, dgp, dgm, dap, amla, do, dzm, delta = _merge_bwd(
            dres, s["yp"], s["ym"], s["gp"], s["gm"], s["o"], s["zm"], w["wout"], w["wpu"], w["wmu"])
        (gw["w_out"][l],) = _weight_grads(merged, [dres], "grad_w_out")
        (gw["pool_w_up"][l],) = _weight_grads(s["a_pool"], [dyp], "grad_pool_w_up")
        (gw["mla_w_up"][l],) = _weight_grads(amla, [dym], "grad_mla_w_up")
        dq, dk, dv, got = _flash_bwd(s["q"], s["k"], s["v"], do, s["lse"], delta,
                                     scatter=grad_stacks(l + 1) if l + 1 < DEPTH else ())
        if got:
            pieces[l + 1] = got
        dcq, dckv, dkr, dwq, dwkn, dwv, gs["q_a"][l], gs["kv_a"][l], dgqn, dgkn = _mla_prep_bwd(
            dq, dk, dv, s["cq"], s["ckv"], s["kr"], tabs, row(q_a_norm_gain, l), row(kv_a_norm_gain, l), row(gqn, l), row(gkn, l),
            w["wq"], w["wkn"], w["wv"])
        gs["q_norm"][l] = dgqn[:, :QK_DIM]
        gs["k_norm"][l] = dgkn[:, :QK_DIM]
        gw["w_q_b"][l] = dwq.reshape(Q_RANK, N_HEADS, HEAD_PAD)[..., :QK_DIM].reshape(Q_RANK, N_HEADS * QK_DIM)
        gw["w_kv_b"][l] = jnp.concatenate([dwkn.reshape(KV_RANK, N_HEADS, HEAD_PAD)[..., :NOPE],
                                           dwv.reshape(KV_RANK, N_HEADS, V_DIM)], axis=-1).reshape(KV_RANK, N_HEADS * (NOPE + V_DIM))
        du, dzp, gs["pool_w_group"][l], gs["pool_scale"][l] = _pool_bwd(dap, s["u"], s["zp"], wg[l], row(pool_scale, l))
        dparts = [du, dzp, dcq, dckv, dzm, dgp, dgm, dkr]
        dres, h, gs["norm_gain"][l], rest0 = _inproj_bwd(dres, s["x"], row(norm_gain, l), w["w_pad"], dparts,
                                                         scatter=grad_stacks(0, names[1:]) if l == 0 else ())
        ga = _weight_grads(h, [du, dzp, dcq, dckv, dkr], "grad_w_in_a", transposed=True)
        gb = _weight_grads(h, [dzm, dgp, dgm], "grad_w_in_b", transposed=True)
        gw["w_in"][l] = _unpad_w_in([ga[0], ga[1], ga[2], ga[3], gb[0], gb[1], gb[2], ga[4]])
    grad_x = dres[ROW0:ROW0 + seq][None]

    pieces[0] = list(_chip_exchange(grad_stacks(0, names[:1]), scatter=True, name="scatter_layer0")) + list(rest0)
    sums = [_sum_pieces(jnp.stack([pieces[l][a] for l in range(DEPTH)], axis=1), "sum_" + n) for a, n in enumerate(names)]
    other = _sibling_exchange(sums, name="swap_core_sums")
    moments = dict(w_in=(tr(m_w_in), tr(v_w_in)), pool_w_up=(m_pool_w_up, v_pool_w_up), w_q_b=(m_w_q_b, v_w_q_b),
                   w_kv_b=(m_w_kv_b, v_w_kv_b), mla_w_up=(m_mla_w_up, v_mla_w_up), w_out=(m_w_out, v_w_out))
    big_out = {n: _adamw(big[n], [sm, ot], moments[n][0], moments[n][1], "adamw_" + n)
               for n, sm, ot in zip(names, sums, other)}

    small_names = ("norm_gain", "pool_w_group", "pool_scale", "q_a", "kv_a", "q_norm", "k_norm")
    small_w = dict(norm_gain=(norm_gain, m_norm_gain, v_norm_gain), pool_w_group=(pool_w_group, m_pool_w_group, v_pool_w_group),
                   pool_scale=(pool_scale, m_pool_scale, v_pool_scale), q_a=(q_a_norm_gain, m_q_a_norm_gain, v_q_a_norm_gain),
                   kv_a=(kv_a_norm_gain, m_kv_a_norm_gain, v_kv_a_norm_gain), q_norm=(q_norm_gain, m_q_norm_gain, v_q_norm_gain),
                   k_norm=(k_norm_gain, m_k_norm_gain, v_k_norm_gain))
    small_g = {n: jnp.stack(gs[n]).reshape(small_w[n][0].shape) for n in small_names}
    blocks = [_flat_rows(small_g[n]) for n in small_names]
    n_rows = [b.shape[0] for b in blocks]
    meta_rows = N_META * D_MODEL // 128
    pack = jnp.concatenate(blocks + [dres[PAD_FRONT:ROW0].reshape(meta_rows, 128), loss_blk], axis=0)
    pack = jnp.pad(pack, ((0, (-pack.shape[0]) % 8), (0, 0)))
    total = _all_reduce_small(pack)
    n_small = sum(n_rows)
    loss = total[n_small + meta_rows, 0]
    gmeta = lax.dynamic_slice_in_dim(total[n_small:n_small + meta_rows].reshape(N_META, D_MODEL), chip * (D_MODEL // N_CHIPS),
                                     D_MODEL // N_CHIPS, axis=1)

    def packed(idx, meta_part):
        p = jnp.concatenate([_flat_rows(small_w[n][idx]) for n in small_names] + [_flat_rows(meta_part)], axis=0)
        return jnp.pad(p, ((0, (-p.shape[0]) % 8), (0, 0)))[None]

    g_pack = jnp.concatenate([total[:n_small], _flat_rows(gmeta)], axis=0)
    g_pack = jnp.pad(g_pack, ((0, (-g_pack.shape[0]) % 8), (0, 0)))[None]
    small_out = _adamw(packed(0, meta_tokens), [g_pack], packed(1, m_meta_tokens), packed(2, v_meta_tokens), "adamw_small")

    def unpack(p):
        res, r0 = {}, 0
        for n, nr in zip(small_names, n_rows):
            shape = small_w[n][0].shape
            res[n] = p[0, r0:r0 + nr].reshape(-1)[:math.prod(shape)].reshape(shape)
            r0 += nr
        res["meta"] = p[0, r0:r0 + N_META * (D_MODEL // N_CHIPS) // 128].reshape(N_META, D_MODEL // N_CHIPS)
        return res

    small_res = [unpack(p) for p in small_out]

    def leaf(kind, name):
        key = {"meta_tokens": "meta", "q_a_norm_gain": "q_a", "kv_a_norm_gain": "kv_a", "q_norm_gain": "q_norm",
               "k_norm_gain": "k_norm"}.get(name, name)
        if name in big_out:
            return tr(big_out[name][kind]) if name == "w_in" else big_out[name][kind]
        return small_res[kind][key]

    order = ("meta_tokens", "norm_gain", "w_in", "pool_w_group", "pool_scale", "pool_w_up", "q_a_norm_gain", "kv_a_norm_gain",
             "w_q_b", "w_kv_b", "q_norm_gain", "k_norm_gain", "mla_w_up", "w_out")
    outs = [loss, grad_x]
    for kind in range(4):
        outs += [leaf(kind, n) for n in order]
    return tuple(outs)
```

```python
import functools
import math

import numpy as np
import jax
import jax.numpy as jnp
from jax import lax
from jax.experimental import pallas as pl
from jax.experimental.pallas import tpu as pltpu

F32 = jnp.float32
BF16 = jnp.bfloat16
MESH = pl.DeviceIdType.MESH

D_MODEL = 1024
DEPTH = 4
N_META = 16
POOL_WIDTH = 512
POOL_WINDOWS = (2, 4, 8, 16)
POOL_GROUPS = 4
GROUP_DIM = 128
N_HEADS = 8
NOPE = 64
ROPE = 32
QK_DIM = 96
V_DIM = 64
MLA_WIDTH = 512
KV_RANK = 256
Q_RANK = 768
ROPE_THETA = 10000.0
EPS = 1e-6
MASK_VALUE = -1e30
ATTN_BLOCK = 128
PAD_FRONT = (-N_META) % ATTN_BLOCK
ROW0 = PAD_FRONT + N_META
HEAD_PAD = 128
HALO = 16
N_CHIPS = 4
N_DEV = 8

IN_NAMES = ("u", "zp", "cq", "ckv", "zm", "gp", "gm", "kr")
IN_WIDTHS = (512, 512, 768, 256, 512, 1024, 1024, 128)
IN_OFFS = tuple(int(v) for v in np.cumsum((0,) + IN_WIDTHS[:-1]))
IN_PAD = sum(IN_WIDTHS)
KR_LANE0 = NOPE

ADAM_LR = 0.001
ADAM_B1 = 0.9
ADAM_B2 = 0.999
ADAM_EPS = 1e-08
ADAM_WD = 0.01
ADAM_STEP = 10

VMEM_LIMIT = 56 * 1024 * 1024
ATTN_TILE = 768
ROW_TILES = (768, 384)
ROW_TILES_HEAVY = (384,)
ROW_BLOCK = 32
LOSS_GROUP = 3
FWD_PAIRS = 4
SCORE_BUFS = 4
BWD_PAIRS = 2
LOG2E = 1.4426950408889634
LN2 = 0.6931471805599453
Q_PRESCALE = LOG2E / math.sqrt(QK_DIM)


def _cp(sem=None, vmem=VMEM_LIMIT):
    kw = dict(vmem_limit_bytes=vmem)
    if sem is not None:
        kw["dimension_semantics"] = sem
    return pltpu.CompilerParams(**kw)


def _row_tile(n_rows, prefs=None):
    for t in prefs or ROW_TILES:
        if n_rows % t == 0:
            return t
    raise ValueError(f"no row tile for {n_rows}")


def _nt(a, b):
    return lax.dot_general(a, b, (((1,), (1,)), ((), ())), preferred_element_type=F32)


def _tn(a, b):
    return lax.dot_general(a, b, (((0,), (0,)), ((), ())), preferred_element_type=F32)


def _mm(a, b):
    return jnp.dot(a, b, preferred_element_type=F32)


def _sigmoid(x):
    return 0.5 * jnp.tanh(0.5 * x) + 0.5


def _resident(shape):
    nd = len(shape)
    return pl.BlockSpec(shape, lambda *_: (0,) * nd, pipeline_mode=pl.Buffered(1))


def _rows(tm, width):
    return pl.BlockSpec((tm, width), lambda i: (i, 0))


def _rope_tables(pos_col, inv_freq_row):
    lp = pos_col.shape[0]
    tm = _row_tile(lp)

    def body(p_ref, f_ref, c_ref, s1_ref, s2_ref):
        ang = p_ref[...].astype(F32) * f_ref[...]
        lane = lax.broadcasted_iota(jnp.int32, ang.shape, 1)
        cs = jnp.cos(ang)
        sn = jnp.sin(ang)
        c_ref[...] = jnp.where(lane < NOPE, 1.0, jnp.where(lane < QK_DIM, cs, 0.0))
        s1_ref[...] = jnp.where((lane >= NOPE) & (lane < NOPE + ROPE // 2), -sn, 0.0)
        s2_ref[...] = jnp.where((lane >= NOPE + ROPE // 2) & (lane < QK_DIM), sn, 0.0)

    out = jax.ShapeDtypeStruct((lp, HEAD_PAD), F32)
    return pl.pallas_call(
        body, name="rope_tables", grid=(lp // tm,),
        in_specs=[pl.BlockSpec((tm, 1), lambda i: (i, 0)), pl.BlockSpec((1, HEAD_PAD), lambda i: (0, 0))],
        out_specs=[_rows(tm, HEAD_PAD)] * 3, out_shape=[out] * 3,
        compiler_params=_cp(("parallel",)),
    )(pos_col, inv_freq_row)


def _rope(y, c, s1, s2):
    return y * c + pltpu.roll(y, HEAD_PAD - ROPE // 2, 1) * s1 + pltpu.roll(y, ROPE // 2, 1) * s2


def _rope_t(g, c, s1, s2):
    return g * c + pltpu.roll(g * s1, ROPE // 2, 1) + pltpu.roll(g * s2, HEAD_PAD - ROPE // 2, 1)


def _inproj_fwd(x, gain, w_pad, gather=()):
    lp = x.shape[0]
    tm = _row_tile(lp)
    n_out = len(IN_WIDTHS)
    ex = _ChipExchange(list(gather), scatter=False)

    def body(x_ref, g_ref, w_ref, *rest):
        ex_in, outs, ex_out, ex_sems = rest[:ex.n], rest[ex.n:ex.n + n_out], rest[ex.n + n_out:2 * ex.n + n_out], rest[2 * ex.n + n_out:]
        if ex.n:
            pl.when(pl.program_id(0) == 0)(lambda: ex.start(ex_in, ex_out, ex_sems))
        xf = x_ref[...]
        inv = lax.rsqrt(jnp.mean(xf * xf, axis=-1, keepdims=True) + EPS)
        h = (xf * inv * g_ref[...]).astype(BF16)
        for o_ref, off, wd in zip(outs, IN_OFFS, IN_WIDTHS):
            o_ref[...] = _nt(h, w_ref[off:off + wd, :]).astype(o_ref.dtype)
        if ex.n:
            pl.when(pl.program_id(0) == lp // tm - 1)(lambda: ex.wait(ex_in, ex_out, ex_sems))

    res = pl.pallas_call(
        body, name="inproj_fwd_gather" if ex.n else "inproj_fwd", grid=(lp // tm,),
        in_specs=[_rows(tm, D_MODEL), pl.BlockSpec((1, D_MODEL), lambda i: (0, 0)), _resident((IN_PAD, D_MODEL))] + ex.specs,
        out_specs=[_rows(tm, wd) for wd in IN_WIDTHS] + ex.specs,
        out_shape=[jax.ShapeDtypeStruct((lp, wd), BF16) for wd in IN_WIDTHS] + ex.out_shape,
        scratch_shapes=ex.sems if ex.n else [],
        compiler_params=_cp(("arbitrary",)),
    )(x, gain, w_pad, *gather)
    return res[:n_out], res[n_out:]


def _inv_counts(tile_idx, tm):
    row = tile_idx * tm + lax.broadcasted_iota(jnp.int32, (tm, 1), 0)
    t1 = jnp.maximum(row - PAD_FRONT + 1, 1).astype(F32)
    return [1.0 / jnp.minimum(t1, float(w)) for w in POOL_WINDOWS]


def _window_sums(e, shift):
    s2 = e + shift(e, 1)
    s4 = s2[:, GROUP_DIM:] + shift(s2[:, GROUP_DIM:], 2)
    s8 = s4[:, GROUP_DIM:] + shift(s4[:, GROUP_DIM:], 4)
    s16 = s8[:, GROUP_DIM:] + shift(s8[:, GROUP_DIM:], 8)
    return (s2[:, :GROUP_DIM], s4[:, :GROUP_DIM], s8[:, :GROUP_DIM], s16)


def _trailing_sums(e):
    return _window_sums(e, lambda x, s: pltpu.roll(x, s, 0))


def _leading_sums(e):
    return _window_sums(e, lambda x, s: pltpu.roll(x, x.shape[0] - s, 0))


def _pool_fwd(u, zp, wg, scale):
    lp = u.shape[0]
    tm = _row_tile(lp)

    def body(u_ref, z_ref, wg_ref, sc_ref, a_ref, ext_ref):
        i = pl.program_id(0)

        @pl.when(i == 0)
        def _():
            ext_ref[0:HALO, :] = jnp.zeros((HALO, POOL_WIDTH), F32)

        ext_ref[HALO:HALO + tm, :] = u_ref[...].astype(F32)
        e = ext_ref[...]
        sums = _trailing_sums(e)
        ext_ref[0:HALO, :] = e[tm:tm + HALO, :]
        inv_cnt = _inv_counts(i, tm)
        for g in range(POOL_GROUPS):
            cols = slice(g * GROUP_DIM, (g + 1) * GROUP_DIM)
            mixed = sums[g][HALO:, :] * inv_cnt[g] - e[HALO:, cols]
            y = _mm(mixed.astype(BF16), wg_ref[g]) * sc_ref[:, cols]
            zf = z_ref[:, cols].astype(F32)
            a_ref[:, cols] = (y * (zf * _sigmoid(zf))).astype(a_ref.dtype)

    return pl.pallas_call(
        body, name="pool_fwd", grid=(lp // tm,),
        in_specs=[_rows(tm, POOL_WIDTH), _rows(tm, POOL_WIDTH),
                  pl.BlockSpec((POOL_GROUPS, GROUP_DIM, GROUP_DIM), lambda i: (0, 0, 0)),
                  pl.BlockSpec((1, POOL_WIDTH), lambda i: (0, 0))],
        out_specs=_rows(tm, POOL_WIDTH), out_shape=jax.ShapeDtypeStruct((lp, POOL_WIDTH), BF16),
        scratch_shapes=[pltpu.VMEM((HALO + tm, POOL_WIDTH), F32)],
        compiler_params=_cp(("arbitrary",)),
    )(u, zp, wg, scale)


def _rms_fwd(xf, gain):
    inv = lax.rsqrt(jnp.mean(xf * xf, axis=-1, keepdims=True) + EPS)
    xhat = xf * inv
    return inv, xhat, xhat * gain


def _rms_bwd(dy, inv, xhat, gain):
    dgain = jnp.sum(dy * xhat, axis=0, keepdims=True)
    dyg = dy * gain
    dx = inv * (dyg - xhat * jnp.mean(dyg * xhat, axis=-1, keepdims=True))
    return dx, dgain


def _head_norm_fwd(xh, gain128):
    inv = lax.rsqrt(jnp.sum(xh * xh, axis=-1, keepdims=True) * (1.0 / QK_DIM) + EPS)
    xhat = xh * inv
    return inv, xhat, xhat * gain128


def _head_norm_bwd(dy, inv, xhat, gain128):
    dyg = dy * gain128
    return inv * (dyg - xhat * (jnp.sum(dyg * xhat, axis=-1, keepdims=True) * (1.0 / QK_DIM)))


def _mla_prep_fwd(cq, ckv, kr, tabs, gqa, gkva, gqn, gkn, wq, wkn, wv):
    lp = cq.shape[0]
    tm = _row_tile(lp)

    def body(cq_ref, ckv_ref, kr_ref, c_ref, s1_ref, s2_ref, gqa_ref, gkva_ref, gqn_ref, gkn_ref,
             wq_ref, wkn_ref, wv_ref, q_ref, k_ref, v_ref):
        c, s1, s2 = c_ref[...], s1_ref[...], s2_ref[...]
        _, _, cqn = _rms_fwd(cq_ref[...].astype(F32), gqa_ref[...])
        qraw = _mm(cqn.astype(BF16), wq_ref[...])
        for h in range(N_HEADS):
            hb = slice(h * HEAD_PAD, (h + 1) * HEAD_PAD)
            _, _, yh = _head_norm_fwd(qraw[:, hb], gqn_ref[...])
            q_ref[:, hb] = (_rope(yh, c, s1, s2) * Q_PRESCALE).astype(q_ref.dtype)
        _, _, ckvn = _rms_fwd(ckv_ref[...].astype(F32), gkva_ref[...])
        ckvn_b = ckvn.astype(BF16)
        knraw = _mm(ckvn_b, wkn_ref[...])
        krs = kr_ref[...].astype(F32)
        for h in range(N_HEADS):
            hb = slice(h * HEAD_PAD, (h + 1) * HEAD_PAD)
            _, _, yh = _head_norm_fwd(knraw[:, hb] + krs, gkn_ref[...])
            k_ref[:, hb] = _rope(yh, c, s1, s2).astype(k_ref.dtype)
        v_ref[...] = _mm(ckvn_b, wv_ref[...]).astype(v_ref.dtype)

    hw = N_HEADS * HEAD_PAD
    vec = lambda n: pl.BlockSpec((1, n), lambda i: (0, 0))
    return pl.pallas_call(
        body, name="mla_prep_fwd", grid=(lp // tm,),
        in_specs=[_rows(tm, Q_RANK), _rows(tm, KV_RANK), _rows(tm, HEAD_PAD)] + [_rows(tm, HEAD_PAD)] * 3
        + [vec(Q_RANK), vec(KV_RANK), vec(HEAD_PAD), vec(HEAD_PAD),
           _resident((Q_RANK, hw)), _resident((KV_RANK, hw)), _resident((KV_RANK, MLA_WIDTH))],
        out_specs=[_rows(tm, hw), _rows(tm, hw), _rows(tm, MLA_WIDTH)],
        out_shape=[jax.ShapeDtypeStruct((lp, hw), BF16), jax.ShapeDtypeStruct((lp, hw), BF16),
                   jax.ShapeDtypeStruct((lp, MLA_WIDTH), BF16)],
        compiler_params=_cp(("parallel",)),
    )(cq, ckv, kr, *tabs, gqa, gkva, gqn, gkn, wq, wkn, wv)


def _causal_mask(s, q0, k0):
    qi = q0 + lax.broadcasted_iota(jnp.int32, s.shape, 0)
    ki = k0 + lax.broadcasted_iota(jnp.int32, s.shape, 1)
    return jnp.where((ki <= qi) & (ki >= PAD_FRONT), s, MASK_VALUE)


def _score_chunks(kind, r, tk):
    if kind == "inner":
        return [(c0, False) for c0 in range(0, tk, ATTN_BLOCK)]
    if kind == "first":
        return [(c0, c0 == 0) for c0 in range(0, tk, ATTN_BLOCK)]
    return [(c0, True) for c0 in range(0, min(tk, (r + 1) * ROW_BLOCK), ATTN_BLOCK)]


def _tile_kinds(i, t):
    return (("diag", t == i), ("first", (t == 0) & (i > 0)), ("inner", (t > 0) & (t < i)))


def _lanes(col, width=HEAD_PAD):
    return jnp.broadcast_to(col, (col.shape[0], width))


def _flash_fwd(q, k, v, gather=()):
    lp = q.shape[0]
    tq = tk = ATTN_TILE
    nq = lp // tq
    nj = N_HEADS // (2 * FWD_PAIRS)
    heads = 2 * FWD_PAIRS
    n_blocks = tq // ROW_BLOCK
    ex = _ChipExchange(list(gather), scatter=False)

    pairs = [(i, t) for i in range(nq) for t in range(i + 1)]
    i_tab = jnp.asarray([p[0] for p in pairs], jnp.int32)
    t_tab = jnp.asarray([p[1] for p in pairs], jnp.int32)

    def body(i_tab_ref, t_tab_ref, q_ref, k_ref, v_ref, *rest):
        ex_in, (o_ref, lse_ref), ex_out = rest[:ex.n], rest[ex.n:ex.n + 2], rest[ex.n + 2:2 * ex.n + 2]
        m_scr, acc_scr, s_scr, p_scr, part_scr, vext_scr = rest[2 * ex.n + 2:2 * ex.n + 8]
        ex_sems = rest[2 * ex.n + 8:]
        j, step_no = pl.program_id(0), pl.program_id(1)
        i, t = i_tab_ref[step_no], t_tab_ref[step_no]
        if ex.n:
            pl.when((j == 0) & (step_no == 0))(lambda: ex.start(ex_in, ex_out, ex_sems))

        @pl.when(t == 0)
        def _():
            m_scr[...] = jnp.full(m_scr.shape, MASK_VALUE, F32)
            acc_scr[...] = jnp.zeros(acc_scr.shape, F32)

        def step(kind):
            def scores(hh, r, c0, masked):
                s = s_scr[hh % SCORE_BUFS, r * ROW_BLOCK:(r + 1) * ROW_BLOCK, c0:c0 + ATTN_BLOCK]
                return _causal_mask(s, i * tq + r * ROW_BLOCK, t * tk + c0) if masked else s

            for pp in range(FWD_PAIRS):
                vext_scr[pp, :, 0:HEAD_PAD] = v_ref[:, pp * HEAD_PAD:(pp + 1) * HEAD_PAD]
                vext_scr[pp, :, HEAD_PAD:2 * HEAD_PAD] = jnp.ones((tk, HEAD_PAD), BF16)
            half = tq // 2
            order = []
            for g0 in range(0, heads, SCORE_BUFS):
                group = range(g0, min(g0 + SCORE_BUFS, heads))
                order += [("scores", hh) for hh in group] + [("softmax", hh) for hh in group]
            for what, hh in order:
                sb = hh % SCORE_BUFS
                if what == "scores":
                    hb = slice(hh * HEAD_PAD, (hh + 1) * HEAD_PAD)
                    if kind == "diag":
                        s_scr[sb, 0:half, 0:half] = _nt(q_ref[0:half, hb], k_ref[0:half, hb])
                        s_scr[sb, half:tq, :] = _nt(q_ref[half:tq, hb], k_ref[:, hb])
                    else:
                        s_scr[sb] = _nt(q_ref[:, hb], k_ref[:, hb])
                    continue
                for r in range(n_blocks):
                    part = None
                    for c0, masked in _score_chunks(kind, r, tk):
                        s = scores(hh, r, c0, masked)
                        part = s if part is None else jnp.maximum(part, s)
                    part_scr[r * ROW_BLOCK:(r + 1) * ROW_BLOCK, :] = part
                m_prev = m_scr[hh]
                m_new = jnp.maximum(m_prev, _lanes(jnp.max(part_scr[...], axis=-1, keepdims=True)))
                alpha = jnp.exp2(m_prev - m_new)
                m_scr[hh] = m_new
                for r in range(n_blocks):
                    rows = slice(r * ROW_BLOCK, (r + 1) * ROW_BLOCK)
                    m_r = m_scr[hh, rows, :]
                    chunks = _score_chunks(kind, r, tk)
                    for c0, masked in chunks:
                        p_scr[sb, rows, c0:c0 + ATTN_BLOCK] = jnp.exp2((scores(hh, r, c0, masked) - m_r).astype(BF16))
                    done = chunks[-1][0] + ATTN_BLOCK
                    if done < tk:
                        p_scr[sb, rows, done:tk] = jnp.zeros((ROW_BLOCK, tk - done), BF16)
                alpha2 = jnp.concatenate([alpha, alpha], axis=1)
                if kind == "diag":
                    acc_scr[hh, 0:half, :] = (alpha2[0:half] * acc_scr[hh, 0:half, :]
                                              + _mm(p_scr[sb, 0:half, 0:half], vext_scr[hh // 2, 0:half, :]))
                    acc_scr[hh, half:tq, :] = (alpha2[half:tq] * acc_scr[hh, half:tq, :]
                                               + _mm(p_scr[sb, half:tq, :], vext_scr[hh // 2]))
                else:
                    acc_scr[hh] = alpha2 * acc_scr[hh] + _mm(p_scr[sb], vext_scr[hh // 2])

        for kind, pred in _tile_kinds(i, t):
            pl.when(pred)(functools.partial(step, kind))

        @pl.when(t == i)
        def _():
            lane = lax.broadcasted_iota(jnp.int32, (tq, HEAD_PAD), 1)
            for pp in range(FWD_PAIRS):
                pb = slice(pp * HEAD_PAD, (pp + 1) * HEAD_PAD)
                h0, h1 = 2 * pp, 2 * pp + 1
                l0, l1 = acc_scr[h0, :, HEAD_PAD:2 * HEAD_PAD], acc_scr[h1, :, HEAD_PAD:2 * HEAD_PAD]
                o = jnp.where(lane < V_DIM, acc_scr[h0, :, 0:HEAD_PAD] / l0, acc_scr[h1, :, 0:HEAD_PAD] / l1)
                o_ref[:, pb] = o.astype(o_ref.dtype)
                lse_ref[:, pb] = jnp.where(lane < V_DIM, m_scr[h0] + jnp.log(l0) * LOG2E, m_scr[h1] + jnp.log(l1) * LOG2E)

        if ex.n:
            pl.when((j == nj - 1) & (step_no == len(pairs) - 1))(lambda: ex.wait(ex_in, ex_out, ex_sems))

    q_idx = lambda j, p, it, tt: (it[p], j)
    kv_idx = lambda j, p, it, tt: (tt[p], j)
    o, lse, *gathered = pl.pallas_call(
        body, name="flash_fwd_gather" if ex.n else "flash_fwd",
        grid_spec=pltpu.PrefetchScalarGridSpec(
            num_scalar_prefetch=2, grid=(nj, len(pairs)),
            in_specs=[pl.BlockSpec((tq, heads * HEAD_PAD), q_idx), pl.BlockSpec((tk, heads * HEAD_PAD), kv_idx),
                      pl.BlockSpec((tk, FWD_PAIRS * HEAD_PAD), kv_idx)] + ex.specs,
            out_specs=[pl.BlockSpec((tq, FWD_PAIRS * HEAD_PAD), q_idx)] * 2 + ex.specs,
            scratch_shapes=[pltpu.VMEM((heads, tq, HEAD_PAD), F32), pltpu.VMEM((heads, tq, 2 * HEAD_PAD), F32),
                            pltpu.VMEM((SCORE_BUFS, tq, tk), F32), pltpu.VMEM((SCORE_BUFS, tq, tk), BF16),
                            pltpu.VMEM((tq, HEAD_PAD), F32), pltpu.VMEM((FWD_PAIRS, tk, 2 * HEAD_PAD), BF16)]
            + (ex.sems if ex.n else [])),
        out_shape=[jax.ShapeDtypeStruct((lp, MLA_WIDTH), BF16), jax.ShapeDtypeStruct((lp, MLA_WIDTH), F32)] + ex.out_shape,
        compiler_params=_cp(("arbitrary",) * 2),
    )(i_tab, t_tab, q, k, v, *gather)
    return o, lse, gathered


def _merge_fwd(x, a_pool, o, zm, gp, gm, wpu, wmu, wout):
    lp = x.shape[0]
    tm = _row_tile(lp)

    def body(x_ref, ap_ref, o_ref, zm_ref, gp_ref, gm_ref, wpu_ref, wmu_ref, wout_ref, xn_ref, yp_ref, ym_ref):
        yp = _mm(ap_ref[...], wpu_ref[...])
        zf = zm_ref[...].astype(F32)
        amla = o_ref[...].astype(F32) * (zf * _sigmoid(zf))
        ym = _mm(amla.astype(BF16), wmu_ref[...])
        merged = _sigmoid(gp_ref[...].astype(F32)) * yp + _sigmoid(gm_ref[...].astype(F32)) * ym
        xn_ref[...] = x_ref[...] + _mm(merged.astype(BF16), wout_ref[...])
        yp_ref[...] = yp.astype(yp_ref.dtype)
        ym_ref[...] = ym.astype(ym_ref.dtype)

    return pl.pallas_call(
        body, name="merge_fwd", grid=(lp // tm,),
        in_specs=[_rows(tm, D_MODEL), _rows(tm, POOL_WIDTH), _rows(tm, MLA_WIDTH), _rows(tm, MLA_WIDTH),
                  _rows(tm, D_MODEL), _rows(tm, D_MODEL),
                  _resident((POOL_WIDTH, D_MODEL)), _resident((MLA_WIDTH, D_MODEL)), _resident((D_MODEL, D_MODEL))],
        out_specs=[_rows(tm, D_MODEL)] * 3,
        out_shape=[jax.ShapeDtypeStruct((lp, D_MODEL), F32), jax.ShapeDtypeStruct((lp, D_MODEL), BF16),
                   jax.ShapeDtypeStruct((lp, D_MODEL), BF16)],
        compiler_params=_cp(("parallel",)),
    )(x, a_pool, o, zm, gp, gm, wpu, wmu, wout)


def _loss_head(y, target):
    lp = y.shape[0]
    blk = ROW0
    n_real = target.shape[0] // blk
    assert (lp // blk) % LOSS_GROUP == 0

    def body(y_ref, *rest):
        t_refs, (d_ref, l_ref) = rest[:LOSS_GROUP], rest[LOSS_GROUP:]
        i = pl.program_id(0)

        @pl.when(i == 0)
        def _():
            l_ref[...] = jnp.zeros(l_ref.shape, F32)

        total = jnp.zeros((), F32)
        for g in range(LOSS_GROUP):
            b = i * LOSS_GROUP + g
            rows = slice(g * blk, (g + 1) * blk)
            err = jnp.where((b >= 1) & (b <= n_real), y_ref[rows, :] - t_refs[g][...], 0.0)
            d_ref[rows, :] = err * (1.0 / D_MODEL)
            total = total + jnp.sum(err * err)
        l_ref[...] += total * (0.5 / D_MODEL)

    def target_block(g):
        return pl.BlockSpec((blk, D_MODEL), lambda i: (jnp.clip(i * LOSS_GROUP + g - 1, 0, n_real - 1), 0))

    return pl.pallas_call(
        body, name="loss_head", grid=(lp // (blk * LOSS_GROUP),),
        in_specs=[_rows(blk * LOSS_GROUP, D_MODEL)] + [target_block(g) for g in range(LOSS_GROUP)],
        out_specs=[_rows(blk * LOSS_GROUP, D_MODEL), pl.BlockSpec((8, 128), lambda i: (0, 0))],
        out_shape=[jax.ShapeDtypeStruct((lp, D_MODEL), F32), jax.ShapeDtypeStruct((8, 128), F32)],
        compiler_params=_cp(("arbitrary",)),
    )(y, *([target] * LOSS_GROUP))


def _pair_rowsum(prod):
    lane = lax.broadcasted_iota(jnp.int32, prod.shape, 1)
    lo = jnp.sum(jnp.where(lane < V_DIM, prod, 0.0), axis=-1, keepdims=True)
    hi = jnp.sum(jnp.where(lane < V_DIM, 0.0, prod), axis=-1, keepdims=True)
    return jnp.where(lane < V_DIM, lo, hi)


def _merge_bwd(dres, yp, ym, gp, gm, o, zm, wout, wpu, wmu):
    lp = dres.shape[0]
    tm = _row_tile(lp, ROW_TILES_HEAVY)

    def body(dres_ref, yp_ref, ym_ref, gp_ref, gm_ref, o_ref, zm_ref, wout_ref, wpu_ref, wmu_ref,
             merged_ref, dyp_ref, dym_ref, dgp_ref, dgm_ref, dap_ref, amla_ref, do_ref, dzm_ref, delta_ref):
        dmerged = _nt(dres_ref[...].astype(BF16), wout_ref[...])
        sp = _sigmoid(gp_ref[...].astype(F32))
        sm = _sigmoid(gm_ref[...].astype(F32))
        ypf = yp_ref[...].astype(F32)
        ymf = ym_ref[...].astype(F32)
        merged_ref[...] = (sp * ypf + sm * ymf).astype(merged_ref.dtype)
        dyp = (dmerged * sp).astype(BF16)
        dym = (dmerged * sm).astype(BF16)
        dyp_ref[...] = dyp
        dym_ref[...] = dym
        dgp_ref[...] = (dmerged * ypf * sp * (1.0 - sp)).astype(dgp_ref.dtype)
        dgm_ref[...] = (dmerged * ymf * sm * (1.0 - sm)).astype(dgm_ref.dtype)
        dap_ref[...] = _nt(dyp, wpu_ref[...]).astype(dap_ref.dtype)
        dam = _nt(dym, wmu_ref[...])
        zf = zm_ref[...].astype(F32)
        sg = _sigmoid(zf)
        si = zf * sg
        of = o_ref[...].astype(F32)
        amla_ref[...] = (of * si).astype(amla_ref.dtype)
        do = dam * si
        do_ref[...] = do.astype(do_ref.dtype)
        dzm_ref[...] = (dam * of * (sg * (1.0 + zf * (1.0 - sg)))).astype(dzm_ref.dtype)
        prod = do * of
        for j in range(N_HEADS // 2):
            hb = slice(j * HEAD_PAD, (j + 1) * HEAD_PAD)
            delta_ref[:, hb] = _pair_rowsum(prod[:, hb])

    bf = lambda w: jax.ShapeDtypeStruct((lp, w), BF16)
    return pl.pallas_call(
        body, name="merge_bwd", grid=(lp // tm,),
        in_specs=[_rows(tm, D_MODEL)] * 5 + [_rows(tm, MLA_WIDTH)] * 2
        + [_resident((D_MODEL, D_MODEL)), _resident((POOL_WIDTH, D_MODEL)), _resident((MLA_WIDTH, D_MODEL))],
        out_specs=[_rows(tm, D_MODEL)] * 5 + [_rows(tm, POOL_WIDTH)] + [_rows(tm, MLA_WIDTH)] * 4,
        out_shape=[bf(D_MODEL)] * 5 + [bf(POOL_WIDTH)] + [bf(MLA_WIDTH)] * 3 + [jax.ShapeDtypeStruct((lp, MLA_WIDTH), F32)],
        compiler_params=_cp(("parallel",)),
    )(dres, yp, ym, gp, gm, o, zm, wout, wpu, wmu)


def _flash_bwd(q, k, v, do, lse, delta, scatter=()):
    lp = q.shape[0]
    tq = tk = ATTN_TILE
    nq = lp // tq
    heads = 2 * BWD_PAIRS
    nj = N_HEADS // heads
    qk_w, v_w = heads * HEAD_PAD, BWD_PAIRS * HEAD_PAD
    scale = 1.0 / math.sqrt(QK_DIM)
    ex = _ChipExchange(list(scatter), scatter=True)

    pairs = [(t, i) for t in range(nq) for i in range(t, nq)]
    t_tab = jnp.asarray([p[0] for p in pairs], jnp.int32)
    i_tab = jnp.asarray([p[1] for p in pairs], jnp.int32)

    def body(t_tab_ref, i_tab_ref, q_ref, k_ref, v_ref, do_ref, lse_ref, dl_ref, *rest):
        ex_in, (dq_hbm, dk_ref, dv_ref), ex_out = rest[:ex.n], rest[ex.n:ex.n + 3], rest[ex.n + 3:2 * ex.n + 3]
        (dq_acc, dk_acc, dv_acc, s_scr, dp_scr, p_scr, ds_scr, doh_scr, stat_scr, stage_scr,
         stage_sem) = rest[2 * ex.n + 3:2 * ex.n + 14]
        ex_sems = rest[2 * ex.n + 14:]
        j, step_no = pl.program_id(0), pl.program_id(1)
        t, i = t_tab_ref[step_no], i_tab_ref[step_no]
        if ex.n:
            pl.when((j == 0) & (step_no == 0))(lambda: ex.start(ex_in, ex_out, ex_sems))

        @pl.when(step_no == 0)
        def _():
            dq_acc[...] = jnp.zeros(dq_acc.shape, F32)

        @pl.when(i == t)
        def _():
            dk_acc[...] = jnp.zeros(dk_acc.shape, F32)
            dv_acc[...] = jnp.zeros(dv_acc.shape, F32)

        def step(kind):
            lane = lax.broadcasted_iota(jnp.int32, (tq, HEAD_PAD), 1)
            half = tq // 2
            for pp in range(BWD_PAIRS):
                pb = slice(pp * HEAD_PAD, (pp + 1) * HEAD_PAD)
                for hh in range(2):
                    hb = slice((2 * pp + hh) * HEAD_PAD, (2 * pp + hh + 1) * HEAD_PAD)
                    mine = (lane < V_DIM) if hh == 0 else (lane >= V_DIM)
                    doh_scr[hh] = jnp.where(mine, do_ref[:, pb], jnp.zeros((tq, HEAD_PAD), BF16))
                    if kind == "diag":
                        s_scr[hh, 0:half, 0:half] = _nt(q_ref[0:half, hb], k_ref[0:half, hb])
                        s_scr[hh, half:tq, :] = _nt(q_ref[half:tq, hb], k_ref[:, hb])
                        dp_scr[hh, 0:half, 0:half] = _nt(doh_scr[hh, 0:half, :], v_ref[0:half, pb])
                        dp_scr[hh, half:tq, :] = _nt(doh_scr[hh, half:tq, :], v_ref[:, pb])
                    else:
                        s_scr[hh] = _nt(q_ref[:, hb], k_ref[:, hb])
                        dp_scr[hh] = _nt(doh_scr[hh], v_ref[:, pb])
                for hh in range(2):
                    head = 2 * pp + hh
                    hb = slice(head * HEAD_PAD, (head + 1) * HEAD_PAD)
                    col = slice(pp * HEAD_PAD + hh * V_DIM, pp * HEAD_PAD + hh * V_DIM + 1)
                    stat_scr[0] = _lanes(lse_ref[:, col])
                    stat_scr[1] = _lanes(dl_ref[:, col])
                    for r in range(tq // ROW_BLOCK):
                        rows = slice(r * ROW_BLOCK, (r + 1) * ROW_BLOCK)
                        lse_r = stat_scr[0, rows, :]
                        dl_r = stat_scr[1, rows, :]
                        chunks = _score_chunks(kind, r, tk)
                        for c0, masked in chunks:
                            cols = slice(c0, c0 + ATTN_BLOCK)
                            s = s_scr[hh, rows, cols]
                            if masked:
                                s = _causal_mask(s, i * tq + r * ROW_BLOCK, t * tk + c0)
                            p = jnp.exp2(s - lse_r)
                            p_scr[hh, rows, cols] = p.astype(BF16)
                            ds_scr[hh, rows, cols] = (p * (dp_scr[hh, rows, cols] - dl_r)).astype(BF16)
                        done = chunks[-1][0] + ATTN_BLOCK
                        if done < tk:
                            zeros = jnp.zeros((ROW_BLOCK, tk - done), BF16)
                            p_scr[hh, rows, done:tk] = zeros
                            ds_scr[hh, rows, done:tk] = zeros
                    tnt = lambda a, b: lax.dot_general(a, b, (((0,), (1,)), ((), ())), preferred_element_type=F32)
                    if kind == "diag":
                        dv_acc[pp, :, 0:half] += _tn(doh_scr[hh], p_scr[hh, :, 0:half])
                        dv_acc[pp, :, half:tk] += _tn(doh_scr[hh, half:tq, :], p_scr[hh, half:tq, half:tk])
                        dk_acc[head, :, 0:half] += _tn(q_ref[:, hb], ds_scr[hh, :, 0:half])
                        dk_acc[head, :, half:tk] += _tn(q_ref[half:tq, hb], ds_scr[hh, half:tq, half:tk])
                        dq_acc[i, hb, 0:half] += tnt(k_ref[0:half, hb], ds_scr[hh, 0:half, 0:half])
                        dq_acc[i, hb, half:tq] += tnt(k_ref[:, hb], ds_scr[hh, half:tq, :])
                    else:
                        dv_acc[pp] += _tn(doh_scr[hh], p_scr[hh])
                        dk_acc[head] += _tn(q_ref[:, hb], ds_scr[hh])
                        dq_acc[i, hb, :] += tnt(k_ref[:, hb], ds_scr[hh])

        for kind, pred in _tile_kinds(i, t):
            pl.when(pred)(functools.partial(step, kind))

        @pl.when(i == nq - 1)
        def _():
            for head in range(heads):
                hb = slice(head * HEAD_PAD, (head + 1) * HEAD_PAD)
                dk_ref[:, hb] = (dk_acc[head].T * LN2).astype(dk_ref.dtype)
            for pp in range(BWD_PAIRS):
                dv_ref[:, pp * HEAD_PAD:(pp + 1) * HEAD_PAD] = dv_acc[pp].T.astype(dv_ref.dtype)

        @pl.when(step_no == len(pairs) - 1)
        def _():
            my_cols = pl.ds(pl.multiple_of(j * qk_w, qk_w), qk_w)
            for qi in range(nq):
                for head in range(heads):
                    hb = slice(head * HEAD_PAD, (head + 1) * HEAD_PAD)
                    stage_scr[:, hb] = (dq_acc[qi, hb, :].T * scale).astype(BF16)
                out = pltpu.make_async_copy(stage_scr, dq_hbm.at[pl.ds(qi * tq, tq), my_cols], stage_sem)
                out.start()
                out.wait()

        if ex.n:
            pl.when((j == nj - 1) & (step_no == len(pairs) - 1))(lambda: ex.wait(ex_in, ex_out, ex_sems))

    q_idx = lambda j, p, tt, it: (it[p], j)
    kv_idx = lambda j, p, tt, it: (tt[p], j)
    hw = N_HEADS * HEAD_PAD
    dq, dk, dv, *pieces = pl.pallas_call(
        body, name="flash_bwd_scatter" if ex.n else "flash_bwd",
        grid_spec=pltpu.PrefetchScalarGridSpec(
            num_scalar_prefetch=2, grid=(nj, len(pairs)),
            in_specs=[pl.BlockSpec((tq, qk_w), q_idx), pl.BlockSpec((tk, qk_w), kv_idx),
                      pl.BlockSpec((tk, v_w), kv_idx), pl.BlockSpec((tq, v_w), q_idx),
                      pl.BlockSpec((tq, v_w), q_idx), pl.BlockSpec((tq, v_w), q_idx)] + ex.specs,
            out_specs=[HBM_SPEC, pl.BlockSpec((tk, qk_w), kv_idx), pl.BlockSpec((tk, v_w), kv_idx)] + ex.specs,
            scratch_shapes=[pltpu.VMEM((nq, qk_w, tq), F32), pltpu.VMEM((heads, HEAD_PAD, tk), F32),
                            pltpu.VMEM((BWD_PAIRS, HEAD_PAD, tk), F32),
                            pltpu.VMEM((2, tq, tk), F32), pltpu.VMEM((2, tq, tk), F32),
                            pltpu.VMEM((2, tq, tk), BF16), pltpu.VMEM((2, tq, tk), BF16),
                            pltpu.VMEM((2, tq, HEAD_PAD), BF16), pltpu.VMEM((2, tq, HEAD_PAD), F32),
                            pltpu.VMEM((tq, qk_w), BF16), pltpu.SemaphoreType.DMA(())]
            + (ex.sems if ex.n else [])),
        out_shape=[jax.ShapeDtypeStruct((lp, hw), BF16), jax.ShapeDtypeStruct((lp, hw), BF16),
                   jax.ShapeDtypeStruct((lp, MLA_WIDTH), BF16)] + ex.out_shape,
        compiler_params=_cp(("arbitrary",) * 2),
    )(t_tab, i_tab, q, k, v, do, lse, delta, *scatter)
    return dq, dk, dv, pieces


def _mla_prep_bwd(dq, dk, dv, cq, ckv, kr, tabs, gqa, gkva, gqn, gkn, wq, wkn, wv):
    lp = cq.shape[0]
    tm = _row_tile(lp)
    hw = N_HEADS * HEAD_PAD

    def body(dq_ref, dk_ref, dv_ref, cq_ref, ckv_ref, kr_ref, c_ref, s1_ref, s2_ref, gqa_ref, gkva_ref, gqn_ref,
             gkn_ref, wq_ref, wkn_ref, wv_ref, dcq_ref, dckv_ref, dkr_ref, dwq_ref, dwkn_ref, dwv_ref,
             dgqa_ref, dgkva_ref, dgqn_ref, dgkn_ref, draw_scr):
        @pl.when(pl.program_id(0) == 0)
        def _():
            for r in (dwq_ref, dwkn_ref, dwv_ref, dgqa_ref, dgkva_ref, dgqn_ref, dgkn_ref):
                r[...] = jnp.zeros(r.shape, F32)

        c, s1, s2 = c_ref[...], s1_ref[...], s2_ref[...]
        lane = lax.broadcasted_iota(jnp.int32, (tm, HEAD_PAD), 1)

        inv_q, xhat_q, cqn = _rms_fwd(cq_ref[...].astype(F32), gqa_ref[...])
        cqn_b = cqn.astype(BF16)
        qraw = _mm(cqn_b, wq_ref[...])
        dgqn = jnp.zeros((1, HEAD_PAD), F32)
        for h in range(N_HEADS):
            hb = slice(h * HEAD_PAD, (h + 1) * HEAD_PAD)
            inv, xhat, _ = _head_norm_fwd(qraw[:, hb], gqn_ref[...])
            dy = _rope_t(dq_ref[:, hb].astype(F32), c, s1, s2)
            dgqn += jnp.sum(dy * xhat, axis=0, keepdims=True)
            draw_scr[:, hb] = _head_norm_bwd(dy, inv, xhat, gqn_ref[...]).astype(BF16)
        dgqn_ref[...] += dgqn
        dqraw = draw_scr[...]
        dwq_ref[...] += _tn(cqn_b, dqraw)
        dcq, dgqa = _rms_bwd(_nt(dqraw, wq_ref[...]), inv_q, xhat_q, gqa_ref[...])
        dcq_ref[...] = dcq.astype(dcq_ref.dtype)
        dgqa_ref[...] += dgqa

        inv_kv, xhat_kv, ckvn = _rms_fwd(ckv_ref[...].astype(F32), gkva_ref[...])
        ckvn_b = ckvn.astype(BF16)
        knraw = _mm(ckvn_b, wkn_ref[...])
        krs = kr_ref[...].astype(F32)
        dgkn = jnp.zeros((1, HEAD_PAD), F32)
        dkr = jnp.zeros((tm, HEAD_PAD), F32)
        for h in range(N_HEADS):
            hb = slice(h * HEAD_PAD, (h + 1) * HEAD_PAD)
            inv, xhat, _ = _head_norm_fwd(knraw[:, hb] + krs, gkn_ref[...])
            dy = _rope_t(dk_ref[:, hb].astype(F32), c, s1, s2)
            dgkn += jnp.sum(dy * xhat, axis=0, keepdims=True)
            dxh = _head_norm_bwd(dy, inv, xhat, gkn_ref[...])
            dkr += dxh
            draw_scr[:, hb] = jnp.where(lane < NOPE, dxh, 0.0).astype(BF16)
        dgkn_ref[...] += dgkn
        dkr_ref[...] = jnp.where((lane >= KR_LANE0) & (lane < QK_DIM), dkr, 0.0).astype(dkr_ref.dtype)
        dknraw = draw_scr[...]
        dvb = dv_ref[...]
        dwkn_ref[...] += _tn(ckvn_b, dknraw)
        dwv_ref[...] += _tn(ckvn_b, dvb)
        dckvn = _nt(dknraw, wkn_ref[...]) + _nt(dvb, wv_ref[...])
        dckv, dgkva = _rms_bwd(dckvn, inv_kv, xhat_kv, gkva_ref[...])
        dckv_ref[...] = dckv.astype(dckv_ref.dtype)
        dgkva_ref[...] += dgkva

    vec = lambda n: pl.BlockSpec((1, n), lambda i: (0, 0))
    whole = lambda r, c: pl.BlockSpec((r, c), lambda i: (0, 0))
    f = lambda r, c: jax.ShapeDtypeStruct((r, c), F32)
    return pl.pallas_call(
        body, name="mla_prep_bwd", grid=(lp // tm,),
        in_specs=[_rows(tm, hw), _rows(tm, hw), _rows(tm, MLA_WIDTH), _rows(tm, Q_RANK), _rows(tm, KV_RANK),
                  _rows(tm, HEAD_PAD)] + [_rows(tm, HEAD_PAD)] * 3
        + [vec(Q_RANK), vec(KV_RANK), vec(HEAD_PAD), vec(HEAD_PAD),
           _resident((Q_RANK, hw)), _resident((KV_RANK, hw)), _resident((KV_RANK, MLA_WIDTH))],
        out_specs=[_rows(tm, Q_RANK), _rows(tm, KV_RANK), _rows(tm, HEAD_PAD),
                   whole(Q_RANK, hw), whole(KV_RANK, hw), whole(KV_RANK, MLA_WIDTH),
                   vec(Q_RANK), vec(KV_RANK), vec(HEAD_PAD), vec(HEAD_PAD)],
        out_shape=[jax.ShapeDtypeStruct((lp, Q_RANK), BF16), jax.ShapeDtypeStruct((lp, KV_RANK), BF16),
                   jax.ShapeDtypeStruct((lp, HEAD_PAD), BF16),
                   f(Q_RANK, hw), f(KV_RANK, hw), f(KV_RANK, MLA_WIDTH),
                   f(1, Q_RANK), f(1, KV_RANK), f(1, HEAD_PAD), f(1, HEAD_PAD)],
        scratch_shapes=[pltpu.VMEM((tm, hw), BF16)],
        compiler_params=_cp(("arbitrary",)),
    )(dq, dk, dv, cq, ckv, kr, *tabs, gqa, gkva, gqn, gkn, wq, wkn, wv)


def _pool_bwd(dap, u, zp, wg, scale):
    lp = u.shape[0]
    tm = _row_tile(lp)
    n = lp // tm
    per = tm // HALO

    def body(dap_ref, u_ref, uh_ref, z_ref, wg_ref, sc_ref, du_ref, dz_ref, dwg_ref, dsc_ref, ext_u, ext_d):
        i = pl.program_id(0)
        r = n - 1 - i

        @pl.when(i == 0)
        def _():
            dwg_ref[...] = jnp.zeros(dwg_ref.shape, F32)
            dsc_ref[...] = jnp.zeros(dsc_ref.shape, F32)
            ext_d[tm:tm + HALO, :] = jnp.zeros((HALO, POOL_WIDTH), F32)

        ext_u[0:HALO, :] = jnp.where(r == 0, 0.0, uh_ref[...].astype(F32))
        ext_u[HALO:HALO + tm, :] = u_ref[...].astype(F32)
        e = ext_u[...]
        sums = _trailing_sums(e)
        inv_cnt = _inv_counts(r, tm)
        dmixed = []
        for g in range(POOL_GROUPS):
            cols = slice(g * GROUP_DIM, (g + 1) * GROUP_DIM)
            mixed_b = (sums[g][HALO:, :] * inv_cnt[g] - e[HALO:, cols]).astype(BF16)
            yg = _mm(mixed_b, wg_ref[g])
            zf = z_ref[:, cols].astype(F32)
            sg = _sigmoid(zf)
            da = dap_ref[:, cols].astype(F32)
            dy = da * (zf * sg)
            dz_ref[:, cols] = (da * (yg * sc_ref[:, cols]) * (sg * (1.0 + zf * (1.0 - sg)))).astype(dz_ref.dtype)
            dsc_ref[:, cols] += jnp.sum(dy * yg, axis=0, keepdims=True)
            dyg = (dy * sc_ref[:, cols]).astype(BF16)
            dwg_ref[g] += _tn(mixed_b, dyg)
            dm = _nt(dyg, wg_ref[g])
            dmixed.append(dm)
            ext_d[0:tm, cols] = dm * inv_cnt[g]
        ed = ext_d[...]
        lead = _leading_sums(ed)
        ext_d[tm:tm + HALO, :] = ed[0:HALO, :]
        for g in range(POOL_GROUPS):
            cols = slice(g * GROUP_DIM, (g + 1) * GROUP_DIM)
            du_ref[:, cols] = (lead[g][0:tm, :] - dmixed[g]).astype(du_ref.dtype)

    rev = lambda i: (n - 1 - i, 0)
    return pl.pallas_call(
        body, name="pool_bwd", grid=(n,),
        in_specs=[pl.BlockSpec((tm, POOL_WIDTH), rev), pl.BlockSpec((tm, POOL_WIDTH), rev),
                  pl.BlockSpec((HALO, POOL_WIDTH), lambda i: (jnp.maximum((n - 1 - i) * per - 1, 0), 0)),
                  pl.BlockSpec((tm, POOL_WIDTH), rev),
                  pl.BlockSpec((POOL_GROUPS, GROUP_DIM, GROUP_DIM), lambda i: (0, 0, 0)),
                  pl.BlockSpec((1, POOL_WIDTH), lambda i: (0, 0))],
        out_specs=[pl.BlockSpec((tm, POOL_WIDTH), rev), pl.BlockSpec((tm, POOL_WIDTH), rev),
                   pl.BlockSpec((POOL_GROUPS, GROUP_DIM, GROUP_DIM), lambda i: (0, 0, 0)),
                   pl.BlockSpec((1, POOL_WIDTH), lambda i: (0, 0))],
        out_shape=[jax.ShapeDtypeStruct((lp, POOL_WIDTH), BF16), jax.ShapeDtypeStruct((lp, POOL_WIDTH), BF16),
                   jax.ShapeDtypeStruct((POOL_GROUPS, GROUP_DIM, GROUP_DIM), F32),
                   jax.ShapeDtypeStruct((1, POOL_WIDTH), F32)],
        scratch_shapes=[pltpu.VMEM((HALO + tm, POOL_WIDTH), F32), pltpu.VMEM((tm + HALO, POOL_WIDTH), F32)],
        compiler_params=_cp(("arbitrary",)),
    )(dap, u, u, zp, wg, scale)


def _inproj_bwd(dres, x, gain, w_pad, dparts, scatter=()):
    lp = x.shape[0]
    tm = _row_tile(lp)
    n_dp = len(IN_WIDTHS)
    ex = _ChipExchange(list(scatter), scatter=True)

    def body(dres_ref, x_ref, g_ref, w_ref, *rest):
        dps, ex_in = rest[:n_dp], rest[n_dp:n_dp + ex.n]
        dprev_ref, h_ref, dg_ref = rest[n_dp + ex.n:n_dp + ex.n + 3]
        ex_out, ex_sems = rest[n_dp + ex.n + 3:n_dp + 2 * ex.n + 3], rest[n_dp + 2 * ex.n + 3:]
        if ex.n:
            pl.when(pl.program_id(0) == 0)(lambda: ex.start(ex_in, ex_out, ex_sems))

        @pl.when(pl.program_id(0) == 0)
        def _():
            dg_ref[...] = jnp.zeros(dg_ref.shape, F32)

        dh = jnp.zeros((tm, D_MODEL), F32)
        for dp_ref, off, wd in zip(dps, IN_OFFS, IN_WIDTHS):
            dh += _mm(dp_ref[...], w_ref[off:off + wd, :])
        inv, xhat, hn = _rms_fwd(x_ref[...], g_ref[...])
        h_ref[...] = hn.astype(h_ref.dtype)
        dx, dgain = _rms_bwd(dh, inv, xhat, g_ref[...])
        dg_ref[...] += dgain
        dprev_ref[...] = dres_ref[...] + dx
        if ex.n:
            pl.when(pl.program_id(0) == lp // tm - 1)(lambda: ex.wait(ex_in, ex_out, ex_sems))

    dprev, h, dgain, *pieces = pl.pallas_call(
        body, name="inproj_bwd_scatter" if ex.n else "inproj_bwd", grid=(lp // tm,),
        in_specs=[_rows(tm, D_MODEL), _rows(tm, D_MODEL), pl.BlockSpec((1, D_MODEL), lambda i: (0, 0)),
                  _resident((IN_PAD, D_MODEL))] + [_rows(tm, wd) for wd in IN_WIDTHS] + ex.specs,
        out_specs=[_rows(tm, D_MODEL), _rows(tm, D_MODEL), pl.BlockSpec((1, D_MODEL), lambda i: (0, 0))] + ex.specs,
        out_shape=[jax.ShapeDtypeStruct((lp, D_MODEL), F32), jax.ShapeDtypeStruct((lp, D_MODEL), BF16),
                   jax.ShapeDtypeStruct((1, D_MODEL), F32)] + ex.out_shape,
        scratch_shapes=ex.sems if ex.n else [],
        compiler_params=_cp(("arbitrary",)),
    )(dres, x, gain, w_pad, *dparts, *scatter)
    return dprev, h, dgain, pieces


def _weight_grads(a, bs, name, transposed=False):
    lp, m = a.shape
    tk = _row_tile(lp)
    nb = len(bs)
    shapes = [(b.shape[1], m) if transposed else (m, b.shape[1]) for b in bs]

    def body(a_ref, *rest):
        b_refs, o_refs = rest[:nb], rest[nb:]

        @pl.when(pl.program_id(0) == 0)
        def _():
            for o_ref in o_refs:
                o_ref[...] = jnp.zeros(o_ref.shape, F32)

        ab = a_ref[...].astype(BF16)
        for b_ref, o_ref in zip(b_refs, o_refs):
            bb = b_ref[...].astype(BF16)
            o_ref[...] += _tn(bb, ab) if transposed else _tn(ab, bb)

    return pl.pallas_call(
        body, name=name, grid=(lp // tk,),
        in_specs=[_rows(tk, m)] + [_rows(tk, b.shape[1]) for b in bs],
        out_specs=[pl.BlockSpec(s, lambda i: (0, 0)) for s in shapes],
        out_shape=[jax.ShapeDtypeStruct(s, F32) for s in shapes],
        compiler_params=_cp(("arbitrary",)),
    )(a, *bs)


HBM_SPEC = pl.BlockSpec(memory_space=pltpu.HBM)


def _my_place():
    return lax.axis_index("x"), lax.axis_index("y"), lax.axis_index("c")


def _other_chips(x, y):
    return [(1 - x, y), (x, 1 - y), (1 - x, 1 - y)]


class _ChipExchange:
    def __init__(self, arrs, scatter):
        self.n = len(arrs)
        self.scatter = scatter
        self.out_shape = [jax.ShapeDtypeStruct(a.shape if scatter else (N_CHIPS,) + a.shape, a.dtype) for a in arrs]
        self.specs = [HBM_SPEC] * self.n
        self.sems = [pltpu.SemaphoreType.DMA((3 * self.n,)), pltpu.SemaphoreType.DMA((3 * self.n,)),
                     pltpu.SemaphoreType.DMA((self.n,))]

    def _copies(self, ins, outs, sems):
        send_sems, recv_sems, local_sems = sems
        x, y, c = _my_place()
        me = 2 * x + y
        chips = _other_chips(x, y)
        mine = lambda a: ins[a].at[me] if self.scatter else ins[a]

        def remote(a, k, arriving):
            px, py = chips[k]
            there = 2 * px + py
            return pltpu.make_async_remote_copy(
                src_ref=mine(a) if arriving or not self.scatter else ins[a].at[there],
                dst_ref=outs[a].at[there if arriving else me],
                send_sem=send_sems.at[a * 3 + k], recv_sem=recv_sems.at[a * 3 + k],
                device_id=(px, py, c), device_id_type=MESH)

        pairs = [(a, k) for a in range(self.n) for k in range(3)]
        local = [pltpu.make_async_copy(mine(a), outs[a].at[me], local_sems.at[a]) for a in range(self.n)]
        return local, [remote(a, k, False) for a, k in pairs], [remote(a, k, True) for a, k in pairs]

    def start(self, ins, outs, sems):
        local, sends, _ = self._copies(ins, outs, sems)
        for cp in local + sends:
            cp.start()

    def wait(self, ins, outs, sems):
        local, sends, arrivals = self._copies(ins, outs, sems)
        for cp in arrivals:
            cp.wait_recv()
        for cp in sends:
            cp.wait_send()
        for cp in local:
            cp.wait()


def _chip_exchange(arrs, scatter, name):
    ex = _ChipExchange(arrs, scatter)

    def body(*refs):
        ins, outs, sems = refs[:ex.n], refs[ex.n:2 * ex.n], refs[2 * ex.n:]
        ex.start(ins, outs, sems)
        ex.wait(ins, outs, sems)

    return pl.pallas_call(body, name=name, in_specs=ex.specs, out_specs=ex.specs, out_shape=ex.out_shape,
                          scratch_shapes=ex.sems)(*arrs)


def _sibling_exchange(arrs, name):
    n = len(arrs)

    def body(*refs):
        ins, outs = refs[:n], refs[n:2 * n]
        send_sems, recv_sems = refs[2 * n:]
        x, y, c = _my_place()
        cps = [pltpu.make_async_remote_copy(src_ref=ins[a], dst_ref=outs[a], send_sem=send_sems.at[a],
                                            recv_sem=recv_sems.at[a], device_id=(x, y, 1 - c), device_id_type=MESH)
               for a in range(n)]
        for cp in cps:
            cp.start()
        for cp in cps:
            cp.wait_recv()
        for cp in cps:
            cp.wait_send()

    return pl.pallas_call(
        body, name=name, in_specs=[HBM_SPEC] * n, out_specs=[HBM_SPEC] * n,
        out_shape=[jax.ShapeDtypeStruct(a.shape, a.dtype) for a in arrs],
        scratch_shapes=[pltpu.SemaphoreType.DMA((n,)), pltpu.SemaphoreType.DMA((n,))],
    )(*arrs)


def _all_reduce_small(pack):
    rows = pack.shape[0]

    def body(p_ref, o_ref, g_scr, send_sems, recv_sems):
        x, y, c = _my_place()
        me = 4 * x + 2 * y + c
        flips = [(dx, dy, dc) for dx in (0, 1) for dy in (0, 1) for dc in (0, 1) if (dx, dy, dc) != (0, 0, 0)]

        def peer(f):
            return (x if f[0] == 0 else 1 - x, y if f[1] == 0 else 1 - y, c if f[2] == 0 else 1 - c)

        def copy(k, slot):
            return pltpu.make_async_remote_copy(src_ref=p_ref, dst_ref=g_scr.at[slot], send_sem=send_sems.at[k],
                                                recv_sem=recv_sems.at[k], device_id=peer(flips[k]), device_id_type=MESH)

        sends = [copy(k, me) for k in range(len(flips))]
        for cp in sends:
            cp.start()
        g_scr[me] = p_ref[...]
        for k, f in enumerate(flips):
            px, py, pc = peer(f)
            copy(k, 4 * px + 2 * py + pc).wait_recv()
        for cp in sends:
            cp.wait_send()
        acc = g_scr[0]
        for d in range(1, N_DEV):
            acc = acc + g_scr[d]
        o_ref[...] = acc

    vm = pl.BlockSpec(memory_space=pltpu.VMEM)
    return pl.pallas_call(
        body, name="all_reduce_small", in_specs=[vm], out_specs=vm,
        out_shape=jax.ShapeDtypeStruct(pack.shape, F32),
        scratch_shapes=[pltpu.VMEM((N_DEV, rows, 128), F32), pltpu.SemaphoreType.DMA((N_DEV - 1,)),
                        pltpu.SemaphoreType.DMA((N_DEV - 1,))],
        compiler_params=_cp(),
    )(pack)


def _as3d(a):
    return a.reshape((-1,) + a.shape[-2:])


def _row_block(r, sublanes=8, cap=512):
    fits = [t for t in range(sublanes, min(r, cap) + 1, sublanes) if r % t == 0]
    return fits[-1] if fits else r


def _sum_pieces(pieces, name):
    _, na, r, c = pieces.shape
    rt = _row_block(r, sublanes=16)

    def body(p_ref, o_ref):
        acc = p_ref[0, 0].astype(F32)
        for s in range(1, N_CHIPS):
            acc = acc + p_ref[s, 0].astype(F32)
        o_ref[0] = acc

    return pl.pallas_call(
        body, name=name, grid=(na, r // rt),
        in_specs=[pl.BlockSpec((N_CHIPS, 1, rt, c), lambda a, i: (0, a, i, 0))],
        out_specs=pl.BlockSpec((1, rt, c), lambda a, i: (a, i, 0)),
        out_shape=jax.ShapeDtypeStruct((na, r, c), F32),
        compiler_params=_cp(("parallel", "parallel")),
    )(pieces)


def _adamw(w, g_parts, m, v, name):
    na, r, c = w.shape
    rt = _row_block(r)
    ng = len(g_parts)

    def body(w_ref, *rest):
        g_refs = rest[:ng]
        m_ref, v_ref, g_out, d_out, m_out, v_out = rest[ng:]
        g = g_refs[0][...]
        for gr in g_refs[1:]:
            g = g + gr[...]
        m_new = ADAM_B1 * m_ref[...] + (1.0 - ADAM_B1) * g
        v_new = ADAM_B2 * v_ref[...] + (1.0 - ADAM_B2) * (g * g)
        m_hat = m_new / (1.0 - ADAM_B1 ** ADAM_STEP)
        v_hat = v_new / (1.0 - ADAM_B2 ** ADAM_STEP)
        g_out[...] = g
        d_out[...] = -ADAM_LR * (m_hat / (jnp.sqrt(v_hat) + ADAM_EPS) + ADAM_WD * w_ref[...])
        m_out[...] = m_new
        v_out[...] = v_new

    spec = pl.BlockSpec((1, rt, c), lambda a, i: (a, i, 0))
    out = jax.ShapeDtypeStruct((na, r, c), F32)
    return pl.pallas_call(
        body, name=name, grid=(na, r // rt), in_specs=[spec] * (3 + ng), out_specs=[spec] * 4, out_shape=[out] * 4,
        compiler_params=_cp(("parallel", "parallel")),
    )(w, *g_parts, m, v)


def _cols_from_shards(g):
    g = jnp.moveaxis(g, 0, -2)
    return g.reshape(g.shape[:-2] + (g.shape[-2] * g.shape[-1],))


def _rows_from_shards(g):
    g = jnp.moveaxis(g, 0, -3)
    return g.reshape(g.shape[:-3] + (g.shape[-3] * g.shape[-2], g.shape[-1]))


def _cols_to_shards(w):
    w = w.reshape(w.shape[:-1] + (N_CHIPS, w.shape[-1] // N_CHIPS))
    return jnp.moveaxis(w, -2, 0)


def _rows_to_shards(w):
    w = w.reshape(w.shape[:-2] + (N_CHIPS, w.shape[-2] // N_CHIPS, w.shape[-1]))
    return jnp.moveaxis(w, -3, 0)


def _pad_w_in(wt):
    z = lambda n: jnp.zeros((n, wt.shape[1]), wt.dtype)
    return jnp.concatenate([wt[:2048], wt[2080:4640], z(KR_LANE0), wt[2048:2080], z(HEAD_PAD - QK_DIM)], axis=0)


def _unpad_w_in(parts):
    u, zp, cq, ckv, zm, gp, gm, kr = parts
    return jnp.concatenate([u, zp, cq, ckv, kr[KR_LANE0:QK_DIM], zm, gp, gm], axis=0)


def _pad_heads(w, real):
    w = w.reshape(w.shape[:-1] + (N_HEADS, real))
    w = jnp.pad(w, [(0, 0)] * (w.ndim - 1) + [(0, HEAD_PAD - real)])
    return w.reshape(w.shape[:-2] + (N_HEADS * HEAD_PAD,))


def _flat_rows(a):
    a = a.reshape(-1)
    return jnp.pad(a, (0, (-a.shape[0]) % (8 * 128))).reshape(-1, 128)


def kernel(x, positions, meta_tokens, norm_gain, w_in, pool_w_group, pool_scale, pool_w_up, q_a_norm_gain, kv_a_norm_gain, w_q_b, w_kv_b, q_norm_gain, k_norm_gain, mla_w_up, w_out, loss_target, m_meta_tokens, m_norm_gain, m_w_in, m_pool_w_group, m_pool_scale, m_pool_w_up, m_q_a_norm_gain, m_kv_a_norm_gain, m_w_q_b, m_w_kv_b, m_q_norm_gain, m_k_norm_gain, m_mla_w_up, m_w_out, v_meta_tokens, v_norm_gain, v_w_in, v_pool_w_group, v_pool_scale, v_pool_w_up, v_q_a_norm_gain, v_kv_a_norm_gain, v_w_q_b, v_w_kv_b, v_q_norm_gain, v_k_norm_gain, v_mla_w_up, v_w_out):
    seq = x.shape[1]
    lp = -(-(ROW0 + seq) // ATTN_TILE) * ATTN_TILE
    pad_back = lp - ROW0 - seq
    chip = 2 * lax.axis_index("x") + lax.axis_index("y")

    tr = lambda a: jnp.swapaxes(a, 1, 2)
    big = dict(w_in=tr(w_in), pool_w_up=pool_w_up, w_q_b=w_q_b, w_kv_b=w_kv_b, mla_w_up=mla_w_up, w_out=w_out)
    row_sharded = ("w_in", "w_q_b", "w_out")
    names = list(big)
    shards = [[big[n][l].astype(BF16) for n in names] for l in range(DEPTH)]
    from_shards = lambda n: _rows_from_shards if n in row_sharded else _cols_from_shards
    to_shards = lambda n: _rows_to_shards if n in row_sharded else _cols_to_shards

    def in_weights(g_w_in):
        return dict(w_pad=_pad_w_in(from_shards("w_in")(g_w_in)))

    def rest_weights(gathered):
        w = {n: from_shards(n)(g) for n, g in zip(names[1:], gathered)}
        wkv = w["w_kv_b"].reshape(KV_RANK, N_HEADS, NOPE + V_DIM)
        return dict(wq=_pad_heads(w["w_q_b"], QK_DIM),
                    wkn=_pad_heads(wkv[..., :NOPE].reshape(KV_RANK, N_HEADS * NOPE), NOPE),
                    wv=wkv[..., NOPE:].reshape(KV_RANK, MLA_WIDTH),
                    wpu=w["pool_w_up"], wmu=w["mla_w_up"], wout=w["w_out"])

    g_in0, meta_g = _chip_exchange([shards[0][0], meta_tokens], scatter=False, name="gather_layer0")
    weights = [in_weights(g_in0)]
    meta_full = _cols_from_shards(meta_g)
    wg = pool_w_group.astype(BF16)
    gqn = jnp.pad(q_norm_gain, ((0, 0), (0, HEAD_PAD - QK_DIM)))
    gkn = jnp.pad(k_norm_gain, ((0, 0), (0, HEAD_PAD - QK_DIM)))

    x_pad = jnp.concatenate([jnp.zeros((PAD_FRONT, D_MODEL), F32), meta_full, x[0], jnp.zeros((pad_back, D_MODEL), F32)], axis=0)
    pos_pad = jnp.concatenate([jnp.zeros((PAD_FRONT,), jnp.int32), jnp.arange(N_META, dtype=jnp.int32),
                               positions[0] + N_META, jnp.zeros((pad_back,), jnp.int32)])
    half = ROPE // 2
    inv_freq = (ROPE_THETA ** (-np.arange(half, dtype=np.float32) / half)).astype(np.float32)
    freq_row = np.zeros((1, HEAD_PAD), np.float32)
    freq_row[0, NOPE:NOPE + half] = inv_freq
    freq_row[0, NOPE + half:QK_DIM] = inv_freq
    tabs = _rope_tables(pos_pad[:, None], jnp.asarray(freq_row))

    row = lambda a, l: a[l][None, :]

    saved = []
    h_res = x_pad
    for l in range(DEPTH):
        w = weights[l]
        (u, zp, cq, ckv, zm, gp, gm, kr), rest0 = _inproj_fwd(h_res, row(norm_gain, l), w["w_pad"],
                                                              gather=shards[0][1:] if l == 0 else ())
        if rest0:
            w.update(rest_weights(rest0))
        a_pool = _pool_fwd(u, zp, wg[l], row(pool_scale, l))
        q, k, v = _mla_prep_fwd(cq, ckv, kr, tabs, row(q_a_norm_gain, l), row(kv_a_norm_gain, l), row(gqn, l), row(gkn, l),
                                w["wq"], w["wkn"], w["wv"])
        o, lse, nxt = _flash_fwd(q, k, v, gather=shards[l + 1] if l + 1 < DEPTH else ())
        if nxt:
            weights.append({**in_weights(nxt[0]), **rest_weights(nxt[1:])})
        h_next, yp, ym = _merge_fwd(h_res, a_pool, o, zm, gp, gm, w["wpu"], w["wmu"], w["wout"])
        saved.append(dict(x=h_res, u=u, zp=zp, cq=cq, ckv=ckv, zm=zm, gp=gp, gm=gm, kr=kr, a_pool=a_pool, q=q, k=k, v=v,
                          o=o, lse=lse, yp=yp, ym=ym))
        h_res = h_next
    dres, loss_blk = _loss_head(h_res, loss_target[0])

    gw = {n: [None] * DEPTH for n in names}
    pieces = [None] * DEPTH
    grad_stacks = lambda l, which=names: [to_shards(n)(gw[n][l]).astype(BF16) for n in which]
    gs = {n: [None] * DEPTH for n in ("norm_gain", "pool_w_group", "pool_scale", "q_a", "kv_a", "q_norm", "k_norm")}
    for l in reversed(range(DEPTH)):
        s, w = saved[l], weights[l]
        merged, dyp, dym, dgp, dgm, dap, amla, do, dzm, delta = _merge_bwd(
            dres, s["yp"], s["ym"], s["gp"], s["gm"], s["o"], s["zm"], w["wout"], w["wpu"], w["wmu"])
        (gw["w_out"][l],) = _weight_grads(merged, [dres], "grad_w_out")
        (gw["pool_w_up"][l],) = _weight_grads(s["a_pool"], [dyp], "grad_pool_w_up")
        (gw["mla_w_up"][l],) = _weight_grads(amla, [dym], "grad_mla_w_up")
        dq, dk, dv, got = _flash_bwd(s["q"], s["k"], s["v"], do, s["lse"], delta,
                                     scatter=grad_stacks(l + 1) if l + 1 < DEPTH else ())
        if got:
            pieces[l + 1] = got
        dcq, dckv, dkr, dwq, dwkn, dwv, gs["q_a"][l], gs["kv_a"][l], dgqn, dgkn = _mla_prep_bwd(
            dq, dk, dv, s["cq"], s["ckv"], s["kr"], tabs, row(q_a_norm_gain, l), row(kv_a_norm_gain, l), row(gqn, l), row(gkn, l),
            w["wq"], w["wkn"], w["wv"])
        gs["q_norm"][l] = dgqn[:, :QK_DIM]
        gs["k_norm"][l] = dgkn[:, :QK_DIM]
        gw["w_q_b"][l] = dwq.reshape(Q_RANK, N_HEADS, HEAD_PAD)[..., :QK_DIM].reshape(Q_RANK, N_HEADS * QK_DIM)
        gw["w_kv_b"][l] = jnp.concatenate([dwkn.reshape(KV_RANK, N_HEADS, HEAD_PAD)[..., :NOPE],
                                           dwv.reshape(KV_RANK, N_HEADS, V_DIM)], axis=-1).reshape(KV_RANK, N_HEADS * (NOPE + V_DIM))
        du, dzp, gs["pool_w_group"][l], gs["pool_scale"][l] = _pool_bwd(dap, s["u"], s["zp"], wg[l], row(pool_scale, l))
        dparts = [du, dzp, dcq, dckv, dzm, dgp, dgm, dkr]
        dres, h, gs["norm_gain"][l], rest0 = _inproj_bwd(dres, s["x"], row(norm_gain, l), w["w_pad"], dparts,
                                                         scatter=grad_stacks(0, names[1:]) if l == 0 else ())
        ga = _weight_grads(h, [du, dzp, dcq, dckv, dkr], "grad_w_in_a", transposed=True)
        gb = _weight_grads(h, [dzm, dgp, dgm], "grad_w_in_b", transposed=True)
        gw["w_in"][l] = _unpad_w_in([ga[0], ga[1], ga[2], ga[3], gb[0], gb[1], gb[2], ga[4]])
    grad_x = dres[ROW0:ROW0 + seq][None]

    pieces[0] = list(_chip_exchange(grad_stacks(0, names[:1]), scatter=True, name="scatter_layer0")) + list(rest0)
    sums = [_sum_pieces(jnp.stack([pieces[l][a] for l in range(DEPTH)], axis=1), "sum_" + n) for a, n in enumerate(names)]
    other = _sibling_exchange(sums, name="swap_core_sums")
    moments = dict(w_in=(tr(m_w_in), tr(v_w_in)), pool_w_up=(m_pool_w_up, v_pool_w_up), w_q_b=(m_w_q_b, v_w_q_b),
                   w_kv_b=(m_w_kv_b, v_w_kv_b), mla_w_up=(m_mla_w_up, v_mla_w_up), w_out=(m_w_out, v_w_out))
    big_out = {n: _adamw(big[n], [sm, ot], moments[n][0], moments[n][1], "adamw_" + n)
               for n, sm, ot in zip(names, sums, other)}

    small_names = ("norm_gain", "pool_w_group", "pool_scale", "q_a", "kv_a", "q_norm", "k_norm")
    small_w = dict(norm_gain=(norm_gain, m_norm_gain, v_norm_gain), pool_w_group=(pool_w_group, m_pool_w_group, v_pool_w_group),
                   pool_scale=(pool_scale, m_pool_scale, v_pool_scale), q_a=(q_a_norm_gain, m_q_a_norm_gain, v_q_a_norm_gain),
                   kv_a=(kv_a_norm_gain, m_kv_a_norm_gain, v_kv_a_norm_gain), q_norm=(q_norm_gain, m_q_norm_gain, v_q_norm_gain),
                   k_norm=(k_norm_gain, m_k_norm_gain, v_k_norm_gain))
    small_g = {n: jnp.stack(gs[n]).reshape(small_w[n][0].shape) for n in small_names}
    blocks = [_flat_rows(small_g[n]) for n in small_names]
    n_rows = [b.shape[0] for b in blocks]
    meta_rows = N_META * D_MODEL // 128
    pack = jnp.concatenate(blocks + [dres[PAD_FRONT:ROW0].reshape(meta_rows, 128), loss_blk], axis=0)
    pack = jnp.pad(pack, ((0, (-pack.shape[0]) % 8), (0, 0)))
    total = _all_reduce_small(pack)
    n_small = sum(n_rows)
    loss = total[n_small + meta_rows, 0]
    gmeta = lax.dynamic_slice_in_dim(total[n_small:n_small + meta_rows].reshape(N_META, D_MODEL), chip * (D_MODEL // N_CHIPS),
                                     D_MODEL // N_CHIPS, axis=1)

    def packed(idx, meta_part):
        p = jnp.concatenate([_flat_rows(small_w[n][idx]) for n in small_names] + [_flat_rows(meta_part)], axis=0)
        return jnp.pad(p, ((0, (-p.shape[0]) % 8), (0, 0)))[None]

    g_pack = jnp.concatenate([total[:n_small], _flat_rows(gmeta)], axis=0)
    g_pack = jnp.pad(g_pack, ((0, (-g_pack.shape[0]) % 8), (0, 0)))[None]
    small_out = _adamw(packed(0, meta_tokens), [g_pack], packed(1, m_meta_tokens), packed(2, v_meta_tokens), "adamw_small")

    def unpack(p):
        res, r0 = {}, 0
        for n, nr in zip(small_names, n_rows):
            shape = small_w[n][0].shape
            res[n] = p[0, r0:r0 + nr].reshape(-1)[:math.prod(shape)].reshape(shape)
            r0 += nr
        res["meta"] = p[0, r0:r0 + N_META * (D_MODEL // N_CHIPS) // 128].reshape(N_META, D_MODEL // N_CHIPS)
        return res

    small_res = [unpack(p) for p in small_out]

    def leaf(kind, name):
        key = {"meta_tokens": "meta", "q_a_norm_gain": "q_a", "kv_a_norm_gain": "kv_a", "q_norm_gain": "q_norm",
               "k_norm_gain": "k_norm"}.get(name, name)
        if name in big_out:
            return tr(big_out[name][kind]) if name == "w_in" else big_out[name][kind]
        return small_res[kind][key]

    order = ("meta_tokens", "norm_gain", "w_in", "pool_w_group", "pool_scale", "pool_w_up", "q_a_norm_gain", "kv_a_norm_gain",
             "w_q_b", "w_kv_b", "q_norm_gain", "k_norm_gain", "mla_w_up", "w_out")
    outs = [loss, grad_x]
    for kind in range(4):
        outs += [leaf(kind, n) for n in order]
    return tuple(outs)
```
